```python
import math
import jax, jax.numpy as jnp
from jax import lax
import numpy as np

D_MODEL = 2048
BATCH = 2
SEQ = 4096
DEPTH = 4
DEC_BATCH = 8
DEC_SEQ = 1
PAST_LEN = 16384
PAGE_SIZE = 128

D_HEAD = 128
MIX_HEADS = D_MODEL // D_HEAD
A_HEADS = MIX_HEADS // 2
A_KV = 2
B_HEADS = MIX_HEADS // 4
C_HEADS = MIX_HEADS - A_HEADS - B_HEADS
C_KV = 2
C_HALF = D_HEAD // 2
IDX_HEADS = 16
IDX_DIM = 64
IDX_TOPK = 256
CMP_LEN = 32
CMP_STRIDE = 16
SEL_BLOCK = 64
SEL_TOPN = 16
WINDOW = 512
FORCE_BONUS = 1.0e4
Q_BLOCK = 128
NSA_Q_BLOCK = 32
D_FF = ((8 * D_MODEL + 3 * 256 - 1) // (3 * 256)) * 256
ROPE_THETA = 10000.0
EPS = 1e-6
NEG = -1e30
N_IN = (A_HEADS * D_HEAD + 6 * A_KV * D_HEAD + 3 * A_HEADS
        + B_HEADS * D_HEAD + 2 * D_HEAD + IDX_HEADS * IDX_DIM + IDX_HEADS + IDX_DIM
        + C_HEADS * D_HEAD + 2 * C_KV * D_HEAD)

kernel_name = 'hybrid_nsa_dsa_diff_decode_step'


def _split_points():
    sizes = (A_HEADS * D_HEAD, 6 * A_KV * D_HEAD, 3 * A_HEADS,
             B_HEADS * D_HEAD, 2 * D_HEAD, IDX_HEADS * IDX_DIM, IDX_HEADS, IDX_DIM,
             C_HEADS * D_HEAD, 2 * C_KV * D_HEAD)
    return [int(v) for v in np.cumsum(sizes)[:-1]]


def rmsnorm(x, g):
    xf = x.astype(jnp.float32)
    y = xf * lax.rsqrt(jnp.mean(xf * xf, axis=-1, keepdims=True) + EPS)
    return (y * g.astype(jnp.float32)).astype(x.dtype)


def rope(x, pos):
    half = x.shape[-1] // 2
    inv = ROPE_THETA ** (-jnp.arange(half, dtype=jnp.float32) / half)
    ang = pos.astype(jnp.float32)[:, None] * inv[None, :]
    cos = jnp.cos(ang)[:, None, :]
    sin = jnp.sin(ang)[:, None, :]
    xf = x.astype(jnp.float32)
    x1, x2 = xf[..., :half], xf[..., half:]
    return jnp.concatenate([x1 * cos - x2 * sin, x2 * cos + x1 * sin], axis=-1).astype(x.dtype)


def masked_softmax(s, mask):
    s = jnp.where(mask, s.astype(jnp.float32), NEG)
    m = jnp.max(s, axis=-1, keepdims=True)
    e = jnp.where(mask, jnp.exp(s - m), 0.0)
    return e / jnp.maximum(jnp.sum(e, axis=-1, keepdims=True), 1e-30)


def sweep(fn, qpos, block, *qs):
    T = qpos.shape[0]
    qb = min(block, T)
    n = -(-T // qb)
    pad = n * qb - T

    def to_blocks(a):
        a = jnp.pad(a, [(0, 0), (0, pad)] + [(0, 0)] * (a.ndim - 2), mode='edge')
        return jnp.moveaxis(a.reshape(a.shape[0], n, qb, *a.shape[2:]), 1, 0)

    qpos_b = jnp.pad(qpos, (0, pad), mode='edge').reshape(n, qb)
    out = lax.map(lambda args: fn(*args), (qpos_b,) + tuple(to_blocks(a) for a in qs))
    out = jnp.moveaxis(out, 0, 1)
    return out.reshape(out.shape[0], n * qb, *out.shape[3:])[:, :T]


def compress_blocks(rows, w):
    B, L = rows.shape[:2]
    n = -(-L // CMP_STRIDE)
    r = CMP_LEN // CMP_STRIDE
    rows = jnp.pad(rows, ((0, 0), (0, n * CMP_STRIDE - L), (0, 0), (0, 0)))
    rows = rows.reshape(B, n, CMP_STRIDE, *rows.shape[2:])
    wr = w.reshape(r, CMP_STRIDE)
    nc = n - r + 1
    out = jnp.einsum('bnjkd,j->bnkd', rows[:, :nc], wr[0])
    for i in range(1, r):
        out = out + jnp.einsum('bnjkd,j->bnkd', rows[:, i:i + nc], wr[i])
    cend = jnp.arange(nc, dtype=jnp.int32) * CMP_STRIDE + CMP_LEN - 1
    return out, cend


def nsa_mixer(q, gate, qpos, kv_all, win_all, win_offset, cmp_w):
    B, Lk = kv_all.shape[:2]
    G = A_HEADS // A_KV
    scale = D_HEAD ** -0.5
    kc, cend = compress_blocks(kv_all[:, :, 0], cmp_w[0])
    vc, _ = compress_blocks(kv_all[:, :, 1], cmp_w[1])
    kc = rope(kc, cend)
    n_cmp = kc.shape[1]
    n_sel = -(-Lk // SEL_BLOCK)
    n_top = min(SEL_TOPN, n_sel)

    def to_sel_blocks(a):
        a = jnp.pad(a, ((0, 0), (0, n_sel * SEL_BLOCK - Lk), (0, 0), (0, 0)))
        return a.reshape(B, n_sel, SEL_BLOCK, A_KV, D_HEAD).transpose(0, 3, 1, 2, 4)

    ks_blocks = to_sel_blocks(kv_all[:, :, 2])
    vs_blocks = to_sel_blocks(kv_all[:, :, 3])
    ci = jnp.arange(n_cmp)[:, None] * CMP_STRIDE
    sj = jnp.arange(n_sel)[None, :] * SEL_BLOCK
    overlap = ((ci < sj + SEL_BLOCK) & (ci + CMP_LEN > sj)).astype(jnp.float32)
    jidx = jnp.arange(n_sel)
    gather_blocks = jax.vmap(jax.vmap(lambda rows, idx: rows[idx]))
    wpad = ((0, 0), (WINDOW, NSA_Q_BLOCK), (0, 0), (0, 0))
    kw = jnp.pad(win_all[:, :, 0], wpad)
    vw = jnp.pad(win_all[:, :, 1], wpad)
    band = WINDOW + NSA_Q_BLOCK

    def block(qp, qb, gb):
        nq = qb.shape[1]
        qg = qb.reshape(B, nq, A_KV, G, D_HEAD)
        s_c = jnp.einsum('bqkgd,bckd->bkgqc', qg, kc).astype(jnp.float32) * scale
        p_c = masked_softmax(s_c, cend[None, :] <= qp[:, None])
        o_c = jnp.einsum('bkgqc,bckd->bqkgd', p_c, vc.astype(jnp.float32))
        imp = jnp.einsum('bkgqc,cj->bkqj', p_c, overlap)
        jq = (qp // SEL_BLOCK)[:, None]
        forced = (jidx[None, :] == 0) | (jidx[None, :] == jq) | (jidx[None, :] == jq - 1)
        imp = jnp.where(forced, imp + FORCE_BONUS, imp)
        imp = jnp.where(jidx[None, :] > jq, NEG, imp)
        _, sel = lax.top_k(imp, n_top)
        m = n_top * SEL_BLOCK
        ks = gather_blocks(ks_blocks, sel).reshape(B, A_KV, nq, m, D_HEAD)
        vs = gather_blocks(vs_blocks, sel).reshape(B, A_KV, nq, m, D_HEAD)
        tok = (sel[..., None] * SEL_BLOCK + jnp.arange(SEL_BLOCK)).reshape(B, A_KV, nq, m)
        s_s = jnp.einsum('bqkgd,bkqmd->bkgqm', qg, ks).astype(jnp.float32) * scale
        p_s = masked_softmax(s_s, (tok <= qp[None, None, :, None])[:, :, None])
        o_s = jnp.einsum('bkgqm,bkqmd->bqkgd', p_s, vs.astype(jnp.float32))
        i0 = qp[0] - win_offset
        kwb = lax.dynamic_slice_in_dim(kw, i0, band, axis=1)
        vwb = lax.dynamic_slice_in_dim(vw, i0, band, axis=1)
        kpos = qp[0] - WINDOW + jnp.arange(band)
        dist = qp[:, None] - kpos[None, :]
        mask_w = (dist >= 0) & (dist <= WINDOW) & (kpos[None, :] >= win_offset)
        s_w = jnp.einsum('bqkgd,bskd->bkgqs', qg, kwb).astype(jnp.float32) * scale
        p_w = masked_softmax(s_w, mask_w)
        o_w = jnp.einsum('bkgqs,bskd->bqkgd', p_w, vwb.astype(jnp.float32))
        g = jax.nn.sigmoid(gb.astype(jnp.float32)).reshape(B, nq, A_KV, G, 3)
        o = g[..., 0:1] * o_c + g[..., 1:2] * o_s + g[..., 2:3] * o_w
        return o.reshape(B, nq, A_HEADS * D_HEAD)

    return sweep(block, qpos, NSA_Q_BLOCK, q, gate)


def dsa_mixer(q, iq, iw, qpos, kv_all, idx_all):
    B, Lk = kv_all.shape[:2]
    k_top = min(IDX_TOPK, Lk // 4)
    k_all, v_all = kv_all[:, :, 0], kv_all[:, :, 1]
    kpos = jnp.arange(Lk, dtype=jnp.int32)
    gather = jax.vmap(lambda rows, idx: rows[idx])

    def block(qp, qb, iqb, iwb):
        nq = qb.shape[1]
        score = jax.nn.relu(jnp.einsum('bqhd,bsd->bqhs', iqb, idx_all).astype(jnp.float32) * IDX_DIM ** -0.5)
        score = jnp.einsum('bqhs,bqh->bqs', score, iwb.astype(jnp.float32)) * IDX_HEADS ** -0.5
        score = jnp.where(kpos[None, None, :] <= qp[None, :, None], score, NEG)
        _, sel = lax.top_k(score, k_top)
        ks = gather(k_all, sel)
        vs = gather(v_all, sel)
        s = jnp.einsum('bqhd,bqkd->bhqk', qb, ks).astype(jnp.float32) * D_HEAD ** -0.5
        p = masked_softmax(s, (sel <= qp[None, :, None])[:, None])
        o = jnp.einsum('bhqk,bqkd->bqhd', p, vs.astype(jnp.float32))
        return o.reshape(B, nq, B_HEADS * D_HEAD)

    return sweep(block, qpos, Q_BLOCK, q, iq, iw)


def diff_mixer(q, qpos, kv_all, lam_params, subln, layer_idx):
    B, Lk = kv_all.shape[:2]
    G = C_HEADS // C_KV
    k_all = kv_all[:, :, 0].reshape(B, Lk, C_KV, 2, C_HALF)
    v_all = kv_all[:, :, 1].astype(jnp.float32)
    kpos = jnp.arange(Lk, dtype=jnp.int32)
    lam_init = 0.8 - 0.6 * math.exp(-0.3 * layer_idx)
    lp = lam_params.astype(jnp.float32)
    lam = jnp.exp(jnp.sum(lp[0] * lp[1])) - jnp.exp(jnp.sum(lp[2] * lp[3])) + lam_init

    def block(qp, qb):
        nq = qb.shape[1]
        qg = qb.reshape(B, nq, C_KV, G, 2, C_HALF)
        s = jnp.einsum('bqkgid,bskid->ibkgqs', qg, k_all).astype(jnp.float32) * C_HALF ** -0.5
        a = jax.nn.softmax(jnp.where(kpos[None, :] <= qp[:, None], s, NEG), axis=-1)
        o = jnp.einsum('bkgqs,bskd->bqkgd', a[0] - lam * a[1], v_all)
        o = rmsnorm(o, subln) * (1.0 - lam_init)
        return o.reshape(B, nq, C_HEADS * D_HEAD)

    return sweep(block, qpos, Q_BLOCK, q)


def trunk_layer(x, pos0, past, layer_idx, attn_norm, w_in, nsa_qk_norm, nsa_cmp_w, dsa_qk_norm,
                dsa_idx_knorm, diff_qk_norm, diff_lambda, diff_subln, w_out, ffn_norm, w_gate_up, w_down):
    B, T, _ = x.shape
    qpos = pos0 + jnp.arange(T, dtype=jnp.int32)
    h = rmsnorm(x, attn_norm)
    a_q, a_kv, a_gate, b_q, b_kv, b_iq, b_iw, b_ik, c_q, c_kv = jnp.split(h @ w_in, _split_points(), axis=-1)
    q_a = rope(rmsnorm(a_q.reshape(B, T, A_HEADS, D_HEAD), nsa_qk_norm[0]), qpos)
    a_kv = a_kv.reshape(B, T, 6, A_KV, D_HEAD)
    k_cmp = rmsnorm(a_kv[:, :, 0], nsa_qk_norm[1])
    k_sel = rope(rmsnorm(a_kv[:, :, 2], nsa_qk_norm[2]), qpos)
    k_win = rope(rmsnorm(a_kv[:, :, 4], nsa_qk_norm[3]), qpos)
    new_nsa = jnp.stack([k_cmp, a_kv[:, :, 1], k_sel, a_kv[:, :, 3]], axis=2)
    new_win = jnp.stack([k_win, a_kv[:, :, 5]], axis=2)
    q_b = rope(rmsnorm(b_q.reshape(B, T, B_HEADS, D_HEAD), dsa_qk_norm[0]), qpos)
    b_kv = b_kv.reshape(B, T, 2, D_HEAD)
    k_b = rope(rmsnorm(b_kv[:, :, :1], dsa_qk_norm[1]), qpos)[:, :, 0]
    new_dsa = jnp.stack([k_b, b_kv[:, :, 1]], axis=2)
    iq = rope(b_iq.reshape(B, T, IDX_HEADS, IDX_DIM), qpos)
    ik = rope(rmsnorm(b_ik, dsa_idx_knorm)[:, :, None], qpos)[:, :, 0]
    q_c = rope(rmsnorm(c_q.reshape(B, T, 2 * C_HEADS, C_HALF), diff_qk_norm[0]), qpos).reshape(B, T, C_HEADS, D_HEAD)
    c_kv = c_kv.reshape(B, T, 2, C_KV, D_HEAD)
    k_c = rope(rmsnorm(c_kv[:, :, 0].reshape(B, T, 2 * C_KV, C_HALF), diff_qk_norm[1]), qpos).reshape(B, T, C_KV, D_HEAD)
    new_diff = jnp.stack([k_c, c_kv[:, :, 1]], axis=2)
    if past is None:
        nsa_all, win_all, dsa_all, idx_all, diff_all = new_nsa, new_win, new_dsa, ik, new_diff
    else:
        p_nsa, p_win, p_dsa, p_idx, p_diff = past
        nsa_all = jnp.concatenate([p_nsa, new_nsa], axis=1)
        win_all = jnp.concatenate([p_win, new_win], axis=1)
        dsa_all = jnp.concatenate([p_dsa, new_dsa], axis=1)
        idx_all = jnp.concatenate([p_idx, ik], axis=1)
        diff_all = jnp.concatenate([p_diff, new_diff], axis=1)
    win_offset = pos0 + T - win_all.shape[1]
    o_a = nsa_mixer(q_a, a_gate, qpos, nsa_all, win_all, win_offset, nsa_cmp_w)
    o_b = dsa_mixer(q_b, iq, b_iw, qpos, dsa_all, idx_all)
    o_c = diff_mixer(q_c, qpos, diff_all, diff_lambda, diff_subln, layer_idx)
    mix = jnp.concatenate([o_a, o_b, o_c], axis=-1).astype(x.dtype)
    x = x + mix @ w_out
    gate, up = jnp.split(rmsnorm(x, ffn_norm) @ w_gate_up, 2, axis=-1)
    x = x + (jax.nn.silu(gate) * up) @ w_down
    win_state = win_all[:, win_all.shape[1] - min(WINDOW, win_all.shape[1]):]
    return x, (new_nsa, win_state, new_dsa, ik, new_diff)


def gather_pages(cache_l, page_table):
    g = cache_l[page_table]
    return g.reshape(g.shape[0], g.shape[1] * g.shape[2], *g.shape[3:])


def setup_inputs(seed: int = 0) -> dict:
    key = jax.random.key(seed)
    ks = jax.random.split(key, 24)
    n_pages = PAST_LEN // PAGE_SIZE
    n_pool = (5 * DEC_BATCH * n_pages) // 4
    win_cache = min(WINDOW, PAST_LEN)
    f32 = jnp.float32

    def nrm(k, shape, s=1.0):
        return s * jax.random.normal(k, shape, f32)

    def gain(k, shape):
        return 1.0 + 0.1 * jax.random.normal(k, shape, f32)

    page_table = jax.random.permutation(ks[7], n_pool)[:DEC_BATCH * n_pages].reshape(DEC_BATCH, n_pages).astype(jnp.int32)
    return {
        'x_prompt': nrm(ks[0], (BATCH, SEQ, D_MODEL)),
        'x_sample': nrm(ks[1], (DEC_BATCH, DEC_SEQ, D_MODEL)),
        'cache_nsa_kv': nrm(ks[2], (DEPTH, n_pool, PAGE_SIZE, 4, A_KV, D_HEAD)),
        'state_nsa_win': nrm(ks[3], (DEPTH, DEC_BATCH, win_cache, 2, A_KV, D_HEAD)),
        'cache_dsa_kv': nrm(ks[4], (DEPTH, n_pool, PAGE_SIZE, 2, D_HEAD)),
        'cache_dsa_idx': nrm(ks[5], (DEPTH, n_pool, PAGE_SIZE, IDX_DIM)),
        'cache_diff_kv': nrm(ks[6], (DEPTH, n_pool, PAGE_SIZE, 2, C_KV, D_HEAD)),
        'page_table': page_table,
        'attn_norm': gain(ks[8], (DEPTH, D_MODEL)),
        'w_in': nrm(ks[9], (DEPTH, D_MODEL, N_IN), D_MODEL ** -0.5),
        'nsa_qk_norm': gain(ks[10], (DEPTH, 4, D_HEAD)),
        'nsa_cmp_w': (1.0 + 0.1 * jax.random.normal(ks[11], (DEPTH, 2, CMP_LEN), f32)) * CMP_LEN ** -0.5,
        'dsa_qk_norm': gain(ks[12], (DEPTH, 2, D_HEAD)),
        'dsa_idx_knorm': gain(ks[13], (DEPTH, IDX_DIM)),
        'diff_qk_norm': gain(ks[14], (DEPTH, 2, C_HALF)),
        'diff_lambda': nrm(ks[15], (DEPTH, 4, C_HALF), 0.1),
        'diff_subln': gain(ks[16], (DEPTH, D_HEAD)),
        'w_out': nrm(ks[17], (DEPTH, MIX_HEADS * D_HEAD, D_MODEL), (MIX_HEADS * D_HEAD) ** -0.5),
        'ffn_norm': gain(ks[18], (DEPTH, D_MODEL)),
        'w_gate_up': nrm(ks[19], (DEPTH, D_MODEL, 2 * D_FF), D_MODEL ** -0.5),
        'w_down': nrm(ks[20], (DEPTH, D_FF, D_MODEL), D_FF ** -0.5),
    }


def reference(x_prompt, x_sample, cache_nsa_kv, state_nsa_win, cache_dsa_kv, cache_dsa_idx, cache_diff_kv,
              page_table, attn_norm, w_in, nsa_qk_norm, nsa_cmp_w, dsa_qk_norm, dsa_idx_knorm, diff_qk_norm,
              diff_lambda, diff_subln, w_out, ffn_norm, w_gate_up, w_down):
    yp, ys = x_prompt, x_sample
    rows_p, rows_s = [], []
    for l in range(DEPTH):
        weights = (attn_norm[l], w_in[l], nsa_qk_norm[l], nsa_cmp_w[l], dsa_qk_norm[l], dsa_idx_knorm[l],
                   diff_qk_norm[l], diff_lambda[l], diff_subln[l], w_out[l], ffn_norm[l], w_gate_up[l], w_down[l])
        yp, new_p = trunk_layer(yp, 0, None, l, *weights)
        past = (gather_pages(cache_nsa_kv[l], page_table), state_nsa_win[l],
                gather_pages(cache_dsa_kv[l], page_table), gather_pages(cache_dsa_idx[l], page_table),
                gather_pages(cache_diff_kv[l], page_table))
        ys, new_s = trunk_layer(ys, PAST_LEN, past, l, *weights)
        rows_p.append(new_p)
        rows_s.append(new_s)

    def stacked(rows, i):
        return jnp.stack([r[i] for r in rows], axis=0)

    nsa_kv_p, nsa_kv_s = stacked(rows_p, 0), stacked(rows_s, 0)
    win_p, win_s = stacked(rows_p, 1), stacked(rows_s, 1)
    dsa_kv_p, dsa_kv_s = stacked(rows_p, 2), stacked(rows_s, 2)
    dsa_idx_p, dsa_idx_s = stacked(rows_p, 3), stacked(rows_s, 3)
    diff_kv_p, diff_kv_s = stacked(rows_p, 4), stacked(rows_s, 4)
    return (yp, ys, nsa_kv_p, nsa_kv_s, win_p, win_s, dsa_kv_p, dsa_kv_s, dsa_idx_p, dsa_idx_s, diff_kv_p, diff_kv_s)
```

```python
import functools
import math

import numpy as np
import jax
import jax.numpy as jnp
from jax import lax
from jax.experimental import pallas as pl
from jax.experimental.pallas import tpu as pltpu

F32 = jnp.float32
BF16 = jnp.bfloat16
I32 = jnp.int32
HI = lax.Precision.HIGHEST

D_MODEL = 2048
PAGE_SIZE = 128
D_HEAD = 128
A_HEADS = 8
A_KV = 2
A_G = A_HEADS // A_KV
B_HEADS = 4
C_HEADS = 4
C_KV = 2
C_HALF = 64
IDX_HEADS = 16
IDX_DIM = 64
IDX_TOPK = 256
CMP_LEN = 32
CMP_STRIDE = 16
SEL_BLOCK = 64
SEL_TOPN = 16
WINDOW = 512
FORCE_BONUS = 1.0e4
D_FF = 5632
ROPE_THETA = 10000.0
EPS = 1e-6
NEG = -1e30
INT_MIN = -2147483648

OFF_AQ = 0
OFF_AKV = 1024
OFF_BQ = 2560
OFF_BKV = 3072
OFF_BIQ = 3328
OFF_CQ = 4352
OFF_CKV = 4864
OFF_MISC = 5376
N_PROJ = 5632
MISC_GATE = 64
MISC_IW = 88

TQ = 128
TK = 512
PAGES_PER_STEP = 8
SEL_SHIFT = 6
SCALE_D = D_HEAD ** -0.5
VMEM_LIMIT = 56 * 1024 * 1024


def _cparams(sem):
    return pltpu.CompilerParams(dimension_semantics=sem, vmem_limit_bytes=VMEM_LIMIT)


def _dot(a, b, precision=None):
    return jnp.dot(a, b, preferred_element_type=F32, precision=precision)


def _dot_nt(a, b):
    return lax.dot_general(a, b, (((1,), (1,)), ((), ())), preferred_element_type=F32)


def _norm_mm_kernel(x_ref, g_ref, w_ref, o_ref, xn_ref):
    @pl.when(pl.program_id(1) == 0)
    def _():
        x = x_ref[...]
        ms = jnp.mean(x * x, axis=-1, keepdims=True)
        xn_ref[...] = (x * lax.rsqrt(ms + EPS) * g_ref[...]).astype(BF16)

    o_ref[...] = _dot(xn_ref[...], w_ref[...])


def _norm_matmul(x, g, w, layer, tm, tn=512):
    M, K = x.shape
    N = w.shape[2]
    return pl.pallas_call(
        _norm_mm_kernel,
        grid=(M // tm, N // tn),
        in_specs=[
            pl.BlockSpec((tm, K), lambda i, j: (i, 0)),
            pl.BlockSpec((1, K), lambda i, j: (0, 0)),
            pl.BlockSpec((None, K, tn), lambda i, j: (layer, 0, j)),
        ],
        out_specs=pl.BlockSpec((tm, tn), lambda i, j: (i, j)),
        out_shape=jax.ShapeDtypeStruct((M, N), F32),
        scratch_shapes=[pltpu.VMEM((tm, K), BF16)],
        compiler_params=_cparams(("parallel", "arbitrary")),
        name="norm_matmul",
    )(x, g, w)


def _norm_swiglu_kernel(x_ref, g_ref, wg_ref, wu_ref, o_ref, xn_ref):
    @pl.when(pl.program_id(1) == 0)
    def _():
        x = x_ref[...]
        ms = jnp.mean(x * x, axis=-1, keepdims=True)
        xn_ref[...] = (x * lax.rsqrt(ms + EPS) * g_ref[...]).astype(BF16)

    xn = xn_ref[...]
    gate = _dot(xn, wg_ref[...])
    up = _dot(xn, wu_ref[...])
    o_ref[...] = (gate * jax.nn.sigmoid(gate) * up).astype(o_ref.dtype)


def _norm_swiglu(x, g, w, layer, tm, tn=512):
    M, K = x.shape
    nj = D_FF // tn
    return pl.pallas_call(
        _norm_swiglu_kernel,
        grid=(M // tm, nj),
        in_specs=[
            pl.BlockSpec((tm, K), lambda i, j: (i, 0)),
            pl.BlockSpec((1, K), lambda i, j: (0, 0)),
            pl.BlockSpec((None, K, tn), lambda i, j: (layer, 0, j)),
            pl.BlockSpec((None, K, tn), lambda i, j: (layer, 0, j + nj)),
        ],
        out_specs=pl.BlockSpec((tm, tn), lambda i, j: (i, j)),
        out_shape=jax.ShapeDtypeStruct((M, D_FF), BF16),
        scratch_shapes=[pltpu.VMEM((tm, K), BF16)],
        compiler_params=_cparams(("parallel", "arbitrary")),
        name="norm_swiglu",
    )(x, g, w, w)


def _mm_res_kernel(a_ref, w_ref, r_ref, o_ref):
    o_ref[...] = r_ref[...] + _dot(a_ref[...], w_ref[...])


def _matmul_residual(a, w, res, layer, tm, tn=512):
    M, K = a.shape
    N = w.shape[2]
    return pl.pallas_call(
        _mm_res_kernel,
        grid=(M // tm, N // tn),
        in_specs=[
            pl.BlockSpec((tm, K), lambda i, j: (i, 0)),
            pl.BlockSpec((None, K, tn), lambda i, j: (layer, 0, j)),
            pl.BlockSpec((tm, tn), lambda i, j: (i, j)),
        ],
        out_specs=pl.BlockSpec((tm, tn), lambda i, j: (i, j)),
        out_shape=jax.ShapeDtypeStruct((M, N), F32),
        compiler_params=_cparams(("parallel", "arbitrary")),
        name="matmul_residual",
    )(a, w, res)


def _post_kernel(*refs, emit_cmp, tm):
    if emit_cmp:
        x_ref, c1_ref, s1_ref, c2_ref, s2_ref, prm_ref, wc_ref = refs[:7]
        outs = refs[7:]
    else:
        x_ref, c1_ref, s1_ref, c2_ref, s2_ref, prm_ref = refs[:6]
        wc_ref = None
        outs = refs[6:]
    (nsa_ref, win_ref, dsa_ref, ik_ref, dif_ref, qa_ref, ksel_ref, vsel_ref, kw_ref, vw_ref,
     qb_ref, kb_ref, vb_ref, iq_ref, ikd_ref, iw_ref, qc_ref, kcd_ref, vcd_ref, gat_ref) = outs[:20]

    c1, s1, c2, s2 = c1_ref[...], s1_ref[...], c2_ref[...], s2_ref[...]
    prm = prm_ref[...]
    lane = lax.broadcasted_iota(I32, (tm, 128), 1)
    lo = lane < 64
    inner = (lane & 63) < 32

    def col(a):
        return x_ref[:, a:a + 128]

    def gain(r):
        return prm[r:r + 1, :]

    def rms128(v, g):
        return v * lax.rsqrt(jnp.mean(v * v, axis=-1, keepdims=True) + EPS) * g

    def rope128(v):
        return v * c1 + pltpu.roll(v, 64, 1) * s1

    def half_ms(v):
        sq = v * v
        a = jnp.sum(jnp.where(lo, sq, 0.0), axis=-1, keepdims=True)
        b = jnp.sum(jnp.where(lo, 0.0, sq), axis=-1, keepdims=True)
        return a * (1.0 / 64), b * (1.0 / 64)

    def rms64(v, g):
        a, b = half_ms(v)
        return v * lax.rsqrt(jnp.where(lo, a, b) + EPS) * g

    def rope64(v):
        rot = jnp.where(inner, pltpu.roll(v, 96, 1), pltpu.roll(v, 32, 1))
        return v * c2 + rot * s2

    for h in range(A_HEADS):
        qa_ref[h] = (rope128(rms128(col(OFF_AQ + h * 128), gain(0))) * SCALE_D).astype(qa_ref.dtype)
    for kv in range(A_KV):
        o = kv * 128
        kc = rms128(col(OFF_AKV + o), gain(1))
        vc = col(OFF_AKV + 256 + o)
        ks = rope128(rms128(col(OFF_AKV + 512 + o), gain(2)))
        vs = col(OFF_AKV + 768 + o)
        kw = rope128(rms128(col(OFF_AKV + 1024 + o), gain(3)))
        vw = col(OFF_AKV + 1280 + o)
        nsa_ref[:, o:o + 128] = kc
        nsa_ref[:, 256 + o:384 + o] = vc
        nsa_ref[:, 512 + o:640 + o] = ks
        nsa_ref[:, 768 + o:896 + o] = vs
        win_ref[:, o:o + 128] = kw
        win_ref[:, 256 + o:384 + o] = vw
        ksel_ref[kv] = ks.astype(ksel_ref.dtype)
        vsel_ref[kv] = vs.astype(vsel_ref.dtype)
        kw_ref[kv] = kw.astype(kw_ref.dtype)
        vw_ref[kv] = vw.astype(vw_ref.dtype)
        if emit_cmp:
            pa_ref, pb_ref = outs[20], outs[21]
            pa_ref[kv] = _dot(wc_ref[0], kc, HI)
            pb_ref[kv] = _dot(wc_ref[1], kc, HI)
            pa_ref[2 + kv] = _dot(wc_ref[2], vc, HI)
            pb_ref[2 + kv] = _dot(wc_ref[3], vc, HI)
    for h in range(B_HEADS):
        qb_ref[h] = (rope128(rms128(col(OFF_BQ + h * 128), gain(4))) * SCALE_D).astype(qb_ref.dtype)
    kb = rope128(rms128(col(OFF_BKV), gain(5)))
    vb = col(OFF_BKV + 128)
    dsa_ref[:, 0:128] = kb
    dsa_ref[:, 128:256] = vb
    kb_ref[...] = kb.astype(kb_ref.dtype)
    vb_ref[...] = vb.astype(vb_ref.dtype)
    for p in range(IDX_HEADS // 2):
        v = rope64(col(OFF_BIQ + p * 128))
        iq_ref[2 * p] = jnp.where(lo, v, 0.0).astype(iq_ref.dtype)
        iq_ref[2 * p + 1] = jnp.where(lo, 0.0, v).astype(iq_ref.dtype)
    for h in range(C_HEADS):
        v = rope64(rms64(col(OFF_CQ + h * 128), gain(7))) * (C_HALF ** -0.5)
        kv, g = h // 2, h % 2
        qc_ref[kv, 2 * g] = jnp.where(lo, v, 0.0).astype(qc_ref.dtype)
        qc_ref[kv, 2 * g + 1] = jnp.where(lo, 0.0, v).astype(qc_ref.dtype)
    for kv in range(C_KV):
        o = kv * 128
        kk = rope64(rms64(col(OFF_CKV + o), gain(8)))
        vv = col(OFF_CKV + 256 + o)
        dif_ref[:, o:o + 128] = kk
        dif_ref[:, 256 + o:384 + o] = vv
        kcd_ref[kv] = kk.astype(kcd_ref.dtype)
        vcd_ref[kv] = vv.astype(vcd_ref.dtype)
    m = col(OFF_MISC)
    a, _ = half_ms(m)
    ikr = rope64(m * lax.rsqrt(a + EPS) * gain(6))
    ik_ref[...] = ikr[:, 0:64]
    ikd_ref[...] = jnp.where(lo, ikr, pltpu.roll(ikr, 64, 1)).astype(ikd_ref.dtype)
    sig = jax.nn.sigmoid(m)
    for kv in range(A_KV):
        gat_ref[kv] = pltpu.roll(sig, 128 - MISC_GATE - 12 * kv, 1)
    iw_ref[...] = pltpu.roll(m, 128 - MISC_IW, 1) * ((IDX_DIM ** -0.5) * (IDX_HEADS ** -0.5))


def _post_project(proj, tabs, prm, wc, tm, qdt):
    B, T, _ = proj.shape
    emit_cmp = wc is not None
    nt = T // tm

    def row(c):
        return pl.BlockSpec((None, tm, c), lambda b, i: (b, i, 0))

    def heads(*lead):
        n = len(lead)
        return pl.BlockSpec((None,) + lead + (tm, 128), lambda b, i: (b,) + (0,) * n + (i, 0))

    tab = pl.BlockSpec((tm, 128), lambda b, i: (i, 0))
    in_specs = [row(N_PROJ), tab, tab, tab, tab, pl.BlockSpec((16, 128), lambda b, i: (0, 0))]
    args = [proj, *tabs, prm]
    if emit_cmp:
        in_specs.append(pl.BlockSpec((4, tm // CMP_STRIDE, tm), lambda b, i: (0, 0, 0)))
        args.append(wc)

    def sds(shape, dt):
        return jax.ShapeDtypeStruct(shape, dt)

    out_shape = [
        sds((B, T, 1024), F32), sds((B, T, 512), F32), sds((B, T, 256), F32), sds((B, T, 64), F32),
        sds((B, T, 512), F32),
        sds((B, A_HEADS, T, 128), qdt),
        sds((B, A_KV, T, 128), qdt), sds((B, A_KV, T, 128), qdt),
        sds((B, A_KV, T, 128), qdt), sds((B, A_KV, T, 128), qdt),
        sds((B, B_HEADS, T, 128), qdt), sds((B, T, 128), qdt), sds((B, T, 128), qdt),
        sds((B, IDX_HEADS, T, 128), qdt), sds((B, T, 128), qdt), sds((B, T, 128), F32),
        sds((B, C_KV, 4, T, 128), qdt), sds((B, C_KV, T, 128), qdt), sds((B, C_KV, T, 128), qdt),
        sds((B, A_KV, T, 128), F32),
    ]
    out_specs = [
        row(1024), row(512), row(256), row(64), row(512),
        heads(A_HEADS), heads(A_KV), heads(A_KV), heads(A_KV), heads(A_KV),
        heads(B_HEADS), row(128), row(128), heads(IDX_HEADS), row(128), row(128),
        heads(C_KV, 4), heads(C_KV), heads(C_KV), heads(A_KV),
    ]
    if emit_cmp:
        nc = T // CMP_STRIDE
        out_shape += [sds((B, 4, nc, 128), F32), sds((B, 4, nc, 128), F32)]
        spec = pl.BlockSpec((None, 4, tm // CMP_STRIDE, 128), lambda b, i: (b, 0, i, 0))
        out_specs += [spec, spec]
    return pl.pallas_call(
        functools.partial(_post_kernel, emit_cmp=emit_cmp, tm=tm),
        grid=(B, nt),
        in_specs=in_specs,
        out_specs=out_specs,
        out_shape=out_shape,
        compiler_params=_cparams(("parallel", "parallel")),
        name="post_project",
    )(*args)


def _masked_softmax(s, mask):
    s = jnp.where(mask, s, NEG)
    m = jnp.max(s, axis=-1, keepdims=True)
    e = jnp.where(mask, jnp.exp(s - m), 0.0)
    return e / jnp.maximum(jnp.sum(e, axis=-1, keepdims=True), 1e-30)


def _online_step(s, mask, v, m, l, acc):
    if mask is not None:
        s = jnp.where(mask, s, NEG)
    m_new = jnp.maximum(m, jnp.max(s, axis=-1, keepdims=True))
    alpha = jnp.exp(m - m_new)
    e = jnp.exp(s - m_new)
    if mask is not None:
        e = jnp.where(mask, e, 0.0)
    l_new = alpha * l + jnp.sum(e, axis=-1, keepdims=True)
    rows = acc.shape[0]
    pv = _dot(e.reshape(rows, e.shape[-1]).astype(BF16), v)
    return m_new, l_new, alpha.reshape(rows, 1) * acc + pv


def _online_single(s, keep, v_row, m, l, acc):
    if keep is not None:
        s = jnp.where(keep, s, NEG)
    m_new = jnp.maximum(m, s)
    alpha = jnp.exp(m - m_new)
    e = jnp.exp(s - m_new)
    if keep is not None:
        e = jnp.where(keep, e, 0.0)
    return m_new, alpha * l + e, alpha * acc + e.astype(BF16).astype(F32) * v_row


def _rowdot(q, k_row):
    return jnp.sum(q.astype(BF16).astype(F32) * k_row.astype(BF16).astype(F32), axis=-1, keepdims=True)


def _sortable(x):
    b = lax.bitcast_convert_type(x + 0.0, I32)
    return jnp.where(b < 0, b ^ jnp.int32(0x7FFFFFFF), b)


def _lambda_of(lp, lam_init):
    a = jnp.sum(lp[0:1] * lp[1:2], axis=-1, keepdims=True)
    b = jnp.sum(lp[2:3] * lp[3:4], axis=-1, keepdims=True)
    return jnp.exp(a) - jnp.exp(b) + lam_init


def _nsa_prompt_kernel(q_ref, pak_ref, pbk_ref, pav_ref, pbv_ref, cc_ref, sc_ref,
                       ks_ref, vs_ref, kw_ref, vw_ref, g_ref, o_ref, kc_ref, vc_ref, *, T):
    qi = pl.program_id(2)
    nc = T // CMP_STRIDE
    ns = T // SEL_BLOCK
    R = A_G * TQ
    band = WINDOW + TQ

    @pl.when(qi == 0)
    def _():
        kraw = pak_ref[...] + pltpu.roll(pbk_ref[...], nc - 1, 0)
        kc_ref[...] = (kraw * cc_ref[...] + pltpu.roll(kraw, 64, 1) * sc_ref[...]).astype(BF16)
        vc_ref[...] = (pav_ref[...] + pltpu.roll(pbv_ref[...], nc - 1, 0)).astype(BF16)

    q = q_ref[...].reshape(R, 128)
    t0 = qi * TQ
    qp = t0 + lax.broadcasted_iota(I32, (TQ, 1), 0)
    qp3 = t0 + lax.broadcasted_iota(I32, (1, TQ, 1), 1)

    s_c = _dot_nt(q, kc_ref[...]).reshape(A_G, TQ, nc)
    cend = lax.broadcasted_iota(I32, (1, TQ, nc), 2) * CMP_STRIDE + (CMP_LEN - 1)
    p_c = _masked_softmax(s_c, cend <= qp3)
    o_c = _dot(p_c.reshape(R, nc).astype(BF16), vc_ref[...])

    ci = lax.broadcasted_iota(I32, (nc, ns), 0) * CMP_STRIDE
    sj = lax.broadcasted_iota(I32, (nc, ns), 1) * SEL_BLOCK
    overlap = ((ci < sj + SEL_BLOCK) & (ci + CMP_LEN > sj)).astype(F32)
    imp = _dot(jnp.sum(p_c, axis=0), overlap, HI)
    jidx = lax.broadcasted_iota(I32, (TQ, ns), 1)
    jq = qp >> SEL_SHIFT
    forced = (jidx == 0) | (jidx == jq) | (jidx == jq - 1)
    imp = jnp.where(forced, imp + FORCE_BONUS, imp)
    imp = jnp.where(jidx > jq, NEG, imp)
    rank = jnp.zeros((TQ, ns), I32)
    for j in range(ns):
        cj = imp[:, j:j + 1]
        rank = rank + ((cj > imp) | ((cj == imp) & (jidx > j))).astype(I32)
    selb = (rank < min(SEL_TOPN, ns)).astype(BF16)

    erow = lax.broadcasted_iota(I32, (ns, TK), 0)
    ecol = lax.broadcasted_iota(I32, (ns, TK), 1)
    tcol = lax.broadcasted_iota(I32, (1, TQ, TK), 2)

    def sel_step(kt, carry):
        m, l, acc = carry
        base = pl.multiple_of(kt * TK, TK)
        k = ks_ref[pl.ds(base, TK), :]
        v = vs_ref[pl.ds(base, TK), :]
        s = _dot_nt(q, k).reshape(A_G, TQ, TK)
        expand = (erow == ((ecol + base) >> SEL_SHIFT)).astype(BF16)
        chosen = _dot(selb, expand).reshape(1, TQ, TK) > 0.5
        mask = chosen & (tcol + base <= qp3)
        return _online_step(s, mask, v, m, l, acc)

    nkt = (t0 + TQ + TK - 1) // TK
    init = (jnp.full((A_G, TQ, 1), NEG, F32), jnp.zeros((A_G, TQ, 1), F32), jnp.zeros((R, 128), F32))
    _, l_s, acc_s = lax.fori_loop(0, nkt, sel_step, init)
    o_s = acc_s / jnp.maximum(l_s.reshape(R, 1), 1e-30)

    start = pl.multiple_of(jnp.maximum(t0 - WINDOW, 0), TQ)
    kwin = kw_ref[pl.ds(start, band), :]
    vwin = vw_ref[pl.ds(start, band), :]
    dist = qp3 - (start + lax.broadcasted_iota(I32, (1, TQ, band), 2))
    s_w = _dot_nt(q, kwin).reshape(A_G, TQ, band)
    p_w = _masked_softmax(s_w, (dist >= 0) & (dist <= WINDOW))
    o_w = _dot(p_w.reshape(R, band).astype(BF16), vwin)

    g = g_ref[...]
    for h in range(A_G):
        r = slice(h * TQ, (h + 1) * TQ)
        o = g[:, 3 * h:3 * h + 1] * o_c[r] + g[:, 3 * h + 1:3 * h + 2] * o_s[r] + g[:, 3 * h + 2:3 * h + 3] * o_w[r]
        o_ref[:, h * 128:(h + 1) * 128] = o.astype(o_ref.dtype)


def _nsa_prompt(qa, pa, pb, cc, sc, ksel, vsel, kw, vw, gat):
    B, _, T, _ = qa.shape
    nc = T // CMP_STRIDE
    part_k = pl.BlockSpec((None, None, nc, 128), lambda b, kv, i: (b, kv, 0, 0))
    part_v = pl.BlockSpec((None, None, nc, 128), lambda b, kv, i: (b, 2 + kv, 0, 0))
    tabc = pl.BlockSpec((nc, 128), lambda b, kv, i: (0, 0))
    full = pl.BlockSpec((None, None, T, 128), lambda b, kv, i: (b, kv, 0, 0))
    return pl.pallas_call(
        functools.partial(_nsa_prompt_kernel, T=T),
        grid=(B, A_KV, T // TQ),
        in_specs=[
            pl.BlockSpec((None, A_G, TQ, 128), lambda b, kv, i: (b, kv, i, 0)),
            part_k, part_k, part_v, part_v, tabc, tabc, full, full, full, full,
            pl.BlockSpec((None, None, TQ, 128), lambda b, kv, i: (b, kv, i, 0)),
        ],
        out_specs=pl.BlockSpec((None, TQ, A_G * 128), lambda b, kv, i: (b, i, kv)),
        out_shape=jax.ShapeDtypeStruct((B, T, A_HEADS * 128), BF16),
        scratch_shapes=[pltpu.VMEM((nc, 128), BF16), pltpu.VMEM((nc, 128), BF16)],
        compiler_params=_cparams(("parallel", "parallel", "arbitrary")),
        name="nsa_prompt",
    )(qa, pa, pb, pa, pb, cc, sc, ksel, vsel, kw, vw, gat)


def _kth_threshold(count_ge, shape, k):
    def step(it, t):
        cand = t + jnp.left_shift(jnp.int32(1), 31 - it)
        return jnp.where(count_ge(cand) >= k, cand, t)

    return lax.fori_loop(0, 32, step, jnp.full(shape, INT_MIN, I32))


def _dsa_prompt_kernel(q_ref, iq_ref, iw_ref, ik_ref, k_ref, v_ref, o_ref, key_ref, *, T):
    qi = pl.program_id(1)
    t0 = qi * TQ
    nkt = (t0 + TQ + TK - 1) // TK
    k_top = min(IDX_TOPK, T // 4)
    qp = t0 + lax.broadcasted_iota(I32, (TQ, 1), 0)
    tcol = lax.broadcasted_iota(I32, (TQ, TK), 1)
    iq = iq_ref[...].reshape(IDX_HEADS * TQ, 128)
    iw = iw_ref[...]

    def score_step(kt, _):
        base = pl.multiple_of(kt * TK, TK)
        s = _dot_nt(iq, ik_ref[pl.ds(base, TK), :])
        sc = jnp.zeros((TQ, TK), F32)
        for h in range(IDX_HEADS):
            sc = sc + iw[:, h:h + 1] * jnp.maximum(s[h * TQ:(h + 1) * TQ], 0.0)
        sc = jnp.where(tcol + base <= qp, sc, NEG)
        key_ref[:, pl.ds(base, TK)] = _sortable(sc)
        return 0

    lax.fori_loop(0, nkt, score_step, 0)

    def count_ge(cand):
        def cstep(kt, c):
            base = pl.multiple_of(kt * TK, TK)
            hit = key_ref[:, pl.ds(base, TK)] >= cand
            return c + jnp.sum(jnp.where(hit, 1.0, 0.0), axis=-1, keepdims=True)

        return lax.fori_loop(0, nkt, cstep, jnp.zeros((TQ, 1), F32))

    thr3 = _kth_threshold(count_ge, (TQ, 1), float(k_top)).reshape(1, TQ, 1)
    qp3 = t0 + lax.broadcasted_iota(I32, (1, TQ, 1), 1)
    tcol3 = lax.broadcasted_iota(I32, (1, TQ, TK), 2)

    q = q_ref[...].reshape(B_HEADS * TQ, 128)

    def att_step(kt, carry):
        m, l, acc = carry
        base = pl.multiple_of(kt * TK, TK)
        s = _dot_nt(q, k_ref[pl.ds(base, TK), :]).reshape(B_HEADS, TQ, TK)
        keys = key_ref[:, pl.ds(base, TK)].reshape(1, TQ, TK)
        mask = (keys >= thr3) & (tcol3 + base <= qp3)
        return _online_step(s, mask, v_ref[pl.ds(base, TK), :], m, l, acc)

    R = B_HEADS * TQ
    init = (jnp.full((B_HEADS, TQ, 1), NEG, F32), jnp.zeros((B_HEADS, TQ, 1), F32), jnp.zeros((R, 128), F32))
    _, l, acc = lax.fori_loop(0, nkt, att_step, init)
    o = acc / jnp.maximum(l.reshape(R, 1), 1e-30)
    for h in range(B_HEADS):
        o_ref[:, h * 128:(h + 1) * 128] = o[h * TQ:(h + 1) * TQ].astype(o_ref.dtype)


def _dsa_prompt(qb, iq, iw, ikd, kb, vb):
    B, _, T, _ = qb.shape
    full = pl.BlockSpec((None, T, 128), lambda b, i: (b, 0, 0))
    return pl.pallas_call(
        functools.partial(_dsa_prompt_kernel, T=T),
        grid=(B, T // TQ),
        in_specs=[
            pl.BlockSpec((None, B_HEADS, TQ, 128), lambda b, i: (b, 0, i, 0)),
            pl.BlockSpec((None, IDX_HEADS, TQ, 128), lambda b, i: (b, 0, i, 0)),
            pl.BlockSpec((None, TQ, 128), lambda b, i: (b, i, 0)),
            full, full, full,
        ],
        out_specs=pl.BlockSpec((None, TQ, B_HEADS * 128), lambda b, i: (b, i, 0)),
        out_shape=jax.ShapeDtypeStruct((B, T, B_HEADS * 128), BF16),
        scratch_shapes=[pltpu.VMEM((TQ, T), I32)],
        compiler_params=_cparams(("parallel", "arbitrary")),
        name="dsa_prompt",
    )(qb, iq, iw, ikd, kb, vb)


def _diff_finish(acc, l, lam, subln, lam_init, rows):
    o = acc / jnp.maximum(l, 1e-30)
    outs = []
    for g in range(2):
        a0 = o[(2 * g) * rows:(2 * g + 1) * rows]
        a1 = o[(2 * g + 1) * rows:(2 * g + 2) * rows]
        d = a0 - lam * a1
        d = d * lax.rsqrt(jnp.mean(d * d, axis=-1, keepdims=True) + EPS) * subln
        outs.append(d * (1.0 - lam_init))
    return outs


def _diff_prompt_kernel(q_ref, k_ref, v_ref, lp_ref, sub_ref, o_ref, *, lam_init):
    qi = pl.program_id(2)
    t0 = qi * TQ
    nkt = (t0 + TQ + TK - 1) // TK
    R = 4 * TQ
    qp = t0 + lax.broadcasted_iota(I32, (1, TQ, 1), 1)
    tcol = lax.broadcasted_iota(I32, (1, TQ, TK), 2)
    q = q_ref[...].reshape(R, 128)

    def step(kt, carry):
        m, l, acc = carry
        base = pl.multiple_of(kt * TK, TK)
        s = _dot_nt(q, k_ref[pl.ds(base, TK), :]).reshape(4, TQ, TK)
        mask = tcol + base <= qp
        return _online_step(s, mask, v_ref[pl.ds(base, TK), :], m, l, acc)

    init = (jnp.full((4, TQ, 1), NEG, F32), jnp.zeros((4, TQ, 1), F32), jnp.zeros((R, 128), F32))
    _, l, acc = lax.fori_loop(0, nkt, step, init)
    lam = _lambda_of(lp_ref[...], lam_init)
    outs = _diff_finish(acc, l.reshape(R, 1), lam, sub_ref[...], lam_init, TQ)
    for g in range(2):
        o_ref[:, g * 128:(g + 1) * 128] = outs[g].astype(o_ref.dtype)


def _diff_prompt(qc, kcd, vcd, lp, subln, lam_init):
    B, _, _, T, _ = qc.shape
    full = pl.BlockSpec((None, None, T, 128), lambda b, kv, i: (b, kv, 0, 0))
    return pl.pallas_call(
        functools.partial(_diff_prompt_kernel, lam_init=lam_init),
        grid=(B, C_KV, T // TQ),
        in_specs=[
            pl.BlockSpec((None, None, 4, TQ, 128), lambda b, kv, i: (b, kv, 0, i, 0)),
            full, full,
            pl.BlockSpec((4, C_HALF), lambda b, kv, i: (0, 0)),
            pl.BlockSpec((1, 128), lambda b, kv, i: (0, 0)),
        ],
        out_specs=pl.BlockSpec((None, TQ, 256), lambda b, kv, i: (b, i, kv)),
        out_shape=jax.ShapeDtypeStruct((B, T, C_HEADS * 128), BF16),
        compiler_params=_cparams(("parallel", "parallel", "arbitrary")),
        name="diff_prompt",
    )(qc, kcd, vcd, lp, subln)


def _page_specs(block, layer, n_lead_zero):
    specs = []
    for u in range(PAGES_PER_STEP):
        def imap(b, pc, pt, u=u):
            return (layer, pt[b, pc * PAGES_PER_STEP + u]) + (0,) * n_lead_zero
        specs.append(pl.BlockSpec(block, imap))
    return specs


def _nsa_cmp_decode_kernel(pt_ref, *refs, past):
    pages = refs[:PAGES_PER_STEP]
    q_ref, new_ref, wc_ref, cc_ref, sc_ref, oc_ref, sel_ref, a_ref, b_ref = refs[PAGES_PER_STEP:]
    pc = pl.program_id(1)
    nc = past // CMP_STRIDE
    ns = past // SEL_BLOCK + 1
    nsp = ((ns + 127) // 128) * 128
    cpp = PAGE_SIZE // CMP_STRIDE

    for u in range(PAGES_PER_STEP):
        x = pages[u][...].reshape(PAGE_SIZE, 2 * A_KV * 128)
        r0 = pl.multiple_of((pc * PAGES_PER_STEP + u) * cpp, cpp)
        a_ref[pl.ds(r0, cpp), 0:256] = _dot(wc_ref[0], x[:, 0:256], HI)
        b_ref[pl.ds(r0, cpp), 0:256] = _dot(wc_ref[1], x[:, 0:256], HI)
        a_ref[pl.ds(r0, cpp), 256:512] = _dot(wc_ref[2], x[:, 256:512], HI)
        b_ref[pl.ds(r0, cpp), 256:512] = _dot(wc_ref[3], x[:, 256:512], HI)

    @pl.when(pc == pl.num_programs(1) - 1)
    def _():
        rowi = lax.broadcasted_iota(I32, (nc, 128), 0)
        cend = lax.broadcasted_iota(I32, (A_G, nc), 1) * CMP_STRIDE + (CMP_LEN - 1)
        ci = lax.broadcasted_iota(I32, (nc, nsp), 0) * CMP_STRIDE
        sj = lax.broadcasted_iota(I32, (nc, nsp), 1) * SEL_BLOCK
        overlap = ((ci < sj + SEL_BLOCK) & (ci + CMP_LEN > sj)).astype(F32)
        jrow = lax.broadcasted_iota(I32, (1, nsp), 1)
        jq = past // SEL_BLOCK
        ii = lax.broadcasted_iota(I32, (nsp, nsp), 0)
        jj = lax.broadcasted_iota(I32, (nsp, nsp), 1)
        rr = lax.broadcasted_iota(I32, (SEL_TOPN, nsp), 0)
        jr = lax.broadcasted_iota(I32, (SEL_TOPN, nsp), 1).astype(F32)
        new = new_ref[...]
        q = q_ref[...].astype(BF16)
        for kv in range(A_KV):
            ko, vo = kv * 128, 256 + kv * 128
            bk = jnp.where(rowi == nc - 1, wc_ref[1][0:1, 0:1] * new[:, ko:ko + 128],
                           pltpu.roll(b_ref[:, ko:ko + 128], nc - 1, 0))
            bv = jnp.where(rowi == nc - 1, wc_ref[3][0:1, 0:1] * new[:, vo:vo + 128],
                           pltpu.roll(b_ref[:, vo:vo + 128], nc - 1, 0))
            kraw = a_ref[:, ko:ko + 128] + bk
            kc = (kraw * cc_ref[...] + pltpu.roll(kraw, 64, 1) * sc_ref[...]).astype(BF16)
            vc = (a_ref[:, vo:vo + 128] + bv).astype(BF16)
            s_c = _dot_nt(q[kv * A_G:(kv + 1) * A_G], kc)
            p_c = _masked_softmax(s_c, cend <= past)
            oc_ref[kv * A_G:(kv + 1) * A_G, :] = _dot(p_c.astype(BF16), vc)
            imp = _dot(jnp.sum(p_c, axis=0, keepdims=True), overlap, HI)
            forced = (jrow == 0) | (jrow == jq) | (jrow == jq - 1)
            imp = jnp.where(forced, imp + FORCE_BONUS, imp)
            imp = jnp.where(jrow > jq, NEG, imp)
            imp_col = jnp.sum(jnp.where(ii == jj, jnp.broadcast_to(imp, (nsp, nsp)), 0.0), axis=1, keepdims=True)
            beats = (imp_col > imp) | ((imp_col == imp) & (ii < jj))
            rank = jnp.sum(jnp.where(beats, 1.0, 0.0), axis=0, keepdims=True)
            pick = jnp.sum(jnp.where(rank == rr.astype(F32), jr, 0.0), axis=1, keepdims=True)
            sel_ref[kv] = pick.astype(I32)


def _nsa_cmp_decode(page_table, cache, layer, q, new_rows, wc, cc, sc, past):
    DB = q.shape[0]
    npc = page_table.shape[1] // PAGES_PER_STEP
    nc = past // CMP_STRIDE
    block = (None, None, PAGE_SIZE, 2, A_KV, 128)
    const2 = lambda b, pc, pt: (0, 0)
    return pl.pallas_call(
        functools.partial(_nsa_cmp_decode_kernel, past=past),
        grid_spec=pltpu.PrefetchScalarGridSpec(
            num_scalar_prefetch=1,
            grid=(DB, npc),
            in_specs=_page_specs(block, layer, 4) + [
                pl.BlockSpec((None, A_HEADS, 128), lambda b, pc, pt: (b, 0, 0)),
                pl.BlockSpec((None, 1, 1024), lambda b, pc, pt: (b, 0, 0)),
                pl.BlockSpec((4, PAGE_SIZE // CMP_STRIDE, PAGE_SIZE), lambda b, pc, pt: (0, 0, 0)),
                pl.BlockSpec((nc, 128), const2),
                pl.BlockSpec((nc, 128), const2),
            ],
            out_specs=[
                pl.BlockSpec((None, A_HEADS, 128), lambda b, pc, pt: (b, 0, 0)),
                pl.BlockSpec((None, A_KV, SEL_TOPN, 1), lambda b, pc, pt: (b, 0, 0, 0)),
            ],
            scratch_shapes=[pltpu.VMEM((nc, 512), F32), pltpu.VMEM((nc, 512), F32)],
        ),
        out_shape=[jax.ShapeDtypeStruct((DB, A_HEADS, 128), F32),
                   jax.ShapeDtypeStruct((DB, A_KV, SEL_TOPN, 1), I32)],
        compiler_params=_cparams(("parallel", "arbitrary")),
        name="nsa_cmp_decode",
    )(page_table, *([cache] * PAGES_PER_STEP), q, new_rows, wc, cc, sc)


def _nsa_sel_decode_kernel(pt_ref, sel_ref, blk_ref, win_ref, q_ref, new_ref, neww_ref, oc_ref, g_ref,
                           o_ref, m_ref, l_ref, acc_ref, *, past):
    b, kv, s = pl.program_id(0), pl.program_id(1), pl.program_id(2)
    ns = past // SEL_BLOCK + 1
    qf = q_ref[...]
    q = jnp.where(kv == 0, qf[0:A_G], qf[A_G:2 * A_G]).astype(BF16)
    new = new_ref[...]

    @pl.when(s == 0)
    def _():
        k_new = jnp.where(kv == 0, new[:, 512:640], new[:, 640:768]).astype(BF16)
        v_new = jnp.where(kv == 0, new[:, 768:896], new[:, 896:1024]).astype(BF16)
        m_ref[...] = _rowdot(q, k_new)
        l_ref[...] = jnp.ones((A_G, 1), F32)
        acc_ref[...] = jnp.broadcast_to(v_new.astype(F32), (A_G, 128))

    @pl.when(sel_ref[b, kv, s] != ns - 1)
    def _():
        k = jnp.where(kv == 0, blk_ref[:, 0, 0, :], blk_ref[:, 0, 1, :]).astype(BF16)
        v = jnp.where(kv == 0, blk_ref[:, 1, 0, :], blk_ref[:, 1, 1, :]).astype(BF16)
        sc = _dot_nt(q, k)
        m, l, acc = _online_step(sc, None, v, m_ref[...], l_ref[...], acc_ref[...])
        m_ref[...] = m
        l_ref[...] = l
        acc_ref[...] = acc

    @pl.when(s == pl.num_programs(2) - 1)
    def _():
        o_s = acc_ref[...] / jnp.maximum(l_ref[...], 1e-30)
        neww = neww_ref[...]
        kw = jnp.where(kv == 0, win_ref[:, 0, 0, :], win_ref[:, 0, 1, :]).astype(BF16)
        vw = jnp.where(kv == 0, win_ref[:, 1, 0, :], win_ref[:, 1, 1, :]).astype(BF16)
        kw_new = jnp.where(kv == 0, neww[:, 0:128], neww[:, 128:256]).astype(BF16)
        vw_new = jnp.where(kv == 0, neww[:, 256:384], neww[:, 384:512]).astype(BF16)
        s_w = _dot_nt(q, kw)
        s_n = _rowdot(q, kw_new)
        mw = jnp.maximum(jnp.max(s_w, axis=-1, keepdims=True), s_n)
        e_w = jnp.exp(s_w - mw)
        e_n = jnp.exp(s_n - mw)
        den = jnp.sum(e_w, axis=-1, keepdims=True) + e_n
        o_w = (_dot(e_w.astype(BF16), vw) + e_n.astype(BF16).astype(F32) * vw_new.astype(F32)) / den
        ocf = oc_ref[...]
        o_c = jnp.where(kv == 0, ocf[0:A_G], ocf[A_G:2 * A_G])
        gf = g_ref[...]
        g = jnp.where(kv == 0, gf[0:A_G], gf[A_G:2 * A_G])
        o_ref[...] = g[:, 0:1] * o_c + g[:, 1:2] * o_s + g[:, 2:3] * o_w


def _nsa_sel_decode(page_table, sel, cache, win, layer, q, new_rows, new_win, o_c, gates, past):
    DB = q.shape[0]
    n_pages = page_table.shape[1]
    lw = win.shape[2]

    def blk_map(b, kv, s, pt, sel):
        j = sel[b, kv, s]
        return (layer, pt[b, jnp.minimum(j // 2, n_pages - 1)], j % 2, 1, 0, 0)

    per_b = lambda b, kv, s, pt, sel: (b, 0, 0)
    return pl.pallas_call(
        functools.partial(_nsa_sel_decode_kernel, past=past),
        grid_spec=pltpu.PrefetchScalarGridSpec(
            num_scalar_prefetch=2,
            grid=(DB, A_KV, SEL_TOPN),
            in_specs=[
                pl.BlockSpec((None, None, SEL_BLOCK, 2, A_KV, 128), blk_map),
                pl.BlockSpec((None, None, lw, 2, A_KV, 128), lambda b, kv, s, pt, sel: (layer, b, 0, 0, 0, 0)),
                pl.BlockSpec((None, A_HEADS, 128), per_b),
                pl.BlockSpec((None, 1, 1024), per_b),
                pl.BlockSpec((None, 1, 512), per_b),
                pl.BlockSpec((None, A_HEADS, 128), per_b),
                pl.BlockSpec((None, A_HEADS, 128), per_b),
            ],
            out_specs=pl.BlockSpec((None, None, A_G, 128), lambda b, kv, s, pt, sel: (b, kv, 0, 0)),
            scratch_shapes=[pltpu.VMEM((A_G, 1), F32), pltpu.VMEM((A_G, 1), F32), pltpu.VMEM((A_G, 128), F32)],
        ),
        out_shape=jax.ShapeDtypeStruct((DB, A_KV, A_G, 128), F32),
        compiler_params=_cparams(("parallel", "parallel", "arbitrary")),
        name="nsa_sel_decode",
    )(page_table, sel, cache, win, q, new_rows, new_win, o_c, gates)


def _dsa_idx_decode_kernel(pt_ref, *refs, past):
    pages = refs[:PAGES_PER_STEP]
    iq_ref, iw_ref, ikn_ref, mask_ref, sc_ref = refs[PAGES_PER_STEP:]
    pc = pl.program_id(1)
    n_pages = past // PAGE_SIZE
    k_top = min(IDX_TOPK, (past + 1) // 4)
    iq = iq_ref[...].astype(BF16)
    iw = iw_ref[...]

    @pl.when(pc == 0)
    def _():
        sc_ref[...] = jnp.full(sc_ref.shape, NEG, F32)

    for u in range(PAGES_PER_STEP):
        s = _dot_nt(iq, pages[u][...].astype(BF16))
        sc_ref[pl.ds(pc * PAGES_PER_STEP + u, 1), :] = jnp.sum(iw * jnp.maximum(s, 0.0), axis=0, keepdims=True)

    @pl.when(pc == pl.num_programs(1) - 1)
    def _():
        s_new = _rowdot(iq, ikn_ref[...])
        s_new = jnp.sum(iw * jnp.maximum(s_new, 0.0), axis=0, keepdims=True)
        lane0 = lax.broadcasted_iota(I32, (1, PAGE_SIZE), 1) == 0
        sc_ref[n_pages:n_pages + 1, :] = jnp.where(lane0, s_new, NEG)
        keys = _sortable(sc_ref[...])

        def count_ge(cand):
            c = jnp.sum(jnp.where(keys >= cand, 1.0, 0.0), axis=-1, keepdims=True)
            return jnp.sum(c, axis=0, keepdims=True)

        thr = _kth_threshold(count_ge, (1, 1), float(k_top))
        mask_ref[...] = jnp.where((keys >= thr) & (sc_ref[...] > 0.5 * NEG), 1.0, 0.0)


def _dsa_idx_decode(page_table, cache, layer, iq, iw, ik_new, past):
    DB = iq.shape[0]
    n_pages = page_table.shape[1]
    npc = n_pages // PAGES_PER_STEP
    rows = ((n_pages + 1 + 7) // 8) * 8
    per_b = lambda b, pc, pt: (b, 0, 0)
    return pl.pallas_call(
        functools.partial(_dsa_idx_decode_kernel, past=past),
        grid_spec=pltpu.PrefetchScalarGridSpec(
            num_scalar_prefetch=1,
            grid=(DB, npc),
            in_specs=_page_specs((None, None, PAGE_SIZE, IDX_DIM), layer, 2) + [
                pl.BlockSpec((None, IDX_HEADS, IDX_DIM), per_b),
                pl.BlockSpec((None, IDX_HEADS, 1), per_b),
                pl.BlockSpec((None, 1, IDX_DIM), per_b),
            ],
            out_specs=pl.BlockSpec((None, rows, PAGE_SIZE), per_b),
            scratch_shapes=[pltpu.VMEM((rows, PAGE_SIZE), F32)],
        ),
        out_shape=jax.ShapeDtypeStruct((DB, rows, PAGE_SIZE), F32),
        compiler_params=_cparams(("parallel", "arbitrary")),
        name="dsa_idx_decode",
    )(page_table, *([cache] * PAGES_PER_STEP), iq, iw, ik_new)


def _dsa_att_decode_kernel(pt_ref, *refs, past):
    pages = refs[:PAGES_PER_STEP]
    q_ref, new_ref, mask_ref, o_ref, m_ref, l_ref, acc_ref = refs[PAGES_PER_STEP:]
    pc = pl.program_id(1)
    n_pages = past // PAGE_SIZE
    q = q_ref[...].astype(BF16)

    @pl.when(pc == 0)
    def _():
        m_ref[...] = jnp.full(m_ref.shape, NEG, F32)
        l_ref[...] = jnp.zeros(l_ref.shape, F32)
        acc_ref[...] = jnp.zeros(acc_ref.shape, F32)

    k = jnp.concatenate([pages[u][:, 0, :] for u in range(PAGES_PER_STEP)], axis=0).astype(BF16)
    v = jnp.concatenate([pages[u][:, 1, :] for u in range(PAGES_PER_STEP)], axis=0).astype(BF16)
    r0 = pl.multiple_of(pc * PAGES_PER_STEP, PAGES_PER_STEP)
    mrows = mask_ref[pl.ds(r0, PAGES_PER_STEP), :]
    mask = jnp.concatenate([mrows[u:u + 1, :] for u in range(PAGES_PER_STEP)], axis=1) > 0.5
    m, l, acc = _online_step(_dot_nt(q, k), mask, v, m_ref[...], l_ref[...], acc_ref[...])
    m_ref[...] = m
    l_ref[...] = l
    acc_ref[...] = acc

    @pl.when(pc == pl.num_programs(1) - 1)
    def _():
        new = new_ref[...]
        v_new = new[:, 128:256].astype(BF16).astype(F32)
        keep = mask_ref[n_pages:n_pages + 1, 0:1] > 0.5
        _, l2, acc2 = _online_single(_rowdot(q, new[:, 0:128]), keep, v_new, m_ref[...], l_ref[...], acc_ref[...])
        o_ref[...] = acc2 / jnp.maximum(l2, 1e-30)


def _dsa_att_decode(page_table, cache, layer, q, new_rows, mask, past):
    DB = q.shape[0]
    npc = page_table.shape[1] // PAGES_PER_STEP
    rows = mask.shape[1]
    per_b = lambda b, pc, pt: (b, 0, 0)
    return pl.pallas_call(
        functools.partial(_dsa_att_decode_kernel, past=past),
        grid_spec=pltpu.PrefetchScalarGridSpec(
            num_scalar_prefetch=1,
            grid=(DB, npc),
            in_specs=_page_specs((None, None, PAGE_SIZE, 2, 128), layer, 3) + [
                pl.BlockSpec((None, B_HEADS, 128), per_b),
                pl.BlockSpec((None, 1, 256), per_b),
                pl.BlockSpec((None, rows, PAGE_SIZE), per_b),
            ],
            out_specs=pl.BlockSpec((None, B_HEADS, 128), per_b),
            scratch_shapes=[pltpu.VMEM((B_HEADS, 1), F32), pltpu.VMEM((B_HEADS, 1), F32),
                            pltpu.VMEM((B_HEADS, 128), F32)],
        ),
        out_shape=jax.ShapeDtypeStruct((DB, B_HEADS, 128), F32),
        compiler_params=_cparams(("parallel", "arbitrary")),
        name="dsa_att_decode",
    )(page_table, *([cache] * PAGES_PER_STEP), q, new_rows, mask)


def _diff_decode_kernel(pt_ref, *refs, lam_init):
    pages = refs[:PAGES_PER_STEP]
    q_ref, new_ref, lp_ref, sub_ref, o_ref, m_ref, l_ref, acc_ref = refs[PAGES_PER_STEP:]
    pc = pl.program_id(1)

    @pl.when(pc == 0)
    def _():
        m_ref[...] = jnp.full(m_ref.shape, NEG, F32)
        l_ref[...] = jnp.zeros(l_ref.shape, F32)
        acc_ref[...] = jnp.zeros(acc_ref.shape, F32)

    for kv in range(C_KV):
        q = q_ref[kv].astype(BF16)
        k = jnp.concatenate([pages[u][:, 0, kv, :] for u in range(PAGES_PER_STEP)], axis=0).astype(BF16)
        v = jnp.concatenate([pages[u][:, 1, kv, :] for u in range(PAGES_PER_STEP)], axis=0).astype(BF16)
        sc = _dot_nt(q, k)
        m, l, acc = _online_step(sc, None, v, m_ref[kv], l_ref[kv], acc_ref[kv])
        m_ref[kv] = m
        l_ref[kv] = l
        acc_ref[kv] = acc

    @pl.when(pc == pl.num_programs(1) - 1)
    def _():
        new = new_ref[...]
        lam = _lambda_of(lp_ref[...], lam_init)
        for kv in range(C_KV):
            k_new = new[:, kv * 128:(kv + 1) * 128]
            v_new = new[:, 256 + kv * 128:384 + kv * 128].astype(BF16).astype(F32)
            _, l, acc = _online_single(_rowdot(q_ref[kv], k_new), None, v_new, m_ref[kv], l_ref[kv], acc_ref[kv])
            outs = _diff_finish(acc, l, lam, sub_ref[...], lam_init, 1)
            o_ref[2 * kv:2 * kv + 1, :] = outs[0]
            o_ref[2 * kv + 1:2 * kv + 2, :] = outs[1]


def _diff_decode(page_table, cache, layer, q, new_rows, lp, subln, lam_init):
    DB = q.shape[0]
    npc = page_table.shape[1] // PAGES_PER_STEP
    per_b = lambda b, pc, pt: (b, 0, 0)
    return pl.pallas_call(
        functools.partial(_diff_decode_kernel, lam_init=lam_init),
        grid_spec=pltpu.PrefetchScalarGridSpec(
            num_scalar_prefetch=1,
            grid=(DB, npc),
            in_specs=_page_specs((None, None, PAGE_SIZE, 2, C_KV, 128), layer, 4) + [
                pl.BlockSpec((None, C_KV, 4, 128), lambda b, pc, pt: (b, 0, 0, 0)),
                pl.BlockSpec((None, 1, 512), per_b),
                pl.BlockSpec((4, C_HALF), lambda b, pc, pt: (0, 0)),
                pl.BlockSpec((1, 128), lambda b, pc, pt: (0, 0)),
            ],
            out_specs=pl.BlockSpec((None, C_HEADS, 128), per_b),
            scratch_shapes=[pltpu.VMEM((C_KV, 4, 1), F32), pltpu.VMEM((C_KV, 4, 1), F32),
                            pltpu.VMEM((C_KV, 4, 128), F32)],
        ),
        out_shape=jax.ShapeDtypeStruct((DB, C_HEADS, 128), F32),
        compiler_params=_cparams(("parallel", "arbitrary")),
        name="diff_decode",
    )(page_table, *([cache] * PAGES_PER_STEP), q, new_rows, lp, subln)


def _rope_tables(pos, d):
    half = d // 2
    inv = ROPE_THETA ** (-jnp.arange(half, dtype=F32) / half)
    ang = pos.astype(F32)[:, None] * inv[None, :]
    cos, sin = jnp.cos(ang), jnp.sin(ang)
    reps = 128 // d
    return jnp.tile(jnp.concatenate([cos, cos], axis=-1), (1, reps)), jnp.tile(jnp.concatenate([-sin, sin], axis=-1), (1, reps))


def _permute_w_in(w_in):
    sizes = (1024, 1536, 24, 512, 256, 1024, 16, 64, 512, 512)
    offs = np.concatenate([[0], np.cumsum(sizes)])
    a_q, a_kv, a_gate, b_q, b_kv, b_iq, b_iw, b_ik, c_q, c_kv = [
        w_in[:, :, int(offs[i]):int(offs[i + 1])] for i in range(10)]
    pad = jnp.zeros(w_in.shape[:2] + (N_PROJ - 5480,), w_in.dtype)
    return jnp.concatenate([a_q, a_kv, b_q, b_kv, b_iq, c_q, c_kv, b_ik, a_gate, b_iw, pad], axis=-1).astype(BF16)


def _pack_params(nsa_qk_norm, dsa_qk_norm, dsa_idx_knorm, diff_qk_norm):
    rows = [nsa_qk_norm, dsa_qk_norm, jnp.tile(dsa_idx_knorm, 2)[None], jnp.tile(diff_qk_norm, (1, 2))]
    p = jnp.concatenate(rows, axis=0).astype(F32)
    return jnp.pad(p, ((0, 16 - p.shape[0]), (0, 0)))


def _compress_weights(cmp_w, rows):
    eye = jnp.eye(rows // CMP_STRIDE, dtype=F32)
    mats = []
    for c in range(2):
        for half in range(2):
            mats.append(jnp.kron(eye, cmp_w[c, half * CMP_STRIDE:(half + 1) * CMP_STRIDE][None, :]))
    return jnp.stack(mats, axis=0)


def kernel(x_prompt, x_sample, cache_nsa_kv, state_nsa_win, cache_dsa_kv, cache_dsa_idx, cache_diff_kv, page_table, attn_norm, w_in, nsa_qk_norm, nsa_cmp_w, dsa_qk_norm, dsa_idx_knorm, diff_qk_norm, diff_lambda, diff_subln, w_out, ffn_norm, w_gate_up, w_down):
    B, T, D = x_prompt.shape
    DB = x_sample.shape[0]
    depth = w_in.shape[0]
    n_pages = page_table.shape[1]
    past = n_pages * PAGE_SIZE
    M = B * T
    assert x_sample.shape[1] == 1 and T % TK == 0 and T >= WINDOW + TQ and n_pages % PAGES_PER_STEP == 0
    tm = min(1024, M)
    tm_ffn = min(512, M)
    tm_post = 256

    w_in_p = _permute_w_in(w_in)
    w_out_b = w_out.astype(BF16)
    w_gu_b = w_gate_up.astype(BF16)
    w_down_b = w_down.astype(BF16)

    pos_p = jnp.arange(T, dtype=I32)
    tabs_p = _rope_tables(pos_p, 128) + _rope_tables(pos_p, 64)
    pos_s = jnp.full((DB,), past, I32)
    tabs_s = _rope_tables(pos_s, 128) + _rope_tables(pos_s, 64)
    cend_p = jnp.arange(T // CMP_STRIDE, dtype=I32) * CMP_STRIDE + (CMP_LEN - 1)
    cc_p, sc_p = _rope_tables(cend_p, 128)
    cend_s = jnp.arange(past // CMP_STRIDE, dtype=I32) * CMP_STRIDE + (CMP_LEN - 1)
    cc_s, sc_s = _rope_tables(cend_s, 128)

    yp = x_prompt.reshape(M, D)
    ys = x_sample.reshape(DB, D)
    rows_p, rows_s = [], []
    for l in range(depth):
        lam_init = 0.8 - 0.6 * math.exp(-0.3 * l)
        prm = _pack_params(nsa_qk_norm[l], dsa_qk_norm[l], dsa_idx_knorm[l], diff_qk_norm[l])
        g_attn = attn_norm[l][None, :]
        g_ffn = ffn_norm[l][None, :]
        lp = diff_lambda[l].astype(F32)
        subln = diff_subln[l][None, :].astype(F32)

        proj = _norm_matmul(yp, g_attn, w_in_p, l, tm).reshape(B, T, N_PROJ)
        wc = _compress_weights(nsa_cmp_w[l], tm_post)
        (nsa, win, dsa, ik, dif, qa, ksel, vsel, kw, vw, qb, kb, vb, iq, ikd, iw, qc, kcd, vcd, gat,
         pa, pb) = _post_project(proj, tabs_p, prm, wc, tm_post, BF16)
        o_a = _nsa_prompt(qa, pa, pb, cc_p, sc_p, ksel, vsel, kw, vw, gat)
        o_b = _dsa_prompt(qb, iq, iw, ikd, kb, vb)
        o_c = _diff_prompt(qc, kcd, vcd, lp, subln, lam_init)
        mix = jnp.concatenate([o_a, o_b, o_c], axis=-1).reshape(M, D)
        yp = _matmul_residual(mix, w_out_b, yp, l, tm)
        act = _norm_swiglu(yp, g_ffn, w_gu_b, l, tm_ffn)
        yp = _matmul_residual(act, w_down_b, yp, l, tm_ffn)
        rows_p.append((nsa.reshape(B, T, 4, A_KV, 128), win[:, T - min(WINDOW, T):].reshape(B, -1, 2, A_KV, 128),
                       dsa.reshape(B, T, 2, 128), ik, dif.reshape(B, T, 2, C_KV, 128)))

        proj_s = _norm_matmul(ys, g_attn, w_in_p, l, DB).reshape(1, DB, N_PROJ)
        (nsa_s, win_s, dsa_s, ik_s, dif_s, qa_s, _, _, _, _, qb_s, _, _, iq_s, _, iw_s, qc_s, _, _,
         gat_s) = _post_project(proj_s, tabs_s, prm, None, DB, F32)
        nsa_new = nsa_s.reshape(DB, 1, 1024)
        win_new = win_s.reshape(DB, 1, 512)
        dsa_new = dsa_s.reshape(DB, 1, 256)
        ik_new = ik_s.reshape(DB, 1, IDX_DIM)
        dif_new = dif_s.reshape(DB, 1, 512)
        qa_d = jnp.transpose(qa_s[0], (1, 0, 2))
        qb_d = jnp.transpose(qb_s[0], (1, 0, 2))
        iq_d = jnp.transpose(iq_s[0], (1, 0, 2))
        iq_d = iq_d[:, :, :64] + iq_d[:, :, 64:]
        iw_d = iw_s[0, :, :IDX_HEADS, None]
        qc_d = jnp.transpose(qc_s[0], (2, 0, 1, 3))
        g_d = jnp.transpose(gat_s[0, :, :, :12].reshape(A_KV, DB, A_G, 3), (1, 0, 2, 3)).reshape(DB, A_HEADS, 3)
        g_d = jnp.pad(g_d, ((0, 0), (0, 0), (0, 125)))

        wc_s = _compress_weights(nsa_cmp_w[l], PAGE_SIZE)
        oc_d, sel = _nsa_cmp_decode(page_table, cache_nsa_kv, l, qa_d, nsa_new, wc_s, cc_s, sc_s, past)
        oa_d = _nsa_sel_decode(page_table, sel.reshape(DB, A_KV, SEL_TOPN), cache_nsa_kv, state_nsa_win, l,
                               qa_d, nsa_new, win_new, oc_d, g_d, past)
        mask = _dsa_idx_decode(page_table, cache_dsa_idx, l, iq_d, iw_d, ik_new, past)
        ob_d = _dsa_att_decode(page_table, cache_dsa_kv, l, qb_d, dsa_new, mask, past)
        od_d = _diff_decode(page_table, cache_diff_kv, l, qc_d, dif_new, lp, subln, lam_init)
        mix_s = jnp.concatenate([oa_d.reshape(DB, 1024), ob_d.reshape(DB, 512), od_d.reshape(DB, 512)],
                                axis=-1).astype(BF16)
        ys = _matmul_residual(mix_s, w_out_b, ys, l, DB)
        act_s = _norm_swiglu(ys, g_ffn, w_gu_b, l, DB)
        ys = _matmul_residual(act_s, w_down_b, ys, l, DB)
        lw = state_nsa_win.shape[2]
        win_all = jnp.concatenate([state_nsa_win[l], win_new.reshape(DB, 1, 2, A_KV, 128)], axis=1)
        rows_s.append((nsa_new.reshape(DB, 1, 4, A_KV, 128), win_all[:, win_all.shape[1] - min(WINDOW, lw + 1):],
                       dsa_new.reshape(DB, 1, 2, 128), ik_new, dif_new.reshape(DB, 1, 2, C_KV, 128)))

    def stacked(rows, i):
        return jnp.stack([r[i] for r in rows], axis=0)

    return (yp.reshape(B, T, D), ys.reshape(DB, 1, D),
            stacked(rows_p, 0), stacked(rows_s, 0), stacked(rows_p, 1), stacked(rows_s, 1),
            stacked(rows_p, 2), stacked(rows_s, 2), stacked(rows_p, 3), stacked(rows_s, 3),
            stacked(rows_p, 4), stacked(rows_s, 4))
```

```python
import functools
import math

import numpy as np
import jax
import jax.numpy as jnp
from jax import lax
from jax.experimental import pallas as pl
from jax.experimental.pallas import tpu as pltpu

F32 = jnp.float32
BF16 = jnp.bfloat16
I32 = jnp.int32
HI = lax.Precision.HIGHEST

D_MODEL = 2048
PAGE_SIZE = 128
D_HEAD = 128
A_HEADS = 8
A_KV = 2
A_G = A_HEADS // A_KV
B_HEADS = 4
C_HEADS = 4
C_KV = 2
C_HALF = 64
IDX_HEADS = 16
IDX_DIM = 64
IDX_TOPK = 256
CMP_LEN = 32
CMP_STRIDE = 16
SEL_BLOCK = 64
SEL_TOPN = 16
WINDOW = 512
FORCE_BONUS = 1.0e4
D_FF = 5632
ROPE_THETA = 10000.0
EPS = 1e-6
NEG = -1e30
INT_MIN = -2147483648

OFF_AQ = 0
OFF_AKV = 1024
OFF_BQ = 2560
OFF_BKV = 3072
OFF_BIQ = 3328
OFF_CQ = 4352
OFF_CKV = 4864
OFF_MISC = 5376
N_PROJ = 5632
MISC_GATE = 64
MISC_IW = 88

TQ = 128
TK = 512
PAGES_PER_STEP = 8
SEL_SHIFT = 6
LOG2E = math.log2(math.e)
SCALE_D = D_HEAD ** -0.5 * LOG2E
SCALE_C = C_HALF ** -0.5 * LOG2E
MASKED = -2e30
VMEM_LIMIT = 56 * 1024 * 1024


def _cparams(sem):
    return pltpu.CompilerParams(dimension_semantics=sem, vmem_limit_bytes=VMEM_LIMIT)


def _dot(a, b, precision=None):
    return jnp.dot(a, b, preferred_element_type=F32, precision=precision)


def _dot3(a, b):
    a_hi = a.astype(BF16)
    b_hi = b.astype(BF16)
    a_lo = (a - a_hi.astype(F32)).astype(BF16)
    b_lo = (b - b_hi.astype(F32)).astype(BF16)
    return _dot(a_hi, b_hi) + (_dot(a_hi, b_lo) + _dot(a_lo, b_hi))


def _dot_nt(a, b, precision=None):
    return lax.dot_general(a, b, (((1,), (1,)), ((), ())), preferred_element_type=F32, precision=precision)


def _norm_mm_kernel(x_ref, g_ref, w_ref, o_ref, xn_ref):
    @pl.when(pl.program_id(1) == 0)
    def _():
        x = x_ref[...]
        ms = jnp.mean(x * x, axis=-1, keepdims=True)
        xn_ref[...] = (x * lax.rsqrt(ms + EPS) * g_ref[...]).astype(BF16)

    o_ref[...] = _dot(xn_ref[...], w_ref[...])


def _norm_matmul(x, g, w, layer, tm, tn=512):
    M, K = x.shape
    N = w.shape[2]
    return pl.pallas_call(
        _norm_mm_kernel,
        grid=(M // tm, N // tn),
        in_specs=[
            pl.BlockSpec((tm, K), lambda i, j: (i, 0)),
            pl.BlockSpec((1, K), lambda i, j: (0, 0)),
            pl.BlockSpec((None, K, tn), lambda i, j: (layer, 0, j)),
        ],
        out_specs=pl.BlockSpec((tm, tn), lambda i, j: (i, j)),
        out_shape=jax.ShapeDtypeStruct((M, N), F32),
        scratch_shapes=[pltpu.VMEM((tm, K), BF16)],
        compiler_params=_cparams(("parallel", "arbitrary")),
        name="norm_matmul",
    )(x, g, w)


def _norm_swiglu_kernel(x_ref, g_ref, wg_ref, wu_ref, o_ref, xn_ref):
    @pl.when(pl.program_id(1) == 0)
    def _():
        x = x_ref[...]
        ms = jnp.mean(x * x, axis=-1, keepdims=True)
        xn_ref[...] = (x * lax.rsqrt(ms + EPS) * g_ref[...]).astype(BF16)

    xn = xn_ref[...]
    gate = _dot(xn, wg_ref[...])
    up = _dot(xn, wu_ref[...])
    o_ref[...] = (gate * jax.nn.sigmoid(gate) * up).astype(o_ref.dtype)


def _norm_swiglu(x, g, w, layer, tm, tn=512):
    M, K = x.shape
    nj = D_FF // tn
    return pl.pallas_call(
        _norm_swiglu_kernel,
        grid=(M // tm, nj),
        in_specs=[
            pl.BlockSpec((tm, K), lambda i, j: (i, 0)),
            pl.BlockSpec((1, K), lambda i, j: (0, 0)),
            pl.BlockSpec((None, K, tn), lambda i, j: (layer, 0, j)),
            pl.BlockSpec((None, K, tn), lambda i, j: (layer, 0, j + nj)),
        ],
        out_specs=pl.BlockSpec((tm, tn), lambda i, j: (i, j)),
        out_shape=jax.ShapeDtypeStruct((M, D_FF), BF16),
        scratch_shapes=[pltpu.VMEM((tm, K), BF16)],
        compiler_params=_cparams(("parallel", "arbitrary")),
        name="norm_swiglu",
    )(x, g, w, w)


def _mm_res_kernel(a_ref, w_ref, r_ref, o_ref):
    o_ref[...] = r_ref[...] + _dot(a_ref[...], w_ref[...])


def _matmul_residual(a, w, res, layer, tm, tn=512):
    M, K = a.shape
    N = w.shape[2]
    return pl.pallas_call(
        _mm_res_kernel,
        grid=(M // tm, N // tn),
        in_specs=[
            pl.BlockSpec((tm, K), lambda i, j: (i, 0)),
            pl.BlockSpec((None, K, tn), lambda i, j: (layer, 0, j)),
            pl.BlockSpec((tm, tn), lambda i, j: (i, j)),
        ],
        out_specs=pl.BlockSpec((tm, tn), lambda i, j: (i, j)),
        out_shape=jax.ShapeDtypeStruct((M, N), F32),
        compiler_params=_cparams(("parallel", "arbitrary")),
        name="matmul_residual",
    )(a, w, res)


def _post_kernel(*refs, emit_cmp, tm):
    if emit_cmp:
        x_ref, c1_ref, s1_ref, c2_ref, s2_ref, prm_ref, wc_ref = refs[:7]
        outs = refs[7:]
    else:
        x_ref, c1_ref, s1_ref, c2_ref, s2_ref, prm_ref = refs[:6]
        wc_ref = None
        outs = refs[6:]
    (nsa_ref, win_ref, dsa_ref, ik_ref, dif_ref, qa_ref, ksel_ref, vsel_ref, kw_ref, vw_ref,
     qb_ref, kb_ref, vb_ref, iq_ref, ikd_ref, iw_ref, qc_ref, kcd_ref, vcd_ref, gat_ref) = outs[:20]

    c1, s1, c2, s2 = c1_ref[...], s1_ref[...], c2_ref[...], s2_ref[...]
    prm = prm_ref[...]
    lane = lax.broadcasted_iota(I32, (tm, 128), 1)
    lo = lane < 64
    inner = (lane & 63) < 32

    def col(a):
        return x_ref[:, a:a + 128]

    def gain(r):
        return prm[r:r + 1, :]

    def rms128(v, g):
        return v * lax.rsqrt(jnp.mean(v * v, axis=-1, keepdims=True) + EPS) * g

    def rope128(v):
        return v * c1 + pltpu.roll(v, 64, 1) * s1

    def half_ms(v):
        sq = v * v
        a = jnp.sum(jnp.where(lo, sq, 0.0), axis=-1, keepdims=True)
        b = jnp.sum(jnp.where(lo, 0.0, sq), axis=-1, keepdims=True)
        return a * (1.0 / 64), b * (1.0 / 64)

    def rms64(v, g):
        a, b = half_ms(v)
        return v * lax.rsqrt(jnp.where(lo, a, b) + EPS) * g

    def rope64(v):
        rot = jnp.where(inner, pltpu.roll(v, 96, 1), pltpu.roll(v, 32, 1))
        return v * c2 + rot * s2

    for h in range(A_HEADS):
        qa_ref[h] = (rope128(rms128(col(OFF_AQ + h * 128), gain(0))) * SCALE_D).astype(qa_ref.dtype)
    for kv in range(A_KV):
        o = kv * 128
        kc = rms128(col(OFF_AKV + o), gain(1))
        vc = col(OFF_AKV + 256 + o)
        ks = rope128(rms128(col(OFF_AKV + 512 + o), gain(2)))
        vs = col(OFF_AKV + 768 + o)
        kw = rope128(rms128(col(OFF_AKV + 1024 + o), gain(3)))
        vw = col(OFF_AKV + 1280 + o)
        for slab, val in ((kv, kc), (2 + kv, vc), (4 + kv, ks), (6 + kv, vs)):
            nsa_ref[pl.ds(slab, tm, stride=8), :] = val
        win_ref[pl.ds(kv, tm, stride=4), :] = kw
        win_ref[pl.ds(2 + kv, tm, stride=4), :] = vw
        ksel_ref[kv] = ks.astype(ksel_ref.dtype)
        vsel_ref[kv] = vs.astype(vsel_ref.dtype)
        kw_ref[kv] = kw.astype(kw_ref.dtype)
        vw_ref[kv] = vw.astype(vw_ref.dtype)
        if emit_cmp:
            pa_ref, pb_ref = outs[20], outs[21]
            nch = tm // CMP_STRIDE
            pk = _dot3(wc_ref[0], kc)
            pv = _dot3(wc_ref[1], vc)
            pa_ref[kv] = pk[0:nch]
            pb_ref[kv] = pk[nch:2 * nch]
            pa_ref[2 + kv] = pv[0:nch]
            pb_ref[2 + kv] = pv[nch:2 * nch]
    for h in range(B_HEADS):
        qb_ref[h] = (rope128(rms128(col(OFF_BQ + h * 128), gain(4))) * SCALE_D).astype(qb_ref.dtype)
    kb = rope128(rms128(col(OFF_BKV), gain(5)))
    vb = col(OFF_BKV + 128)
    dsa_ref[pl.ds(0, tm, stride=2), :] = kb
    dsa_ref[pl.ds(1, tm, stride=2), :] = vb
    kb_ref[...] = kb.astype(kb_ref.dtype)
    vb_ref[...] = vb.astype(vb_ref.dtype)
    for p in range(IDX_HEADS // 2):
        v = rope64(col(OFF_BIQ + p * 128))
        iq_ref[2 * p] = jnp.where(lo, v, 0.0).astype(iq_ref.dtype)
        iq_ref[2 * p + 1] = jnp.where(lo, 0.0, v).astype(iq_ref.dtype)
    for h in range(C_HEADS):
        v = rope64(rms64(col(OFF_CQ + h * 128), gain(7))) * SCALE_C
        kv, g = h // 2, h % 2
        qc_ref[kv, 2 * g] = jnp.where(lo, v, 0.0).astype(qc_ref.dtype)
        qc_ref[kv, 2 * g + 1] = jnp.where(lo, 0.0, v).astype(qc_ref.dtype)
    for kv in range(C_KV):
        o = kv * 128
        kk = rope64(rms64(col(OFF_CKV + o), gain(8)))
        vv = col(OFF_CKV + 256 + o)
        dif_ref[pl.ds(kv, tm, stride=4), :] = kk
        dif_ref[pl.ds(2 + kv, tm, stride=4), :] = vv
        kcd_ref[kv] = kk.astype(kcd_ref.dtype)
        vcd_ref[kv] = vv.astype(vcd_ref.dtype)
    m = col(OFF_MISC)
    a, _ = half_ms(m)
    ikr = rope64(m * lax.rsqrt(a + EPS) * gain(6))
    ik_ref[...] = ikr[:, 0:64]
    ikd_ref[...] = jnp.where(lo, ikr, pltpu.roll(ikr, 64, 1)).astype(ikd_ref.dtype)
    sig = jax.nn.sigmoid(m)
    for kv in range(A_KV):
        gat_ref[kv] = pltpu.roll(sig, 128 - MISC_GATE - 12 * kv, 1)
    iw_ref[...] = pltpu.roll(m, 128 - MISC_IW, 1) * ((IDX_DIM ** -0.5) * (IDX_HEADS ** -0.5))


def _post_project(proj, tabs, prm, wc, tm, qdt):
    B, T, _ = proj.shape
    emit_cmp = wc is not None
    nt = T // tm

    def row(c):
        return pl.BlockSpec((None, tm, c), lambda b, i: (b, i, 0))

    def heads(*lead):
        n = len(lead)
        return pl.BlockSpec((None,) + lead + (tm, 128), lambda b, i: (b,) + (0,) * n + (i, 0))

    tab = pl.BlockSpec((tm, 128), lambda b, i: (i, 0))
    in_specs = [row(N_PROJ), tab, tab, tab, tab, pl.BlockSpec((16, 128), lambda b, i: (0, 0))]
    args = [proj, *tabs, prm]
    if emit_cmp:
        in_specs.append(pl.BlockSpec((2, 2 * tm // CMP_STRIDE, tm), lambda b, i: (0, 0, 0)))
        args.append(wc)

    def sds(shape, dt):
        return jax.ShapeDtypeStruct(shape, dt)

    def slabs(n):
        return pl.BlockSpec((None, tm * n, 128), lambda b, i: (b, i, 0))

    out_shape = [
        sds((B, T * 8, 128), F32), sds((B, T * 4, 128), F32), sds((B, T * 2, 128), F32), sds((B, T, 64), F32),
        sds((B, T * 4, 128), F32),
        sds((B, A_HEADS, T, 128), qdt),
        sds((B, A_KV, T, 128), qdt), sds((B, A_KV, T, 128), qdt),
        sds((B, A_KV, T, 128), qdt), sds((B, A_KV, T, 128), qdt),
        sds((B, B_HEADS, T, 128), qdt), sds((B, T, 128), qdt), sds((B, T, 128), qdt),
        sds((B, IDX_HEADS, T, 128), qdt), sds((B, T, 128), qdt), sds((B, T, 128), F32),
        sds((B, C_KV, 4, T, 128), qdt), sds((B, C_KV, T, 128), qdt), sds((B, C_KV, T, 128), qdt),
        sds((B, A_KV, T, 128), F32),
    ]
    out_specs = [
        slabs(8), slabs(4), slabs(2), row(64), slabs(4),
        heads(A_HEADS), heads(A_KV), heads(A_KV), heads(A_KV), heads(A_KV),
        heads(B_HEADS), row(128), row(128), heads(IDX_HEADS), row(128), row(128),
        heads(C_KV, 4), heads(C_KV), heads(C_KV), heads(A_KV),
    ]
    if emit_cmp:
        nc = T // CMP_STRIDE
        out_shape += [sds((B, 4, nc, 128), F32), sds((B, 4, nc, 128), F32)]
        spec = pl.BlockSpec((None, 4, tm // CMP_STRIDE, 128), lambda b, i: (b, 0, i, 0))
        out_specs += [spec, spec]
    return pl.pallas_call(
        functools.partial(_post_kernel, emit_cmp=emit_cmp, tm=tm),
        grid=(B, nt),
        in_specs=in_specs,
        out_specs=out_specs,
        out_shape=out_shape,
        compiler_params=_cparams(("parallel", "parallel")),
        name="post_project",
    )(*args)


def _bias(mask):
    return jnp.where(mask, 0.0, MASKED)


def _masked_softmax(s, bias):
    s = s + bias
    m = jnp.maximum(jnp.max(s, axis=-1, keepdims=True), NEG)
    e = jnp.exp2(s - m)
    return e / jnp.maximum(jnp.sum(e, axis=-1, keepdims=True), 1e-30)


def _online_step(s, bias, v, m, l, acc):
    if bias is not None:
        s = s + bias
    m_new = jnp.maximum(m, jnp.max(s, axis=-1, keepdims=True))
    alpha = jnp.exp2(m - m_new)
    e = jnp.exp2(s - m_new)
    l_new = alpha * l + jnp.sum(e, axis=-1, keepdims=True)
    rows = acc.shape[0]
    pv = _dot(e.reshape(rows, e.shape[-1]).astype(BF16), v)
    return m_new, l_new, alpha.reshape(rows, 1) * acc + pv


def _online_single(s, bias, v_row, m, l, acc):
    if bias is not None:
        s = s + bias
    m_new = jnp.maximum(m, s)
    alpha = jnp.exp2(m - m_new)
    e = jnp.exp2(s - m_new)
    return m_new, alpha * l + e, alpha * acc + e.astype(BF16).astype(F32) * v_row


def _rowdot(q, k_row):
    return jnp.sum(q.astype(BF16).astype(F32) * k_row.astype(BF16).astype(F32), axis=-1, keepdims=True)


def _sortable(x):
    b = lax.bitcast_convert_type(x + 0.0, I32)
    return jnp.where(b < 0, b ^ jnp.int32(0x7FFFFFFF), b)


def _lambda_of(lp, lam_init):
    a = jnp.sum(lp[0:1] * lp[1:2], axis=-1, keepdims=True)
    b = jnp.sum(lp[2:3] * lp[3:4], axis=-1, keepdims=True)
    return jnp.exp(a) - jnp.exp(b) + lam_init


def _nsa_prompt_kernel(q_ref, pak_ref, pbk_ref, pav_ref, pbv_ref, cc_ref, sc_ref,
                       ks_ref, vs_ref, kw_ref, vw_ref, g_ref, o_ref, kc_ref, vc_ref, *, T):
    qi = pl.program_id(2)
    nc = T // CMP_STRIDE
    ns = T // SEL_BLOCK
    R = A_G * TQ
    band = WINDOW + TQ

    @pl.when(qi == 0)
    def _():
        kraw = pak_ref[...] + pltpu.roll(pbk_ref[...], nc - 1, 0)
        kc_ref[...] = (kraw * cc_ref[...] + pltpu.roll(kraw, 64, 1) * sc_ref[...]).astype(BF16)
        vc_ref[...] = (pav_ref[...] + pltpu.roll(pbv_ref[...], nc - 1, 0)).astype(BF16)

    q = q_ref[...].reshape(R, 128)
    t0 = qi * TQ
    qp3 = t0 + lax.broadcasted_iota(I32, (1, TQ, 1), 1)

    s_c = _dot_nt(q, kc_ref[...]).reshape(A_G, TQ, nc)
    cend = lax.broadcasted_iota(I32, (1, TQ, nc), 2) * CMP_STRIDE + (CMP_LEN - 1)
    p_c = _masked_softmax(s_c, _bias(cend <= qp3))
    o_c = _dot(p_c.reshape(R, nc).astype(BF16), vc_ref[...])

    sj = lax.broadcasted_iota(I32, (ns, nc), 0) * SEL_BLOCK
    ci = lax.broadcasted_iota(I32, (ns, nc), 1) * CMP_STRIDE
    overlap_t = ((ci < sj + SEL_BLOCK) & (ci + CMP_LEN > sj)).astype(F32)
    imp = _dot_nt(overlap_t, jnp.sum(p_c, axis=0), HI)
    jidx = lax.broadcasted_iota(I32, (ns, TQ), 0)
    jq = (t0 + lax.broadcasted_iota(I32, (1, TQ), 1)) >> SEL_SHIFT
    forced = (jidx == 0) | (jidx == jq) | (jidx == jq - 1)
    imp = jnp.where(forced, imp + FORCE_BONUS, imp)
    imp = jnp.where(jidx > jq, NEG, imp)
    ng = ns // 8
    sub = lax.broadcasted_iota(I32, (8, TQ), 0)
    imp_g = [imp[8 * g:8 * g + 8, :] for g in range(ng)]
    rank_g = [jnp.zeros((8, TQ), F32) for _ in range(ng)]
    for j in range(ns):
        rj = jnp.broadcast_to(imp[j:j + 1, :], (8, TQ))
        for g in range(ng):
            if g < j // 8:
                ahead = rj > imp_g[g]
            elif g > j // 8:
                ahead = rj >= imp_g[g]
            else:
                ahead = (rj > imp_g[g]) | ((rj == imp_g[g]) & (sub > j % 8))
            rank_g[g] = rank_g[g] + jnp.where(ahead, 1.0, 0.0)
    rank = jnp.concatenate(rank_g, axis=0)
    selb = jnp.where(rank < min(SEL_TOPN, ns), 1.0, 0.0).T.astype(BF16)

    erow = lax.broadcasted_iota(I32, (ns, TK), 0)
    ecol = lax.broadcasted_iota(I32, (ns, TK), 1)
    tcol = lax.broadcasted_iota(I32, (1, TQ, TK), 2)

    def sel_step(kt, carry):
        m, l, acc = carry
        base = pl.multiple_of(kt * TK, TK)
        k = ks_ref[pl.ds(base, TK), :]
        v = vs_ref[pl.ds(base, TK), :]
        s = _dot_nt(q, k).reshape(A_G, TQ, TK)
        expand = (erow == ((ecol + base) >> SEL_SHIFT)).astype(BF16)
        chosen = _dot(selb, expand).reshape(1, TQ, TK) > 0.5
        bias = _bias(chosen & (tcol + base <= qp3))
        return _online_step(s, bias, v, m, l, acc)

    nkt = (t0 + TQ + TK - 1) // TK
    init = (jnp.full((A_G, TQ, 1), NEG, F32), jnp.zeros((A_G, TQ, 1), F32), jnp.zeros((R, 128), F32))
    _, l_s, acc_s = lax.fori_loop(0, nkt, sel_step, init)
    o_s = acc_s / jnp.maximum(l_s.reshape(R, 1), 1e-30)

    start = pl.multiple_of(jnp.maximum(t0 - WINDOW, 0), TQ)
    kwin = kw_ref[pl.ds(start, band), :]
    vwin = vw_ref[pl.ds(start, band), :]
    dist = qp3 - (start + lax.broadcasted_iota(I32, (1, TQ, band), 2))
    s_w = _dot_nt(q, kwin).reshape(A_G, TQ, band)
    p_w = _masked_softmax(s_w, _bias((dist >= 0) & (dist <= WINDOW)))
    o_w = _dot(p_w.reshape(R, band).astype(BF16), vwin)

    g = g_ref[...]
    for h in range(A_G):
        r = slice(h * TQ, (h + 1) * TQ)
        o = g[:, 3 * h:3 * h + 1] * o_c[r] + g[:, 3 * h + 1:3 * h + 2] * o_s[r] + g[:, 3 * h + 2:3 * h + 3] * o_w[r]
        o_ref[:, h * 128:(h + 1) * 128] = o.astype(o_ref.dtype)


def _nsa_prompt(qa, pa, pb, cc, sc, ksel, vsel, kw, vw, gat):
    B, _, T, _ = qa.shape
    nc = T // CMP_STRIDE
    part_k = pl.BlockSpec((None, None, nc, 128), lambda b, kv, i: (b, kv, 0, 0))
    part_v = pl.BlockSpec((None, None, nc, 128), lambda b, kv, i: (b, 2 + kv, 0, 0))
    tabc = pl.BlockSpec((nc, 128), lambda b, kv, i: (0, 0))
    full = pl.BlockSpec((None, None, T, 128), lambda b, kv, i: (b, kv, 0, 0))
    return pl.pallas_call(
        functools.partial(_nsa_prompt_kernel, T=T),
        grid=(B, A_KV, T // TQ),
        in_specs=[
            pl.BlockSpec((None, A_G, TQ, 128), lambda b, kv, i: (b, kv, i, 0)),
            part_k, part_k, part_v, part_v, tabc, tabc, full, full, full, full,
            pl.BlockSpec((None, None, TQ, 128), lambda b, kv, i: (b, kv, i, 0)),
        ],
        out_specs=pl.BlockSpec((None, TQ, A_G * 128), lambda b, kv, i: (b, i, kv)),
        out_shape=jax.ShapeDtypeStruct((B, T, A_HEADS * 128), BF16),
        scratch_shapes=[pltpu.VMEM((nc, 128), BF16), pltpu.VMEM((nc, 128), BF16)],
        compiler_params=_cparams(("parallel", "parallel", "arbitrary")),
        name="nsa_prompt",
    )(qa, pa, pb, pa, pb, cc, sc, ksel, vsel, kw, vw, gat)


def _kth_threshold(count_ge, shape, k):
    def step(it, t):
        cand = t + jnp.left_shift(jnp.int32(1), 31 - it)
        return jnp.where(count_ge(cand) >= k, cand, t)

    return lax.fori_loop(0, 32, step, jnp.full(shape, INT_MIN, I32))


def _dsa_prompt_kernel(q_ref, iq_ref, iw_ref, ik_ref, k_ref, v_ref, o_ref, key_ref, *, T):
    qi = pl.program_id(1)
    t0 = qi * TQ
    nkt = (t0 + TQ + TK - 1) // TK
    k_top = min(IDX_TOPK, T // 4)
    qp = t0 + lax.broadcasted_iota(I32, (TQ, 1), 0)
    tcol = lax.broadcasted_iota(I32, (TQ, TK), 1)
    iq = iq_ref[...].reshape(IDX_HEADS * TQ, 128)
    iw = iw_ref[...]

    def score_step(kt, _):
        base = pl.multiple_of(kt * TK, TK)
        s = _dot_nt(iq, ik_ref[pl.ds(base, TK), :])
        sc = jnp.zeros((TQ, TK), F32)
        for h in range(IDX_HEADS):
            sc = sc + iw[:, h:h + 1] * jnp.maximum(s[h * TQ:(h + 1) * TQ], 0.0)
        sc = jnp.where(tcol + base <= qp, sc, NEG)
        key_ref[:, pl.ds(base, TK)] = _sortable(sc)
        return 0

    lax.fori_loop(0, nkt, score_step, 0)

    def count_ge(cand):
        cb = jnp.broadcast_to(cand, (TQ, 128))

        def cstep(kt, c):
            base = pl.multiple_of(kt * TK, TK)
            for j in range(TK // 128):
                c = c + jnp.where(key_ref[:, pl.ds(base + j * 128, 128)] >= cb, 1.0, 0.0)
            return c

        c = lax.fori_loop(0, nkt, cstep, jnp.zeros((TQ, 128), F32))
        return jnp.sum(c, axis=-1, keepdims=True)

    thr3 = _kth_threshold(count_ge, (TQ, 1), float(k_top)).reshape(1, TQ, 1)
    qp3 = t0 + lax.broadcasted_iota(I32, (1, TQ, 1), 1)
    tcol3 = lax.broadcasted_iota(I32, (1, TQ, TK), 2)

    q = q_ref[...].reshape(B_HEADS * TQ, 128)

    def att_step(kt, carry):
        m, l, acc = carry
        base = pl.multiple_of(kt * TK, TK)
        s = _dot_nt(q, k_ref[pl.ds(base, TK), :]).reshape(B_HEADS, TQ, TK)
        keys = key_ref[:, pl.ds(base, TK)].reshape(1, TQ, TK)
        bias = _bias((keys >= thr3) & (tcol3 + base <= qp3))
        return _online_step(s, bias, v_ref[pl.ds(base, TK), :], m, l, acc)

    R = B_HEADS * TQ
    init = (jnp.full((B_HEADS, TQ, 1), NEG, F32), jnp.zeros((B_HEADS, TQ, 1), F32), jnp.zeros((R, 128), F32))
    _, l, acc = lax.fori_loop(0, nkt, att_step, init)
    o = acc / jnp.maximum(l.reshape(R, 1), 1e-30)
    for h in range(B_HEADS):
        o_ref[:, h * 128:(h + 1) * 128] = o[h * TQ:(h + 1) * TQ].astype(o_ref.dtype)


def _dsa_prompt(qb, iq, iw, ikd, kb, vb):
    B, _, T, _ = qb.shape
    full = pl.BlockSpec((None, T, 128), lambda b, i: (b, 0, 0))
    return pl.pallas_call(
        functools.partial(_dsa_prompt_kernel, T=T),
        grid=(B, T // TQ),
        in_specs=[
            pl.BlockSpec((None, B_HEADS, TQ, 128), lambda b, i: (b, 0, i, 0)),
            pl.BlockSpec((None, IDX_HEADS, TQ, 128), lambda b, i: (b, 0, i, 0)),
            pl.BlockSpec((None, TQ, 128), lambda b, i: (b, i, 0)),
            full, full, full,
        ],
        out_specs=pl.BlockSpec((None, TQ, B_HEADS * 128), lambda b, i: (b, i, 0)),
        out_shape=jax.ShapeDtypeStruct((B, T, B_HEADS * 128), BF16),
        scratch_shapes=[pltpu.VMEM((TQ, T), I32)],
        compiler_params=_cparams(("parallel", "arbitrary")),
        name="dsa_prompt",
    )(qb, iq, iw, ikd, kb, vb)


def _diff_finish(acc, l, lam, subln, lam_init, rows):
    o = acc / jnp.maximum(l, 1e-30)
    outs = []
    for g in range(2):
        a0 = o[(2 * g) * rows:(2 * g + 1) * rows]
        a1 = o[(2 * g + 1) * rows:(2 * g + 2) * rows]
        d = a0 - lam * a1
        d = d * lax.rsqrt(jnp.mean(d * d, axis=-1, keepdims=True) + EPS) * subln
        outs.append(d * (1.0 - lam_init))
    return outs


def _diff_prompt_kernel(q_ref, k_ref, v_ref, lp_ref, sub_ref, o_ref, *, lam_init):
    qi = pl.program_id(2)
    t0 = qi * TQ
    nkt = (t0 + TQ + TK - 1) // TK
    R = 4 * TQ
    qp = t0 + lax.broadcasted_iota(I32, (1, TQ, 1), 1)
    tcol = lax.broadcasted_iota(I32, (1, TQ, TK), 2)
    q = q_ref[...].reshape(R, 128)

    def step(kt, carry, causal):
        m, l, acc = carry
        base = pl.multiple_of(kt * TK, TK)
        s = _dot_nt(q, k_ref[pl.ds(base, TK), :]).reshape(4, TQ, TK)
        bias = _bias(tcol + base <= qp) if causal else None
        return _online_step(s, bias, v_ref[pl.ds(base, TK), :], m, l, acc)

    init = (jnp.full((4, TQ, 1), NEG, F32), jnp.zeros((4, TQ, 1), F32), jnp.zeros((R, 128), F32))
    carry = lax.fori_loop(0, nkt - 1, functools.partial(step, causal=False), init)
    _, l, acc = step(nkt - 1, carry, True)
    lam = _lambda_of(lp_ref[...], lam_init)
    outs = _diff_finish(acc, l.reshape(R, 1), lam, sub_ref[...], lam_init, TQ)
    for g in range(2):
        o_ref[:, g * 128:(g + 1) * 128] = outs[g].astype(o_ref.dtype)


def _diff_prompt(qc, kcd, vcd, lp, subln, lam_init):
    B, _, _, T, _ = qc.shape
    full = pl.BlockSpec((None, None, T, 128), lambda b, kv, i: (b, kv, 0, 0))
    return pl.pallas_call(
        functools.partial(_diff_prompt_kernel, lam_init=lam_init),
        grid=(B, C_KV, T // TQ),
        in_specs=[
            pl.BlockSpec((None, None, 4, TQ, 128), lambda b, kv, i: (b, kv, 0, i, 0)),
            full, full,
            pl.BlockSpec((4, C_HALF), lambda b, kv, i: (0, 0)),
            pl.BlockSpec((1, 128), lambda b, kv, i: (0, 0)),
        ],
        out_specs=pl.BlockSpec((None, TQ, 256), lambda b, kv, i: (b, i, kv)),
        out_shape=jax.ShapeDtypeStruct((B, T, C_HEADS * 128), BF16),
        compiler_params=_cparams(("parallel", "parallel", "arbitrary")),
        name="diff_prompt",
    )(qc, kcd, vcd, lp, subln)


def _page_specs(block, layer, n_lead_zero):
    specs = []
    for u in range(PAGES_PER_STEP):
        def imap(b, pc, pt, u=u):
            return (layer, pt[b, pc * PAGES_PER_STEP + u]) + (0,) * n_lead_zero
        specs.append(pl.BlockSpec(block, imap))
    return specs


def _nsa_cmp_decode_kernel(pt_ref, *refs, past):
    pages = refs[:PAGES_PER_STEP]
    q_ref, new_ref, wc_ref, cc_ref, sc_ref, oc_ref, sel_ref, a_ref, b_ref = refs[PAGES_PER_STEP:]
    pc = pl.program_id(1)
    nc = past // CMP_STRIDE
    ns = past // SEL_BLOCK + 1
    nsp = ((ns + 127) // 128) * 128
    cpp = PAGE_SIZE // CMP_STRIDE

    def slab(ref, s):
        return ref[pl.ds(s, PAGE_SIZE, stride=8), :]

    for u in range(PAGES_PER_STEP):
        r0 = pl.multiple_of((pc * PAGES_PER_STEP + u) * cpp, cpp)
        for c in range(2):
            x = jnp.concatenate([slab(pages[u], 2 * c), slab(pages[u], 2 * c + 1)], axis=1)
            part = _dot3(wc_ref[c], x)
            a_ref[pl.ds(r0, cpp), 256 * c:256 * (c + 1)] = part[0:cpp]
            b_ref[pl.ds(r0, cpp), 256 * c:256 * (c + 1)] = part[cpp:2 * cpp]

    @pl.when(pc == pl.num_programs(1) - 1)
    def _():
        rowi = lax.broadcasted_iota(I32, (nc, 128), 0)
        cend = lax.broadcasted_iota(I32, (A_G, nc), 1) * CMP_STRIDE + (CMP_LEN - 1)
        ci = lax.broadcasted_iota(I32, (nc, nsp), 0) * CMP_STRIDE
        sj = lax.broadcasted_iota(I32, (nc, nsp), 1) * SEL_BLOCK
        overlap = ((ci < sj + SEL_BLOCK) & (ci + CMP_LEN > sj)).astype(F32)
        jrow = lax.broadcasted_iota(I32, (1, nsp), 1)
        jq = past // SEL_BLOCK
        ii = lax.broadcasted_iota(I32, (nsp, nsp), 0)
        jj = lax.broadcasted_iota(I32, (nsp, nsp), 1)
        rr = lax.broadcasted_iota(I32, (SEL_TOPN, nsp), 0)
        jr = lax.broadcasted_iota(I32, (SEL_TOPN, nsp), 1).astype(F32)
        new = new_ref[...]
        q = q_ref[...].astype(BF16)
        for kv in range(A_KV):
            ko, vo = kv * 128, 256 + kv * 128
            bk = jnp.where(rowi == nc - 1, wc_ref[0][cpp:cpp + 1, 0:1] * new[:, ko:ko + 128],
                           pltpu.roll(b_ref[:, ko:ko + 128], nc - 1, 0))
            bv = jnp.where(rowi == nc - 1, wc_ref[1][cpp:cpp + 1, 0:1] * new[:, vo:vo + 128],
                           pltpu.roll(b_ref[:, vo:vo + 128], nc - 1, 0))
            kraw = a_ref[:, ko:ko + 128] + bk
            kc = (kraw * cc_ref[...] + pltpu.roll(kraw, 64, 1) * sc_ref[...]).astype(BF16)
            vc = (a_ref[:, vo:vo + 128] + bv).astype(BF16)
            s_c = _dot_nt(q[kv * A_G:(kv + 1) * A_G], kc)
            p_c = _masked_softmax(s_c, _bias(cend <= past))
            oc_ref[kv * A_G:(kv + 1) * A_G, :] = _dot(p_c.astype(BF16), vc)
            imp = _dot(jnp.sum(p_c, axis=0, keepdims=True), overlap, HI)
            forced = (jrow == 0) | (jrow == jq) | (jrow == jq - 1)
            imp = jnp.where(forced, imp + FORCE_BONUS, imp)
            imp = jnp.where(jrow > jq, NEG, imp)
            imp_col = jnp.sum(jnp.where(ii == jj, jnp.broadcast_to(imp, (nsp, nsp)), 0.0), axis=1, keepdims=True)
            beats = (imp_col > imp) | ((imp_col == imp) & (ii < jj))
            rank = jnp.sum(jnp.where(beats, 1.0, 0.0), axis=0, keepdims=True)
            pick = jnp.sum(jnp.where(rank == rr.astype(F32), jr, 0.0), axis=1, keepdims=True)
            sel_ref[kv] = pick.astype(I32)


def _nsa_cmp_decode(page_table, cache, layer, q, new_rows, wc, cc, sc, past):
    DB = q.shape[0]
    npc = page_table.shape[1] // PAGES_PER_STEP
    nc = past // CMP_STRIDE
    block = (None, None, PAGE_SIZE * 8, 128)
    const2 = lambda b, pc, pt: (0, 0)
    return pl.pallas_call(
        functools.partial(_nsa_cmp_decode_kernel, past=past),
        grid_spec=pltpu.PrefetchScalarGridSpec(
            num_scalar_prefetch=1,
            grid=(DB, npc),
            in_specs=_page_specs(block, layer, 2) + [
                pl.BlockSpec((None, A_HEADS, 128), lambda b, pc, pt: (b, 0, 0)),
                pl.BlockSpec((None, 1, 1024), lambda b, pc, pt: (b, 0, 0)),
                pl.BlockSpec((2, 2 * PAGE_SIZE // CMP_STRIDE, PAGE_SIZE), lambda b, pc, pt: (0, 0, 0)),
                pl.BlockSpec((nc, 128), const2),
                pl.BlockSpec((nc, 128), const2),
            ],
            out_specs=[
                pl.BlockSpec((None, A_HEADS, 128), lambda b, pc, pt: (b, 0, 0)),
                pl.BlockSpec((None, A_KV, SEL_TOPN, 1), lambda b, pc, pt: (b, 0, 0, 0)),
            ],
            scratch_shapes=[pltpu.VMEM((nc, 512), F32), pltpu.VMEM((nc, 512), F32)],
        ),
        out_shape=[jax.ShapeDtypeStruct((DB, A_HEADS, 128), F32),
                   jax.ShapeDtypeStruct((DB, A_KV, SEL_TOPN, 1), I32)],
        compiler_params=_cparams(("parallel", "arbitrary")),
        name="nsa_cmp_decode",
    )(page_table, *([cache] * PAGES_PER_STEP), q, new_rows, wc, cc, sc)


def _nsa_sel_decode_kernel(pt_ref, sel_ref, blk0_ref, blk1_ref, win_ref, q_ref, new_ref, neww_ref, oc_ref, g_ref,
                           o_ref, m_ref, l_ref, acc_ref, *, past, lw):
    b, s = pl.program_id(0), pl.program_id(1)
    ns = past // SEL_BLOCK + 1
    new = new_ref[...]
    neww = neww_ref[...]
    blks = (blk0_ref, blk1_ref)

    for kv in range(A_KV):
        q = q_ref[kv * A_G:(kv + 1) * A_G, :].astype(BF16)

        @pl.when(s == 0)
        def _():
            m_ref[kv] = _rowdot(q, new[:, 512 + kv * 128:640 + kv * 128])
            l_ref[kv] = jnp.ones((A_G, 1), F32)
            v_new = new[:, 768 + kv * 128:896 + kv * 128].astype(BF16).astype(F32)
            acc_ref[kv] = jnp.broadcast_to(v_new, (A_G, 128))

        @pl.when(sel_ref[b, kv, s] != ns - 1)
        def _():
            k = blks[kv][pl.ds(4 + kv, SEL_BLOCK, stride=8), :].astype(BF16)
            v = blks[kv][pl.ds(6 + kv, SEL_BLOCK, stride=8), :].astype(BF16)
            m, l, acc = _online_step(_dot_nt(q, k), None, v, m_ref[kv], l_ref[kv], acc_ref[kv])
            m_ref[kv] = m
            l_ref[kv] = l
            acc_ref[kv] = acc

        @pl.when(s == pl.num_programs(1) - 1)
        def _():
            o_s = acc_ref[kv] / jnp.maximum(l_ref[kv], 1e-30)
            kw = win_ref[pl.ds(kv, lw, stride=4), :].astype(BF16)
            vw = win_ref[pl.ds(2 + kv, lw, stride=4), :].astype(BF16)
            vw_new = neww[:, 256 + kv * 128:384 + kv * 128].astype(BF16).astype(F32)
            s_w = _dot_nt(q, kw)
            s_n = _rowdot(q, neww[:, kv * 128:(kv + 1) * 128])
            mw = jnp.maximum(jnp.max(s_w, axis=-1, keepdims=True), s_n)
            e_w = jnp.exp2(s_w - mw)
            e_n = jnp.exp2(s_n - mw)
            den = jnp.sum(e_w, axis=-1, keepdims=True) + e_n
            o_w = (_dot(e_w.astype(BF16), vw) + e_n.astype(BF16).astype(F32) * vw_new) / den
            o_c = oc_ref[kv * A_G:(kv + 1) * A_G, :]
            g = g_ref[kv * A_G:(kv + 1) * A_G, :]
            o_ref[kv * A_G:(kv + 1) * A_G, :] = g[:, 0:1] * o_c + g[:, 1:2] * o_s + g[:, 2:3] * o_w


def _nsa_sel_decode(page_table, sel, cache, win, layer, q, new_rows, new_win, o_c, gates, past):
    DB = q.shape[0]
    n_pages = page_table.shape[1]
    lw = win.shape[2] // 4

    def blk_map(kv):
        def imap(b, s, pt, sel):
            j = sel[b, kv, s]
            return (layer, pt[b, jnp.minimum(j // 2, n_pages - 1)], j % 2, 0)
        return imap

    per_b = lambda b, s, pt, sel: (b, 0, 0)
    return pl.pallas_call(
        functools.partial(_nsa_sel_decode_kernel, past=past, lw=lw),
        grid_spec=pltpu.PrefetchScalarGridSpec(
            num_scalar_prefetch=2,
            grid=(DB, SEL_TOPN),
            in_specs=[
                pl.BlockSpec((None, None, SEL_BLOCK * 8, 128), blk_map(0)),
                pl.BlockSpec((None, None, SEL_BLOCK * 8, 128), blk_map(1)),
                pl.BlockSpec((None, None, lw * 4, 128), lambda b, s, pt, sel: (layer, b, 0, 0)),
                pl.BlockSpec((None, A_HEADS, 128), per_b),
                pl.BlockSpec((None, 1, 1024), per_b),
                pl.BlockSpec((None, 1, 512), per_b),
                pl.BlockSpec((None, A_HEADS, 128), per_b),
                pl.BlockSpec((None, A_HEADS, 128), per_b),
            ],
            out_specs=pl.BlockSpec((None, A_HEADS, 128), per_b),
            scratch_shapes=[pltpu.VMEM((A_KV, A_G, 1), F32), pltpu.VMEM((A_KV, A_G, 1), F32),
                            pltpu.VMEM((A_KV, A_G, 128), F32)],
        ),
        out_shape=jax.ShapeDtypeStruct((DB, A_HEADS, 128), F32),
        compiler_params=_cparams(("parallel", "arbitrary")),
        name="nsa_sel_decode",
    )(page_table, sel, cache, cache, win, q, new_rows, new_win, o_c, gates)


def _dsa_idx_decode_kernel(pt_ref, *refs, past):
    pages = refs[:PAGES_PER_STEP]
    iq_ref, iw_ref, ikn_ref, mask_ref, sc_ref = refs[PAGES_PER_STEP:]
    pc = pl.program_id(1)
    n_pages = past // PAGE_SIZE
    k_top = min(IDX_TOPK, (past + 1) // 4)
    iq = iq_ref[...].astype(BF16)
    iw = iw_ref[...]

    @pl.when(pc == 0)
    def _():
        sc_ref[...] = jnp.full(sc_ref.shape, NEG, F32)

    for u in range(PAGES_PER_STEP):
        s = _dot_nt(iq, pages[u][...].astype(BF16))
        sc_ref[pl.ds(pc * PAGES_PER_STEP + u, 1), :] = jnp.sum(iw * jnp.maximum(s, 0.0), axis=0, keepdims=True)

    @pl.when(pc == pl.num_programs(1) - 1)
    def _():
        s_new = _rowdot(iq, ikn_ref[...])
        s_new = jnp.sum(iw * jnp.maximum(s_new, 0.0), axis=0, keepdims=True)
        lane0 = lax.broadcasted_iota(I32, (1, PAGE_SIZE), 1) == 0
        sc_ref[n_pages:n_pages + 1, :] = jnp.where(lane0, s_new, NEG)
        keys = _sortable(sc_ref[...])

        def count_ge(cand):
            c = jnp.sum(jnp.where(keys >= cand, 1.0, 0.0), axis=-1, keepdims=True)
            return jnp.sum(c, axis=0, keepdims=True)

        thr = _kth_threshold(count_ge, (1, 1), float(k_top))
        mask_ref[...] = _bias((keys >= thr) & (sc_ref[...] > 0.5 * NEG))


def _dsa_idx_decode(page_table, cache, layer, iq, iw, ik_new, past):
    DB = iq.shape[0]
    n_pages = page_table.shape[1]
    npc = n_pages // PAGES_PER_STEP
    rows = ((n_pages + 1 + 7) // 8) * 8
    per_b = lambda b, pc, pt: (b, 0, 0)
    return pl.pallas_call(
        functools.partial(_dsa_idx_decode_kernel, past=past),
        grid_spec=pltpu.PrefetchScalarGridSpec(
            num_scalar_prefetch=1,
            grid=(DB, npc),
            in_specs=_page_specs((None, None, PAGE_SIZE, IDX_DIM), layer, 2) + [
                pl.BlockSpec((None, IDX_HEADS, IDX_DIM), per_b),
                pl.BlockSpec((None, IDX_HEADS, 1), per_b),
                pl.BlockSpec((None, 1, IDX_DIM), per_b),
            ],
            out_specs=pl.BlockSpec((None, rows, PAGE_SIZE), per_b),
            scratch_shapes=[pltpu.VMEM((rows, PAGE_SIZE), F32)],
        ),
        out_shape=jax.ShapeDtypeStruct((DB, rows, PAGE_SIZE), F32),
        compiler_params=_cparams(("parallel", "arbitrary")),
        name="dsa_idx_decode",
    )(page_table, *([cache] * PAGES_PER_STEP), iq, iw, ik_new)


def _dsa_att_decode_kernel(pt_ref, *refs, past):
    pages = refs[:PAGES_PER_STEP]
    q_ref, new_ref, mask_ref, o_ref, m_ref, l_ref, acc_ref = refs[PAGES_PER_STEP:]
    pc = pl.program_id(1)
    n_pages = past // PAGE_SIZE
    q = q_ref[...].astype(BF16)

    @pl.when(pc == 0)
    def _():
        m_ref[...] = jnp.full(m_ref.shape, NEG, F32)
        l_ref[...] = jnp.zeros(l_ref.shape, F32)
        acc_ref[...] = jnp.zeros(acc_ref.shape, F32)

    k = jnp.concatenate([p[pl.ds(0, PAGE_SIZE, stride=2), :] for p in pages], axis=0).astype(BF16)
    v = jnp.concatenate([p[pl.ds(1, PAGE_SIZE, stride=2), :] for p in pages], axis=0).astype(BF16)
    r0 = pl.multiple_of(pc * PAGES_PER_STEP, PAGES_PER_STEP)
    mrows = mask_ref[pl.ds(r0, PAGES_PER_STEP), :]
    bias = jnp.concatenate([mrows[u:u + 1, :] for u in range(PAGES_PER_STEP)], axis=1)
    m, l, acc = _online_step(_dot_nt(q, k), bias, v, m_ref[...], l_ref[...], acc_ref[...])
    m_ref[...] = m
    l_ref[...] = l
    acc_ref[...] = acc

    @pl.when(pc == pl.num_programs(1) - 1)
    def _():
        new = new_ref[...]
        v_new = new[:, 128:256].astype(BF16).astype(F32)
        keep = mask_ref[n_pages:n_pages + 1, 0:1]
        _, l2, acc2 = _online_single(_rowdot(q, new[:, 0:128]), keep, v_new, m_ref[...], l_ref[...], acc_ref[...])
        o_ref[...] = acc2 / jnp.maximum(l2, 1e-30)


def _dsa_att_decode(page_table, cache, layer, q, new_rows, mask, past):
    DB = q.shape[0]
    npc = page_table.shape[1] // PAGES_PER_STEP
    rows = mask.shape[1]
    per_b = lambda b, pc, pt: (b, 0, 0)
    return pl.pallas_call(
        functools.partial(_dsa_att_decode_kernel, past=past),
        grid_spec=pltpu.PrefetchScalarGridSpec(
            num_scalar_prefetch=1,
            grid=(DB, npc),
            in_specs=_page_specs((None, None, PAGE_SIZE * 2, 128), layer, 2) + [
                pl.BlockSpec((None, B_HEADS, 128), per_b),
                pl.BlockSpec((None, 1, 256), per_b),
                pl.BlockSpec((None, rows, PAGE_SIZE), per_b),
            ],
            out_specs=pl.BlockSpec((None, B_HEADS, 128), per_b),
            scratch_shapes=[pltpu.VMEM((B_HEADS, 1), F32), pltpu.VMEM((B_HEADS, 1), F32),
                            pltpu.VMEM((B_HEADS, 128), F32)],
        ),
        out_shape=jax.ShapeDtypeStruct((DB, B_HEADS, 128), F32),
        compiler_params=_cparams(("parallel", "arbitrary")),
        name="dsa_att_decode",
    )(page_table, *([cache] * PAGES_PER_STEP), q, new_rows, mask)


def _diff_decode_kernel(pt_ref, *refs, lam_init):
    pages = refs[:PAGES_PER_STEP]
    q_ref, new_ref, lp_ref, sub_ref, o_ref, m_ref, l_ref, acc_ref = refs[PAGES_PER_STEP:]
    pc = pl.program_id(1)

    @pl.when(pc == 0)
    def _():
        m_ref[...] = jnp.full(m_ref.shape, NEG, F32)
        l_ref[...] = jnp.zeros(l_ref.shape, F32)
        acc_ref[...] = jnp.zeros(acc_ref.shape, F32)

    for kv in range(C_KV):
        q = q_ref[kv].astype(BF16)
        k = jnp.concatenate([p[pl.ds(kv, PAGE_SIZE, stride=4), :] for p in pages], axis=0).astype(BF16)
        v = jnp.concatenate([p[pl.ds(2 + kv, PAGE_SIZE, stride=4), :] for p in pages], axis=0).astype(BF16)
        sc = _dot_nt(q, k)
        m, l, acc = _online_step(sc, None, v, m_ref[kv], l_ref[kv], acc_ref[kv])
        m_ref[kv] = m
        l_ref[kv] = l
        acc_ref[kv] = acc

    @pl.when(pc == pl.num_programs(1) - 1)
    def _():
        new = new_ref[...]
        lam = _lambda_of(lp_ref[...], lam_init)
        for kv in range(C_KV):
            k_new = new[:, kv * 128:(kv + 1) * 128]
            v_new = new[:, 256 + kv * 128:384 + kv * 128].astype(BF16).astype(F32)
            _, l, acc = _online_single(_rowdot(q_ref[kv], k_new), None, v_new, m_ref[kv], l_ref[kv], acc_ref[kv])
            outs = _diff_finish(acc, l, lam, sub_ref[...], lam_init, 1)
            o_ref[2 * kv:2 * kv + 1, :] = outs[0]
            o_ref[2 * kv + 1:2 * kv + 2, :] = outs[1]


def _diff_decode(page_table, cache, layer, q, new_rows, lp, subln, lam_init):
    DB = q.shape[0]
    npc = page_table.shape[1] // PAGES_PER_STEP
    per_b = lambda b, pc, pt: (b, 0, 0)
    return pl.pallas_call(
        functools.partial(_diff_decode_kernel, lam_init=lam_init),
        grid_spec=pltpu.PrefetchScalarGridSpec(
            num_scalar_prefetch=1,
            grid=(DB, npc),
            in_specs=_page_specs((None, None, PAGE_SIZE * 4, 128), layer, 2) + [
                pl.BlockSpec((None, C_KV, 4, 128), lambda b, pc, pt: (b, 0, 0, 0)),
                pl.BlockSpec((None, 1, 512), per_b),
                pl.BlockSpec((4, C_HALF), lambda b, pc, pt: (0, 0)),
                pl.BlockSpec((1, 128), lambda b, pc, pt: (0, 0)),
            ],
            out_specs=pl.BlockSpec((None, C_HEADS, 128), per_b),
            scratch_shapes=[pltpu.VMEM((C_KV, 4, 1), F32), pltpu.VMEM((C_KV, 4, 1), F32),
                            pltpu.VMEM((C_KV, 4, 128), F32)],
        ),
        out_shape=jax.ShapeDtypeStruct((DB, C_HEADS, 128), F32),
        compiler_params=_cparams(("parallel", "arbitrary")),
        name="diff_decode",
    )(page_table, *([cache] * PAGES_PER_STEP), q, new_rows, lp, subln)


def _rope_tables(pos, d):
    half = d // 2
    inv = ROPE_THETA ** (-jnp.arange(half, dtype=F32) / half)
    ang = pos.astype(F32)[:, None] * inv[None, :]
    cos, sin = jnp.cos(ang), jnp.sin(ang)
    reps = 128 // d
    return jnp.tile(jnp.concatenate([cos, cos], axis=-1), (1, reps)), jnp.tile(jnp.concatenate([-sin, sin], axis=-1), (1, reps))


def _permute_w_in(w_in):
    sizes = (1024, 1536, 24, 512, 256, 1024, 16, 64, 512, 512)
    offs = np.concatenate([[0], np.cumsum(sizes)])
    a_q, a_kv, a_gate, b_q, b_kv, b_iq, b_iw, b_ik, c_q, c_kv = [
        w_in[:, :, int(offs[i]):int(offs[i + 1])] for i in range(10)]
    pad = jnp.zeros(w_in.shape[:2] + (N_PROJ - 5480,), w_in.dtype)
    return jnp.concatenate([a_q, a_kv, b_q, b_kv, b_iq, c_q, c_kv, b_ik, a_gate, b_iw, pad], axis=-1).astype(BF16)


def _pack_params(nsa_qk_norm, dsa_qk_norm, dsa_idx_knorm, diff_qk_norm):
    rows = [nsa_qk_norm, dsa_qk_norm, jnp.tile(dsa_idx_knorm, 2)[None], jnp.tile(diff_qk_norm, (1, 2))]
    p = jnp.concatenate(rows, axis=0).astype(F32)
    return jnp.pad(p, ((0, 16 - p.shape[0]), (0, 0)))


def _compress_weights(cmp_w, rows):
    eye = jnp.eye(rows // CMP_STRIDE, dtype=F32)
    mats = []
    for c in range(2):
        halves = [jnp.kron(eye, cmp_w[c, h * CMP_STRIDE:(h + 1) * CMP_STRIDE][None, :]) for h in range(2)]
        mats.append(jnp.concatenate(halves, axis=0))
    return jnp.stack(mats, axis=0)


def kernel(x_prompt, x_sample, cache_nsa_kv, state_nsa_win, cache_dsa_kv, cache_dsa_idx, cache_diff_kv, page_table, attn_norm, w_in, nsa_qk_norm, nsa_cmp_w, dsa_qk_norm, dsa_idx_knorm, diff_qk_norm, diff_lambda, diff_subln, w_out, ffn_norm, w_gate_up, w_down):
    B, T, D = x_prompt.shape
    DB = x_sample.shape[0]
    depth = w_in.shape[0]
    n_pages = page_table.shape[1]
    past = n_pages * PAGE_SIZE
    M = B * T
    assert x_sample.shape[1] == 1 and T % TK == 0 and T >= WINDOW + TQ and n_pages % PAGES_PER_STEP == 0
    tm = min(1024, M)
    tm_ffn = min(512, M)
    tm_post = 256

    w_in_p = _permute_w_in(w_in)
    w_out_b = w_out.astype(BF16)
    w_gu_b = w_gate_up.astype(BF16)
    w_down_b = w_down.astype(BF16)

    pos_p = jnp.arange(T, dtype=I32)
    tabs_p = _rope_tables(pos_p, 128) + _rope_tables(pos_p, 64)
    pos_s = jnp.full((DB,), past, I32)
    tabs_s = _rope_tables(pos_s, 128) + _rope_tables(pos_s, 64)
    cend_p = jnp.arange(T // CMP_STRIDE, dtype=I32) * CMP_STRIDE + (CMP_LEN - 1)
    cc_p, sc_p = _rope_tables(cend_p, 128)
    cend_s = jnp.arange(past // CMP_STRIDE, dtype=I32) * CMP_STRIDE + (CMP_LEN - 1)
    cc_s, sc_s = _rope_tables(cend_s, 128)

    n_pool = cache_nsa_kv.shape[1]
    nsa_pages = cache_nsa_kv.reshape(depth, n_pool, PAGE_SIZE * 8, 128)
    dsa_pages = cache_dsa_kv.reshape(depth, n_pool, PAGE_SIZE * 2, 128)
    diff_pages = cache_diff_kv.reshape(depth, n_pool, PAGE_SIZE * 4, 128)
    win_rows = state_nsa_win.reshape(depth, DB, state_nsa_win.shape[2] * 4, 128)

    yp = x_prompt.reshape(M, D)
    ys = x_sample.reshape(DB, D)
    rows_p, rows_s = [], []
    for l in range(depth):
        lam_init = 0.8 - 0.6 * math.exp(-0.3 * l)
        prm = _pack_params(nsa_qk_norm[l], dsa_qk_norm[l], dsa_idx_knorm[l], diff_qk_norm[l])
        g_attn = attn_norm[l][None, :]
        g_ffn = ffn_norm[l][None, :]
        lp = diff_lambda[l].astype(F32)
        subln = diff_subln[l][None, :].astype(F32)

        proj = _norm_matmul(yp, g_attn, w_in_p, l, tm).reshape(B, T, N_PROJ)
        wc = _compress_weights(nsa_cmp_w[l], tm_post)
        (nsa, win, dsa, ik, dif, qa, ksel, vsel, kw, vw, qb, kb, vb, iq, ikd, iw, qc, kcd, vcd, gat,
         pa, pb) = _post_project(proj, tabs_p, prm, wc, tm_post, BF16)
        o_a = _nsa_prompt(qa, pa, pb, cc_p, sc_p, ksel, vsel, kw, vw, gat)
        o_b = _dsa_prompt(qb, iq, iw, ikd, kb, vb)
        o_c = _diff_prompt(qc, kcd, vcd, lp, subln, lam_init)
        mix = jnp.concatenate([o_a, o_b, o_c], axis=-1).reshape(M, D)
        yp = _matmul_residual(mix, w_out_b, yp, l, tm)
        act = _norm_swiglu(yp, g_ffn, w_gu_b, l, tm_ffn)
        yp = _matmul_residual(act, w_down_b, yp, l, tm_ffn)
        w_keep = min(WINDOW, T)
        rows_p.append((nsa.reshape(B, T, 4, A_KV, 128), win[:, (T - w_keep) * 4:].reshape(B, w_keep, 2, A_KV, 128),
                       dsa.reshape(B, T, 2, 128), ik, dif.reshape(B, T, 2, C_KV, 128)))

        proj_s = _norm_matmul(ys, g_attn, w_in_p, l, DB).reshape(1, DB, N_PROJ)
        (nsa_s, win_s, dsa_s, ik_s, dif_s, qa_s, _, _, _, _, qb_s, _, _, iq_s, _, iw_s, qc_s, _, _,
         gat_s) = _post_project(proj_s, tabs_s, prm, None, DB, F32)
        nsa_new = nsa_s.reshape(DB, 1, 1024)
        win_new = win_s.reshape(DB, 1, 512)
        dsa_new = dsa_s.reshape(DB, 1, 256)
        ik_new = ik_s.reshape(DB, 1, IDX_DIM)
        dif_new = dif_s.reshape(DB, 1, 512)
        qa_d = jnp.transpose(qa_s[0], (1, 0, 2))
        qb_d = jnp.transpose(qb_s[0], (1, 0, 2))
        iq_d = jnp.transpose(iq_s[0], (1, 0, 2))
        iq_d = iq_d[:, :, :64] + iq_d[:, :, 64:]
        iw_d = iw_s[0, :, :IDX_HEADS, None]
        qc_d = jnp.transpose(qc_s[0], (2, 0, 1, 3))
        g_d = jnp.transpose(gat_s[0, :, :, :12].reshape(A_KV, DB, A_G, 3), (1, 0, 2, 3)).reshape(DB, A_HEADS, 3)
        g_d = jnp.pad(g_d, ((0, 0), (0, 0), (0, 125)))

        wc_s = _compress_weights(nsa_cmp_w[l], PAGE_SIZE)
        oc_d, sel = _nsa_cmp_decode(page_table, nsa_pages, l, qa_d, nsa_new, wc_s, cc_s, sc_s, past)
        oa_d = _nsa_sel_decode(page_table, sel.reshape(DB, A_KV, SEL_TOPN), nsa_pages, win_rows, l,
                               qa_d, nsa_new, win_new, oc_d, g_d, past)
        mask = _dsa_idx_decode(page_table, cache_dsa_idx, l, iq_d, iw_d, ik_new, past)
        ob_d = _dsa_att_decode(page_table, dsa_pages, l, qb_d, dsa_new, mask, past)
        od_d = _diff_decode(page_table, diff_pages, l, qc_d, dif_new, lp, subln, lam_init)
        mix_s = jnp.concatenate([oa_d.reshape(DB, 1024), ob_d.reshape(DB, 512), od_d.reshape(DB, 512)],
                                axis=-1).astype(BF16)
        ys = _matmul_residual(mix_s, w_out_b, ys, l, DB)
        act_s = _norm_swiglu(ys, g_ffn, w_gu_b, l, DB)
        ys = _matmul_residual(act_s, w_down_b, ys, l, DB)
        lw = state_nsa_win.shape[2]
        win_all = jnp.concatenate([state_nsa_win[l], win_new.reshape(DB, 1, 2, A_KV, 128)], axis=1)
        rows_s.append((nsa_new.reshape(DB, 1, 4, A_KV, 128), win_all[:, win_all.shape[1] - min(WINDOW, lw + 1):],
                       dsa_new.reshape(DB, 1, 2, 128), ik_new, dif_new.reshape(DB, 1, 2, C_KV, 128)))

    def stacked(rows, i):
        return jnp.stack([r[i] for r in rows], axis=0)

    return (yp.reshape(B, T, D), ys.reshape(DB, 1, D),
            stacked(rows_p, 0), stacked(rows_s, 0), stacked(rows_p, 1), stacked(rows_s, 1),
            stacked(rows_p, 2), stacked(rows_s, 2), stacked(rows_p, 3), stacked(rows_s, 3),
            stacked(rows_p, 4), stacked(rows_s, 4))
```

```python
import functools
import math

import numpy as np
import jax
import jax.numpy as jnp
from jax import lax
from jax.experimental import pallas as pl
from jax.experimental.pallas import tpu as pltpu

F32 = jnp.float32
BF16 = jnp.bfloat16
I32 = jnp.int32
HI = lax.Precision.HIGHEST

D_MODEL = 2048
PAGE_SIZE = 128
D_HEAD = 128
A_HEADS = 8
A_KV = 2
A_G = A_HEADS // A_KV
B_HEADS = 4
C_HEADS = 4
C_KV = 2
C_HALF = 64
IDX_HEADS = 16
IDX_DIM = 64
IDX_TOPK = 256
CMP_LEN = 32
CMP_STRIDE = 16
SEL_BLOCK = 64
SEL_TOPN = 16
WINDOW = 512
FORCE_BONUS = 1.0e4
D_FF = 5632
ROPE_THETA = 10000.0
EPS = 1e-6
NEG = -1e30
INT_MIN = -2147483648

OFF_AQ = 0
OFF_AKV = 1024
OFF_BQ = 2560
OFF_BKV = 3072
OFF_BIQ = 3328
OFF_CQ = 4352
OFF_CKV = 4864
OFF_MISC = 5376
N_PROJ = 5632
MISC_GATE = 64
MISC_IW = 88

TQ = 128
TK = 512
PAGES_PER_STEP = 16
SEL_SHIFT = 6
LOG2E = math.log2(math.e)
SCALE_D = D_HEAD ** -0.5 * LOG2E
SCALE_C = C_HALF ** -0.5 * LOG2E
MASKED = -2e30
VMEM_LIMIT = 56 * 1024 * 1024


def _cparams(sem):
    return pltpu.CompilerParams(dimension_semantics=sem, vmem_limit_bytes=VMEM_LIMIT)


def _dot(a, b, precision=None):
    return jnp.dot(a, b, preferred_element_type=F32, precision=precision)


def _dot3(a, b):
    a_hi = a.astype(BF16)
    b_hi = b.astype(BF16)
    a_lo = (a - a_hi.astype(F32)).astype(BF16)
    b_lo = (b - b_hi.astype(F32)).astype(BF16)
    return _dot(a_hi, b_hi) + (_dot(a_hi, b_lo) + _dot(a_lo, b_hi))


def _dot_nt(a, b, precision=None):
    return lax.dot_general(a, b, (((1,), (1,)), ((), ())), preferred_element_type=F32, precision=precision)


def _norm_mm_kernel(x_ref, g_ref, w_ref, o_ref, xn_ref):
    @pl.when(pl.program_id(1) == 0)
    def _():
        x = x_ref[...]
        ms = jnp.mean(x * x, axis=-1, keepdims=True)
        xn_ref[...] = (x * lax.rsqrt(ms + EPS) * g_ref[...]).astype(BF16)

    o_ref[...] = _dot(xn_ref[...], w_ref[...])


def _norm_matmul(x, g, w, layer, tm, tn=512):
    M, K = x.shape
    N = w.shape[2]
    return pl.pallas_call(
        _norm_mm_kernel,
        grid=(M // tm, N // tn),
        in_specs=[
            pl.BlockSpec((tm, K), lambda i, j: (i, 0)),
            pl.BlockSpec((1, K), lambda i, j: (0, 0)),
            pl.BlockSpec((None, K, tn), lambda i, j: (layer, 0, j)),
        ],
        out_specs=pl.BlockSpec((tm, tn), lambda i, j: (i, j)),
        out_shape=jax.ShapeDtypeStruct((M, N), F32),
        scratch_shapes=[pltpu.VMEM((tm, K), BF16)],
        compiler_params=_cparams(("parallel", "arbitrary")),
        name="norm_matmul",
    )(x, g, w)


def _norm_swiglu_kernel(x_ref, g_ref, wg_ref, wu_ref, o_ref, xn_ref):
    @pl.when(pl.program_id(1) == 0)
    def _():
        x = x_ref[...]
        ms = jnp.mean(x * x, axis=-1, keepdims=True)
        xn_ref[...] = (x * lax.rsqrt(ms + EPS) * g_ref[...]).astype(BF16)

    xn = xn_ref[...]
    gate = _dot(xn, wg_ref[...])
    up = _dot(xn, wu_ref[...])
    o_ref[...] = (gate * jax.nn.sigmoid(gate) * up).astype(o_ref.dtype)


def _norm_swiglu(x, g, w, layer, tm, tn=512):
    M, K = x.shape
    nj = D_FF // tn
    return pl.pallas_call(
        _norm_swiglu_kernel,
        grid=(M // tm, nj),
        in_specs=[
            pl.BlockSpec((tm, K), lambda i, j: (i, 0)),
            pl.BlockSpec((1, K), lambda i, j: (0, 0)),
            pl.BlockSpec((None, K, tn), lambda i, j: (layer, 0, j)),
            pl.BlockSpec((None, K, tn), lambda i, j: (layer, 0, j + nj)),
        ],
        out_specs=pl.BlockSpec((tm, tn), lambda i, j: (i, j)),
        out_shape=jax.ShapeDtypeStruct((M, D_FF), BF16),
        scratch_shapes=[pltpu.VMEM((tm, K), BF16)],
        compiler_params=_cparams(("parallel", "arbitrary")),
        name="norm_swiglu",
    )(x, g, w, w)


def _mm_res_kernel(a_ref, w_ref, r_ref, o_ref):
    o_ref[...] = r_ref[...] + _dot(a_ref[...], w_ref[...])


def _matmul_residual(a, w, res, layer, tm, tn=512):
    M, K = a.shape
    N = w.shape[2]
    return pl.pallas_call(
        _mm_res_kernel,
        grid=(M // tm, N // tn),
        in_specs=[
            pl.BlockSpec((tm, K), lambda i, j: (i, 0)),
            pl.BlockSpec((None, K, tn), lambda i, j: (layer, 0, j)),
            pl.BlockSpec((tm, tn), lambda i, j: (i, j)),
        ],
        out_specs=pl.BlockSpec((tm, tn), lambda i, j: (i, j)),
        out_shape=jax.ShapeDtypeStruct((M, N), F32),
        compiler_params=_cparams(("parallel", "arbitrary")),
        name="matmul_residual",
    )(a, w, res)


def _post_kernel(*refs, emit_cmp, tm):
    if emit_cmp:
        x_ref, c1_ref, s1_ref, c2_ref, s2_ref, prm_ref, wc_ref = refs[:7]
        outs = refs[7:]
    else:
        x_ref, c1_ref, s1_ref, c2_ref, s2_ref, prm_ref = refs[:6]
        wc_ref = None
        outs = refs[6:]
    (nsa_ref, win_ref, dsa_ref, ik_ref, dif_ref, qa_ref, ksel_ref, vsel_ref, kw_ref, vw_ref,
     qb_ref, kb_ref, vb_ref, iq_ref, ikd_ref, iw_ref, qc_ref, kcd_ref, vcd_ref, gat_ref) = outs[:20]

    c1, s1, c2, s2 = c1_ref[...], s1_ref[...], c2_ref[...], s2_ref[...]
    prm = prm_ref[...]
    lane = lax.broadcasted_iota(I32, (tm, 128), 1)
    lo = lane < 64
    inner = (lane & 63) < 32

    def col(a):
        return x_ref[:, a:a + 128]

    def gain(r):
        return prm[r:r + 1, :]

    def rms128(v, g):
        return v * lax.rsqrt(jnp.mean(v * v, axis=-1, keepdims=True) + EPS) * g

    def rope128(v):
        return v * c1 + pltpu.roll(v, 64, 1) * s1

    def half_ms(v):
        sq = v * v
        a = jnp.sum(jnp.where(lo, sq, 0.0), axis=-1, keepdims=True)
        b = jnp.sum(jnp.where(lo, 0.0, sq), axis=-1, keepdims=True)
        return a * (1.0 / 64), b * (1.0 / 64)

    def rms64(v, g):
        a, b = half_ms(v)
        return v * lax.rsqrt(jnp.where(lo, a, b) + EPS) * g

    def rope64(v):
        rot = jnp.where(inner, pltpu.roll(v, 96, 1), pltpu.roll(v, 32, 1))
        return v * c2 + rot * s2

    for h in range(A_HEADS):
        qa_ref[h] = (rope128(rms128(col(OFF_AQ + h * 128), gain(0))) * SCALE_D).astype(qa_ref.dtype)
    for kv in range(A_KV):
        o = kv * 128
        kc = rms128(col(OFF_AKV + o), gain(1))
        vc = col(OFF_AKV + 256 + o)
        ks = rope128(rms128(col(OFF_AKV + 512 + o), gain(2)))
        vs = col(OFF_AKV + 768 + o)
        kw = rope128(rms128(col(OFF_AKV + 1024 + o), gain(3)))
        vw = col(OFF_AKV + 1280 + o)
        for slab, val in ((kv, kc), (2 + kv, vc), (4 + kv, ks), (6 + kv, vs)):
            nsa_ref[pl.ds(slab, tm, stride=8), :] = val
        win_ref[pl.ds(kv, tm, stride=4), :] = kw
        win_ref[pl.ds(2 + kv, tm, stride=4), :] = vw
        ksel_ref[kv] = ks.astype(ksel_ref.dtype)
        vsel_ref[kv] = vs.astype(vsel_ref.dtype)
        kw_ref[kv] = kw.astype(kw_ref.dtype)
        vw_ref[kv] = vw.astype(vw_ref.dtype)
        if emit_cmp:
            pa_ref, pb_ref = outs[20], outs[21]
            nch = tm // CMP_STRIDE
            pk = _dot3(wc_ref[0], kc)
            pv = _dot3(wc_ref[1], vc)
            pa_ref[kv] = pk[0:nch]
            pb_ref[kv] = pk[nch:2 * nch]
            pa_ref[2 + kv] = pv[0:nch]
            pb_ref[2 + kv] = pv[nch:2 * nch]
    for h in range(B_HEADS):
        qb_ref[h] = (rope128(rms128(col(OFF_BQ + h * 128), gain(4))) * SCALE_D).astype(qb_ref.dtype)
    kb = rope128(rms128(col(OFF_BKV), gain(5)))
    vb = col(OFF_BKV + 128)
    dsa_ref[pl.ds(0, tm, stride=2), :] = kb
    dsa_ref[pl.ds(1, tm, stride=2), :] = vb
    kb_ref[...] = kb.astype(kb_ref.dtype)
    vb_ref[...] = vb.astype(vb_ref.dtype)
    for p in range(IDX_HEADS // 2):
        v = rope64(col(OFF_BIQ + p * 128))
        iq_ref[2 * p] = jnp.where(lo, v, 0.0).astype(iq_ref.dtype)
        iq_ref[2 * p + 1] = jnp.where(lo, 0.0, v).astype(iq_ref.dtype)
    for h in range(C_HEADS):
        v = rope64(rms64(col(OFF_CQ + h * 128), gain(7))) * SCALE_C
        kv, g = h // 2, h % 2
        qc_ref[kv, 2 * g] = jnp.where(lo, v, 0.0).astype(qc_ref.dtype)
        qc_ref[kv, 2 * g + 1] = jnp.where(lo, 0.0, v).astype(qc_ref.dtype)
    for kv in range(C_KV):
        o = kv * 128
        kk = rope64(rms64(col(OFF_CKV + o), gain(8)))
        vv = col(OFF_CKV + 256 + o)
        dif_ref[pl.ds(kv, tm, stride=4), :] = kk
        dif_ref[pl.ds(2 + kv, tm, stride=4), :] = vv
        kcd_ref[kv] = kk.astype(kcd_ref.dtype)
        vcd_ref[kv] = vv.astype(vcd_ref.dtype)
    m = col(OFF_MISC)
    a, _ = half_ms(m)
    ikr = rope64(m * lax.rsqrt(a + EPS) * gain(6))
    ik_ref[...] = ikr[:, 0:64]
    ikd_ref[...] = jnp.where(lo, ikr, pltpu.roll(ikr, 64, 1)).astype(ikd_ref.dtype)
    sig = jax.nn.sigmoid(m)
    for kv in range(A_KV):
        gat_ref[kv] = pltpu.roll(sig, 128 - MISC_GATE - 12 * kv, 1)
    iw_ref[...] = pltpu.roll(m, 128 - MISC_IW, 1) * ((IDX_DIM ** -0.5) * (IDX_HEADS ** -0.5))


def _post_project(proj, tabs, prm, wc, tm, qdt):
    B, T, _ = proj.shape
    emit_cmp = wc is not None
    nt = T // tm

    def row(c):
        return pl.BlockSpec((None, tm, c), lambda b, i: (b, i, 0))

    def heads(*lead):
        n = len(lead)
        return pl.BlockSpec((None,) + lead + (tm, 128), lambda b, i: (b,) + (0,) * n + (i, 0))

    tab = pl.BlockSpec((tm, 128), lambda b, i: (i, 0))
    in_specs = [row(N_PROJ), tab, tab, tab, tab, pl.BlockSpec((16, 128), lambda b, i: (0, 0))]
    args = [proj, *tabs, prm]
    if emit_cmp:
        in_specs.append(pl.BlockSpec((2, 2 * tm // CMP_STRIDE, tm), lambda b, i: (0, 0, 0)))
        args.append(wc)

    def sds(shape, dt):
        return jax.ShapeDtypeStruct(shape, dt)

    def slabs(n):
        return pl.BlockSpec((None, tm * n, 128), lambda b, i: (b, i, 0))

    out_shape = [
        sds((B, T * 8, 128), F32), sds((B, T * 4, 128), F32), sds((B, T * 2, 128), F32), sds((B, T, 64), F32),
        sds((B, T * 4, 128), F32),
        sds((B, A_HEADS, T, 128), qdt),
        sds((B, A_KV, T, 128), qdt), sds((B, A_KV, T, 128), qdt),
        sds((B, A_KV, T, 128), qdt), sds((B, A_KV, T, 128), qdt),
        sds((B, B_HEADS, T, 128), qdt), sds((B, T, 128), qdt), sds((B, T, 128), qdt),
        sds((B, IDX_HEADS, T, 128), qdt), sds((B, T, 128), qdt), sds((B, T, 128), F32),
        sds((B, C_KV, 4, T, 128), qdt), sds((B, C_KV, T, 128), qdt), sds((B, C_KV, T, 128), qdt),
        sds((B, A_KV, T, 128), F32),
    ]
    out_specs = [
        slabs(8), slabs(4), slabs(2), row(64), slabs(4),
        heads(A_HEADS), heads(A_KV), heads(A_KV), heads(A_KV), heads(A_KV),
        heads(B_HEADS), row(128), row(128), heads(IDX_HEADS), row(128), row(128),
        heads(C_KV, 4), heads(C_KV), heads(C_KV), heads(A_KV),
    ]
    if emit_cmp:
        nc = T // CMP_STRIDE
        out_shape += [sds((B, 4, nc, 128), F32), sds((B, 4, nc, 128), F32)]
        spec = pl.BlockSpec((None, 4, tm // CMP_STRIDE, 128), lambda b, i: (b, 0, i, 0))
        out_specs += [spec, spec]
    return pl.pallas_call(
        functools.partial(_post_kernel, emit_cmp=emit_cmp, tm=tm),
        grid=(B, nt),
        in_specs=in_specs,
        out_specs=out_specs,
        out_shape=out_shape,
        compiler_params=_cparams(("parallel", "parallel")),
        name="post_project",
    )(*args)


def _bias(mask):
    return jnp.where(mask, 0.0, MASKED)


def _masked_softmax(s, bias):
    s = s + bias
    m = jnp.maximum(jnp.max(s, axis=-1, keepdims=True), NEG)
    e = jnp.exp2(s - m)
    return e / jnp.maximum(jnp.sum(e, axis=-1, keepdims=True), 1e-30)


def _online_step(s, bias, v, m, l, acc):
    if bias is not None:
        s = s + bias
    m_new = jnp.maximum(m, jnp.max(s, axis=-1, keepdims=True))
    alpha = jnp.exp2(m - m_new)
    e = jnp.exp2(s - m_new)
    l_new = alpha * l + jnp.sum(e, axis=-1, keepdims=True)
    rows = acc.shape[0]
    pv = _dot(e.reshape(rows, e.shape[-1]).astype(BF16), v)
    return m_new, l_new, alpha.reshape(rows, 1) * acc + pv


N_CHAINS = 1


def _flash_init(heads, tq):
    hc = heads // N_CHAINS
    return tuple((jnp.full((hc, tq, 1), NEG, F32), jnp.zeros((hc, tq, 1), F32), jnp.zeros((hc * tq, 128), F32))
                 for _ in range(N_CHAINS))


def _flash_tile(q, k, v, bias, carry):
    out = []
    for c, (m, l, acc) in enumerate(carry):
        rows = acc.shape[0]
        s = _dot_nt(q[c * rows:(c + 1) * rows], k).reshape(m.shape[0], m.shape[1], k.shape[0])
        out.append(_online_step(s, bias, v, m, l, acc))
    return tuple(out)


def _flash_finish(carry):
    return jnp.concatenate([acc / jnp.maximum(l.reshape(acc.shape[0], 1), 1e-30) for _, l, acc in carry], axis=0)


def _online_single(s, bias, v_row, m, l, acc):
    if bias is not None:
        s = s + bias
    m_new = jnp.maximum(m, s)
    alpha = jnp.exp2(m - m_new)
    e = jnp.exp2(s - m_new)
    return m_new, alpha * l + e, alpha * acc + e.astype(BF16).astype(F32) * v_row


def _rowdot(q, k_row):
    return jnp.sum(q.astype(BF16).astype(F32) * k_row.astype(BF16).astype(F32), axis=-1, keepdims=True)


def _sortable(x):
    b = lax.bitcast_convert_type(x + 0.0, I32)
    return jnp.where(b < 0, b ^ jnp.int32(0x7FFFFFFF), b)


def _lambda_of(lp, lam_init):
    a = jnp.sum(lp[0:1] * lp[1:2], axis=-1, keepdims=True)
    b = jnp.sum(lp[2:3] * lp[3:4], axis=-1, keepdims=True)
    return jnp.exp(a) - jnp.exp(b) + lam_init


def _nsa_prompt_kernel(q_ref, pak_ref, pbk_ref, pav_ref, pbv_ref, cc_ref, sc_ref,
                       ks_ref, vs_ref, kw_ref, vw_ref, g_ref, o_ref, kc_ref, vc_ref, *, T):
    qi = pl.program_id(2)
    nc = T // CMP_STRIDE
    ns = T // SEL_BLOCK
    R = A_G * TQ
    band = WINDOW + TQ

    @pl.when(qi == 0)
    def _():
        kraw = pak_ref[...] + pltpu.roll(pbk_ref[...], nc - 1, 0)
        kc_ref[...] = (kraw * cc_ref[...] + pltpu.roll(kraw, 64, 1) * sc_ref[...]).astype(BF16)
        vc_ref[...] = (pav_ref[...] + pltpu.roll(pbv_ref[...], nc - 1, 0)).astype(BF16)

    q = q_ref[...].reshape(R, 128)
    t0 = qi * TQ
    qp3 = t0 + lax.broadcasted_iota(I32, (1, TQ, 1), 1)

    s_c = _dot_nt(q, kc_ref[...]).reshape(A_G, TQ, nc)
    cend = lax.broadcasted_iota(I32, (1, TQ, nc), 2) * CMP_STRIDE + (CMP_LEN - 1)
    p_c = _masked_softmax(s_c, _bias(cend <= qp3))
    o_c = _dot(p_c.reshape(R, nc).astype(BF16), vc_ref[...])

    sj = lax.broadcasted_iota(I32, (ns, nc), 0) * SEL_BLOCK
    ci = lax.broadcasted_iota(I32, (ns, nc), 1) * CMP_STRIDE
    overlap_t = ((ci < sj + SEL_BLOCK) & (ci + CMP_LEN > sj)).astype(F32)
    imp = _dot_nt(overlap_t, jnp.sum(p_c, axis=0), HI)
    jidx = lax.broadcasted_iota(I32, (ns, TQ), 0)
    jq = (t0 + lax.broadcasted_iota(I32, (1, TQ), 1)) >> SEL_SHIFT
    forced = (jidx == 0) | (jidx == jq) | (jidx == jq - 1)
    imp = jnp.where(forced, imp + FORCE_BONUS, imp)
    imp = jnp.where(jidx > jq, NEG, imp)
    ng = ns // 8
    sub = lax.broadcasted_iota(I32, (8, TQ), 0)
    imp_g = [imp[8 * g:8 * g + 8, :] for g in range(ng)]
    rank_g = [jnp.zeros((8, TQ), F32) for _ in range(ng)]
    for j in range(ns):
        rj = jnp.broadcast_to(imp[j:j + 1, :], (8, TQ))
        for g in range(ng):
            if g < j // 8:
                ahead = rj > imp_g[g]
            elif g > j // 8:
                ahead = rj >= imp_g[g]
            else:
                ahead = (rj > imp_g[g]) | ((rj == imp_g[g]) & (sub > j % 8))
            rank_g[g] = rank_g[g] + jnp.where(ahead, 1.0, 0.0)
    rank = jnp.concatenate(rank_g, axis=0)
    selb = jnp.where(rank < min(SEL_TOPN, ns), 1.0, 0.0).T.astype(BF16)

    erow = lax.broadcasted_iota(I32, (ns, TK), 0)
    ecol = lax.broadcasted_iota(I32, (ns, TK), 1)
    tcol = lax.broadcasted_iota(I32, (1, TQ, TK), 2)

    def sel_step(kt, carry):
        base = pl.multiple_of(kt * TK, TK)
        expand = (erow == ((ecol + base) >> SEL_SHIFT)).astype(BF16)
        chosen = _dot(selb, expand).reshape(1, TQ, TK) > 0.5
        bias = _bias(chosen & (tcol + base <= qp3))
        return _flash_tile(q, ks_ref[pl.ds(base, TK), :], vs_ref[pl.ds(base, TK), :], bias, carry)

    nkt = (t0 + TQ + TK - 1) // TK
    o_s = _flash_finish(lax.fori_loop(0, nkt, sel_step, _flash_init(A_G, TQ)))

    start = pl.multiple_of(jnp.maximum(t0 - WINDOW, 0), TQ)
    kwin = kw_ref[pl.ds(start, band), :]
    vwin = vw_ref[pl.ds(start, band), :]
    dist = qp3 - (start + lax.broadcasted_iota(I32, (1, TQ, band), 2))
    s_w = _dot_nt(q, kwin).reshape(A_G, TQ, band)
    p_w = _masked_softmax(s_w, _bias((dist >= 0) & (dist <= WINDOW)))
    o_w = _dot(p_w.reshape(R, band).astype(BF16), vwin)

    g = g_ref[...]
    for h in range(A_G):
        r = slice(h * TQ, (h + 1) * TQ)
        o = g[:, 3 * h:3 * h + 1] * o_c[r] + g[:, 3 * h + 1:3 * h + 2] * o_s[r] + g[:, 3 * h + 2:3 * h + 3] * o_w[r]
        o_ref[:, h * 128:(h + 1) * 128] = o.astype(o_ref.dtype)


def _nsa_prompt(qa, pa, pb, cc, sc, ksel, vsel, kw, vw, gat):
    B, _, T, _ = qa.shape
    nc = T // CMP_STRIDE
    part_k = pl.BlockSpec((None, None, nc, 128), lambda b, kv, i: (b, kv, 0, 0))
    part_v = pl.BlockSpec((None, None, nc, 128), lambda b, kv, i: (b, 2 + kv, 0, 0))
    tabc = pl.BlockSpec((nc, 128), lambda b, kv, i: (0, 0))
    full = pl.BlockSpec((None, None, T, 128), lambda b, kv, i: (b, kv, 0, 0))
    return pl.pallas_call(
        functools.partial(_nsa_prompt_kernel, T=T),
        grid=(B, A_KV, T // TQ),
        in_specs=[
            pl.BlockSpec((None, A_G, TQ, 128), lambda b, kv, i: (b, kv, i, 0)),
            part_k, part_k, part_v, part_v, tabc, tabc, full, full, full, full,
            pl.BlockSpec((None, None, TQ, 128), lambda b, kv, i: (b, kv, i, 0)),
        ],
        out_specs=pl.BlockSpec((None, TQ, A_G * 128), lambda b, kv, i: (b, i, kv)),
        out_shape=jax.ShapeDtypeStruct((B, T, A_HEADS * 128), BF16),
        scratch_shapes=[pltpu.VMEM((nc, 128), BF16), pltpu.VMEM((nc, 128), BF16)],
        compiler_params=_cparams(("parallel", "parallel", "arbitrary")),
        name="nsa_prompt",
    )(qa, pa, pb, pa, pb, cc, sc, ksel, vsel, kw, vw, gat)


def _kth_threshold(count_ge, shape, k):
    def step(it, t):
        cand = t + jnp.left_shift(jnp.int32(1), 31 - it)
        return jnp.where(count_ge(cand) >= k, cand, t)

    return lax.fori_loop(0, 32, step, jnp.full(shape, INT_MIN, I32))


def _dsa_prompt_kernel(q_ref, iq_ref, iw_ref, ik_ref, k_ref, v_ref, o_ref, key_ref, *, T):
    qi = pl.program_id(1)
    t0 = qi * TQ
    nkt = (t0 + TQ + TK - 1) // TK
    k_top = min(IDX_TOPK, T // 4)
    qp_l = t0 + lax.broadcasted_iota(I32, (1, TQ), 1)
    trow = lax.broadcasted_iota(I32, (TK, TQ), 0)
    iq = iq_ref[...].reshape(IDX_HEADS * TQ, 128)
    iw_t = iw_ref[...].T

    def score_step(kt, _):
        base = pl.multiple_of(kt * TK, TK)
        s = _dot_nt(ik_ref[pl.ds(base, TK), :], iq)
        sc = jnp.zeros((TK, TQ), F32)
        for h in range(IDX_HEADS):
            sc = sc + iw_t[h:h + 1, :] * jnp.maximum(s[:, h * TQ:(h + 1) * TQ], 0.0)
        sc = jnp.where(trow + base <= qp_l, sc, NEG)
        key_ref[pl.ds(base, TK), :] = _sortable(sc)
        return 0

    lax.fori_loop(0, nkt, score_step, 0)

    def count_ge(cand):
        def cstep(kt, c):
            base = pl.multiple_of(kt * TK, TK)
            hit = jnp.where(key_ref[pl.ds(base, TK), :] >= cand, 1.0, 0.0)
            return c + jnp.sum(hit.reshape(TK // 64, 8, 8, TQ), axis=0)

        c = lax.fori_loop(0, nkt, cstep, jnp.zeros((8, 8, TQ), F32))
        return jnp.sum(jnp.sum(c, axis=0), axis=0, keepdims=True)

    thr = _kth_threshold(count_ge, (1, TQ), float(k_top))

    q = q_ref[...].reshape(B_HEADS * TQ, 128)

    def att_step(kt, carry):
        base = pl.multiple_of(kt * TK, TK)
        keep = (key_ref[pl.ds(base, TK), :] >= thr) & (trow + base <= qp_l)
        bias = _bias(keep).T.reshape(1, TQ, TK)
        return _flash_tile(q, k_ref[pl.ds(base, TK), :], v_ref[pl.ds(base, TK), :], bias, carry)

    o = _flash_finish(lax.fori_loop(0, nkt, att_step, _flash_init(B_HEADS, TQ)))
    for h in range(B_HEADS):
        o_ref[:, h * 128:(h + 1) * 128] = o[h * TQ:(h + 1) * TQ].astype(o_ref.dtype)


def _dsa_prompt(qb, iq, iw, ikd, kb, vb):
    B, _, T, _ = qb.shape
    full = pl.BlockSpec((None, T, 128), lambda b, i: (b, 0, 0))
    return pl.pallas_call(
        functools.partial(_dsa_prompt_kernel, T=T),
        grid=(B, T // TQ),
        in_specs=[
            pl.BlockSpec((None, B_HEADS, TQ, 128), lambda b, i: (b, 0, i, 0)),
            pl.BlockSpec((None, IDX_HEADS, TQ, 128), lambda b, i: (b, 0, i, 0)),
            pl.BlockSpec((None, TQ, 128), lambda b, i: (b, i, 0)),
            full, full, full,
        ],
        out_specs=pl.BlockSpec((None, TQ, B_HEADS * 128), lambda b, i: (b, i, 0)),
        out_shape=jax.ShapeDtypeStruct((B, T, B_HEADS * 128), BF16),
        scratch_shapes=[pltpu.VMEM((T, TQ), I32)],
        compiler_params=_cparams(("parallel", "arbitrary")),
        name="dsa_prompt",
    )(qb, iq, iw, ikd, kb, vb)


def _diff_finish(o, lam, subln, lam_init, rows):
    outs = []
    for g in range(2):
        a0 = o[(2 * g) * rows:(2 * g + 1) * rows]
        a1 = o[(2 * g + 1) * rows:(2 * g + 2) * rows]
        d = a0 - lam * a1
        d = d * lax.rsqrt(jnp.mean(d * d, axis=-1, keepdims=True) + EPS) * subln
        outs.append(d * (1.0 - lam_init))
    return outs


def _diff_prompt_kernel(q_ref, k_ref, v_ref, lp_ref, sub_ref, o_ref, *, lam_init):
    qi = pl.program_id(2)
    t0 = qi * TQ
    nkt = (t0 + TQ + TK - 1) // TK
    R = 4 * TQ
    qp = t0 + lax.broadcasted_iota(I32, (1, TQ, 1), 1)
    tcol = lax.broadcasted_iota(I32, (1, TQ, TK), 2)
    q = q_ref[...].reshape(R, 128)

    def step(kt, carry, causal):
        base = pl.multiple_of(kt * TK, TK)
        bias = _bias(tcol + base <= qp) if causal else None
        return _flash_tile(q, k_ref[pl.ds(base, TK), :], v_ref[pl.ds(base, TK), :], bias, carry)

    carry = lax.fori_loop(0, nkt - 1, functools.partial(step, causal=False), _flash_init(4, TQ))
    o = _flash_finish(step(nkt - 1, carry, True))
    lam = _lambda_of(lp_ref[...], lam_init)
    outs = _diff_finish(o, lam, sub_ref[...], lam_init, TQ)
    for g in range(2):
        o_ref[:, g * 128:(g + 1) * 128] = outs[g].astype(o_ref.dtype)


def _diff_prompt(qc, kcd, vcd, lp, subln, lam_init):
    B, _, _, T, _ = qc.shape
    full = pl.BlockSpec((None, None, T, 128), lambda b, kv, i: (b, kv, 0, 0))
    return pl.pallas_call(
        functools.partial(_diff_prompt_kernel, lam_init=lam_init),
        grid=(B, C_KV, T // TQ),
        in_specs=[
            pl.BlockSpec((None, None, 4, TQ, 128), lambda b, kv, i: (b, kv, 0, i, 0)),
            full, full,
            pl.BlockSpec((4, C_HALF), lambda b, kv, i: (0, 0)),
            pl.BlockSpec((1, 128), lambda b, kv, i: (0, 0)),
        ],
        out_specs=pl.BlockSpec((None, TQ, 256), lambda b, kv, i: (b, i, kv)),
        out_shape=jax.ShapeDtypeStruct((B, T, C_HEADS * 128), BF16),
        compiler_params=_cparams(("parallel", "parallel", "arbitrary")),
        name="diff_prompt",
    )(qc, kcd, vcd, lp, subln)


def _page_specs(block, layer, n_lead_zero):
    specs = []
    for u in range(PAGES_PER_STEP):
        def imap(b, pc, pt, u=u):
            return (layer, pt[b, pc * PAGES_PER_STEP + u]) + (0,) * n_lead_zero
        specs.append(pl.BlockSpec(block, imap))
    return specs


def _nsa_cmp_decode_kernel(pt_ref, *refs, past):
    pages = refs[:PAGES_PER_STEP]
    q_ref, new_ref, wc_ref, cc_ref, sc_ref, oc_ref, sel_ref, a_ref, b_ref = refs[PAGES_PER_STEP:]
    pc = pl.program_id(1)
    nc = past // CMP_STRIDE
    ns = past // SEL_BLOCK + 1
    nsp = ((ns + 127) // 128) * 128
    cpp = PAGE_SIZE // CMP_STRIDE

    def slab(ref, s):
        return ref[pl.ds(s, PAGE_SIZE, stride=8), :]

    for u in range(PAGES_PER_STEP):
        r0 = pl.multiple_of((pc * PAGES_PER_STEP + u) * cpp, cpp)
        for c in range(2):
            x = jnp.concatenate([slab(pages[u], 2 * c), slab(pages[u], 2 * c + 1)], axis=1)
            part = _dot3(wc_ref[c], x)
            a_ref[pl.ds(r0, cpp), 256 * c:256 * (c + 1)] = part[0:cpp]
            b_ref[pl.ds(r0, cpp), 256 * c:256 * (c + 1)] = part[cpp:2 * cpp]

    @pl.when(pc == pl.num_programs(1) - 1)
    def _():
        rowi = lax.broadcasted_iota(I32, (nc, 128), 0)
        cend = lax.broadcasted_iota(I32, (A_G, nc), 1) * CMP_STRIDE + (CMP_LEN - 1)
        ci = lax.broadcasted_iota(I32, (nc, nsp), 0) * CMP_STRIDE
        sj = lax.broadcasted_iota(I32, (nc, nsp), 1) * SEL_BLOCK
        overlap = ((ci < sj + SEL_BLOCK) & (ci + CMP_LEN > sj)).astype(F32)
        jrow = lax.broadcasted_iota(I32, (1, nsp), 1)
        jq = past // SEL_BLOCK
        ii = lax.broadcasted_iota(I32, (nsp, nsp), 0)
        jj = lax.broadcasted_iota(I32, (nsp, nsp), 1)
        rr = lax.broadcasted_iota(I32, (SEL_TOPN, nsp), 0)
        jr = lax.broadcasted_iota(I32, (SEL_TOPN, nsp), 1).astype(F32)
        new = new_ref[...]
        q = q_ref[...].astype(BF16)
        for kv in range(A_KV):
            ko, vo = kv * 128, 256 + kv * 128
            bk = jnp.where(rowi == nc - 1, wc_ref[0][cpp:cpp + 1, 0:1] * new[:, ko:ko + 128],
                           pltpu.roll(b_ref[:, ko:ko + 128], nc - 1, 0))
            bv = jnp.where(rowi == nc - 1, wc_ref[1][cpp:cpp + 1, 0:1] * new[:, vo:vo + 128],
                           pltpu.roll(b_ref[:, vo:vo + 128], nc - 1, 0))
            kraw = a_ref[:, ko:ko + 128] + bk
            kc = (kraw * cc_ref[...] + pltpu.roll(kraw, 64, 1) * sc_ref[...]).astype(BF16)
            vc = (a_ref[:, vo:vo + 128] + bv).astype(BF16)
            s_c = _dot_nt(q[kv * A_G:(kv + 1) * A_G], kc)
            p_c = _masked_softmax(s_c, _bias(cend <= past))
            oc_ref[kv * A_G:(kv + 1) * A_G, :] = _dot(p_c.astype(BF16), vc)
            imp = _dot(jnp.sum(p_c, axis=0, keepdims=True), overlap, HI)
            forced = (jrow == 0) | (jrow == jq) | (jrow == jq - 1)
            imp = jnp.where(forced, imp + FORCE_BONUS, imp)
            imp = jnp.where(jrow > jq, NEG, imp)
            imp_col = jnp.sum(jnp.where(ii == jj, jnp.broadcast_to(imp, (nsp, nsp)), 0.0), axis=1, keepdims=True)
            beats = (imp_col > imp) | ((imp_col == imp) & (ii < jj))
            rank = jnp.sum(jnp.where(beats, 1.0, 0.0), axis=0, keepdims=True)
            pick = jnp.sum(jnp.where(rank == rr.astype(F32), jr, 0.0), axis=1, keepdims=True)
            sel_ref[kv] = pick.astype(I32)


def _nsa_cmp_decode(page_table, cache, layer, q, new_rows, wc, cc, sc, past):
    DB = q.shape[0]
    npc = page_table.shape[1] // PAGES_PER_STEP
    nc = past // CMP_STRIDE
    block = (None, None, PAGE_SIZE * 8, 128)
    const2 = lambda b, pc, pt: (0, 0)
    return pl.pallas_call(
        functools.partial(_nsa_cmp_decode_kernel, past=past),
        grid_spec=pltpu.PrefetchScalarGridSpec(
            num_scalar_prefetch=1,
            grid=(DB, npc),
            in_specs=_page_specs(block, layer, 2) + [
                pl.BlockSpec((None, A_HEADS, 128), lambda b, pc, pt: (b, 0, 0)),
                pl.BlockSpec((None, 1, 1024), lambda b, pc, pt: (b, 0, 0)),
                pl.BlockSpec((2, 2 * PAGE_SIZE // CMP_STRIDE, PAGE_SIZE), lambda b, pc, pt: (0, 0, 0)),
                pl.BlockSpec((nc, 128), const2),
                pl.BlockSpec((nc, 128), const2),
            ],
            out_specs=[
                pl.BlockSpec((None, A_HEADS, 128), lambda b, pc, pt: (b, 0, 0)),
                pl.BlockSpec((None, A_KV, SEL_TOPN, 1), lambda b, pc, pt: (b, 0, 0, 0)),
            ],
            scratch_shapes=[pltpu.VMEM((nc, 512), F32), pltpu.VMEM((nc, 512), F32)],
        ),
        out_shape=[jax.ShapeDtypeStruct((DB, A_HEADS, 128), F32),
                   jax.ShapeDtypeStruct((DB, A_KV, SEL_TOPN, 1), I32)],
        compiler_params=_cparams(("parallel", "arbitrary")),
        name="nsa_cmp_decode",
    )(page_table, *([cache] * PAGES_PER_STEP), q, new_rows, wc, cc, sc)


def _nsa_sel_decode_kernel(pt_ref, sel_ref, blk0_ref, blk1_ref, win_ref, q_ref, new_ref, neww_ref, oc_ref, g_ref,
                           o_ref, m_ref, l_ref, acc_ref, *, past, lw):
    b, s = pl.program_id(0), pl.program_id(1)
    ns = past // SEL_BLOCK + 1
    new = new_ref[...]
    neww = neww_ref[...]
    blks = (blk0_ref, blk1_ref)

    for kv in range(A_KV):
        q = q_ref[kv * A_G:(kv + 1) * A_G, :].astype(BF16)

        @pl.when(s == 0)
        def _():
            m_ref[kv] = _rowdot(q, new[:, 512 + kv * 128:640 + kv * 128])
            l_ref[kv] = jnp.ones((A_G, 1), F32)
            v_new = new[:, 768 + kv * 128:896 + kv * 128].astype(BF16).astype(F32)
            acc_ref[kv] = jnp.broadcast_to(v_new, (A_G, 128))

        @pl.when(sel_ref[b, kv, s] != ns - 1)
        def _():
            k = blks[kv][pl.ds(4 + kv, SEL_BLOCK, stride=8), :].astype(BF16)
            v = blks[kv][pl.ds(6 + kv, SEL_BLOCK, stride=8), :].astype(BF16)
            m, l, acc = _online_step(_dot_nt(q, k), None, v, m_ref[kv], l_ref[kv], acc_ref[kv])
            m_ref[kv] = m
            l_ref[kv] = l
            acc_ref[kv] = acc

        @pl.when(s == pl.num_programs(1) - 1)
        def _():
            o_s = acc_ref[kv] / jnp.maximum(l_ref[kv], 1e-30)
            kw = win_ref[pl.ds(kv, lw, stride=4), :].astype(BF16)
            vw = win_ref[pl.ds(2 + kv, lw, stride=4), :].astype(BF16)
            vw_new = neww[:, 256 + kv * 128:384 + kv * 128].astype(BF16).astype(F32)
            s_w = _dot_nt(q, kw)
            s_n = _rowdot(q, neww[:, kv * 128:(kv + 1) * 128])
            mw = jnp.maximum(jnp.max(s_w, axis=-1, keepdims=True), s_n)
            e_w = jnp.exp2(s_w - mw)
            e_n = jnp.exp2(s_n - mw)
            den = jnp.sum(e_w, axis=-1, keepdims=True) + e_n
            o_w = (_dot(e_w.astype(BF16), vw) + e_n.astype(BF16).astype(F32) * vw_new) / den
            o_c = oc_ref[kv * A_G:(kv + 1) * A_G, :]
            g = g_ref[kv * A_G:(kv + 1) * A_G, :]
            o_ref[kv * A_G:(kv + 1) * A_G, :] = g[:, 0:1] * o_c + g[:, 1:2] * o_s + g[:, 2:3] * o_w


def _nsa_sel_decode(page_table, sel, cache, win, layer, q, new_rows, new_win, o_c, gates, past):
    DB = q.shape[0]
    n_pages = page_table.shape[1]
    lw = win.shape[2] // 4

    def blk_map(kv):
        def imap(b, s, pt, sel):
            j = sel[b, kv, s]
            return (layer, pt[b, jnp.minimum(j // 2, n_pages - 1)], j % 2, 0)
        return imap

    per_b = lambda b, s, pt, sel: (b, 0, 0)
    return pl.pallas_call(
        functools.partial(_nsa_sel_decode_kernel, past=past, lw=lw),
        grid_spec=pltpu.PrefetchScalarGridSpec(
            num_scalar_prefetch=2,
            grid=(DB, SEL_TOPN),
            in_specs=[
                pl.BlockSpec((None, None, SEL_BLOCK * 8, 128), blk_map(0)),
                pl.BlockSpec((None, None, SEL_BLOCK * 8, 128), blk_map(1)),
                pl.BlockSpec((None, None, lw * 4, 128), lambda b, s, pt, sel: (layer, b, 0, 0)),
                pl.BlockSpec((None, A_HEADS, 128), per_b),
                pl.BlockSpec((None, 1, 1024), per_b),
                pl.BlockSpec((None, 1, 512), per_b),
                pl.BlockSpec((None, A_HEADS, 128), per_b),
                pl.BlockSpec((None, A_HEADS, 128), per_b),
            ],
            out_specs=pl.BlockSpec((None, A_HEADS, 128), per_b),
            scratch_shapes=[pltpu.VMEM((A_KV, A_G, 1), F32), pltpu.VMEM((A_KV, A_G, 1), F32),
                            pltpu.VMEM((A_KV, A_G, 128), F32)],
        ),
        out_shape=jax.ShapeDtypeStruct((DB, A_HEADS, 128), F32),
        compiler_params=_cparams(("parallel", "arbitrary")),
        name="nsa_sel_decode",
    )(page_table, sel, cache, cache, win, q, new_rows, new_win, o_c, gates)


def _dsa_idx_decode_kernel(pt_ref, *refs, past):
    pages = refs[:PAGES_PER_STEP]
    iq_ref, iw_ref, ikn_ref, mask_ref, sc_ref = refs[PAGES_PER_STEP:]
    pc = pl.program_id(1)
    n_pages = past // PAGE_SIZE
    k_top = min(IDX_TOPK, (past + 1) // 4)
    iq = iq_ref[...].astype(BF16)
    iw = iw_ref[...]

    @pl.when(pc == 0)
    def _():
        sc_ref[...] = jnp.full(sc_ref.shape, NEG, F32)

    for u in range(PAGES_PER_STEP):
        s = _dot(iq, pages[u][...].astype(BF16))
        sc_ref[pl.ds(pc * PAGES_PER_STEP + u, 1), :] = jnp.sum(iw * jnp.maximum(s, 0.0), axis=0, keepdims=True)

    @pl.when(pc == pl.num_programs(1) - 1)
    def _():
        s_new = _rowdot(iq, ikn_ref[...])
        s_new = jnp.sum(iw * jnp.maximum(s_new, 0.0), axis=0, keepdims=True)
        lane0 = lax.broadcasted_iota(I32, (1, PAGE_SIZE), 1) == 0
        sc_ref[n_pages:n_pages + 1, :] = jnp.where(lane0, s_new, NEG)
        keys = _sortable(sc_ref[...])

        def count_ge(cand):
            c = jnp.sum(jnp.where(keys >= cand, 1.0, 0.0), axis=-1, keepdims=True)
            return jnp.sum(c, axis=0, keepdims=True)

        thr = _kth_threshold(count_ge, (1, 1), float(k_top))
        mask_ref[...] = _bias((keys >= thr) & (sc_ref[...] > 0.5 * NEG))


def _dsa_idx_decode(page_table, cache, layer, iq, iw, ik_new, past):
    DB = iq.shape[0]
    n_pages = page_table.shape[1]
    npc = n_pages // PAGES_PER_STEP
    rows = ((n_pages + 1 + 7) // 8) * 8
    per_b = lambda b, pc, pt: (b, 0, 0)
    return pl.pallas_call(
        functools.partial(_dsa_idx_decode_kernel, past=past),
        grid_spec=pltpu.PrefetchScalarGridSpec(
            num_scalar_prefetch=1,
            grid=(DB, npc),
            in_specs=_page_specs((None, None, IDX_DIM, PAGE_SIZE), layer, 2) + [
                pl.BlockSpec((None, IDX_HEADS, IDX_DIM), per_b),
                pl.BlockSpec((None, IDX_HEADS, 1), per_b),
                pl.BlockSpec((None, 1, IDX_DIM), per_b),
            ],
            out_specs=pl.BlockSpec((None, rows, PAGE_SIZE), per_b),
            scratch_shapes=[pltpu.VMEM((rows, PAGE_SIZE), F32)],
        ),
        out_shape=jax.ShapeDtypeStruct((DB, rows, PAGE_SIZE), F32),
        compiler_params=_cparams(("parallel", "arbitrary")),
        name="dsa_idx_decode",
    )(page_table, *([cache] * PAGES_PER_STEP), iq, iw, ik_new)


def _dsa_att_decode_kernel(pt_ref, *refs, past):
    pages = refs[:PAGES_PER_STEP]
    q_ref, new_ref, mask_ref, o_ref, m_ref, l_ref, acc_ref = refs[PAGES_PER_STEP:]
    pc = pl.program_id(1)
    n_pages = past // PAGE_SIZE
    q = q_ref[...].astype(BF16)

    @pl.when(pc == 0)
    def _():
        m_ref[...] = jnp.full(m_ref.shape, NEG, F32)
        l_ref[...] = jnp.zeros(l_ref.shape, F32)
        acc_ref[...] = jnp.zeros(acc_ref.shape, F32)

    k = jnp.concatenate([p[pl.ds(0, PAGE_SIZE, stride=2), :] for p in pages], axis=0).astype(BF16)
    v = jnp.concatenate([p[pl.ds(1, PAGE_SIZE, stride=2), :] for p in pages], axis=0).astype(BF16)
    r0 = pl.multiple_of(pc * PAGES_PER_STEP, PAGES_PER_STEP)
    mrows = mask_ref[pl.ds(r0, PAGES_PER_STEP), :]
    bias = jnp.concatenate([mrows[u:u + 1, :] for u in range(PAGES_PER_STEP)], axis=1)
    m, l, acc = _online_step(_dot_nt(q, k), bias, v, m_ref[...], l_ref[...], acc_ref[...])
    m_ref[...] = m
    l_ref[...] = l
    acc_ref[...] = acc

    @pl.when(pc == pl.num_programs(1) - 1)
    def _():
        new = new_ref[...]
        v_new = new[:, 128:256].astype(BF16).astype(F32)
        keep = mask_ref[n_pages:n_pages + 1, 0:1]
        _, l2, acc2 = _online_single(_rowdot(q, new[:, 0:128]), keep, v_new, m_ref[...], l_ref[...], acc_ref[...])
        o_ref[...] = acc2 / jnp.maximum(l2, 1e-30)


def _dsa_att_decode(page_table, cache, layer, q, new_rows, mask, past):
    DB = q.shape[0]
    npc = page_table.shape[1] // PAGES_PER_STEP
    rows = mask.shape[1]
    per_b = lambda b, pc, pt: (b, 0, 0)
    return pl.pallas_call(
        functools.partial(_dsa_att_decode_kernel, past=past),
        grid_spec=pltpu.PrefetchScalarGridSpec(
            num_scalar_prefetch=1,
            grid=(DB, npc),
            in_specs=_page_specs((None, None, PAGE_SIZE * 2, 128), layer, 2) + [
                pl.BlockSpec((None, B_HEADS, 128), per_b),
                pl.BlockSpec((None, 1, 256), per_b),
                pl.BlockSpec((None, rows, PAGE_SIZE), per_b),
            ],
            out_specs=pl.BlockSpec((None, B_HEADS, 128), per_b),
            scratch_shapes=[pltpu.VMEM((B_HEADS, 1), F32), pltpu.VMEM((B_HEADS, 1), F32),
                            pltpu.VMEM((B_HEADS, 128), F32)],
        ),
        out_shape=jax.ShapeDtypeStruct((DB, B_HEADS, 128), F32),
        compiler_params=_cparams(("parallel", "arbitrary")),
        name="dsa_att_decode",
    )(page_table, *([cache] * PAGES_PER_STEP), q, new_rows, mask)


def _diff_decode_kernel(pt_ref, *refs, lam_init):
    pages = refs[:PAGES_PER_STEP]
    q_ref, new_ref, lp_ref, sub_ref, o_ref, m_ref, l_ref, acc_ref = refs[PAGES_PER_STEP:]
    pc = pl.program_id(1)

    @pl.when(pc == 0)
    def _():
        m_ref[...] = jnp.full(m_ref.shape, NEG, F32)
        l_ref[...] = jnp.zeros(l_ref.shape, F32)
        acc_ref[...] = jnp.zeros(acc_ref.shape, F32)

    for kv in range(C_KV):
        q = q_ref[kv].astype(BF16)
        k = jnp.concatenate([p[pl.ds(kv, PAGE_SIZE, stride=4), :] for p in pages], axis=0).astype(BF16)
        v = jnp.concatenate([p[pl.ds(2 + kv, PAGE_SIZE, stride=4), :] for p in pages], axis=0).astype(BF16)
        sc = _dot_nt(q, k)
        m, l, acc = _online_step(sc, None, v, m_ref[kv], l_ref[kv], acc_ref[kv])
        m_ref[kv] = m
        l_ref[kv] = l
        acc_ref[kv] = acc

    @pl.when(pc == pl.num_programs(1) - 1)
    def _():
        new = new_ref[...]
        lam = _lambda_of(lp_ref[...], lam_init)
        for kv in range(C_KV):
            k_new = new[:, kv * 128:(kv + 1) * 128]
            v_new = new[:, 256 + kv * 128:384 + kv * 128].astype(BF16).astype(F32)
            _, l, acc = _online_single(_rowdot(q_ref[kv], k_new), None, v_new, m_ref[kv], l_ref[kv], acc_ref[kv])
            outs = _diff_finish(acc / jnp.maximum(l, 1e-30), lam, sub_ref[...], lam_init, 1)
            o_ref[2 * kv:2 * kv + 1, :] = outs[0]
            o_ref[2 * kv + 1:2 * kv + 2, :] = outs[1]


def _diff_decode(page_table, cache, layer, q, new_rows, lp, subln, lam_init):
    DB = q.shape[0]
    npc = page_table.shape[1] // PAGES_PER_STEP
    per_b = lambda b, pc, pt: (b, 0, 0)
    return pl.pallas_call(
        functools.partial(_diff_decode_kernel, lam_init=lam_init),
        grid_spec=pltpu.PrefetchScalarGridSpec(
            num_scalar_prefetch=1,
            grid=(DB, npc),
            in_specs=_page_specs((None, None, PAGE_SIZE * 4, 128), layer, 2) + [
                pl.BlockSpec((None, C_KV, 4, 128), lambda b, pc, pt: (b, 0, 0, 0)),
                pl.BlockSpec((None, 1, 512), per_b),
                pl.BlockSpec((4, C_HALF), lambda b, pc, pt: (0, 0)),
                pl.BlockSpec((1, 128), lambda b, pc, pt: (0, 0)),
            ],
            out_specs=pl.BlockSpec((None, C_HEADS, 128), per_b),
            scratch_shapes=[pltpu.VMEM((C_KV, 4, 1), F32), pltpu.VMEM((C_KV, 4, 1), F32),
                            pltpu.VMEM((C_KV, 4, 128), F32)],
        ),
        out_shape=jax.ShapeDtypeStruct((DB, C_HEADS, 128), F32),
        compiler_params=_cparams(("parallel", "arbitrary")),
        name="diff_decode",
    )(page_table, *([cache] * PAGES_PER_STEP), q, new_rows, lp, subln)


def _rope_tables(pos, d):
    half = d // 2
    inv = ROPE_THETA ** (-jnp.arange(half, dtype=F32) / half)
    ang = pos.astype(F32)[:, None] * inv[None, :]
    cos, sin = jnp.cos(ang), jnp.sin(ang)
    reps = 128 // d
    return jnp.tile(jnp.concatenate([cos, cos], axis=-1), (1, reps)), jnp.tile(jnp.concatenate([-sin, sin], axis=-1), (1, reps))


def _permute_w_in(w_in):
    sizes = (1024, 1536, 24, 512, 256, 1024, 16, 64, 512, 512)
    offs = np.concatenate([[0], np.cumsum(sizes)])
    a_q, a_kv, a_gate, b_q, b_kv, b_iq, b_iw, b_ik, c_q, c_kv = [
        w_in[:, :, int(offs[i]):int(offs[i + 1])] for i in range(10)]
    pad = jnp.zeros(w_in.shape[:2] + (N_PROJ - 5480,), w_in.dtype)
    return jnp.concatenate([a_q, a_kv, b_q, b_kv, b_iq, c_q, c_kv, b_ik, a_gate, b_iw, pad], axis=-1).astype(BF16)


def _pack_params(nsa_qk_norm, dsa_qk_norm, dsa_idx_knorm, diff_qk_norm):
    rows = [nsa_qk_norm, dsa_qk_norm, jnp.tile(dsa_idx_knorm, 2)[None], jnp.tile(diff_qk_norm, (1, 2))]
    p = jnp.concatenate(rows, axis=0).astype(F32)
    return jnp.pad(p, ((0, 16 - p.shape[0]), (0, 0)))


def _compress_weights(cmp_w, rows):
    eye = jnp.eye(rows // CMP_STRIDE, dtype=F32)
    mats = []
    for c in range(2):
        halves = [jnp.kron(eye, cmp_w[c, h * CMP_STRIDE:(h + 1) * CMP_STRIDE][None, :]) for h in range(2)]
        mats.append(jnp.concatenate(halves, axis=0))
    return jnp.stack(mats, axis=0)


def kernel(x_prompt, x_sample, cache_nsa_kv, state_nsa_win, cache_dsa_kv, cache_dsa_idx, cache_diff_kv, page_table, attn_norm, w_in, nsa_qk_norm, nsa_cmp_w, dsa_qk_norm, dsa_idx_knorm, diff_qk_norm, diff_lambda, diff_subln, w_out, ffn_norm, w_gate_up, w_down):
    B, T, D = x_prompt.shape
    DB = x_sample.shape[0]
    depth = w_in.shape[0]
    n_pages = page_table.shape[1]
    past = n_pages * PAGE_SIZE
    M = B * T
    assert x_sample.shape[1] == 1 and T % TK == 0 and T >= WINDOW + TQ and n_pages % PAGES_PER_STEP == 0
    tm = min(1024, M)
    tm_ffn = min(1024, M)
    tm_post = 256

    w_in_p = _permute_w_in(w_in)
    w_out_b = w_out.astype(BF16)
    w_gu_b = w_gate_up.astype(BF16)
    w_down_b = w_down.astype(BF16)

    pos_p = jnp.arange(T, dtype=I32)
    tabs_p = _rope_tables(pos_p, 128) + _rope_tables(pos_p, 64)
    pos_s = jnp.full((DB,), past, I32)
    tabs_s = _rope_tables(pos_s, 128) + _rope_tables(pos_s, 64)
    cend_p = jnp.arange(T // CMP_STRIDE, dtype=I32) * CMP_STRIDE + (CMP_LEN - 1)
    cc_p, sc_p = _rope_tables(cend_p, 128)
    cend_s = jnp.arange(past // CMP_STRIDE, dtype=I32) * CMP_STRIDE + (CMP_LEN - 1)
    cc_s, sc_s = _rope_tables(cend_s, 128)

    n_pool = cache_nsa_kv.shape[1]
    nsa_pages = cache_nsa_kv.reshape(depth, n_pool, PAGE_SIZE * 8, 128)
    dsa_pages = cache_dsa_kv.reshape(depth, n_pool, PAGE_SIZE * 2, 128)
    diff_pages = cache_diff_kv.reshape(depth, n_pool, PAGE_SIZE * 4, 128)
    win_rows = state_nsa_win.reshape(depth, DB, state_nsa_win.shape[2] * 4, 128)
    idx_pages = jnp.swapaxes(cache_dsa_idx, 2, 3)

    yp = x_prompt.reshape(M, D)
    ys = x_sample.reshape(DB, D)
    rows_p, rows_s = [], []
    for l in range(depth):
        lam_init = 0.8 - 0.6 * math.exp(-0.3 * l)
        prm = _pack_params(nsa_qk_norm[l], dsa_qk_norm[l], dsa_idx_knorm[l], diff_qk_norm[l])
        g_attn = attn_norm[l][None, :]
        g_ffn = ffn_norm[l][None, :]
        lp = diff_lambda[l].astype(F32)
        subln = diff_subln[l][None, :].astype(F32)

        proj = _norm_matmul(yp, g_attn, w_in_p, l, tm).reshape(B, T, N_PROJ)
        wc = _compress_weights(nsa_cmp_w[l], tm_post)
        (nsa, win, dsa, ik, dif, qa, ksel, vsel, kw, vw, qb, kb, vb, iq, ikd, iw, qc, kcd, vcd, gat,
         pa, pb) = _post_project(proj, tabs_p, prm, wc, tm_post, BF16)
        o_a = _nsa_prompt(qa, pa, pb, cc_p, sc_p, ksel, vsel, kw, vw, gat)
        o_b = _dsa_prompt(qb, iq, iw, ikd, kb, vb)
        o_c = _diff_prompt(qc, kcd, vcd, lp, subln, lam_init)
        mix = jnp.concatenate([o_a, o_b, o_c], axis=-1).reshape(M, D)
        yp = _matmul_residual(mix, w_out_b, yp, l, tm)
        act = _norm_swiglu(yp, g_ffn, w_gu_b, l, tm_ffn)
        yp = _matmul_residual(act, w_down_b, yp, l, tm_ffn)
        w_keep = min(WINDOW, T)
        rows_p.append((nsa.reshape(B, T, 4, A_KV, 128), win[:, (T - w_keep) * 4:].reshape(B, w_keep, 2, A_KV, 128),
                       dsa.reshape(B, T, 2, 128), ik, dif.reshape(B, T, 2, C_KV, 128)))

        proj_s = _norm_matmul(ys, g_attn, w_in_p, l, DB).reshape(1, DB, N_PROJ)
        (nsa_s, win_s, dsa_s, ik_s, dif_s, qa_s, _, _, _, _, qb_s, _, _, iq_s, _, iw_s, qc_s, _, _,
         gat_s) = _post_project(proj_s, tabs_s, prm, None, DB, F32)
        nsa_new = nsa_s.reshape(DB, 1, 1024)
        win_new = win_s.reshape(DB, 1, 512)
        dsa_new = dsa_s.reshape(DB, 1, 256)
        ik_new = ik_s.reshape(DB, 1, IDX_DIM)
        dif_new = dif_s.reshape(DB, 1, 512)
        qa_d = jnp.transpose(qa_s[0], (1, 0, 2))
        qb_d = jnp.transpose(qb_s[0], (1, 0, 2))
        iq_d = jnp.transpose(iq_s[0], (1, 0, 2))
        iq_d = iq_d[:, :, :64] + iq_d[:, :, 64:]
        iw_d = iw_s[0, :, :IDX_HEADS, None]
        qc_d = jnp.transpose(qc_s[0], (2, 0, 1, 3))
        g_d = jnp.transpose(gat_s[0, :, :, :12].reshape(A_KV, DB, A_G, 3), (1, 0, 2, 3)).reshape(DB, A_HEADS, 3)
        g_d = jnp.pad(g_d, ((0, 0), (0, 0), (0, 125)))

        wc_s = _compress_weights(nsa_cmp_w[l], PAGE_SIZE)
        oc_d, sel = _nsa_cmp_decode(page_table, nsa_pages, l, qa_d, nsa_new, wc_s, cc_s, sc_s, past)
        oa_d = _nsa_sel_decode(page_table, sel.reshape(DB, A_KV, SEL_TOPN), nsa_pages, win_rows, l,
                               qa_d, nsa_new, win_new, oc_d, g_d, past)
        mask = _dsa_idx_decode(page_table, idx_pages, l, iq_d, iw_d, ik_new, past)
        ob_d = _dsa_att_decode(page_table, dsa_pages, l, qb_d, dsa_new, mask, past)
        od_d = _diff_decode(page_table, diff_pages, l, qc_d, dif_new, lp, subln, lam_init)
        mix_s = jnp.concatenate([oa_d.reshape(DB, 1024), ob_d.reshape(DB, 512), od_d.reshape(DB, 512)],
                                axis=-1).astype(BF16)
        ys = _matmul_residual(mix_s, w_out_b, ys, l, DB)
        act_s = _norm_swiglu(ys, g_ffn, w_gu_b, l, DB)
        ys = _matmul_residual(act_s, w_down_b, ys, l, DB)
        lw = state_nsa_win.shape[2]
        win_all = jnp.concatenate([state_nsa_win[l], win_new.reshape(DB, 1, 2, A_KV, 128)], axis=1)
        rows_s.append((nsa_new.reshape(DB, 1, 4, A_KV, 128), win_all[:, win_all.shape[1] - min(WINDOW, lw + 1):],
                       dsa_new.reshape(DB, 1, 2, 128), ik_new, dif_new.reshape(DB, 1, 2, C_KV, 128)))

    def stacked(rows, i):
        return jnp.stack([r[i] for r in rows], axis=0)

    return (yp.reshape(B, T, D), ys.reshape(DB, 1, D),
            stacked(rows_p, 0), stacked(rows_s, 0), stacked(rows_p, 1), stacked(rows_s, 1),
            stacked(rows_p, 2), stacked(rows_s, 2), stacked(rows_p, 3), stacked(rows_s, 3),
            stacked(rows_p, 4), stacked(rows_s, 4))
```

```python
import functools
import math

import numpy as np
import jax
import jax.numpy as jnp
from jax import lax
from jax.experimental import pallas as pl
from jax.experimental.pallas import tpu as pltpu

F32 = jnp.float32
BF16 = jnp.bfloat16
I32 = jnp.int32
HI = lax.Precision.HIGHEST

D_MODEL = 2048
PAGE_SIZE = 128
D_HEAD = 128
A_HEADS = 8
A_KV = 2
A_G = A_HEADS // A_KV
B_HEADS = 4
C_HEADS = 4
C_KV = 2
C_HALF = 64
IDX_HEADS = 16
IDX_DIM = 64
IDX_TOPK = 256
CMP_LEN = 32
CMP_STRIDE = 16
SEL_BLOCK = 64
SEL_TOPN = 16
WINDOW = 512
FORCE_BONUS = 1.0e4
D_FF = 5632
ROPE_THETA = 10000.0
EPS = 1e-6
NEG = -1e30
INT_MIN = -2147483648

OFF_AQ = 0
OFF_AKV = 1024
OFF_BQ = 2560
OFF_BKV = 3072
OFF_BIQ = 3328
OFF_CQ = 4352
OFF_CKV = 4864
OFF_MISC = 5376
N_PROJ = 5632
MISC_GATE = 64
MISC_IW = 88

TQ_NSA = 256
TQ_DSA = 128
TQ_DIFF = 256
TK = 512
PAGES_PER_STEP = 16
SEL_SHIFT = 6
LOG2E = math.log2(math.e)
SCALE_D = D_HEAD ** -0.5 * LOG2E
SCALE_C = C_HALF ** -0.5 * LOG2E
MASKED = -2e30
VMEM_LIMIT = 56 * 1024 * 1024


def _cparams(sem):
    return pltpu.CompilerParams(dimension_semantics=sem, vmem_limit_bytes=VMEM_LIMIT)


def _dot(a, b, precision=None):
    return jnp.dot(a, b, preferred_element_type=F32, precision=precision)


def _dot3(a, b):
    a_hi = a.astype(BF16)
    b_hi = b.astype(BF16)
    a_lo = (a - a_hi.astype(F32)).astype(BF16)
    b_lo = (b - b_hi.astype(F32)).astype(BF16)
    return _dot(a_hi, b_hi) + (_dot(a_hi, b_lo) + _dot(a_lo, b_hi))


def _dot_nt(a, b, precision=None):
    return lax.dot_general(a, b, (((1,), (1,)), ((), ())), preferred_element_type=F32, precision=precision)


def _norm_mm_kernel(x_ref, g_ref, w_ref, o_ref, xn_ref):
    @pl.when(pl.program_id(1) == 0)
    def _():
        x = x_ref[...]
        ms = jnp.mean(x * x, axis=-1, keepdims=True)
        xn_ref[...] = (x * lax.rsqrt(ms + EPS) * g_ref[...]).astype(BF16)

    o_ref[...] = _dot(xn_ref[...], w_ref[...])


def _norm_matmul(x, g, w, layer, tm, tn=512):
    M, K = x.shape
    N = w.shape[2]
    return pl.pallas_call(
        _norm_mm_kernel,
        grid=(M // tm, N // tn),
        in_specs=[
            pl.BlockSpec((tm, K), lambda i, j: (i, 0)),
            pl.BlockSpec((1, K), lambda i, j: (0, 0)),
            pl.BlockSpec((None, K, tn), lambda i, j: (layer, 0, j)),
        ],
        out_specs=pl.BlockSpec((tm, tn), lambda i, j: (i, j)),
        out_shape=jax.ShapeDtypeStruct((M, N), F32),
        scratch_shapes=[pltpu.VMEM((tm, K), BF16)],
        compiler_params=_cparams(("parallel", "arbitrary")),
        name="norm_matmul",
    )(x, g, w)


def _norm_swiglu_kernel(x_ref, g_ref, wg_ref, wu_ref, o_ref, xn_ref):
    @pl.when(pl.program_id(1) == 0)
    def _():
        x = x_ref[...]
        ms = jnp.mean(x * x, axis=-1, keepdims=True)
        xn_ref[...] = (x * lax.rsqrt(ms + EPS) * g_ref[...]).astype(BF16)

    xn = xn_ref[...]
    gate = _dot(xn, wg_ref[...])
    up = _dot(xn, wu_ref[...])
    o_ref[...] = (gate * jax.nn.sigmoid(gate) * up).astype(o_ref.dtype)


def _norm_swiglu(x, g, w, layer, tm, tn=512):
    M, K = x.shape
    nj = D_FF // tn
    return pl.pallas_call(
        _norm_swiglu_kernel,
        grid=(M // tm, nj),
        in_specs=[
            pl.BlockSpec((tm, K), lambda i, j: (i, 0)),
            pl.BlockSpec((1, K), lambda i, j: (0, 0)),
            pl.BlockSpec((None, K, tn), lambda i, j: (layer, 0, j)),
            pl.BlockSpec((None, K, tn), lambda i, j: (layer, 0, j + nj)),
        ],
        out_specs=pl.BlockSpec((tm, tn), lambda i, j: (i, j)),
        out_shape=jax.ShapeDtypeStruct((M, D_FF), BF16),
        scratch_shapes=[pltpu.VMEM((tm, K), BF16)],
        compiler_params=_cparams(("parallel", "arbitrary")),
        name="norm_swiglu",
    )(x, g, w, w)


def _mm_res_kernel(a_ref, w_ref, r_ref, o_ref):
    o_ref[...] = r_ref[...] + _dot(a_ref[...], w_ref[...])


def _matmul_residual(a, w, res, layer, tm, tn=512):
    M, K = a.shape
    N = w.shape[2]
    return pl.pallas_call(
        _mm_res_kernel,
        grid=(M // tm, N // tn),
        in_specs=[
            pl.BlockSpec((tm, K), lambda i, j: (i, 0)),
            pl.BlockSpec((None, K, tn), lambda i, j: (layer, 0, j)),
            pl.BlockSpec((tm, tn), lambda i, j: (i, j)),
        ],
        out_specs=pl.BlockSpec((tm, tn), lambda i, j: (i, j)),
        out_shape=jax.ShapeDtypeStruct((M, N), F32),
        compiler_params=_cparams(("parallel", "arbitrary")),
        name="matmul_residual",
    )(a, w, res)


def _post_kernel(*refs, emit_cmp, tm):
    if emit_cmp:
        x_ref, c1_ref, s1_ref, c2_ref, s2_ref, prm_ref, wc_ref = refs[:7]
        outs = refs[7:]
    else:
        x_ref, c1_ref, s1_ref, c2_ref, s2_ref, prm_ref = refs[:6]
        wc_ref = None
        outs = refs[6:]
    (nsa_ref, win_ref, dsa_ref, ik_ref, dif_ref, qa_ref, ksel_ref, vsel_ref, kw_ref, vw_ref,
     qb_ref, kb_ref, vb_ref, iq_ref, ikd_ref, iw_ref, qc_ref, kcd_ref, vcd_ref, gat_ref) = outs[:20]

    c1, s1, c2, s2 = c1_ref[...], s1_ref[...], c2_ref[...], s2_ref[...]
    prm = prm_ref[...]
    lane = lax.broadcasted_iota(I32, (tm, 128), 1)
    lo = lane < 64
    inner = (lane & 63) < 32

    def col(a):
        return x_ref[:, a:a + 128]

    def gain(r):
        return prm[r:r + 1, :]

    def rms128(v, g):
        return v * lax.rsqrt(jnp.mean(v * v, axis=-1, keepdims=True) + EPS) * g

    def rope128(v):
        return v * c1 + pltpu.roll(v, 64, 1) * s1

    def half_ms(v):
        sq = v * v
        a = jnp.sum(jnp.where(lo, sq, 0.0), axis=-1, keepdims=True)
        b = jnp.sum(jnp.where(lo, 0.0, sq), axis=-1, keepdims=True)
        return a * (1.0 / 64), b * (1.0 / 64)

    def rms64(v, g):
        a, b = half_ms(v)
        return v * lax.rsqrt(jnp.where(lo, a, b) + EPS) * g

    def rope64(v):
        rot = jnp.where(inner, pltpu.roll(v, 96, 1), pltpu.roll(v, 32, 1))
        return v * c2 + rot * s2

    for h in range(A_HEADS):
        qa_ref[h] = (rope128(rms128(col(OFF_AQ + h * 128), gain(0))) * SCALE_D).astype(qa_ref.dtype)
    for kv in range(A_KV):
        o = kv * 128
        kc = rms128(col(OFF_AKV + o), gain(1))
        vc = col(OFF_AKV + 256 + o)
        ks = rope128(rms128(col(OFF_AKV + 512 + o), gain(2)))
        vs = col(OFF_AKV + 768 + o)
        kw = rope128(rms128(col(OFF_AKV + 1024 + o), gain(3)))
        vw = col(OFF_AKV + 1280 + o)
        for slab, val in ((kv, kc), (2 + kv, vc), (4 + kv, ks), (6 + kv, vs)):
            nsa_ref[pl.ds(slab, tm, stride=8), :] = val
        win_ref[pl.ds(kv, tm, stride=4), :] = kw
        win_ref[pl.ds(2 + kv, tm, stride=4), :] = vw
        ksel_ref[kv] = ks.astype(ksel_ref.dtype)
        vsel_ref[kv] = vs.astype(vsel_ref.dtype)
        kw_ref[kv] = kw.astype(kw_ref.dtype)
        vw_ref[kv] = vw.astype(vw_ref.dtype)
        if emit_cmp:
            pa_ref, pb_ref = outs[20], outs[21]
            nch = tm // CMP_STRIDE
            pk = _dot3(wc_ref[0], kc)
            pv = _dot3(wc_ref[1], vc)
            pa_ref[kv] = pk[0:nch]
            pb_ref[kv] = pk[nch:2 * nch]
            pa_ref[2 + kv] = pv[0:nch]
            pb_ref[2 + kv] = pv[nch:2 * nch]
    for h in range(B_HEADS):
        qb_ref[h] = (rope128(rms128(col(OFF_BQ + h * 128), gain(4))) * SCALE_D).astype(qb_ref.dtype)
    kb = rope128(rms128(col(OFF_BKV), gain(5)))
    vb = col(OFF_BKV + 128)
    dsa_ref[pl.ds(0, tm, stride=2), :] = kb
    dsa_ref[pl.ds(1, tm, stride=2), :] = vb
    kb_ref[...] = kb.astype(kb_ref.dtype)
    vb_ref[...] = vb.astype(vb_ref.dtype)
    for p in range(IDX_HEADS // 2):
        v = rope64(col(OFF_BIQ + p * 128))
        iq_ref[2 * p] = jnp.where(lo, v, 0.0).astype(iq_ref.dtype)
        iq_ref[2 * p + 1] = jnp.where(lo, 0.0, v).astype(iq_ref.dtype)
    for h in range(C_HEADS):
        v = rope64(rms64(col(OFF_CQ + h * 128), gain(7))) * SCALE_C
        kv, g = h // 2, h % 2
        qc_ref[kv, 2 * g] = jnp.where(lo, v, 0.0).astype(qc_ref.dtype)
        qc_ref[kv, 2 * g + 1] = jnp.where(lo, 0.0, v).astype(qc_ref.dtype)
    for kv in range(C_KV):
        o = kv * 128
        kk = rope64(rms64(col(OFF_CKV + o), gain(8)))
        vv = col(OFF_CKV + 256 + o)
        dif_ref[pl.ds(kv, tm, stride=4), :] = kk
        dif_ref[pl.ds(2 + kv, tm, stride=4), :] = vv
        kcd_ref[kv] = kk.astype(kcd_ref.dtype)
        vcd_ref[kv] = vv.astype(vcd_ref.dtype)
    m = col(OFF_MISC)
    a, _ = half_ms(m)
    ikr = rope64(m * lax.rsqrt(a + EPS) * gain(6))
    ik_ref[...] = ikr[:, 0:64]
    ikd_ref[...] = jnp.where(lo, ikr, pltpu.roll(ikr, 64, 1)).astype(ikd_ref.dtype)
    sig = jax.nn.sigmoid(m)
    for kv in range(A_KV):
        gat_ref[kv] = pltpu.roll(sig, 128 - MISC_GATE - 12 * kv, 1)
    iw_ref[...] = pltpu.roll(m, 128 - MISC_IW, 1) * ((IDX_DIM ** -0.5) * (IDX_HEADS ** -0.5))


def _post_project(proj, tabs, prm, wc, tm, qdt):
    B, T, _ = proj.shape
    emit_cmp = wc is not None
    nt = T // tm

    def row(c):
        return pl.BlockSpec((None, tm, c), lambda b, i: (b, i, 0))

    def heads(*lead):
        n = len(lead)
        return pl.BlockSpec((None,) + lead + (tm, 128), lambda b, i: (b,) + (0,) * n + (i, 0))

    tab = pl.BlockSpec((tm, 128), lambda b, i: (i, 0))
    in_specs = [row(N_PROJ), tab, tab, tab, tab, pl.BlockSpec((16, 128), lambda b, i: (0, 0))]
    args = [proj, *tabs, prm]
    if emit_cmp:
        in_specs.append(pl.BlockSpec((2, 2 * tm // CMP_STRIDE, tm), lambda b, i: (0, 0, 0)))
        args.append(wc)

    def sds(shape, dt):
        return jax.ShapeDtypeStruct(shape, dt)

    def slabs(n):
        return pl.BlockSpec((None, tm * n, 128), lambda b, i: (b, i, 0))

    out_shape = [
        sds((B, T * 8, 128), F32), sds((B, T * 4, 128), F32), sds((B, T * 2, 128), F32), sds((B, T, 64), F32),
        sds((B, T * 4, 128), F32),
        sds((B, A_HEADS, T, 128), qdt),
        sds((B, A_KV, T, 128), qdt), sds((B, A_KV, T, 128), qdt),
        sds((B, A_KV, T, 128), qdt), sds((B, A_KV, T, 128), qdt),
        sds((B, B_HEADS, T, 128), qdt), sds((B, T, 128), qdt), sds((B, T, 128), qdt),
        sds((B, IDX_HEADS, T, 128), qdt), sds((B, T, 128), qdt), sds((B, T, 128), F32),
        sds((B, C_KV, 4, T, 128), qdt), sds((B, C_KV, T, 128), qdt), sds((B, C_KV, T, 128), qdt),
        sds((B, A_KV, T, 128), F32),
    ]
    out_specs = [
        slabs(8), slabs(4), slabs(2), row(64), slabs(4),
        heads(A_HEADS), heads(A_KV), heads(A_KV), heads(A_KV), heads(A_KV),
        heads(B_HEADS), row(128), row(128), heads(IDX_HEADS), row(128), row(128),
        heads(C_KV, 4), heads(C_KV), heads(C_KV), heads(A_KV),
    ]
    if emit_cmp:
        nc = T // CMP_STRIDE
        out_shape += [sds((B, 4, nc, 128), F32), sds((B, 4, nc, 128), F32)]
        spec = pl.BlockSpec((None, 4, tm // CMP_STRIDE, 128), lambda b, i: (b, 0, i, 0))
        out_specs += [spec, spec]
    return pl.pallas_call(
        functools.partial(_post_kernel, emit_cmp=emit_cmp, tm=tm),
        grid=(B, nt),
        in_specs=in_specs,
        out_specs=out_specs,
        out_shape=out_shape,
        compiler_params=_cparams(("parallel", "parallel")),
        name="post_project",
    )(*args)


def _bias(mask):
    return jnp.where(mask, 0.0, MASKED)


def _masked_softmax(s, bias):
    s = s + bias
    m = jnp.maximum(jnp.max(s, axis=-1, keepdims=True), NEG)
    e = jnp.exp2(s - m)
    return e / jnp.maximum(jnp.sum(e, axis=-1, keepdims=True), 1e-30)


def _online_step(s, bias, v, m, l, acc):
    if bias is not None:
        s = s + bias
    m_new = jnp.maximum(m, jnp.max(s, axis=-1, keepdims=True))
    alpha = jnp.exp2(m - m_new)
    e = jnp.exp2(s - m_new)
    l_new = alpha * l + jnp.sum(e, axis=-1, keepdims=True)
    rows = acc.shape[0]
    pv = _dot(e.reshape(rows, e.shape[-1]).astype(BF16), v)
    return m_new, l_new, alpha.reshape(rows, 1) * acc + pv


N_CHAINS = 1


def _flash_init(heads, tq):
    hc = heads // N_CHAINS
    return tuple((jnp.full((hc, tq, 1), NEG, F32), jnp.zeros((hc, tq, 1), F32), jnp.zeros((hc * tq, 128), F32))
                 for _ in range(N_CHAINS))


def _flash_tile(q, k, v, bias, carry):
    out = []
    for c, (m, l, acc) in enumerate(carry):
        rows = acc.shape[0]
        s = _dot_nt(q[c * rows:(c + 1) * rows], k).reshape(m.shape[0], m.shape[1], k.shape[0])
        out.append(_online_step(s, bias, v, m, l, acc))
    return tuple(out)


def _flash_finish(carry):
    return jnp.concatenate([acc / jnp.maximum(l.reshape(acc.shape[0], 1), 1e-30) for _, l, acc in carry], axis=0)


def _online_single(s, bias, v_row, m, l, acc):
    if bias is not None:
        s = s + bias
    m_new = jnp.maximum(m, s)
    alpha = jnp.exp2(m - m_new)
    e = jnp.exp2(s - m_new)
    return m_new, alpha * l + e, alpha * acc + e.astype(BF16).astype(F32) * v_row


def _rowdot(q, k_row):
    return jnp.sum(q.astype(BF16).astype(F32) * k_row.astype(BF16).astype(F32), axis=-1, keepdims=True)


def _sortable(x):
    b = lax.bitcast_convert_type(x + 0.0, I32)
    return jnp.where(b < 0, b ^ jnp.int32(0x7FFFFFFF), b)


def _lambda_of(lp, lam_init):
    a = jnp.sum(lp[0:1] * lp[1:2], axis=-1, keepdims=True)
    b = jnp.sum(lp[2:3] * lp[3:4], axis=-1, keepdims=True)
    return jnp.exp(a) - jnp.exp(b) + lam_init


def _nsa_prompt_kernel(q_ref, pak_ref, pbk_ref, pav_ref, pbv_ref, cc_ref, sc_ref,
                       ks_ref, vs_ref, kw_ref, vw_ref, g_ref, o_ref, kc_ref, vc_ref, *, T):
    TQ = TQ_NSA
    qi = pl.program_id(2)
    nc = T // CMP_STRIDE
    ns = T // SEL_BLOCK
    R = A_G * TQ
    band = WINDOW + TQ

    @pl.when(qi == 0)
    def _():
        kraw = pak_ref[...] + pltpu.roll(pbk_ref[...], nc - 1, 0)
        kc_ref[...] = (kraw * cc_ref[...] + pltpu.roll(kraw, 64, 1) * sc_ref[...]).astype(BF16)
        vc_ref[...] = (pav_ref[...] + pltpu.roll(pbv_ref[...], nc - 1, 0)).astype(BF16)

    q = q_ref[...].reshape(R, 128)
    t0 = qi * TQ
    qp3 = t0 + lax.broadcasted_iota(I32, (1, TQ, 1), 1)

    s_c = _dot_nt(q, kc_ref[...]).reshape(A_G, TQ, nc)
    cend = lax.broadcasted_iota(I32, (1, TQ, nc), 2) * CMP_STRIDE + (CMP_LEN - 1)
    p_c = _masked_softmax(s_c, _bias(cend <= qp3))
    o_c = _dot(p_c.reshape(R, nc).astype(BF16), vc_ref[...])

    sj = lax.broadcasted_iota(I32, (ns, nc), 0) * SEL_BLOCK
    ci = lax.broadcasted_iota(I32, (ns, nc), 1) * CMP_STRIDE
    overlap_t = ((ci < sj + SEL_BLOCK) & (ci + CMP_LEN > sj)).astype(F32)
    imp = _dot_nt(overlap_t, jnp.sum(p_c, axis=0), HI)
    jidx = lax.broadcasted_iota(I32, (ns, TQ), 0)
    jq = (t0 + lax.broadcasted_iota(I32, (1, TQ), 1)) >> SEL_SHIFT
    forced = (jidx == 0) | (jidx == jq) | (jidx == jq - 1)
    imp = jnp.where(forced, imp + FORCE_BONUS, imp)
    imp = jnp.where(jidx > jq, NEG, imp)
    ng = ns // 8
    sub = lax.broadcasted_iota(I32, (8, TQ), 0)
    imp_g = [imp[8 * g:8 * g + 8, :] for g in range(ng)]
    rank_g = [jnp.zeros((8, TQ), F32) for _ in range(ng)]
    for j in range(ns):
        rj = jnp.broadcast_to(imp[j:j + 1, :], (8, TQ))
        for g in range(ng):
            if g < j // 8:
                ahead = rj > imp_g[g]
            elif g > j // 8:
                ahead = rj >= imp_g[g]
            else:
                ahead = (rj > imp_g[g]) | ((rj == imp_g[g]) & (sub > j % 8))
            rank_g[g] = rank_g[g] + jnp.where(ahead, 1.0, 0.0)
    rank = jnp.concatenate(rank_g, axis=0)
    selb = jnp.where(rank < min(SEL_TOPN, ns), 1.0, 0.0).T.astype(BF16)

    erow = lax.broadcasted_iota(I32, (ns, TK), 0)
    ecol = lax.broadcasted_iota(I32, (ns, TK), 1)
    tcol = lax.broadcasted_iota(I32, (1, TQ, TK), 2)

    def sel_step(kt, carry):
        base = pl.multiple_of(kt * TK, TK)
        expand = (erow == ((ecol + base) >> SEL_SHIFT)).astype(BF16)
        chosen = _dot(selb, expand).reshape(1, TQ, TK) > 0.5
        bias = _bias(chosen & (tcol + base <= qp3))
        return _flash_tile(q, ks_ref[pl.ds(base, TK), :], vs_ref[pl.ds(base, TK), :], bias, carry)

    nkt = (t0 + TQ + TK - 1) // TK
    o_s = _flash_finish(lax.fori_loop(0, nkt, sel_step, _flash_init(A_G, TQ)))

    start = pl.multiple_of(jnp.maximum(t0 - WINDOW, 0), TQ)
    kwin = kw_ref[pl.ds(start, band), :]
    vwin = vw_ref[pl.ds(start, band), :]
    dist = qp3 - (start + lax.broadcasted_iota(I32, (1, TQ, band), 2))
    s_w = _dot_nt(q, kwin).reshape(A_G, TQ, band)
    p_w = _masked_softmax(s_w, _bias((dist >= 0) & (dist <= WINDOW)))
    o_w = _dot(p_w.reshape(R, band).astype(BF16), vwin)

    g = g_ref[...]
    for h in range(A_G):
        r = slice(h * TQ, (h + 1) * TQ)
        o = g[:, 3 * h:3 * h + 1] * o_c[r] + g[:, 3 * h + 1:3 * h + 2] * o_s[r] + g[:, 3 * h + 2:3 * h + 3] * o_w[r]
        o_ref[:, h * 128:(h + 1) * 128] = o.astype(o_ref.dtype)


def _nsa_prompt(qa, pa, pb, cc, sc, ksel, vsel, kw, vw, gat):
    B, _, T, _ = qa.shape
    TQ = TQ_NSA
    nc = T // CMP_STRIDE
    part_k = pl.BlockSpec((None, None, nc, 128), lambda b, kv, i: (b, kv, 0, 0))
    part_v = pl.BlockSpec((None, None, nc, 128), lambda b, kv, i: (b, 2 + kv, 0, 0))
    tabc = pl.BlockSpec((nc, 128), lambda b, kv, i: (0, 0))
    full = pl.BlockSpec((None, None, T, 128), lambda b, kv, i: (b, kv, 0, 0))
    return pl.pallas_call(
        functools.partial(_nsa_prompt_kernel, T=T),
        grid=(B, A_KV, T // TQ),
        in_specs=[
            pl.BlockSpec((None, A_G, TQ, 128), lambda b, kv, i: (b, kv, i, 0)),
            part_k, part_k, part_v, part_v, tabc, tabc, full, full, full, full,
            pl.BlockSpec((None, None, TQ, 128), lambda b, kv, i: (b, kv, i, 0)),
        ],
        out_specs=pl.BlockSpec((None, TQ, A_G * 128), lambda b, kv, i: (b, i, kv)),
        out_shape=jax.ShapeDtypeStruct((B, T, A_HEADS * 128), BF16),
        scratch_shapes=[pltpu.VMEM((nc, 128), BF16), pltpu.VMEM((nc, 128), BF16)],
        compiler_params=_cparams(("parallel", "parallel", "arbitrary")),
        name="nsa_prompt",
    )(qa, pa, pb, pa, pb, cc, sc, ksel, vsel, kw, vw, gat)


def _kth_threshold(count_ge, shape, k):
    def step(it, t):
        cand = t + jnp.left_shift(jnp.int32(1), 31 - it)
        return jnp.where(count_ge(cand) >= k, cand, t)

    return lax.fori_loop(0, 32, step, jnp.full(shape, INT_MIN, I32))


def _dsa_prompt_kernel(q_ref, iq_ref, iw_ref, ik_ref, k_ref, v_ref, o_ref, key_ref, *, T):
    TQ = TQ_DSA
    qi = pl.program_id(1)
    t0 = qi * TQ
    nkt = (t0 + TQ + TK - 1) // TK
    k_top = min(IDX_TOPK, T // 4)
    qp_l = t0 + lax.broadcasted_iota(I32, (1, TQ), 1)
    trow = lax.broadcasted_iota(I32, (TK, TQ), 0)
    iq = iq_ref[...].reshape(IDX_HEADS * TQ, 128)
    iw_t = iw_ref[...].T

    def score_step(kt, _):
        base = pl.multiple_of(kt * TK, TK)
        s = _dot_nt(ik_ref[pl.ds(base, TK), :], iq)
        sc = jnp.zeros((TK, TQ), F32)
        for h in range(IDX_HEADS):
            sc = sc + iw_t[h:h + 1, :] * jnp.maximum(s[:, h * TQ:(h + 1) * TQ], 0.0)
        sc = jnp.where(trow + base <= qp_l, sc, NEG)
        key_ref[pl.ds(base, TK), :] = _sortable(sc)
        return 0

    lax.fori_loop(0, nkt, score_step, 0)

    def count_ge(cand):
        def cstep(kt, c):
            base = pl.multiple_of(kt * TK, TK)
            hit = jnp.where(key_ref[pl.ds(base, TK), :] >= cand, 1.0, 0.0)
            return c + jnp.sum(hit.reshape(TK // 64, 8, 8, TQ), axis=0)

        c = lax.fori_loop(0, nkt, cstep, jnp.zeros((8, 8, TQ), F32))
        return jnp.sum(jnp.sum(c, axis=0), axis=0, keepdims=True)

    thr = _kth_threshold(count_ge, (1, TQ), float(k_top))

    q = q_ref[...].reshape(B_HEADS * TQ, 128)

    def att_step(kt, carry):
        base = pl.multiple_of(kt * TK, TK)
        keep = (key_ref[pl.ds(base, TK), :] >= thr) & (trow + base <= qp_l)
        bias = _bias(keep).T.reshape(1, TQ, TK)
        return _flash_tile(q, k_ref[pl.ds(base, TK), :], v_ref[pl.ds(base, TK), :], bias, carry)

    o = _flash_finish(lax.fori_loop(0, nkt, att_step, _flash_init(B_HEADS, TQ)))
    for h in range(B_HEADS):
        o_ref[:, h * 128:(h + 1) * 128] = o[h * TQ:(h + 1) * TQ].astype(o_ref.dtype)


def _dsa_prompt(qb, iq, iw, ikd, kb, vb):
    B, _, T, _ = qb.shape
    TQ = TQ_DSA
    full = pl.BlockSpec((None, T, 128), lambda b, i: (b, 0, 0))
    return pl.pallas_call(
        functools.partial(_dsa_prompt_kernel, T=T),
        grid=(B, T // TQ),
        in_specs=[
            pl.BlockSpec((None, B_HEADS, TQ, 128), lambda b, i: (b, 0, i, 0)),
            pl.BlockSpec((None, IDX_HEADS, TQ, 128), lambda b, i: (b, 0, i, 0)),
            pl.BlockSpec((None, TQ, 128), lambda b, i: (b, i, 0)),
            full, full, full,
        ],
        out_specs=pl.BlockSpec((None, TQ, B_HEADS * 128), lambda b, i: (b, i, 0)),
        out_shape=jax.ShapeDtypeStruct((B, T, B_HEADS * 128), BF16),
        scratch_shapes=[pltpu.VMEM((T, TQ), I32)],
        compiler_params=_cparams(("parallel", "arbitrary")),
        name="dsa_prompt",
    )(qb, iq, iw, ikd, kb, vb)


def _diff_finish(o, lam, subln, lam_init, rows):
    outs = []
    for g in range(2):
        a0 = o[(2 * g) * rows:(2 * g + 1) * rows]
        a1 = o[(2 * g + 1) * rows:(2 * g + 2) * rows]
        d = a0 - lam * a1
        d = d * lax.rsqrt(jnp.mean(d * d, axis=-1, keepdims=True) + EPS) * subln
        outs.append(d * (1.0 - lam_init))
    return outs


def _diff_prompt_kernel(q_ref, k_ref, v_ref, lp_ref, sub_ref, o_ref, *, lam_init):
    TQ = TQ_DIFF
    qi = pl.program_id(2)
    t0 = qi * TQ
    nkt = (t0 + TQ + TK - 1) // TK
    R = 4 * TQ
    qp = t0 + lax.broadcasted_iota(I32, (1, TQ, 1), 1)
    tcol = lax.broadcasted_iota(I32, (1, TQ, TK), 2)
    q = q_ref[...].reshape(R, 128)

    def step(kt, carry, causal):
        base = pl.multiple_of(kt * TK, TK)
        bias = _bias(tcol + base <= qp) if causal else None
        return _flash_tile(q, k_ref[pl.ds(base, TK), :], v_ref[pl.ds(base, TK), :], bias, carry)

    carry = lax.fori_loop(0, nkt - 1, functools.partial(step, causal=False), _flash_init(4, TQ))
    o = _flash_finish(step(nkt - 1, carry, True))
    lam = _lambda_of(lp_ref[...], lam_init)
    outs = _diff_finish(o, lam, sub_ref[...], lam_init, TQ)
    for g in range(2):
        o_ref[:, g * 128:(g + 1) * 128] = outs[g].astype(o_ref.dtype)


def _diff_prompt(qc, kcd, vcd, lp, subln, lam_init):
    B, _, _, T, _ = qc.shape
    TQ = TQ_DIFF
    full = pl.BlockSpec((None, None, T, 128), lambda b, kv, i: (b, kv, 0, 0))
    return pl.pallas_call(
        functools.partial(_diff_prompt_kernel, lam_init=lam_init),
        grid=(B, C_KV, T // TQ),
        in_specs=[
            pl.BlockSpec((None, None, 4, TQ, 128), lambda b, kv, i: (b, kv, 0, i, 0)),
            full, full,
            pl.BlockSpec((4, C_HALF), lambda b, kv, i: (0, 0)),
            pl.BlockSpec((1, 128), lambda b, kv, i: (0, 0)),
        ],
        out_specs=pl.BlockSpec((None, TQ, 256), lambda b, kv, i: (b, i, kv)),
        out_shape=jax.ShapeDtypeStruct((B, T, C_HEADS * 128), BF16),
        compiler_params=_cparams(("parallel", "parallel", "arbitrary")),
        name="diff_prompt",
    )(qc, kcd, vcd, lp, subln)


def _page_specs(block, layer, n_lead_zero):
    specs = []
    for u in range(PAGES_PER_STEP):
        def imap(b, pc, pt, u=u):
            return (layer, pt[b, pc * PAGES_PER_STEP + u]) + (0,) * n_lead_zero
        specs.append(pl.BlockSpec(block, imap))
    return specs


def _nsa_cmp_decode_kernel(pt_ref, *refs, past):
    pages = refs[:PAGES_PER_STEP]
    q_ref, new_ref, wc_ref, cc_ref, sc_ref, oc_ref, sel_ref, a_ref, b_ref = refs[PAGES_PER_STEP:]
    pc = pl.program_id(1)
    nc = past // CMP_STRIDE
    ns = past // SEL_BLOCK + 1
    nsp = ((ns + 127) // 128) * 128
    cpp = PAGE_SIZE // CMP_STRIDE

    def slab(ref, s):
        return ref[pl.ds(s, PAGE_SIZE, stride=8), :]

    for u in range(PAGES_PER_STEP):
        r0 = pl.multiple_of((pc * PAGES_PER_STEP + u) * cpp, cpp)
        for c in range(2):
            x = jnp.concatenate([slab(pages[u], 2 * c), slab(pages[u], 2 * c + 1)], axis=1)
            part = _dot3(wc_ref[c], x)
            a_ref[pl.ds(r0, cpp), 256 * c:256 * (c + 1)] = part[0:cpp]
            b_ref[pl.ds(r0, cpp), 256 * c:256 * (c + 1)] = part[cpp:2 * cpp]

    @pl.when(pc == pl.num_programs(1) - 1)
    def _():
        rowi = lax.broadcasted_iota(I32, (nc, 128), 0)
        cend = lax.broadcasted_iota(I32, (A_G, nc), 1) * CMP_STRIDE + (CMP_LEN - 1)
        ci = lax.broadcasted_iota(I32, (nc, nsp), 0) * CMP_STRIDE
        sj = lax.broadcasted_iota(I32, (nc, nsp), 1) * SEL_BLOCK
        overlap = ((ci < sj + SEL_BLOCK) & (ci + CMP_LEN > sj)).astype(F32)
        jrow = lax.broadcasted_iota(I32, (1, nsp), 1)
        jq = past // SEL_BLOCK
        ii = lax.broadcasted_iota(I32, (nsp, nsp), 0)
        jj = lax.broadcasted_iota(I32, (nsp, nsp), 1)
        rr = lax.broadcasted_iota(I32, (SEL_TOPN, nsp), 0)
        jr = lax.broadcasted_iota(I32, (SEL_TOPN, nsp), 1).astype(F32)
        new = new_ref[...]
        q = q_ref[...].astype(BF16)
        for kv in range(A_KV):
            ko, vo = kv * 128, 256 + kv * 128
            bk = jnp.where(rowi == nc - 1, wc_ref[0][cpp:cpp + 1, 0:1] * new[:, ko:ko + 128],
                           pltpu.roll(b_ref[:, ko:ko + 128], nc - 1, 0))
            bv = jnp.where(rowi == nc - 1, wc_ref[1][cpp:cpp + 1, 0:1] * new[:, vo:vo + 128],
                           pltpu.roll(b_ref[:, vo:vo + 128], nc - 1, 0))
            kraw = a_ref[:, ko:ko + 128] + bk
            kc = (kraw * cc_ref[...] + pltpu.roll(kraw, 64, 1) * sc_ref[...]).astype(BF16)
            vc = (a_ref[:, vo:vo + 128] + bv).astype(BF16)
            s_c = _dot_nt(q[kv * A_G:(kv + 1) * A_G], kc)
            p_c = _masked_softmax(s_c, _bias(cend <= past))
            oc_ref[kv * A_G:(kv + 1) * A_G, :] = _dot(p_c.astype(BF16), vc)
            imp = _dot(jnp.sum(p_c, axis=0, keepdims=True), overlap, HI)
            forced = (jrow == 0) | (jrow == jq) | (jrow == jq - 1)
            imp = jnp.where(forced, imp + FORCE_BONUS, imp)
            imp = jnp.where(jrow > jq, NEG, imp)
            imp_col = jnp.sum(jnp.where(ii == jj, jnp.broadcast_to(imp, (nsp, nsp)), 0.0), axis=1, keepdims=True)
            beats = (imp_col > imp) | ((imp_col == imp) & (ii < jj))
            rank = jnp.sum(jnp.where(beats, 1.0, 0.0), axis=0, keepdims=True)
            pick = jnp.sum(jnp.where(rank == rr.astype(F32), jr, 0.0), axis=1, keepdims=True)
            sel_ref[kv] = pick.astype(I32)


def _nsa_cmp_decode(page_table, cache, layer, q, new_rows, wc, cc, sc, past):
    DB = q.shape[0]
    npc = page_table.shape[1] // PAGES_PER_STEP
    nc = past // CMP_STRIDE
    block = (None, None, PAGE_SIZE * 8, 128)
    const2 = lambda b, pc, pt: (0, 0)
    return pl.pallas_call(
        functools.partial(_nsa_cmp_decode_kernel, past=past),
        grid_spec=pltpu.PrefetchScalarGridSpec(
            num_scalar_prefetch=1,
            grid=(DB, npc),
            in_specs=_page_specs(block, layer, 2) + [
                pl.BlockSpec((None, A_HEADS, 128), lambda b, pc, pt: (b, 0, 0)),
                pl.BlockSpec((None, 1, 1024), lambda b, pc, pt: (b, 0, 0)),
                pl.BlockSpec((2, 2 * PAGE_SIZE // CMP_STRIDE, PAGE_SIZE), lambda b, pc, pt: (0, 0, 0)),
                pl.BlockSpec((nc, 128), const2),
                pl.BlockSpec((nc, 128), const2),
            ],
            out_specs=[
                pl.BlockSpec((None, A_HEADS, 128), lambda b, pc, pt: (b, 0, 0)),
                pl.BlockSpec((None, A_KV, SEL_TOPN, 1), lambda b, pc, pt: (b, 0, 0, 0)),
            ],
            scratch_shapes=[pltpu.VMEM((nc, 512), F32), pltpu.VMEM((nc, 512), F32)],
        ),
        out_shape=[jax.ShapeDtypeStruct((DB, A_HEADS, 128), F32),
                   jax.ShapeDtypeStruct((DB, A_KV, SEL_TOPN, 1), I32)],
        compiler_params=_cparams(("parallel", "arbitrary")),
        name="nsa_cmp_decode",
    )(page_table, *([cache] * PAGES_PER_STEP), q, new_rows, wc, cc, sc)


def _nsa_sel_decode_kernel(pt_ref, sel_ref, blk0_ref, blk1_ref, win_ref, q_ref, new_ref, neww_ref, oc_ref, g_ref,
                           o_ref, m_ref, l_ref, acc_ref, *, past, lw):
    b, s = pl.program_id(0), pl.program_id(1)
    ns = past // SEL_BLOCK + 1
    new = new_ref[...]
    neww = neww_ref[...]
    blks = (blk0_ref, blk1_ref)

    for kv in range(A_KV):
        q = q_ref[kv * A_G:(kv + 1) * A_G, :].astype(BF16)

        @pl.when(s == 0)
        def _():
            m_ref[kv] = _rowdot(q, new[:, 512 + kv * 128:640 + kv * 128])
            l_ref[kv] = jnp.ones((A_G, 1), F32)
            v_new = new[:, 768 + kv * 128:896 + kv * 128].astype(BF16).astype(F32)
            acc_ref[kv] = jnp.broadcast_to(v_new, (A_G, 128))

        @pl.when(sel_ref[b, kv, s] != ns - 1)
        def _():
            k = blks[kv][pl.ds(4 + kv, SEL_BLOCK, stride=8), :].astype(BF16)
            v = blks[kv][pl.ds(6 + kv, SEL_BLOCK, stride=8), :].astype(BF16)
            m, l, acc = _online_step(_dot_nt(q, k), None, v, m_ref[kv], l_ref[kv], acc_ref[kv])
            m_ref[kv] = m
            l_ref[kv] = l
            acc_ref[kv] = acc

        @pl.when(s == pl.num_programs(1) - 1)
        def _():
            o_s = acc_ref[kv] / jnp.maximum(l_ref[kv], 1e-30)
            kw = win_ref[pl.ds(kv, lw, stride=4), :].astype(BF16)
            vw = win_ref[pl.ds(2 + kv, lw, stride=4), :].astype(BF16)
            vw_new = neww[:, 256 + kv * 128:384 + kv * 128].astype(BF16).astype(F32)
            s_w = _dot_nt(q, kw)
            s_n = _rowdot(q, neww[:, kv * 128:(kv + 1) * 128])
            mw = jnp.maximum(jnp.max(s_w, axis=-1, keepdims=True), s_n)
            e_w = jnp.exp2(s_w - mw)
            e_n = jnp.exp2(s_n - mw)
            den = jnp.sum(e_w, axis=-1, keepdims=True) + e_n
            o_w = (_dot(e_w.astype(BF16), vw) + e_n.astype(BF16).astype(F32) * vw_new) / den
            o_c = oc_ref[kv * A_G:(kv + 1) * A_G, :]
            g = g_ref[kv * A_G:(kv + 1) * A_G, :]
            o_ref[kv * A_G:(kv + 1) * A_G, :] = g[:, 0:1] * o_c + g[:, 1:2] * o_s + g[:, 2:3] * o_w


def _nsa_sel_decode(page_table, sel, cache, win, layer, q, new_rows, new_win, o_c, gates, past):
    DB = q.shape[0]
    n_pages = page_table.shape[1]
    lw = win.shape[2] // 4

    def blk_map(kv):
        def imap(b, s, pt, sel):
            j = sel[b, kv, s]
            return (layer, pt[b, jnp.minimum(j // 2, n_pages - 1)], j % 2, 0)
        return imap

    per_b = lambda b, s, pt, sel: (b, 0, 0)
    return pl.pallas_call(
        functools.partial(_nsa_sel_decode_kernel, past=past, lw=lw),
        grid_spec=pltpu.PrefetchScalarGridSpec(
            num_scalar_prefetch=2,
            grid=(DB, SEL_TOPN),
            in_specs=[
                pl.BlockSpec((None, None, SEL_BLOCK * 8, 128), blk_map(0)),
                pl.BlockSpec((None, None, SEL_BLOCK * 8, 128), blk_map(1)),
                pl.BlockSpec((None, None, lw * 4, 128), lambda b, s, pt, sel: (layer, b, 0, 0)),
                pl.BlockSpec((None, A_HEADS, 128), per_b),
                pl.BlockSpec((None, 1, 1024), per_b),
                pl.BlockSpec((None, 1, 512), per_b),
                pl.BlockSpec((None, A_HEADS, 128), per_b),
                pl.BlockSpec((None, A_HEADS, 128), per_b),
            ],
            out_specs=pl.BlockSpec((None, A_HEADS, 128), per_b),
            scratch_shapes=[pltpu.VMEM((A_KV, A_G, 1), F32), pltpu.VMEM((A_KV, A_G, 1), F32),
                            pltpu.VMEM((A_KV, A_G, 128), F32)],
        ),
        out_shape=jax.ShapeDtypeStruct((DB, A_HEADS, 128), F32),
        compiler_params=_cparams(("parallel", "arbitrary")),
        name="nsa_sel_decode",
    )(page_table, sel, cache, cache, win, q, new_rows, new_win, o_c, gates)


def _dsa_idx_decode_kernel(pt_ref, *refs, past):
    pages = refs[:PAGES_PER_STEP]
    iq_ref, iw_ref, ikn_ref, mask_ref, sc_ref = refs[PAGES_PER_STEP:]
    pc = pl.program_id(1)
    n_pages = past // PAGE_SIZE
    k_top = min(IDX_TOPK, (past + 1) // 4)
    iq = iq_ref[...].astype(BF16)
    iw = iw_ref[...]

    @pl.when(pc == 0)
    def _():
        sc_ref[...] = jnp.full(sc_ref.shape, NEG, F32)

    for u in range(PAGES_PER_STEP):
        s = _dot(iq, pages[u][...].astype(BF16))
        sc_ref[pl.ds(pc * PAGES_PER_STEP + u, 1), :] = jnp.sum(iw * jnp.maximum(s, 0.0), axis=0, keepdims=True)

    @pl.when(pc == pl.num_programs(1) - 1)
    def _():
        s_new = _rowdot(iq, ikn_ref[...])
        s_new = jnp.sum(iw * jnp.maximum(s_new, 0.0), axis=0, keepdims=True)
        lane0 = lax.broadcasted_iota(I32, (1, PAGE_SIZE), 1) == 0
        sc_ref[n_pages:n_pages + 1, :] = jnp.where(lane0, s_new, NEG)
        keys = _sortable(sc_ref[...])

        def count_ge(cand):
            c = jnp.sum(jnp.where(keys >= cand, 1.0, 0.0), axis=-1, keepdims=True)
            return jnp.sum(c, axis=0, keepdims=True)

        thr = _kth_threshold(count_ge, (1, 1), float(k_top))
        mask_ref[...] = _bias((keys >= thr) & (sc_ref[...] > 0.5 * NEG))


def _dsa_idx_decode(page_table, cache, layer, iq, iw, ik_new, past):
    DB = iq.shape[0]
    n_pages = page_table.shape[1]
    npc = n_pages // PAGES_PER_STEP
    rows = ((n_pages + 1 + 7) // 8) * 8
    per_b = lambda b, pc, pt: (b, 0, 0)
    return pl.pallas_call(
        functools.partial(_dsa_idx_decode_kernel, past=past),
        grid_spec=pltpu.PrefetchScalarGridSpec(
            num_scalar_prefetch=1,
            grid=(DB, npc),
            in_specs=_page_specs((None, None, IDX_DIM, PAGE_SIZE), layer, 2) + [
                pl.BlockSpec((None, IDX_HEADS, IDX_DIM), per_b),
                pl.BlockSpec((None, IDX_HEADS, 1), per_b),
                pl.BlockSpec((None, 1, IDX_DIM), per_b),
            ],
            out_specs=pl.BlockSpec((None, rows, PAGE_SIZE), per_b),
            scratch_shapes=[pltpu.VMEM((rows, PAGE_SIZE), F32)],
        ),
        out_shape=jax.ShapeDtypeStruct((DB, rows, PAGE_SIZE), F32),
        compiler_params=_cparams(("parallel", "arbitrary")),
        name="dsa_idx_decode",
    )(page_table, *([cache] * PAGES_PER_STEP), iq, iw, ik_new)


def _dsa_att_decode_kernel(pt_ref, *refs, past):
    pages = refs[:PAGES_PER_STEP]
    q_ref, new_ref, mask_ref, o_ref, m_ref, l_ref, acc_ref = refs[PAGES_PER_STEP:]
    pc = pl.program_id(1)
    n_pages = past // PAGE_SIZE
    q = q_ref[...].astype(BF16)

    @pl.when(pc == 0)
    def _():
        m_ref[...] = jnp.full(m_ref.shape, NEG, F32)
        l_ref[...] = jnp.zeros(l_ref.shape, F32)
        acc_ref[...] = jnp.zeros(acc_ref.shape, F32)

    k = jnp.concatenate([p[pl.ds(0, PAGE_SIZE, stride=2), :] for p in pages], axis=0).astype(BF16)
    v = jnp.concatenate([p[pl.ds(1, PAGE_SIZE, stride=2), :] for p in pages], axis=0).astype(BF16)
    r0 = pl.multiple_of(pc * PAGES_PER_STEP, PAGES_PER_STEP)
    mrows = mask_ref[pl.ds(r0, PAGES_PER_STEP), :]
    bias = jnp.concatenate([mrows[u:u + 1, :] for u in range(PAGES_PER_STEP)], axis=1)
    m, l, acc = _online_step(_dot_nt(q, k), bias, v, m_ref[...], l_ref[...], acc_ref[...])
    m_ref[...] = m
    l_ref[...] = l
    acc_ref[...] = acc

    @pl.when(pc == pl.num_programs(1) - 1)
    def _():
        new = new_ref[...]
        v_new = new[:, 128:256].astype(BF16).astype(F32)
        keep = mask_ref[n_pages:n_pages + 1, 0:1]
        _, l2, acc2 = _online_single(_rowdot(q, new[:, 0:128]), keep, v_new, m_ref[...], l_ref[...], acc_ref[...])
        o_ref[...] = acc2 / jnp.maximum(l2, 1e-30)


def _dsa_att_decode(page_table, cache, layer, q, new_rows, mask, past):
    DB = q.shape[0]
    npc = page_table.shape[1] // PAGES_PER_STEP
    rows = mask.shape[1]
    per_b = lambda b, pc, pt: (b, 0, 0)
    return pl.pallas_call(
        functools.partial(_dsa_att_decode_kernel, past=past),
        grid_spec=pltpu.PrefetchScalarGridSpec(
            num_scalar_prefetch=1,
            grid=(DB, npc),
            in_specs=_page_specs((None, None, PAGE_SIZE * 2, 128), layer, 2) + [
                pl.BlockSpec((None, B_HEADS, 128), per_b),
                pl.BlockSpec((None, 1, 256), per_b),
                pl.BlockSpec((None, rows, PAGE_SIZE), per_b),
            ],
            out_specs=pl.BlockSpec((None, B_HEADS, 128), per_b),
            scratch_shapes=[pltpu.VMEM((B_HEADS, 1), F32), pltpu.VMEM((B_HEADS, 1), F32),
                            pltpu.VMEM((B_HEADS, 128), F32)],
        ),
        out_shape=jax.ShapeDtypeStruct((DB, B_HEADS, 128), F32),
        compiler_params=_cparams(("parallel", "arbitrary")),
        name="dsa_att_decode",
    )(page_table, *([cache] * PAGES_PER_STEP), q, new_rows, mask)


def _diff_decode_kernel(pt_ref, *refs, lam_init):
    pages = refs[:PAGES_PER_STEP]
    q_ref, new_ref, lp_ref, sub_ref, o_ref, m_ref, l_ref, acc_ref = refs[PAGES_PER_STEP:]
    pc = pl.program_id(1)

    @pl.when(pc == 0)
    def _():
        m_ref[...] = jnp.full(m_ref.shape, NEG, F32)
        l_ref[...] = jnp.zeros(l_ref.shape, F32)
        acc_ref[...] = jnp.zeros(acc_ref.shape, F32)

    for kv in range(C_KV):
        q = q_ref[kv].astype(BF16)
        k = jnp.concatenate([p[pl.ds(kv, PAGE_SIZE, stride=4), :] for p in pages], axis=0).astype(BF16)
        v = jnp.concatenate([p[pl.ds(2 + kv, PAGE_SIZE, stride=4), :] for p in pages], axis=0).astype(BF16)
        sc = _dot_nt(q, k)
        m, l, acc = _online_step(sc, None, v, m_ref[kv], l_ref[kv], acc_ref[kv])
        m_ref[kv] = m
        l_ref[kv] = l
        acc_ref[kv] = acc

    @pl.when(pc == pl.num_programs(1) - 1)
    def _():
        new = new_ref[...]
        lam = _lambda_of(lp_ref[...], lam_init)
        for kv in range(C_KV):
            k_new = new[:, kv * 128:(kv + 1) * 128]
            v_new = new[:, 256 + kv * 128:384 + kv * 128].astype(BF16).astype(F32)
            _, l, acc = _online_single(_rowdot(q_ref[kv], k_new), None, v_new, m_ref[kv], l_ref[kv], acc_ref[kv])
            outs = _diff_finish(acc / jnp.maximum(l, 1e-30), lam, sub_ref[...], lam_init, 1)
            o_ref[2 * kv:2 * kv + 1, :] = outs[0]
            o_ref[2 * kv + 1:2 * kv + 2, :] = outs[1]


def _diff_decode(page_table, cache, layer, q, new_rows, lp, subln, lam_init):
    DB = q.shape[0]
    npc = page_table.shape[1] // PAGES_PER_STEP
    per_b = lambda b, pc, pt: (b, 0, 0)
    return pl.pallas_call(
        functools.partial(_diff_decode_kernel, lam_init=lam_init),
        grid_spec=pltpu.PrefetchScalarGridSpec(
            num_scalar_prefetch=1,
            grid=(DB, npc),
            in_specs=_page_specs((None, None, PAGE_SIZE * 4, 128), layer, 2) + [
                pl.BlockSpec((None, C_KV, 4, 128), lambda b, pc, pt: (b, 0, 0, 0)),
                pl.BlockSpec((None, 1, 512), per_b),
                pl.BlockSpec((4, C_HALF), lambda b, pc, pt: (0, 0)),
                pl.BlockSpec((1, 128), lambda b, pc, pt: (0, 0)),
            ],
            out_specs=pl.BlockSpec((None, C_HEADS, 128), per_b),
            scratch_shapes=[pltpu.VMEM((C_KV, 4, 1), F32), pltpu.VMEM((C_KV, 4, 1), F32),
                            pltpu.VMEM((C_KV, 4, 128), F32)],
        ),
        out_shape=jax.ShapeDtypeStruct((DB, C_HEADS, 128), F32),
        compiler_params=_cparams(("parallel", "arbitrary")),
        name="diff_decode",
    )(page_table, *([cache] * PAGES_PER_STEP), q, new_rows, lp, subln)


def _rope_tables(pos, d):
    half = d // 2
    inv = ROPE_THETA ** (-jnp.arange(half, dtype=F32) / half)
    ang = pos.astype(F32)[:, None] * inv[None, :]
    cos, sin = jnp.cos(ang), jnp.sin(ang)
    reps = 128 // d
    return jnp.tile(jnp.concatenate([cos, cos], axis=-1), (1, reps)), jnp.tile(jnp.concatenate([-sin, sin], axis=-1), (1, reps))


def _permute_w_in(w_in):
    sizes = (1024, 1536, 24, 512, 256, 1024, 16, 64, 512, 512)
    offs = np.concatenate([[0], np.cumsum(sizes)])
    a_q, a_kv, a_gate, b_q, b_kv, b_iq, b_iw, b_ik, c_q, c_kv = [
        w_in[:, :, int(offs[i]):int(offs[i + 1])] for i in range(10)]
    pad = jnp.zeros(w_in.shape[:2] + (N_PROJ - 5480,), w_in.dtype)
    return jnp.concatenate([a_q, a_kv, b_q, b_kv, b_iq, c_q, c_kv, b_ik, a_gate, b_iw, pad], axis=-1).astype(BF16)


def _pack_params(nsa_qk_norm, dsa_qk_norm, dsa_idx_knorm, diff_qk_norm):
    rows = [nsa_qk_norm, dsa_qk_norm, jnp.tile(dsa_idx_knorm, 2)[None], jnp.tile(diff_qk_norm, (1, 2))]
    p = jnp.concatenate(rows, axis=0).astype(F32)
    return jnp.pad(p, ((0, 16 - p.shape[0]), (0, 0)))


def _compress_weights(cmp_w, rows):
    eye = jnp.eye(rows // CMP_STRIDE, dtype=F32)
    mats = []
    for c in range(2):
        halves = [jnp.kron(eye, cmp_w[c, h * CMP_STRIDE:(h + 1) * CMP_STRIDE][None, :]) for h in range(2)]
        mats.append(jnp.concatenate(halves, axis=0))
    return jnp.stack(mats, axis=0)


def kernel(x_prompt, x_sample, cache_nsa_kv, state_nsa_win, cache_dsa_kv, cache_dsa_idx, cache_diff_kv, page_table, attn_norm, w_in, nsa_qk_norm, nsa_cmp_w, dsa_qk_norm, dsa_idx_knorm, diff_qk_norm, diff_lambda, diff_subln, w_out, ffn_norm, w_gate_up, w_down):
    B, T, D = x_prompt.shape
    DB = x_sample.shape[0]
    depth = w_in.shape[0]
    n_pages = page_table.shape[1]
    past = n_pages * PAGE_SIZE
    M = B * T
    assert x_sample.shape[1] == 1 and T % TK == 0 and T >= WINDOW + TQ_NSA and n_pages % PAGES_PER_STEP == 0
    tm = min(1024, M)
    tm_ffn = min(1024, M)
    tm_post = 256

    w_in_p = _permute_w_in(w_in)
    w_out_b = w_out.astype(BF16)
    w_gu_b = w_gate_up.astype(BF16)
    w_down_b = w_down.astype(BF16)

    pos_p = jnp.arange(T, dtype=I32)
    tabs_p = _rope_tables(pos_p, 128) + _rope_tables(pos_p, 64)
    pos_s = jnp.full((DB,), past, I32)
    tabs_s = _rope_tables(pos_s, 128) + _rope_tables(pos_s, 64)
    cend_p = jnp.arange(T // CMP_STRIDE, dtype=I32) * CMP_STRIDE + (CMP_LEN - 1)
    cc_p, sc_p = _rope_tables(cend_p, 128)
    cend_s = jnp.arange(past // CMP_STRIDE, dtype=I32) * CMP_STRIDE + (CMP_LEN - 1)
    cc_s, sc_s = _rope_tables(cend_s, 128)

    n_pool = cache_nsa_kv.shape[1]
    nsa_pages = cache_nsa_kv.reshape(depth, n_pool, PAGE_SIZE * 8, 128)
    dsa_pages = cache_dsa_kv.reshape(depth, n_pool, PAGE_SIZE * 2, 128)
    diff_pages = cache_diff_kv.reshape(depth, n_pool, PAGE_SIZE * 4, 128)
    win_rows = state_nsa_win.reshape(depth, DB, state_nsa_win.shape[2] * 4, 128)
    idx_pages = jnp.swapaxes(cache_dsa_idx, 2, 3)

    yp = x_prompt.reshape(M, D)
    ys = x_sample.reshape(DB, D)
    rows_p, rows_s = [], []
    for l in range(depth):
        lam_init = 0.8 - 0.6 * math.exp(-0.3 * l)
        prm = _pack_params(nsa_qk_norm[l], dsa_qk_norm[l], dsa_idx_knorm[l], diff_qk_norm[l])
        g_attn = attn_norm[l][None, :]
        g_ffn = ffn_norm[l][None, :]
        lp = diff_lambda[l].astype(F32)
        subln = diff_subln[l][None, :].astype(F32)

        proj = _norm_matmul(yp, g_attn, w_in_p, l, tm).reshape(B, T, N_PROJ)
        wc = _compress_weights(nsa_cmp_w[l], tm_post)
        (nsa, win, dsa, ik, dif, qa, ksel, vsel, kw, vw, qb, kb, vb, iq, ikd, iw, qc, kcd, vcd, gat,
         pa, pb) = _post_project(proj, tabs_p, prm, wc, tm_post, BF16)
        o_a = _nsa_prompt(qa, pa, pb, cc_p, sc_p, ksel, vsel, kw, vw, gat)
        o_b = _dsa_prompt(qb, iq, iw, ikd, kb, vb)
        o_c = _diff_prompt(qc, kcd, vcd, lp, subln, lam_init)
        mix = jnp.concatenate([o_a, o_b, o_c], axis=-1).reshape(M, D)
        yp = _matmul_residual(mix, w_out_b, yp, l, tm)
        act = _norm_swiglu(yp, g_ffn, w_gu_b, l, tm_ffn)
        yp = _matmul_residual(act, w_down_b, yp, l, tm_ffn)
        w_keep = min(WINDOW, T)
        rows_p.append((nsa.reshape(B, T, 4, A_KV, 128), win[:, (T - w_keep) * 4:].reshape(B, w_keep, 2, A_KV, 128),
                       dsa.reshape(B, T, 2, 128), ik, dif.reshape(B, T, 2, C_KV, 128)))

        proj_s = _norm_matmul(ys, g_attn, w_in_p, l, DB).reshape(1, DB, N_PROJ)
        (nsa_s, win_s, dsa_s, ik_s, dif_s, qa_s, _, _, _, _, qb_s, _, _, iq_s, _, iw_s, qc_s, _, _,
         gat_s) = _post_project(proj_s, tabs_s, prm, None, DB, F32)
        nsa_new = nsa_s.reshape(DB, 1, 1024)
        win_new = win_s.reshape(DB, 1, 512)
        dsa_new = dsa_s.reshape(DB, 1, 256)
        ik_new = ik_s.reshape(DB, 1, IDX_DIM)
        dif_new = dif_s.reshape(DB, 1, 512)
        qa_d = jnp.transpose(qa_s[0], (1, 0, 2))
        qb_d = jnp.transpose(qb_s[0], (1, 0, 2))
        iq_d = jnp.transpose(iq_s[0], (1, 0, 2))
        iq_d = iq_d[:, :, :64] + iq_d[:, :, 64:]
        iw_d = iw_s[0, :, :IDX_HEADS, None]
        qc_d = jnp.transpose(qc_s[0], (2, 0, 1, 3))
        g_d = jnp.transpose(gat_s[0, :, :, :12].reshape(A_KV, DB, A_G, 3), (1, 0, 2, 3)).reshape(DB, A_HEADS, 3)
        g_d = jnp.pad(g_d, ((0, 0), (0, 0), (0, 125)))

        wc_s = _compress_weights(nsa_cmp_w[l], PAGE_SIZE)
        oc_d, sel = _nsa_cmp_decode(page_table, nsa_pages, l, qa_d, nsa_new, wc_s, cc_s, sc_s, past)
        oa_d = _nsa_sel_decode(page_table, sel.reshape(DB, A_KV, SEL_TOPN), nsa_pages, win_rows, l,
                               qa_d, nsa_new, win_new, oc_d, g_d, past)
        mask = _dsa_idx_decode(page_table, idx_pages, l, iq_d, iw_d, ik_new, past)
        ob_d = _dsa_att_decode(page_table, dsa_pages, l, qb_d, dsa_new, mask, past)
        od_d = _diff_decode(page_table, diff_pages, l, qc_d, dif_new, lp, subln, lam_init)
        mix_s = jnp.concatenate([oa_d.reshape(DB, 1024), ob_d.reshape(DB, 512), od_d.reshape(DB, 512)],
                                axis=-1).astype(BF16)
        ys = _matmul_residual(mix_s, w_out_b, ys, l, DB)
        act_s = _norm_swiglu(ys, g_ffn, w_gu_b, l, DB)
        ys = _matmul_residual(act_s, w_down_b, ys, l, DB)
        lw = state_nsa_win.shape[2]
        win_all = jnp.concatenate([state_nsa_win[l], win_new.reshape(DB, 1, 2, A_KV, 128)], axis=1)
        rows_s.append((nsa_new.reshape(DB, 1, 4, A_KV, 128), win_all[:, win_all.shape[1] - min(WINDOW, lw + 1):],
                       dsa_new.reshape(DB, 1, 2, 128), ik_new, dif_new.reshape(DB, 1, 2, C_KV, 128)))

    def stacked(rows, i):
        return jnp.stack([r[i] for r in rows], axis=0)

    return (yp.reshape(B, T, D), ys.reshape(DB, 1, D),
            stacked(rows_p, 0), stacked(rows_s, 0), stacked(rows_p, 1), stacked(rows_s, 1),
            stacked(rows_p, 2), stacked(rows_s, 2), stacked(rows_p, 3), stacked(rows_s, 3),
            stacked(rows_p, 4), stacked(rows_s, 4))
```

```python
import functools
import math

import jax
import jax.numpy as jnp
from jax import lax
from jax.experimental import pallas as pl
from jax.experimental.pallas import tpu as pltpu

F32 = jnp.float32
BF16 = jnp.bfloat16
I32 = jnp.int32
HI = lax.Precision.HIGHEST

D_MODEL = 2048
PAGE_SIZE = 128
D_HEAD = 128
A_HEADS = 8
A_KV = 2
A_G = A_HEADS // A_KV
B_HEADS = 4
C_HEADS = 4
C_KV = 2
C_HALF = 64
IDX_HEADS = 16
IDX_DIM = 64
IDX_TOPK = 256
CMP_LEN = 32
CMP_STRIDE = 16
SEL_BLOCK = 64
SEL_TOPN = 16
WINDOW = 512
FORCE_BONUS = 1.0e4
D_FF = 5632
ROPE_THETA = 10000.0
EPS = 1e-6
NEG = -1e30
INT_MIN = -2147483648

OFF_AQ = 0
OFF_AKV = 1024
OFF_BQ = 2560
OFF_BKV = 3072
OFF_BIQ = 3328
OFF_CQ = 4352
OFF_CKV = 4864
OFF_MISC = 5376
N_PROJ = 5632
MISC_GATE = 64
MISC_IW = 88

TQ_NSA = 256
TQ_DSA = 128
TQ_DIFF = 256
TK = 512
PAGES_PER_STEP = 16
SEL_SHIFT = 6
SEL_PER_STEP = 4
LOG2E = math.log2(math.e)
SCALE_D = D_HEAD ** -0.5 * LOG2E
SCALE_C = C_HALF ** -0.5 * LOG2E
MASKED = -2e30
VMEM_LIMIT = 56 * 1024 * 1024


def _cparams(sem):
    return pltpu.CompilerParams(dimension_semantics=sem, vmem_limit_bytes=VMEM_LIMIT)


def _dot(a, b, precision=None):
    return jnp.dot(a, b, preferred_element_type=F32, precision=precision)


def _dot3(a, b):
    a_hi = a.astype(BF16)
    b_hi = b.astype(BF16)
    a_lo = (a - a_hi.astype(F32)).astype(BF16)
    b_lo = (b - b_hi.astype(F32)).astype(BF16)
    return _dot(a_hi, b_hi) + (_dot(a_hi, b_lo) + _dot(a_lo, b_hi))


def _dot_nt(a, b, precision=None):
    return lax.dot_general(a, b, (((1,), (1,)), ((), ())), preferred_element_type=F32, precision=precision)


def _norm_mm_kernel(x_ref, g_ref, w_ref, o_ref, xn_ref):
    @pl.when(pl.program_id(1) == 0)
    def _():
        x = x_ref[...]
        ms = jnp.mean(x * x, axis=-1, keepdims=True)
        xn_ref[...] = (x * lax.rsqrt(ms + EPS) * g_ref[...]).astype(BF16)

    o_ref[...] = _dot(xn_ref[...], w_ref[...])


def _norm_matmul(x, g, w, layer, tm, tn=512):
    M, K = x.shape
    N = w.shape[2]
    return pl.pallas_call(
        _norm_mm_kernel,
        grid=(M // tm, N // tn),
        in_specs=[
            pl.BlockSpec((tm, K), lambda i, j: (i, 0)),
            pl.BlockSpec((1, K), lambda i, j: (0, 0)),
            pl.BlockSpec((None, K, tn), lambda i, j: (layer, 0, j)),
        ],
        out_specs=pl.BlockSpec((tm, tn), lambda i, j: (i, j)),
        out_shape=jax.ShapeDtypeStruct((M, N), F32),
        scratch_shapes=[pltpu.VMEM((tm, K), BF16)],
        compiler_params=_cparams(("parallel", "arbitrary")),
        name="norm_matmul",
    )(x, g, w)


def _norm_swiglu_kernel(x_ref, g_ref, wg_ref, wu_ref, o_ref, xn_ref):
    @pl.when(pl.program_id(1) == 0)
    def _():
        x = x_ref[...]
        ms = jnp.mean(x * x, axis=-1, keepdims=True)
        xn_ref[...] = (x * lax.rsqrt(ms + EPS) * g_ref[...]).astype(BF16)

    xn = xn_ref[...]
    gate = _dot(xn, wg_ref[...])
    up = _dot(xn, wu_ref[...])
    o_ref[...] = (gate * jax.nn.sigmoid(gate) * up).astype(o_ref.dtype)


def _norm_swiglu(x, g, w, layer, tm, tn=512):
    M, K = x.shape
    nj = D_FF // tn
    return pl.pallas_call(
        _norm_swiglu_kernel,
        grid=(M // tm, nj),
        in_specs=[
            pl.BlockSpec((tm, K), lambda i, j: (i, 0)),
            pl.BlockSpec((1, K), lambda i, j: (0, 0)),
            pl.BlockSpec((None, K, tn), lambda i, j: (layer, 0, j)),
            pl.BlockSpec((None, K, tn), lambda i, j: (layer, 0, j + nj)),
        ],
        out_specs=pl.BlockSpec((tm, tn), lambda i, j: (i, j)),
        out_shape=jax.ShapeDtypeStruct((M, D_FF), BF16),
        scratch_shapes=[pltpu.VMEM((tm, K), BF16)],
        compiler_params=_cparams(("parallel", "arbitrary")),
        name="norm_swiglu",
    )(x, g, w, w)


def _mm_res_kernel(a_ref, w_ref, r_ref, o_ref):
    o_ref[...] = r_ref[...] + _dot(a_ref[...], w_ref[...])


def _matmul_residual(a, w, res, layer, tm, tn=512):
    M, K = a.shape
    N = w.shape[2]
    return pl.pallas_call(
        _mm_res_kernel,
        grid=(M // tm, N // tn),
        in_specs=[
            pl.BlockSpec((tm, K), lambda i, j: (i, 0)),
            pl.BlockSpec((None, K, tn), lambda i, j: (layer, 0, j)),
            pl.BlockSpec((tm, tn), lambda i, j: (i, j)),
        ],
        out_specs=pl.BlockSpec((tm, tn), lambda i, j: (i, j)),
        out_shape=jax.ShapeDtypeStruct((M, N), F32),
        compiler_params=_cparams(("parallel", "arbitrary")),
        name="matmul_residual",
    )(a, w, res)


def _mix_out_kernel(a_ref, b_ref, c_ref, w_ref, r_ref, o_ref):
    ka, kb = a_ref.shape[1], b_ref.shape[1]
    acc = _dot(a_ref[...], w_ref[0:ka, :])
    acc = acc + _dot(b_ref[...], w_ref[ka:ka + kb, :])
    acc = acc + _dot(c_ref[...], w_ref[ka + kb:, :])
    o_ref[...] = r_ref[...] + acc


def _mix_out_projection(o_a, o_b, o_c, w, res, layer, tm, tn=512):
    M, N = res.shape
    K = w.shape[1]

    def rows(x):
        return pl.BlockSpec((tm, x.shape[1]), lambda i, j: (i, 0))

    return pl.pallas_call(
        _mix_out_kernel,
        grid=(M // tm, N // tn),
        in_specs=[
            rows(o_a), rows(o_b), rows(o_c),
            pl.BlockSpec((None, K, tn), lambda i, j: (layer, 0, j)),
            pl.BlockSpec((tm, tn), lambda i, j: (i, j)),
        ],
        out_specs=pl.BlockSpec((tm, tn), lambda i, j: (i, j)),
        out_shape=jax.ShapeDtypeStruct((M, N), F32),
        compiler_params=_cparams(("parallel", "arbitrary")),
        name="mix_out_projection",
    )(o_a, o_b, o_c, w, res)


_W_IN_PIECES = ((OFF_AQ, 0, 2560), (OFF_BQ, 2584, 1792), (OFF_CQ, 4456, 1024), (OFF_MISC, 4392, 64),
                (OFF_MISC + MISC_GATE, 2560, 24), (OFF_MISC + MISC_IW, 4376, 16))
_W_IN_COLS = 5480


def _permute_cast_kernel(x_ref, o_ref):
    for dst, src, width in _W_IN_PIECES:
        o_ref[:, dst:dst + width] = x_ref[:, src:src + width].astype(BF16)
    o_ref[:, _W_IN_COLS:] = jnp.zeros((o_ref.shape[0], N_PROJ - _W_IN_COLS), BF16)


def _permute_cast_w_in(w_in, tk=256):
    depth, K, _ = w_in.shape
    return pl.pallas_call(
        _permute_cast_kernel,
        grid=(depth, K // tk),
        in_specs=[pl.BlockSpec((None, tk, _W_IN_COLS), lambda l, i: (l, i, 0))],
        out_specs=pl.BlockSpec((None, tk, N_PROJ), lambda l, i: (l, i, 0)),
        out_shape=jax.ShapeDtypeStruct((depth, K, N_PROJ), BF16),
        compiler_params=_cparams(("parallel", "parallel")),
        name="permute_cast_w_in",
    )(w_in)


def _post_kernel(*refs, emit_cmp, tm):
    if emit_cmp:
        x_ref, c1_ref, s1_ref, c2_ref, s2_ref, prm_ref, wc_ref = refs[:7]
        outs = refs[7:]
    else:
        x_ref, c1_ref, s1_ref, c2_ref, s2_ref, prm_ref = refs[:6]
        wc_ref = None
        outs = refs[6:]
    (nsa_ref, win_ref, dsa_ref, ik_ref, dif_ref, qa_ref, ksel_ref, vsel_ref, kw_ref, vw_ref,
     qb_ref, kb_ref, vb_ref, iq_ref, ikd_ref, iw_ref, qc_ref, kcd_ref, vcd_ref, gat_ref) = outs[:20]

    c1, s1, c2, s2 = c1_ref[...], s1_ref[...], c2_ref[...], s2_ref[...]
    prm = prm_ref[...]
    lane = lax.broadcasted_iota(I32, (tm, 128), 1)
    lo = lane < 64
    inner = (lane & 63) < 32

    def col(a):
        return x_ref[:, a:a + 128]

    def gain(r):
        return prm[r:r + 1, :]

    def rms128(v, g):
        return v * lax.rsqrt(jnp.mean(v * v, axis=-1, keepdims=True) + EPS) * g

    def rope128(v):
        return v * c1 + pltpu.roll(v, 64, 1) * s1

    def half_ms(v):
        sq = v * v
        a = jnp.sum(jnp.where(lo, sq, 0.0), axis=-1, keepdims=True)
        b = jnp.sum(jnp.where(lo, 0.0, sq), axis=-1, keepdims=True)
        return a * (1.0 / 64), b * (1.0 / 64)

    def rms64(v, g):
        a, b = half_ms(v)
        return v * lax.rsqrt(jnp.where(lo, a, b) + EPS) * g

    def rope64(v):
        rot = jnp.where(inner, pltpu.roll(v, 96, 1), pltpu.roll(v, 32, 1))
        return v * c2 + rot * s2

    for h in range(A_HEADS):
        qa_ref[h] = (rope128(rms128(col(OFF_AQ + h * 128), gain(0))) * SCALE_D).astype(qa_ref.dtype)
    for kv in range(A_KV):
        o = kv * 128
        kc = rms128(col(OFF_AKV + o), gain(1))
        vc = col(OFF_AKV + 256 + o)
        ks = rope128(rms128(col(OFF_AKV + 512 + o), gain(2)))
        vs = col(OFF_AKV + 768 + o)
        kw = rope128(rms128(col(OFF_AKV + 1024 + o), gain(3)))
        vw = col(OFF_AKV + 1280 + o)
        for slab, val in ((kv, kc), (2 + kv, vc), (4 + kv, ks), (6 + kv, vs)):
            nsa_ref[pl.ds(slab, tm, stride=8), :] = val
        win_ref[pl.ds(kv, tm, stride=4), :] = kw
        win_ref[pl.ds(2 + kv, tm, stride=4), :] = vw
        ksel_ref[kv] = ks.astype(ksel_ref.dtype)
        vsel_ref[kv] = vs.astype(vsel_ref.dtype)
        kw_ref[kv] = kw.astype(kw_ref.dtype)
        vw_ref[kv] = vw.astype(vw_ref.dtype)
        if emit_cmp:
            pa_ref, pb_ref = outs[20], outs[21]
            nch = tm // CMP_STRIDE
            pk = _dot3(wc_ref[0], kc)
            pv = _dot3(wc_ref[1], vc)
            pa_ref[kv] = pk[0:nch]
            pb_ref[kv] = pk[nch:2 * nch]
            pa_ref[2 + kv] = pv[0:nch]
            pb_ref[2 + kv] = pv[nch:2 * nch]
    for h in range(B_HEADS):
        qb_ref[h] = (rope128(rms128(col(OFF_BQ + h * 128), gain(4))) * SCALE_D).astype(qb_ref.dtype)
    kb = rope128(rms128(col(OFF_BKV), gain(5)))
    vb = col(OFF_BKV + 128)
    dsa_ref[pl.ds(0, tm, stride=2), :] = kb
    dsa_ref[pl.ds(1, tm, stride=2), :] = vb
    kb_ref[...] = kb.astype(kb_ref.dtype)
    vb_ref[...] = vb.astype(vb_ref.dtype)
    for p in range(IDX_HEADS // 2):
        v = rope64(col(OFF_BIQ + p * 128))
        iq_ref[2 * p] = jnp.where(lo, v, 0.0).astype(iq_ref.dtype)
        iq_ref[2 * p + 1] = jnp.where(lo, 0.0, v).astype(iq_ref.dtype)
    for h in range(C_HEADS):
        v = rope64(rms64(col(OFF_CQ + h * 128), gain(7))) * SCALE_C
        kv, g = h // 2, h % 2
        qc_ref[kv, 2 * g] = jnp.where(lo, v, 0.0).astype(qc_ref.dtype)
        qc_ref[kv, 2 * g + 1] = jnp.where(lo, 0.0, v).astype(qc_ref.dtype)
    for kv in range(C_KV):
        o = kv * 128
        kk = rope64(rms64(col(OFF_CKV + o), gain(8)))
        vv = col(OFF_CKV + 256 + o)
        dif_ref[pl.ds(kv, tm, stride=4), :] = kk
        dif_ref[pl.ds(2 + kv, tm, stride=4), :] = vv
        kcd_ref[kv] = kk.astype(kcd_ref.dtype)
        vcd_ref[kv] = vv.astype(vcd_ref.dtype)
    m = col(OFF_MISC)
    a, _ = half_ms(m)
    ikr = rope64(m * lax.rsqrt(a + EPS) * gain(6))
    ik_ref[...] = ikr[:, 0:64]
    ikd_ref[...] = jnp.where(lo, ikr, pltpu.roll(ikr, 64, 1)).astype(ikd_ref.dtype)
    sig = jax.nn.sigmoid(m)
    for kv in range(A_KV):
        gat_ref[kv] = pltpu.roll(sig, 128 - MISC_GATE - 12 * kv, 1)
    iw_ref[...] = pltpu.roll(m, 128 - MISC_IW, 1) * ((IDX_DIM ** -0.5) * (IDX_HEADS ** -0.5))


def _post_project(proj, tabs, prm, wc, tm, qdt):
    B, T, _ = proj.shape
    emit_cmp = wc is not None
    nt = T // tm

    def row(c):
        return pl.BlockSpec((None, tm, c), lambda b, i: (b, i, 0))

    def heads(*lead):
        n = len(lead)
        return pl.BlockSpec((None,) + lead + (tm, 128), lambda b, i: (b,) + (0,) * n + (i, 0))

    tab = pl.BlockSpec((tm, 128), lambda b, i: (i, 0))
    in_specs = [row(N_PROJ), tab, tab, tab, tab, pl.BlockSpec((16, 128), lambda b, i: (0, 0))]
    args = [proj, *tabs, prm]
    if emit_cmp:
        in_specs.append(pl.BlockSpec((2, 2 * tm // CMP_STRIDE, tm), lambda b, i: (0, 0, 0)))
        args.append(wc)

    def sds(shape, dt):
        return jax.ShapeDtypeStruct(shape, dt)

    def slabs(n):
        return pl.BlockSpec((None, tm * n, 128), lambda b, i: (b, i, 0))

    out_shape = [
        sds((B, T * 8, 128), F32), sds((B, T * 4, 128), F32), sds((B, T * 2, 128), F32), sds((B, T, 64), F32),
        sds((B, T * 4, 128), F32),
        sds((B, A_HEADS, T, 128), qdt),
        sds((B, A_KV, T, 128), qdt), sds((B, A_KV, T, 128), qdt),
        sds((B, A_KV, T, 128), qdt), sds((B, A_KV, T, 128), qdt),
        sds((B, B_HEADS, T, 128), qdt), sds((B, T, 128), qdt), sds((B, T, 128), qdt),
        sds((B, IDX_HEADS, T, 128), qdt), sds((B, T, 128), qdt), sds((B, T, 128), F32),
        sds((B, C_KV, 4, T, 128), qdt), sds((B, C_KV, T, 128), qdt), sds((B, C_KV, T, 128), qdt),
        sds((B, A_KV, T, 128), F32),
    ]
    out_specs = [
        slabs(8), slabs(4), slabs(2), row(64), slabs(4),
        heads(A_HEADS), heads(A_KV), heads(A_KV), heads(A_KV), heads(A_KV),
        heads(B_HEADS), row(128), row(128), heads(IDX_HEADS), row(128), row(128),
        heads(C_KV, 4), heads(C_KV), heads(C_KV), heads(A_KV),
    ]
    if emit_cmp:
        nc = T // CMP_STRIDE
        out_shape += [sds((B, 4, nc, 128), F32), sds((B, 4, nc, 128), F32)]
        spec = pl.BlockSpec((None, 4, tm // CMP_STRIDE, 128), lambda b, i: (b, 0, i, 0))
        out_specs += [spec, spec]
    return pl.pallas_call(
        functools.partial(_post_kernel, emit_cmp=emit_cmp, tm=tm),
        grid=(B, nt),
        in_specs=in_specs,
        out_specs=out_specs,
        out_shape=out_shape,
        compiler_params=_cparams(("parallel", "parallel")),
        name="post_project",
    )(*args)


def _bias(mask):
    return jnp.where(mask, 0.0, MASKED)


def _masked_softmax(s, bias):
    s = s + bias
    m = jnp.maximum(jnp.max(s, axis=-1, keepdims=True), NEG)
    e = jnp.exp2(s - m)
    return e / jnp.maximum(jnp.sum(e, axis=-1, keepdims=True), 1e-30)


def _online_step(s, bias, v, m, l, acc):
    if bias is not None:
        s = s + bias
    m_new = jnp.maximum(m, jnp.max(s, axis=-1, keepdims=True))
    alpha = jnp.exp2(m - m_new)
    e = jnp.exp2(s - m_new)
    l_new = alpha * l + jnp.sum(e, axis=-1, keepdims=True)
    rows = acc.shape[0]
    pv = _dot(e.reshape(rows, e.shape[-1]).astype(BF16), v)
    return m_new, l_new, alpha.reshape(rows, 1) * acc + pv


N_CHAINS = 1


def _flash_init(heads, tq):
    hc = heads // N_CHAINS
    return tuple((jnp.full((hc, tq, 1), NEG, F32), jnp.zeros((hc, tq, 1), F32), jnp.zeros((hc * tq, 128), F32))
                 for _ in range(N_CHAINS))


def _flash_tile(q, k, v, bias, carry):
    out = []
    for c, (m, l, acc) in enumerate(carry):
        rows = acc.shape[0]
        s = _dot_nt(q[c * rows:(c + 1) * rows], k).reshape(m.shape[0], m.shape[1], k.shape[0])
        out.append(_online_step(s, bias, v, m, l, acc))
    return tuple(out)


def _flash_finish(carry):
    return jnp.concatenate([acc / jnp.maximum(l.reshape(acc.shape[0], 1), 1e-30) for _, l, acc in carry], axis=0)


def _online_single(s, bias, v_row, m, l, acc):
    if bias is not None:
        s = s + bias
    m_new = jnp.maximum(m, s)
    alpha = jnp.exp2(m - m_new)
    e = jnp.exp2(s - m_new)
    return m_new, alpha * l + e, alpha * acc + e.astype(BF16).astype(F32) * v_row


def _rowdot(q, k_row):
    return jnp.sum(q.astype(BF16).astype(F32) * k_row.astype(BF16).astype(F32), axis=-1, keepdims=True)


def _sortable(x):
    b = lax.bitcast_convert_type(x + 0.0, I32)
    return jnp.where(b < 0, b ^ jnp.int32(0x7FFFFFFF), b)


def _lambda_of(lp, lam_init):
    a = jnp.sum(lp[0:1] * lp[1:2], axis=-1, keepdims=True)
    b = jnp.sum(lp[2:3] * lp[3:4], axis=-1, keepdims=True)
    return jnp.exp(a) - jnp.exp(b) + lam_init


def _nsa_prompt_kernel(q_ref, pak_ref, pbk_ref, pav_ref, pbv_ref, cc_ref, sc_ref,
                       ks_ref, vs_ref, kw_ref, vw_ref, g_ref, o_ref, kc_ref, vc_ref, *, T):
    TQ = TQ_NSA
    qi = pl.program_id(2)
    nc = T // CMP_STRIDE
    ns = T // SEL_BLOCK
    R = A_G * TQ
    band = WINDOW + TQ

    @pl.when(qi == 0)
    def _():
        kraw = pak_ref[...] + pltpu.roll(pbk_ref[...], nc - 1, 0)
        kc_ref[...] = (kraw * cc_ref[...] + pltpu.roll(kraw, 64, 1) * sc_ref[...]).astype(BF16)
        vc_ref[...] = (pav_ref[...] + pltpu.roll(pbv_ref[...], nc - 1, 0)).astype(BF16)

    q = q_ref[...].reshape(R, 128)
    t0 = qi * TQ
    qp3 = t0 + lax.broadcasted_iota(I32, (1, TQ, 1), 1)

    s_c = _dot_nt(q, kc_ref[...]).reshape(A_G, TQ, nc)
    cend = lax.broadcasted_iota(I32, (1, TQ, nc), 2) * CMP_STRIDE + (CMP_LEN - 1)
    p_c = _masked_softmax(s_c, _bias(cend <= qp3))
    o_c = _dot(p_c.reshape(R, nc).astype(BF16), vc_ref[...])

    sj = lax.broadcasted_iota(I32, (ns, nc), 0) * SEL_BLOCK
    ci = lax.broadcasted_iota(I32, (ns, nc), 1) * CMP_STRIDE
    overlap_t = ((ci < sj + SEL_BLOCK) & (ci + CMP_LEN > sj)).astype(F32)
    imp = _dot_nt(overlap_t, jnp.sum(p_c, axis=0), HI)
    jidx = lax.broadcasted_iota(I32, (ns, TQ), 0)
    jq = (t0 + lax.broadcasted_iota(I32, (1, TQ), 1)) >> SEL_SHIFT
    forced = (jidx == 0) | (jidx == jq) | (jidx == jq - 1)
    imp = jnp.where(forced, imp + FORCE_BONUS, imp)
    imp = jnp.where(jidx > jq, NEG, imp)
    ng = ns // 8
    sub = lax.broadcasted_iota(I32, (8, TQ), 0)
    imp_g = [imp[8 * g:8 * g + 8, :] for g in range(ng)]
    rank_g = [jnp.zeros((8, TQ), F32) for _ in range(ng)]
    for j in range(ns):
        rj = jnp.broadcast_to(imp[j:j + 1, :], (8, TQ))
        for g in range(ng):
            if g < j // 8:
                ahead = rj > imp_g[g]
            elif g > j // 8:
                ahead = rj >= imp_g[g]
            else:
                ahead = (rj > imp_g[g]) | ((rj == imp_g[g]) & (sub > j % 8))
            rank_g[g] = rank_g[g] + jnp.where(ahead, 1.0, 0.0)
    rank = jnp.concatenate(rank_g, axis=0)
    selb = jnp.where(rank < min(SEL_TOPN, ns), 1.0, 0.0).T.astype(BF16)

    erow = lax.broadcasted_iota(I32, (ns, TK), 0)
    ecol = lax.broadcasted_iota(I32, (ns, TK), 1)
    tcol = lax.broadcasted_iota(I32, (1, TQ, TK), 2)

    def sel_step(kt, carry):
        base = pl.multiple_of(kt * TK, TK)
        expand = (erow == ((ecol + base) >> SEL_SHIFT)).astype(BF16)
        chosen = _dot(selb, expand).reshape(1, TQ, TK) > 0.5
        bias = _bias(chosen & (tcol + base <= qp3))
        return _flash_tile(q, ks_ref[pl.ds(base, TK), :], vs_ref[pl.ds(base, TK), :], bias, carry)

    nkt = (t0 + TQ + TK - 1) // TK
    o_s = _flash_finish(lax.fori_loop(0, nkt, sel_step, _flash_init(A_G, TQ)))

    start = pl.multiple_of(jnp.maximum(t0 - WINDOW, 0), TQ)
    kwin = kw_ref[pl.ds(start, band), :]
    vwin = vw_ref[pl.ds(start, band), :]
    dist = qp3 - (start + lax.broadcasted_iota(I32, (1, TQ, band), 2))
    s_w = _dot_nt(q, kwin).reshape(A_G, TQ, band)
    p_w = _masked_softmax(s_w, _bias((dist >= 0) & (dist <= WINDOW)))
    o_w = _dot(p_w.reshape(R, band).astype(BF16), vwin)

    g = g_ref[...]
    for h in range(A_G):
        r = slice(h * TQ, (h + 1) * TQ)
        o = g[:, 3 * h:3 * h + 1] * o_c[r] + g[:, 3 * h + 1:3 * h + 2] * o_s[r] + g[:, 3 * h + 2:3 * h + 3] * o_w[r]
        o_ref[:, h * 128:(h + 1) * 128] = o.astype(o_ref.dtype)


def _nsa_prompt(qa, pa, pb, cc, sc, ksel, vsel, kw, vw, gat):
    B, _, T, _ = qa.shape
    TQ = TQ_NSA
    nc = T // CMP_STRIDE
    part_k = pl.BlockSpec((None, None, nc, 128), lambda b, kv, i: (b, kv, 0, 0))
    part_v = pl.BlockSpec((None, None, nc, 128), lambda b, kv, i: (b, 2 + kv, 0, 0))
    tabc = pl.BlockSpec((nc, 128), lambda b, kv, i: (0, 0))
    full = pl.BlockSpec((None, None, T, 128), lambda b, kv, i: (b, kv, 0, 0))
    return pl.pallas_call(
        functools.partial(_nsa_prompt_kernel, T=T),
        grid=(B, A_KV, T // TQ),
        in_specs=[
            pl.BlockSpec((None, A_G, TQ, 128), lambda b, kv, i: (b, kv, i, 0)),
            part_k, part_k, part_v, part_v, tabc, tabc, full, full, full, full,
            pl.BlockSpec((None, None, TQ, 128), lambda b, kv, i: (b, kv, i, 0)),
        ],
        out_specs=pl.BlockSpec((None, TQ, A_G * 128), lambda b, kv, i: (b, i, kv)),
        out_shape=jax.ShapeDtypeStruct((B, T, A_HEADS * 128), BF16),
        scratch_shapes=[pltpu.VMEM((nc, 128), BF16), pltpu.VMEM((nc, 128), BF16)],
        compiler_params=_cparams(("parallel", "parallel", "arbitrary")),
        name="nsa_prompt",
    )(qa, pa, pb, pa, pb, cc, sc, ksel, vsel, kw, vw, gat)


def _kth_threshold(count_ge, shape, k):
    def step(it, t):
        cand = t + jnp.left_shift(jnp.int32(1), 31 - it)
        return jnp.where(count_ge(cand) >= k, cand, t)

    return lax.fori_loop(0, 32, step, jnp.full(shape, INT_MIN, I32))


def _dsa_prompt_kernel(q_ref, iq_ref, iw_ref, ik_ref, k_ref, v_ref, o_ref, key_ref, *, T):
    TQ = TQ_DSA
    qi = pl.program_id(1)
    t0 = qi * TQ
    nkt = (t0 + TQ + TK - 1) // TK
    k_top = min(IDX_TOPK, T // 4)
    qp_l = t0 + lax.broadcasted_iota(I32, (1, TQ), 1)
    trow = lax.broadcasted_iota(I32, (TK, TQ), 0)
    iq = iq_ref[...].reshape(IDX_HEADS * TQ, 128)
    iw_t = iw_ref[...].T

    def score_step(kt, _):
        base = pl.multiple_of(kt * TK, TK)
        s = _dot_nt(ik_ref[pl.ds(base, TK), :], iq)
        sc = jnp.zeros((TK, TQ), F32)
        for h in range(IDX_HEADS):
            sc = sc + iw_t[h:h + 1, :] * jnp.maximum(s[:, h * TQ:(h + 1) * TQ], 0.0)
        sc = jnp.where(trow + base <= qp_l, sc, NEG)
        key_ref[pl.ds(base, TK), :] = _sortable(sc)
        return 0

    lax.fori_loop(0, nkt, score_step, 0)

    def count_ge(cand):
        def cstep(kt, c):
            base = pl.multiple_of(kt * TK, TK)
            hit = jnp.where(key_ref[pl.ds(base, TK), :] >= cand, 1.0, 0.0)
            return c + jnp.sum(hit.reshape(TK // 64, 8, 8, TQ), axis=0)

        c = lax.fori_loop(0, nkt, cstep, jnp.zeros((8, 8, TQ), F32))
        return jnp.sum(jnp.sum(c, axis=0), axis=0, keepdims=True)

    thr = _kth_threshold(count_ge, (1, TQ), float(k_top))

    q = q_ref[...].reshape(B_HEADS * TQ, 128)

    def att_step(kt, carry):
        base = pl.multiple_of(kt * TK, TK)
        keep = (key_ref[pl.ds(base, TK), :] >= thr) & (trow + base <= qp_l)
        bias = _bias(keep).T.reshape(1, TQ, TK)
        return _flash_tile(q, k_ref[pl.ds(base, TK), :], v_ref[pl.ds(base, TK), :], bias, carry)

    o = _flash_finish(lax.fori_loop(0, nkt, att_step, _flash_init(B_HEADS, TQ)))
    for h in range(B_HEADS):
        o_ref[:, h * 128:(h + 1) * 128] = o[h * TQ:(h + 1) * TQ].astype(o_ref.dtype)


def _dsa_prompt(qb, iq, iw, ikd, kb, vb):
    B, _, T, _ = qb.shape
    TQ = TQ_DSA
    full = pl.BlockSpec((None, T, 128), lambda b, i: (b, 0, 0))
    return pl.pallas_call(
        functools.partial(_dsa_prompt_kernel, T=T),
        grid=(B, T // TQ),
        in_specs=[
            pl.BlockSpec((None, B_HEADS, TQ, 128), lambda b, i: (b, 0, i, 0)),
            pl.BlockSpec((None, IDX_HEADS, TQ, 128), lambda b, i: (b, 0, i, 0)),
            pl.BlockSpec((None, TQ, 128), lambda b, i: (b, i, 0)),
            full, full, full,
        ],
        out_specs=pl.BlockSpec((None, TQ, B_HEADS * 128), lambda b, i: (b, i, 0)),
        out_shape=jax.ShapeDtypeStruct((B, T, B_HEADS * 128), BF16),
        scratch_shapes=[pltpu.VMEM((T, TQ), I32)],
        compiler_params=_cparams(("parallel", "arbitrary")),
        name="dsa_prompt",
    )(qb, iq, iw, ikd, kb, vb)


def _diff_finish(o, lam, subln, lam_init, rows):
    outs = []
    for g in range(2):
        a0 = o[(2 * g) * rows:(2 * g + 1) * rows]
        a1 = o[(2 * g + 1) * rows:(2 * g + 2) * rows]
        d = a0 - lam * a1
        d = d * lax.rsqrt(jnp.mean(d * d, axis=-1, keepdims=True) + EPS) * subln
        outs.append(d * (1.0 - lam_init))
    return outs


def _diff_prompt_kernel(q_ref, k_ref, v_ref, lp_ref, sub_ref, o_ref, *, lam_init):
    TQ = TQ_DIFF
    qi = pl.program_id(2)
    t0 = qi * TQ
    nkt = (t0 + TQ + TK - 1) // TK
    R = 4 * TQ
    qp = t0 + lax.broadcasted_iota(I32, (1, TQ, 1), 1)
    tcol = lax.broadcasted_iota(I32, (1, TQ, TK), 2)
    q = q_ref[...].reshape(R, 128)

    def step(kt, carry, causal):
        base = pl.multiple_of(kt * TK, TK)
        bias = _bias(tcol + base <= qp) if causal else None
        return _flash_tile(q, k_ref[pl.ds(base, TK), :], v_ref[pl.ds(base, TK), :], bias, carry)

    carry = lax.fori_loop(0, nkt - 1, functools.partial(step, causal=False), _flash_init(4, TQ))
    o = _flash_finish(step(nkt - 1, carry, True))
    lam = _lambda_of(lp_ref[...], lam_init)
    outs = _diff_finish(o, lam, sub_ref[...], lam_init, TQ)
    for g in range(2):
        o_ref[:, g * 128:(g + 1) * 128] = outs[g].astype(o_ref.dtype)


def _diff_prompt(qc, kcd, vcd, lp, subln, lam_init):
    B, _, _, T, _ = qc.shape
    TQ = TQ_DIFF
    full = pl.BlockSpec((None, None, T, 128), lambda b, kv, i: (b, kv, 0, 0))
    return pl.pallas_call(
        functools.partial(_diff_prompt_kernel, lam_init=lam_init),
        grid=(B, C_KV, T // TQ),
        in_specs=[
            pl.BlockSpec((None, None, 4, TQ, 128), lambda b, kv, i: (b, kv, 0, i, 0)),
            full, full,
            pl.BlockSpec((4, C_HALF), lambda b, kv, i: (0, 0)),
            pl.BlockSpec((1, 128), lambda b, kv, i: (0, 0)),
        ],
        out_specs=pl.BlockSpec((None, TQ, 256), lambda b, kv, i: (b, i, kv)),
        out_shape=jax.ShapeDtypeStruct((B, T, C_HEADS * 128), BF16),
        compiler_params=_cparams(("parallel", "parallel", "arbitrary")),
        name="diff_prompt",
    )(qc, kcd, vcd, lp, subln)


def _page_specs(block, layer, n_lead_zero):
    specs = []
    for u in range(PAGES_PER_STEP):
        def imap(b, pc, pt, u=u):
            return (layer, pt[b, pc * PAGES_PER_STEP + u]) + (0,) * n_lead_zero
        specs.append(pl.BlockSpec(block, imap))
    return specs


def _nsa_cmp_decode_kernel(pt_ref, *refs, past):
    pages = refs[:PAGES_PER_STEP]
    q_ref, new_ref, wc_ref, cc_ref, sc_ref, oc_ref, sel_ref, a_ref, b_ref = refs[PAGES_PER_STEP:]
    pc = pl.program_id(1)
    nc = past // CMP_STRIDE
    ns = past // SEL_BLOCK + 1
    nsp = ((ns + 127) // 128) * 128
    cpp = PAGE_SIZE // CMP_STRIDE

    def slab(ref, s):
        return ref[pl.ds(s, PAGE_SIZE, stride=8), :]

    for u in range(PAGES_PER_STEP):
        r0 = pl.multiple_of((pc * PAGES_PER_STEP + u) * cpp, cpp)
        for c in range(2):
            x = jnp.concatenate([slab(pages[u], 2 * c), slab(pages[u], 2 * c + 1)], axis=1)
            part = _dot3(wc_ref[c], x)
            a_ref[pl.ds(r0, cpp), 256 * c:256 * (c + 1)] = part[0:cpp]
            b_ref[pl.ds(r0, cpp), 256 * c:256 * (c + 1)] = part[cpp:2 * cpp]

    @pl.when(pc == pl.num_programs(1) - 1)
    def _():
        rowi = lax.broadcasted_iota(I32, (nc, 128), 0)
        cend = lax.broadcasted_iota(I32, (A_G, nc), 1) * CMP_STRIDE + (CMP_LEN - 1)
        ci = lax.broadcasted_iota(I32, (nc, nsp), 0) * CMP_STRIDE
        sj = lax.broadcasted_iota(I32, (nc, nsp), 1) * SEL_BLOCK
        overlap = ((ci < sj + SEL_BLOCK) & (ci + CMP_LEN > sj)).astype(F32)
        jrow = lax.broadcasted_iota(I32, (1, nsp), 1)
        jq = past // SEL_BLOCK
        ii = lax.broadcasted_iota(I32, (nsp, nsp), 0)
        jj = lax.broadcasted_iota(I32, (nsp, nsp), 1)
        rr = lax.broadcasted_iota(I32, (SEL_TOPN, nsp), 0)
        jr = lax.broadcasted_iota(I32, (SEL_TOPN, nsp), 1).astype(F32)
        new = new_ref[...]
        q = q_ref[...].astype(BF16)
        for kv in range(A_KV):
            ko, vo = kv * 128, 256 + kv * 128
            bk = jnp.where(rowi == nc - 1, wc_ref[0][cpp:cpp + 1, 0:1] * new[:, ko:ko + 128],
                           pltpu.roll(b_ref[:, ko:ko + 128], nc - 1, 0))
            bv = jnp.where(rowi == nc - 1, wc_ref[1][cpp:cpp + 1, 0:1] * new[:, vo:vo + 128],
                           pltpu.roll(b_ref[:, vo:vo + 128], nc - 1, 0))
            kraw = a_ref[:, ko:ko + 128] + bk
            kc = (kraw * cc_ref[...] + pltpu.roll(kraw, 64, 1) * sc_ref[...]).astype(BF16)
            vc = (a_ref[:, vo:vo + 128] + bv).astype(BF16)
            s_c = _dot_nt(q[kv * A_G:(kv + 1) * A_G], kc)
            p_c = _masked_softmax(s_c, _bias(cend <= past))
            oc_ref[kv * A_G:(kv + 1) * A_G, :] = _dot(p_c.astype(BF16), vc)
            imp = _dot(jnp.sum(p_c, axis=0, keepdims=True), overlap, HI)
            forced = (jrow == 0) | (jrow == jq) | (jrow == jq - 1)
            imp = jnp.where(forced, imp + FORCE_BONUS, imp)
            imp = jnp.where(jrow > jq, NEG, imp)
            imp_col = jnp.sum(jnp.where(ii == jj, jnp.broadcast_to(imp, (nsp, nsp)), 0.0), axis=1, keepdims=True)
            beats = (imp_col > imp) | ((imp_col == imp) & (ii < jj))
            rank = jnp.sum(jnp.where(beats, 1.0, 0.0), axis=0, keepdims=True)
            pick = jnp.sum(jnp.where(rank == rr.astype(F32), jr, 0.0), axis=1, keepdims=True)
            sel_ref[kv] = pick.astype(I32)


def _nsa_cmp_decode(page_table, cache, layer, q, new_rows, wc, cc, sc, past):
    DB = q.shape[0]
    npc = page_table.shape[1] // PAGES_PER_STEP
    nc = past // CMP_STRIDE
    block = (None, None, PAGE_SIZE * 8, 128)
    const2 = lambda b, pc, pt: (0, 0)
    return pl.pallas_call(
        functools.partial(_nsa_cmp_decode_kernel, past=past),
        grid_spec=pltpu.PrefetchScalarGridSpec(
            num_scalar_prefetch=1,
            grid=(DB, npc),
            in_specs=_page_specs(block, layer, 2) + [
                pl.BlockSpec((None, A_HEADS, 128), lambda b, pc, pt: (b, 0, 0)),
                pl.BlockSpec((None, 1, 1024), lambda b, pc, pt: (b, 0, 0)),
                pl.BlockSpec((2, 2 * PAGE_SIZE // CMP_STRIDE, PAGE_SIZE), lambda b, pc, pt: (0, 0, 0)),
                pl.BlockSpec((nc, 128), const2),
                pl.BlockSpec((nc, 128), const2),
            ],
            out_specs=[
                pl.BlockSpec((None, A_HEADS, 128), lambda b, pc, pt: (b, 0, 0)),
                pl.BlockSpec((None, A_KV, SEL_TOPN, 1), lambda b, pc, pt: (b, 0, 0, 0)),
            ],
            scratch_shapes=[pltpu.VMEM((nc, 512), F32), pltpu.VMEM((nc, 512), F32)],
        ),
        out_shape=[jax.ShapeDtypeStruct((DB, A_HEADS, 128), F32),
                   jax.ShapeDtypeStruct((DB, A_KV, SEL_TOPN, 1), I32)],
        compiler_params=_cparams(("parallel", "arbitrary")),
        name="nsa_cmp_decode",
    )(page_table, *([cache] * PAGES_PER_STEP), q, new_rows, wc, cc, sc)


def _nsa_sel_decode_kernel(pt_ref, sel_ref, *refs, past, lw):
    blks = refs[:A_KV * SEL_PER_STEP]
    win_ref, q_ref, new_ref, neww_ref, oc_ref, g_ref, o_ref, m_ref, l_ref, acc_ref = refs[A_KV * SEL_PER_STEP:]
    b, s = pl.program_id(0), pl.program_id(1)
    ns = past // SEL_BLOCK + 1
    new = new_ref[...]
    neww = neww_ref[...]
    blk_of_lane = lax.broadcasted_iota(I32, (1, SEL_PER_STEP * SEL_BLOCK), 1) >> SEL_SHIFT

    for kv in range(A_KV):
        q = q_ref[kv * A_G:(kv + 1) * A_G, :].astype(BF16)
        mine = blks[kv * SEL_PER_STEP:(kv + 1) * SEL_PER_STEP]

        @pl.when(s == 0)
        def _():
            m_ref[kv] = _rowdot(q, new[:, 512 + kv * 128:640 + kv * 128])
            l_ref[kv] = jnp.ones((A_G, 1), F32)
            v_new = new[:, 768 + kv * 128:896 + kv * 128].astype(BF16).astype(F32)
            acc_ref[kv] = jnp.broadcast_to(v_new, (A_G, 128))

        k = jnp.concatenate([r[pl.ds(4 + kv, SEL_BLOCK, stride=8), :] for r in mine], axis=0).astype(BF16)
        v = jnp.concatenate([r[pl.ds(6 + kv, SEL_BLOCK, stride=8), :] for r in mine], axis=0).astype(BF16)
        bias = jnp.zeros((1, SEL_PER_STEP * SEL_BLOCK), F32)
        for u in range(SEL_PER_STEP):
            keep = sel_ref[b, kv, s * SEL_PER_STEP + u] != ns - 1
            bias = jnp.where(blk_of_lane == u, jnp.where(keep, 0.0, MASKED), bias)
        m, l, acc = _online_step(_dot_nt(q, k), bias, v, m_ref[kv], l_ref[kv], acc_ref[kv])
        m_ref[kv] = m
        l_ref[kv] = l
        acc_ref[kv] = acc

        @pl.when(s == pl.num_programs(1) - 1)
        def _():
            o_s = acc_ref[kv] / jnp.maximum(l_ref[kv], 1e-30)
            kw = win_ref[pl.ds(kv, lw, stride=4), :].astype(BF16)
            vw = win_ref[pl.ds(2 + kv, lw, stride=4), :].astype(BF16)
            vw_new = neww[:, 256 + kv * 128:384 + kv * 128].astype(BF16).astype(F32)
            s_w = _dot_nt(q, kw)
            s_n = _rowdot(q, neww[:, kv * 128:(kv + 1) * 128])
            mw = jnp.maximum(jnp.max(s_w, axis=-1, keepdims=True), s_n)
            e_w = jnp.exp2(s_w - mw)
            e_n = jnp.exp2(s_n - mw)
            den = jnp.sum(e_w, axis=-1, keepdims=True) + e_n
            o_w = (_dot(e_w.astype(BF16), vw) + e_n.astype(BF16).astype(F32) * vw_new) / den
            o_c = oc_ref[kv * A_G:(kv + 1) * A_G, :]
            g = g_ref[kv * A_G:(kv + 1) * A_G, :]
            o_ref[kv * A_G:(kv + 1) * A_G, :] = g[:, 0:1] * o_c + g[:, 1:2] * o_s + g[:, 2:3] * o_w


def _nsa_sel_decode(page_table, sel, cache, win, layer, q, new_rows, new_win, o_c, gates, past):
    DB = q.shape[0]
    n_pages = page_table.shape[1]
    lw = win.shape[2] // 4

    def blk_spec(kv, u):
        def imap(b, s, pt, sel):
            j = sel[b, kv, s * SEL_PER_STEP + u]
            return (layer, pt[b, jnp.minimum(j // 2, n_pages - 1)], j % 2, 0)
        return pl.BlockSpec((None, None, SEL_BLOCK * 8, 128), imap)

    per_b = lambda b, s, pt, sel: (b, 0, 0)
    n_blk = A_KV * SEL_PER_STEP
    return pl.pallas_call(
        functools.partial(_nsa_sel_decode_kernel, past=past, lw=lw),
        grid_spec=pltpu.PrefetchScalarGridSpec(
            num_scalar_prefetch=2,
            grid=(DB, SEL_TOPN // SEL_PER_STEP),
            in_specs=[blk_spec(kv, u) for kv in range(A_KV) for u in range(SEL_PER_STEP)] + [
                pl.BlockSpec((None, None, lw * 4, 128), lambda b, s, pt, sel: (layer, b, 0, 0)),
                pl.BlockSpec((None, A_HEADS, 128), per_b),
                pl.BlockSpec((None, 1, 1024), per_b),
                pl.BlockSpec((None, 1, 512), per_b),
                pl.BlockSpec((None, A_HEADS, 128), per_b),
                pl.BlockSpec((None, A_HEADS, 128), per_b),
            ],
            out_specs=pl.BlockSpec((None, A_HEADS, 128), per_b),
            scratch_shapes=[pltpu.VMEM((A_KV, A_G, 1), F32), pltpu.VMEM((A_KV, A_G, 1), F32),
                            pltpu.VMEM((A_KV, A_G, 128), F32)],
        ),
        out_shape=jax.ShapeDtypeStruct((DB, A_HEADS, 128), F32),
        compiler_params=_cparams(("parallel", "arbitrary")),
        name="nsa_sel_decode",
    )(page_table, sel, *([cache] * n_blk), win, q, new_rows, new_win, o_c, gates)


def _dsa_idx_decode_kernel(pt_ref, *refs, past):
    pages = refs[:PAGES_PER_STEP]
    iq_ref, iw_ref, ikn_ref, mask_ref, sc_ref = refs[PAGES_PER_STEP:]
    pc = pl.program_id(1)
    n_pages = past // PAGE_SIZE
    k_top = min(IDX_TOPK, (past + 1) // 4)
    iq = iq_ref[...].astype(BF16)
    iw = iw_ref[...]

    @pl.when(pc == 0)
    def _():
        sc_ref[...] = jnp.full(sc_ref.shape, NEG, F32)

    for u in range(PAGES_PER_STEP):
        s = _dot(iq, pages[u][...].astype(BF16))
        sc_ref[pl.ds(pc * PAGES_PER_STEP + u, 1), :] = jnp.sum(iw * jnp.maximum(s, 0.0), axis=0, keepdims=True)

    @pl.when(pc == pl.num_programs(1) - 1)
    def _():
        s_new = _rowdot(iq, ikn_ref[...])
        s_new = jnp.sum(iw * jnp.maximum(s_new, 0.0), axis=0, keepdims=True)
        lane0 = lax.broadcasted_iota(I32, (1, PAGE_SIZE), 1) == 0
        sc_ref[n_pages:n_pages + 1, :] = jnp.where(lane0, s_new, NEG)
        keys = _sortable(sc_ref[...])

        def count_ge(cand):
            c = jnp.sum(jnp.where(keys >= cand, 1.0, 0.0), axis=-1, keepdims=True)
            return jnp.sum(c, axis=0, keepdims=True)

        thr = _kth_threshold(count_ge, (1, 1), float(k_top))
        mask_ref[...] = _bias((keys >= thr) & (sc_ref[...] > 0.5 * NEG))


def _dsa_idx_decode(page_table, cache, layer, iq, iw, ik_new, past):
    DB = iq.shape[0]
    n_pages = page_table.shape[1]
    npc = n_pages // PAGES_PER_STEP
    rows = ((n_pages + 1 + 7) // 8) * 8
    per_b = lambda b, pc, pt: (b, 0, 0)
    return pl.pallas_call(
        functools.partial(_dsa_idx_decode_kernel, past=past),
        grid_spec=pltpu.PrefetchScalarGridSpec(
            num_scalar_prefetch=1,
            grid=(DB, npc),
            in_specs=_page_specs((None, None, IDX_DIM, PAGE_SIZE), layer, 2) + [
                pl.BlockSpec((None, IDX_HEADS, IDX_DIM), per_b),
                pl.BlockSpec((None, IDX_HEADS, 1), per_b),
                pl.BlockSpec((None, 1, IDX_DIM), per_b),
            ],
            out_specs=pl.BlockSpec((None, rows, PAGE_SIZE), per_b),
            scratch_shapes=[pltpu.VMEM((rows, PAGE_SIZE), F32)],
        ),
        out_shape=jax.ShapeDtypeStruct((DB, rows, PAGE_SIZE), F32),
        compiler_params=_cparams(("parallel", "arbitrary")),
        name="dsa_idx_decode",
    )(page_table, *([cache] * PAGES_PER_STEP), iq, iw, ik_new)


def _dsa_att_decode_kernel(pt_ref, *refs, past):
    pages = refs[:PAGES_PER_STEP]
    q_ref, new_ref, mask_ref, o_ref, m_ref, l_ref, acc_ref = refs[PAGES_PER_STEP:]
    pc = pl.program_id(1)
    n_pages = past // PAGE_SIZE
    q = q_ref[...].astype(BF16)

    @pl.when(pc == 0)
    def _():
        m_ref[...] = jnp.full(m_ref.shape, NEG, F32)
        l_ref[...] = jnp.zeros(l_ref.shape, F32)
        acc_ref[...] = jnp.zeros(acc_ref.shape, F32)

    k = jnp.concatenate([p[pl.ds(0, PAGE_SIZE, stride=2), :] for p in pages], axis=0).astype(BF16)
    v = jnp.concatenate([p[pl.ds(1, PAGE_SIZE, stride=2), :] for p in pages], axis=0).astype(BF16)
    r0 = pl.multiple_of(pc * PAGES_PER_STEP, PAGES_PER_STEP)
    mrows = mask_ref[pl.ds(r0, PAGES_PER_STEP), :]
    bias = jnp.concatenate([mrows[u:u + 1, :] for u in range(PAGES_PER_STEP)], axis=1)
    m, l, acc = _online_step(_dot_nt(q, k), bias, v, m_ref[...], l_ref[...], acc_ref[...])
    m_ref[...] = m
    l_ref[...] = l
    acc_ref[...] = acc

    @pl.when(pc == pl.num_programs(1) - 1)
    def _():
        new = new_ref[...]
        v_new = new[:, 128:256].astype(BF16).astype(F32)
        keep = mask_ref[n_pages:n_pages + 1, 0:1]
        _, l2, acc2 = _online_single(_rowdot(q, new[:, 0:128]), keep, v_new, m_ref[...], l_ref[...], acc_ref[...])
        o_ref[...] = acc2 / jnp.maximum(l2, 1e-30)


def _dsa_att_decode(page_table, cache, layer, q, new_rows, mask, past):
    DB = q.shape[0]
    npc = page_table.shape[1] // PAGES_PER_STEP
    rows = mask.shape[1]
    per_b = lambda b, pc, pt: (b, 0, 0)
    return pl.pallas_call(
        functools.partial(_dsa_att_decode_kernel, past=past),
        grid_spec=pltpu.PrefetchScalarGridSpec(
            num_scalar_prefetch=1,
            grid=(DB, npc),
            in_specs=_page_specs((None, None, PAGE_SIZE * 2, 128), layer, 2) + [
                pl.BlockSpec((None, B_HEADS, 128), per_b),
                pl.BlockSpec((None, 1, 256), per_b),
                pl.BlockSpec((None, rows, PAGE_SIZE), per_b),
            ],
            out_specs=pl.BlockSpec((None, B_HEADS, 128), per_b),
            scratch_shapes=[pltpu.VMEM((B_HEADS, 1), F32), pltpu.VMEM((B_HEADS, 1), F32),
                            pltpu.VMEM((B_HEADS, 128), F32)],
        ),
        out_shape=jax.ShapeDtypeStruct((DB, B_HEADS, 128), F32),
        compiler_params=_cparams(("parallel", "arbitrary")),
        name="dsa_att_decode",
    )(page_table, *([cache] * PAGES_PER_STEP), q, new_rows, mask)


def _diff_decode_kernel(pt_ref, *refs, lam_init):
    pages = refs[:PAGES_PER_STEP]
    q_ref, new_ref, lp_ref, sub_ref, o_ref, m_ref, l_ref, acc_ref = refs[PAGES_PER_STEP:]
    pc = pl.program_id(1)

    @pl.when(pc == 0)
    def _():
        m_ref[...] = jnp.full(m_ref.shape, NEG, F32)
        l_ref[...] = jnp.zeros(l_ref.shape, F32)
        acc_ref[...] = jnp.zeros(acc_ref.shape, F32)

    for kv in range(C_KV):
        q = q_ref[kv].astype(BF16)
        k = jnp.concatenate([p[pl.ds(kv, PAGE_SIZE, stride=4), :] for p in pages], axis=0).astype(BF16)
        v = jnp.concatenate([p[pl.ds(2 + kv, PAGE_SIZE, stride=4), :] for p in pages], axis=0).astype(BF16)
        sc = _dot_nt(q, k)
        m, l, acc = _online_step(sc, None, v, m_ref[kv], l_ref[kv], acc_ref[kv])
        m_ref[kv] = m
        l_ref[kv] = l
        acc_ref[kv] = acc

    @pl.when(pc == pl.num_programs(1) - 1)
    def _():
        new = new_ref[...]
        lam = _lambda_of(lp_ref[...], lam_init)
        for kv in range(C_KV):
            k_new = new[:, kv * 128:(kv + 1) * 128]
            v_new = new[:, 256 + kv * 128:384 + kv * 128].astype(BF16).astype(F32)
            _, l, acc = _online_single(_rowdot(q_ref[kv], k_new), None, v_new, m_ref[kv], l_ref[kv], acc_ref[kv])
            outs = _diff_finish(acc / jnp.maximum(l, 1e-30), lam, sub_ref[...], lam_init, 1)
            o_ref[2 * kv:2 * kv + 1, :] = outs[0]
            o_ref[2 * kv + 1:2 * kv + 2, :] = outs[1]


def _diff_decode(page_table, cache, layer, q, new_rows, lp, subln, lam_init):
    DB = q.shape[0]
    npc = page_table.shape[1] // PAGES_PER_STEP
    per_b = lambda b, pc, pt: (b, 0, 0)
    return pl.pallas_call(
        functools.partial(_diff_decode_kernel, lam_init=lam_init),
        grid_spec=pltpu.PrefetchScalarGridSpec(
            num_scalar_prefetch=1,
            grid=(DB, npc),
            in_specs=_page_specs((None, None, PAGE_SIZE * 4, 128), layer, 2) + [
                pl.BlockSpec((None, C_KV, 4, 128), lambda b, pc, pt: (b, 0, 0, 0)),
                pl.BlockSpec((None, 1, 512), per_b),
                pl.BlockSpec((4, C_HALF), lambda b, pc, pt: (0, 0)),
                pl.BlockSpec((1, 128), lambda b, pc, pt: (0, 0)),
            ],
            out_specs=pl.BlockSpec((None, C_HEADS, 128), per_b),
            scratch_shapes=[pltpu.VMEM((C_KV, 4, 1), F32), pltpu.VMEM((C_KV, 4, 1), F32),
                            pltpu.VMEM((C_KV, 4, 128), F32)],
        ),
        out_shape=jax.ShapeDtypeStruct((DB, C_HEADS, 128), F32),
        compiler_params=_cparams(("parallel", "arbitrary")),
        name="diff_decode",
    )(page_table, *([cache] * PAGES_PER_STEP), q, new_rows, lp, subln)


def _rope_tables(pos, d):
    half = d // 2
    inv = ROPE_THETA ** (-jnp.arange(half, dtype=F32) / half)
    ang = pos.astype(F32)[:, None] * inv[None, :]
    cos, sin = jnp.cos(ang), jnp.sin(ang)
    reps = 128 // d
    return jnp.tile(jnp.concatenate([cos, cos], axis=-1), (1, reps)), jnp.tile(jnp.concatenate([-sin, sin], axis=-1), (1, reps))


def _pack_params(nsa_qk_norm, dsa_qk_norm, dsa_idx_knorm, diff_qk_norm):
    rows = [nsa_qk_norm, dsa_qk_norm, jnp.tile(dsa_idx_knorm, 2)[None], jnp.tile(diff_qk_norm, (1, 2))]
    p = jnp.concatenate(rows, axis=0).astype(F32)
    return jnp.pad(p, ((0, 16 - p.shape[0]), (0, 0)))


def _compress_weights(cmp_w, rows):
    eye = jnp.eye(rows // CMP_STRIDE, dtype=F32)
    mats = []
    for c in range(2):
        halves = [jnp.kron(eye, cmp_w[c, h * CMP_STRIDE:(h + 1) * CMP_STRIDE][None, :]) for h in range(2)]
        mats.append(jnp.concatenate(halves, axis=0))
    return jnp.stack(mats, axis=0)


def kernel(x_prompt, x_sample, cache_nsa_kv, state_nsa_win, cache_dsa_kv, cache_dsa_idx, cache_diff_kv, page_table, attn_norm, w_in, nsa_qk_norm, nsa_cmp_w, dsa_qk_norm, dsa_idx_knorm, diff_qk_norm, diff_lambda, diff_subln, w_out, ffn_norm, w_gate_up, w_down):
    B, T, D = x_prompt.shape
    DB = x_sample.shape[0]
    depth = w_in.shape[0]
    n_pages = page_table.shape[1]
    past = n_pages * PAGE_SIZE
    M = B * T
    assert x_sample.shape[1] == 1 and T % TK == 0 and T >= WINDOW + TQ_NSA and n_pages % PAGES_PER_STEP == 0
    tm = min(1024, M)
    tm_ffn = min(1024, M)
    tm_post = 256

    w_in_p = _permute_cast_w_in(w_in)
    w_out_b = w_out.astype(BF16)
    w_gu_b = w_gate_up.astype(BF16)
    w_down_b = w_down.astype(BF16)

    pos_p = jnp.arange(T, dtype=I32)
    tabs_p = _rope_tables(pos_p, 128) + _rope_tables(pos_p, 64)
    pos_s = jnp.full((DB,), past, I32)
    tabs_s = _rope_tables(pos_s, 128) + _rope_tables(pos_s, 64)
    cend_p = jnp.arange(T // CMP_STRIDE, dtype=I32) * CMP_STRIDE + (CMP_LEN - 1)
    cc_p, sc_p = _rope_tables(cend_p, 128)
    cend_s = jnp.arange(past // CMP_STRIDE, dtype=I32) * CMP_STRIDE + (CMP_LEN - 1)
    cc_s, sc_s = _rope_tables(cend_s, 128)

    n_pool = cache_nsa_kv.shape[1]
    nsa_pages = cache_nsa_kv.reshape(depth, n_pool, PAGE_SIZE * 8, 128)
    dsa_pages = cache_dsa_kv.reshape(depth, n_pool, PAGE_SIZE * 2, 128)
    diff_pages = cache_diff_kv.reshape(depth, n_pool, PAGE_SIZE * 4, 128)
    win_rows = state_nsa_win.reshape(depth, DB, state_nsa_win.shape[2] * 4, 128)
    idx_pages = jnp.swapaxes(cache_dsa_idx, 2, 3)

    yp = x_prompt.reshape(M, D)
    ys = x_sample.reshape(DB, D)
    rows_p, rows_s = [], []
    for l in range(depth):
        lam_init = 0.8 - 0.6 * math.exp(-0.3 * l)
        prm = _pack_params(nsa_qk_norm[l], dsa_qk_norm[l], dsa_idx_knorm[l], diff_qk_norm[l])
        g_attn = attn_norm[l][None, :]
        g_ffn = ffn_norm[l][None, :]
        lp = diff_lambda[l].astype(F32)
        subln = diff_subln[l][None, :].astype(F32)

        proj = _norm_matmul(yp, g_attn, w_in_p, l, tm).reshape(B, T, N_PROJ)
        wc = _compress_weights(nsa_cmp_w[l], tm_post)
        (nsa, win, dsa, ik, dif, qa, ksel, vsel, kw, vw, qb, kb, vb, iq, ikd, iw, qc, kcd, vcd, gat,
         pa, pb) = _post_project(proj, tabs_p, prm, wc, tm_post, BF16)
        o_a = _nsa_prompt(qa, pa, pb, cc_p, sc_p, ksel, vsel, kw, vw, gat)
        o_b = _dsa_prompt(qb, iq, iw, ikd, kb, vb)
        o_c = _diff_prompt(qc, kcd, vcd, lp, subln, lam_init)
        yp = _mix_out_projection(o_a.reshape(M, -1), o_b.reshape(M, -1), o_c.reshape(M, -1), w_out_b, yp, l, tm)
        act = _norm_swiglu(yp, g_ffn, w_gu_b, l, tm_ffn)
        yp = _matmul_residual(act, w_down_b, yp, l, tm_ffn)
        w_keep = min(WINDOW, T)
        rows_p.append((nsa.reshape(B, T, 4, A_KV, 128), win[:, (T - w_keep) * 4:].reshape(B, w_keep, 2, A_KV, 128),
                       dsa.reshape(B, T, 2, 128), ik, dif.reshape(B, T, 2, C_KV, 128)))

        proj_s = _norm_matmul(ys, g_attn, w_in_p, l, DB).reshape(1, DB, N_PROJ)
        (nsa_s, win_s, dsa_s, ik_s, dif_s, qa_s, _, _, _, _, qb_s, _, _, iq_s, _, iw_s, qc_s, _, _,
         gat_s) = _post_project(proj_s, tabs_s, prm, None, DB, F32)
        nsa_new = nsa_s.reshape(DB, 1, 1024)
        win_new = win_s.reshape(DB, 1, 512)
        dsa_new = dsa_s.reshape(DB, 1, 256)
        ik_new = ik_s.reshape(DB, 1, IDX_DIM)
        dif_new = dif_s.reshape(DB, 1, 512)
        qa_d = jnp.transpose(qa_s[0], (1, 0, 2))
        qb_d = jnp.transpose(qb_s[0], (1, 0, 2))
        iq_d = jnp.transpose(iq_s[0], (1, 0, 2))
        iq_d = iq_d[:, :, :64] + iq_d[:, :, 64:]
        iw_d = iw_s[0, :, :IDX_HEADS, None]
        qc_d = jnp.transpose(qc_s[0], (2, 0, 1, 3))
        g_d = jnp.transpose(gat_s[0, :, :, :12].reshape(A_KV, DB, A_G, 3), (1, 0, 2, 3)).reshape(DB, A_HEADS, 3)
        g_d = jnp.pad(g_d, ((0, 0), (0, 0), (0, 125)))

        wc_s = _compress_weights(nsa_cmp_w[l], PAGE_SIZE)
        oc_d, sel = _nsa_cmp_decode(page_table, nsa_pages, l, qa_d, nsa_new, wc_s, cc_s, sc_s, past)
        oa_d = _nsa_sel_decode(page_table, sel.reshape(DB, A_KV, SEL_TOPN), nsa_pages, win_rows, l,
                               qa_d, nsa_new, win_new, oc_d, g_d, past)
        mask = _dsa_idx_decode(page_table, idx_pages, l, iq_d, iw_d, ik_new, past)
        ob_d = _dsa_att_decode(page_table, dsa_pages, l, qb_d, dsa_new, mask, past)
        od_d = _diff_decode(page_table, diff_pages, l, qc_d, dif_new, lp, subln, lam_init)
        ys = _mix_out_projection(oa_d.reshape(DB, 1024).astype(BF16), ob_d.reshape(DB, 512).astype(BF16),
                                 od_d.reshape(DB, 512).astype(BF16), w_out_b, ys, l, DB)
        act_s = _norm_swiglu(ys, g_ffn, w_gu_b, l, DB)
        ys = _matmul_residual(act_s, w_down_b, ys, l, DB)
        lw = state_nsa_win.shape[2]
        win_all = jnp.concatenate([state_nsa_win[l], win_new.reshape(DB, 1, 2, A_KV, 128)], axis=1)
        rows_s.append((nsa_new.reshape(DB, 1, 4, A_KV, 128), win_all[:, win_all.shape[1] - min(WINDOW, lw + 1):],
                       dsa_new.reshape(DB, 1, 2, 128), ik_new, dif_new.reshape(DB, 1, 2, C_KV, 128)))

    def stacked(rows, i):
        return jnp.stack([r[i] for r in rows], axis=0)

    return (yp.reshape(B, T, D), ys.reshape(DB, 1, D),
            stacked(rows_p, 0), stacked(rows_s, 0), stacked(rows_p, 1), stacked(rows_s, 1),
            stacked(rows_p, 2), stacked(rows_s, 2), stacked(rows_p, 3), stacked(rows_s, 3),
            stacked(rows_p, 4), stacked(rows_s, 4))
```

```python
import functools
import math

import jax
import jax.numpy as jnp
from jax import lax
from jax.experimental import pallas as pl
from jax.experimental.pallas import tpu as pltpu

F32 = jnp.float32
BF16 = jnp.bfloat16
I32 = jnp.int32
HI = lax.Precision.HIGHEST

D_MODEL = 2048
PAGE_SIZE = 128
D_HEAD = 128
A_HEADS = 8
A_KV = 2
A_G = A_HEADS // A_KV
B_HEADS = 4
C_HEADS = 4
C_KV = 2
C_HALF = 64
IDX_HEADS = 16
IDX_DIM = 64
IDX_TOPK = 256
CMP_LEN = 32
CMP_STRIDE = 16
SEL_BLOCK = 64
SEL_TOPN = 16
WINDOW = 512
FORCE_BONUS = 1.0e4
D_FF = 5632
ROPE_THETA = 10000.0
EPS = 1e-6
NEG = -1e30
INT_MIN = -2147483648

OFF_AQ = 0
OFF_AKV = 1024
OFF_BQ = 2560
OFF_BKV = 3072
OFF_BIQ = 3328
OFF_CQ = 4352
OFF_CKV = 4864
OFF_MISC = 5376
N_PROJ = 5632
MISC_GATE = 64
MISC_IW = 96

TQ_NSA = 256
TQ_DSA = 128
TQ_DIFF = 256
TK = 512
PAGES_PER_STEP = 16
SEL_SHIFT = 6
SEL_PER_STEP = 4
LOG2E = math.log2(math.e)
SCALE_D = D_HEAD ** -0.5 * LOG2E
SCALE_C = C_HALF ** -0.5 * LOG2E
MASKED = -2e30
VMEM_LIMIT = 56 * 1024 * 1024


def _cparams(sem):
    return pltpu.CompilerParams(dimension_semantics=sem, vmem_limit_bytes=VMEM_LIMIT)


def _dot(a, b, precision=None):
    return jnp.dot(a, b, preferred_element_type=F32, precision=precision)


def _dot3(a, b):
    a_hi = a.astype(BF16)
    b_hi = b.astype(BF16)
    a_lo = (a - a_hi.astype(F32)).astype(BF16)
    b_lo = (b - b_hi.astype(F32)).astype(BF16)
    return _dot(a_hi, b_hi) + (_dot(a_hi, b_lo) + _dot(a_lo, b_hi))


def _dot_nt(a, b, precision=None):
    return lax.dot_general(a, b, (((1,), (1,)), ((), ())), preferred_element_type=F32, precision=precision)


def _norm_mm_kernel(x_ref, g_ref, w_ref, o_ref, xn_ref):
    @pl.when(pl.program_id(1) == 0)
    def _():
        x = x_ref[...]
        ms = jnp.mean(x * x, axis=-1, keepdims=True)
        xn_ref[...] = (x * lax.rsqrt(ms + EPS) * g_ref[...]).astype(BF16)

    o_ref[...] = _dot_nt(xn_ref[...], w_ref[...])


def _norm_matmul(x, g, w_t, layer, tm, tn=512):
    M, K = x.shape
    N = w_t.shape[1]
    return pl.pallas_call(
        _norm_mm_kernel,
        grid=(M // tm, N // tn),
        in_specs=[
            pl.BlockSpec((tm, K), lambda i, j: (i, 0)),
            pl.BlockSpec((1, K), lambda i, j: (0, 0)),
            pl.BlockSpec((None, tn, K), lambda i, j: (layer, j, 0)),
        ],
        out_specs=pl.BlockSpec((tm, tn), lambda i, j: (i, j)),
        out_shape=jax.ShapeDtypeStruct((M, N), F32),
        scratch_shapes=[pltpu.VMEM((tm, K), BF16)],
        compiler_params=_cparams(("parallel", "arbitrary")),
        name="norm_matmul",
    )(x, g, w_t)


def _norm_swiglu_kernel(x_ref, g_ref, wg_ref, wu_ref, o_ref, xn_ref):
    @pl.when(pl.program_id(1) == 0)
    def _():
        x = x_ref[...]
        ms = jnp.mean(x * x, axis=-1, keepdims=True)
        xn_ref[...] = (x * lax.rsqrt(ms + EPS) * g_ref[...]).astype(BF16)

    xn = xn_ref[...]
    gate = _dot(xn, wg_ref[...])
    up = _dot(xn, wu_ref[...])
    o_ref[...] = (gate * jax.nn.sigmoid(gate) * up).astype(o_ref.dtype)


def _norm_swiglu(x, g, w, layer, tm, tn=512):
    M, K = x.shape
    nj = D_FF // tn
    return pl.pallas_call(
        _norm_swiglu_kernel,
        grid=(M // tm, nj),
        in_specs=[
            pl.BlockSpec((tm, K), lambda i, j: (i, 0)),
            pl.BlockSpec((1, K), lambda i, j: (0, 0)),
            pl.BlockSpec((None, K, tn), lambda i, j: (layer, 0, j)),
            pl.BlockSpec((None, K, tn), lambda i, j: (layer, 0, j + nj)),
        ],
        out_specs=pl.BlockSpec((tm, tn), lambda i, j: (i, j)),
        out_shape=jax.ShapeDtypeStruct((M, D_FF), BF16),
        scratch_shapes=[pltpu.VMEM((tm, K), BF16)],
        compiler_params=_cparams(("parallel", "arbitrary")),
        name="norm_swiglu",
    )(x, g, w, w)


def _mm_res_kernel(a_ref, w_ref, r_ref, o_ref):
    o_ref[...] = r_ref[...] + _dot(a_ref[...], w_ref[...])


def _matmul_residual(a, w, res, layer, tm, tn=512):
    M, K = a.shape
    N = w.shape[2]
    return pl.pallas_call(
        _mm_res_kernel,
        grid=(M // tm, N // tn),
        in_specs=[
            pl.BlockSpec((tm, K), lambda i, j: (i, 0)),
            pl.BlockSpec((None, K, tn), lambda i, j: (layer, 0, j)),
            pl.BlockSpec((tm, tn), lambda i, j: (i, j)),
        ],
        out_specs=pl.BlockSpec((tm, tn), lambda i, j: (i, j)),
        out_shape=jax.ShapeDtypeStruct((M, N), F32),
        compiler_params=_cparams(("parallel", "arbitrary")),
        name="matmul_residual",
    )(a, w, res)


def _mix_out_kernel(a_ref, b_ref, c_ref, w_ref, r_ref, o_ref):
    ka, kb = a_ref.shape[1], b_ref.shape[1]
    acc = _dot(a_ref[...], w_ref[0:ka, :])
    acc = acc + _dot(b_ref[...], w_ref[ka:ka + kb, :])
    acc = acc + _dot(c_ref[...], w_ref[ka + kb:, :])
    o_ref[...] = r_ref[...] + acc


def _mix_out_projection(o_a, o_b, o_c, w, res, layer, tm, tn=512):
    M, N = res.shape
    K = w.shape[1]

    def rows(x):
        return pl.BlockSpec((tm, x.shape[1]), lambda i, j: (i, 0))

    return pl.pallas_call(
        _mix_out_kernel,
        grid=(M // tm, N // tn),
        in_specs=[
            rows(o_a), rows(o_b), rows(o_c),
            pl.BlockSpec((None, K, tn), lambda i, j: (layer, 0, j)),
            pl.BlockSpec((tm, tn), lambda i, j: (i, j)),
        ],
        out_specs=pl.BlockSpec((tm, tn), lambda i, j: (i, j)),
        out_shape=jax.ShapeDtypeStruct((M, N), F32),
        compiler_params=_cparams(("parallel", "arbitrary")),
        name="mix_out_projection",
    )(o_a, o_b, o_c, w, res)


_W_IN_PIECES = ((OFF_AQ, 0, 2560), (OFF_BQ, 2584, 1792), (OFF_CQ, 4456, 1024), (OFF_MISC, 4392, 64),
                (OFF_MISC + MISC_GATE, 2560, 24), (OFF_MISC + MISC_IW, 4376, 16))
_W_IN_COLS = 5480


def _permute_cast_kernel(x_ref, o_ref):
    o_ref[OFF_MISC + MISC_GATE:, :] = jnp.zeros((N_PROJ - OFF_MISC - MISC_GATE, o_ref.shape[1]), BF16)
    for dst, src, width in _W_IN_PIECES:
        o_ref[dst:dst + width, :] = x_ref[src:src + width, :].astype(BF16)


def _permute_cast_w_in(w_in_t, tk=512):
    depth, _, K = w_in_t.shape
    return pl.pallas_call(
        _permute_cast_kernel,
        grid=(depth, K // tk),
        in_specs=[pl.BlockSpec((None, _W_IN_COLS, tk), lambda l, i: (l, 0, i))],
        out_specs=pl.BlockSpec((None, N_PROJ, tk), lambda l, i: (l, 0, i)),
        out_shape=jax.ShapeDtypeStruct((depth, N_PROJ, K), BF16),
        compiler_params=_cparams(("parallel", "parallel")),
        name="permute_cast_w_in",
    )(w_in_t)


def _post_kernel(*refs, emit_cmp, tm):
    if emit_cmp:
        x_ref, c1_ref, s1_ref, c2_ref, s2_ref, prm_ref, wc_ref = refs[:7]
        outs = refs[7:]
    else:
        x_ref, c1_ref, s1_ref, c2_ref, s2_ref, prm_ref = refs[:6]
        wc_ref = None
        outs = refs[6:]
    (nsa_ref, win_ref, dsa_ref, ik_ref, dif_ref, qa_ref, ksel_ref, vsel_ref, kw_ref, vw_ref,
     qb_ref, kb_ref, vb_ref, iq_ref, ikd_ref, iw_ref, qc_ref, kcd_ref, vcd_ref, gat_ref) = outs[:20]

    c1, s1, c2, s2 = c1_ref[...], s1_ref[...], c2_ref[...], s2_ref[...]
    prm = prm_ref[...]
    lane = lax.broadcasted_iota(I32, (tm, 128), 1)
    lo = lane < 64
    inner = (lane & 63) < 32

    def col(a):
        return x_ref[:, a:a + 128]

    def gain(r):
        return prm[r:r + 1, :]

    def rms128(v, g):
        return v * lax.rsqrt(jnp.mean(v * v, axis=-1, keepdims=True) + EPS) * g

    def rope128(v):
        return v * c1 + pltpu.roll(v, 64, 1) * s1

    def half_ms(v):
        sq = v * v
        a = jnp.sum(jnp.where(lo, sq, 0.0), axis=-1, keepdims=True)
        b = jnp.sum(jnp.where(lo, 0.0, sq), axis=-1, keepdims=True)
        return a * (1.0 / 64), b * (1.0 / 64)

    def rms64(v, g):
        a, b = half_ms(v)
        return v * lax.rsqrt(jnp.where(lo, a, b) + EPS) * g

    def rope64(v):
        rot = jnp.where(inner, pltpu.roll(v, 96, 1), pltpu.roll(v, 32, 1))
        return v * c2 + rot * s2

    for h in range(A_HEADS):
        qa_ref[h] = (rope128(rms128(col(OFF_AQ + h * 128), gain(0))) * SCALE_D).astype(qa_ref.dtype)
    for kv in range(A_KV):
        o = kv * 128
        kc = rms128(col(OFF_AKV + o), gain(1))
        vc = col(OFF_AKV + 256 + o)
        ks = rope128(rms128(col(OFF_AKV + 512 + o), gain(2)))
        vs = col(OFF_AKV + 768 + o)
        kw = rope128(rms128(col(OFF_AKV + 1024 + o), gain(3)))
        vw = col(OFF_AKV + 1280 + o)
        for slab, val in ((kv, kc), (2 + kv, vc), (4 + kv, ks), (6 + kv, vs)):
            nsa_ref[pl.ds(slab, tm, stride=8), :] = val
        win_ref[pl.ds(kv, tm, stride=4), :] = kw
        win_ref[pl.ds(2 + kv, tm, stride=4), :] = vw
        ksel_ref[kv] = ks.astype(ksel_ref.dtype)
        vsel_ref[kv] = vs.astype(vsel_ref.dtype)
        kw_ref[kv] = kw.astype(kw_ref.dtype)
        vw_ref[kv] = vw.astype(vw_ref.dtype)
        if emit_cmp:
            pa_ref, pb_ref = outs[20], outs[21]
            nch = tm // CMP_STRIDE
            pk = _dot3(wc_ref[0], kc)
            pv = _dot3(wc_ref[1], vc)
            pa_ref[kv] = pk[0:nch]
            pb_ref[kv] = pk[nch:2 * nch]
            pa_ref[2 + kv] = pv[0:nch]
            pb_ref[2 + kv] = pv[nch:2 * nch]
    for h in range(B_HEADS):
        qb_ref[h] = (rope128(rms128(col(OFF_BQ + h * 128), gain(4))) * SCALE_D).astype(qb_ref.dtype)
    kb = rope128(rms128(col(OFF_BKV), gain(5)))
    vb = col(OFF_BKV + 128)
    dsa_ref[pl.ds(0, tm, stride=2), :] = kb
    dsa_ref[pl.ds(1, tm, stride=2), :] = vb
    kb_ref[...] = kb.astype(kb_ref.dtype)
    vb_ref[...] = vb.astype(vb_ref.dtype)
    for p in range(IDX_HEADS // 2):
        v = rope64(col(OFF_BIQ + p * 128))
        iq_ref[2 * p] = jnp.where(lo, v, 0.0).astype(iq_ref.dtype)
        iq_ref[2 * p + 1] = jnp.where(lo, 0.0, v).astype(iq_ref.dtype)
    for h in range(C_HEADS):
        v = rope64(rms64(col(OFF_CQ + h * 128), gain(7))) * SCALE_C
        kv, g = h // 2, h % 2
        qc_ref[kv, 2 * g] = jnp.where(lo, v, 0.0).astype(qc_ref.dtype)
        qc_ref[kv, 2 * g + 1] = jnp.where(lo, 0.0, v).astype(qc_ref.dtype)
    for kv in range(C_KV):
        o = kv * 128
        kk = rope64(rms64(col(OFF_CKV + o), gain(8)))
        vv = col(OFF_CKV + 256 + o)
        dif_ref[pl.ds(kv, tm, stride=4), :] = kk
        dif_ref[pl.ds(2 + kv, tm, stride=4), :] = vv
        kcd_ref[kv] = kk.astype(kcd_ref.dtype)
        vcd_ref[kv] = vv.astype(vcd_ref.dtype)
    m = col(OFF_MISC)
    a, _ = half_ms(m)
    ikr = rope64(m * lax.rsqrt(a + EPS) * gain(6))
    ik_ref[...] = ikr[:, 0:64]
    ikd_ref[...] = jnp.where(lo, ikr, pltpu.roll(ikr, 64, 1)).astype(ikd_ref.dtype)
    sig = jax.nn.sigmoid(m)
    for kv in range(A_KV):
        gat_ref[kv] = pltpu.roll(sig, 128 - MISC_GATE - 12 * kv, 1)
    iw_ref[...] = pltpu.roll(m, 128 - MISC_IW, 1) * ((IDX_DIM ** -0.5) * (IDX_HEADS ** -0.5))


def _post_project(proj, tabs, prm, wc, tm, qdt):
    B, T, _ = proj.shape
    emit_cmp = wc is not None
    nt = T // tm

    def row(c):
        return pl.BlockSpec((None, tm, c), lambda b, i: (b, i, 0))

    def heads(*lead):
        n = len(lead)
        return pl.BlockSpec((None,) + lead + (tm, 128), lambda b, i: (b,) + (0,) * n + (i, 0))

    tab = pl.BlockSpec((tm, 128), lambda b, i: (i, 0))
    in_specs = [row(N_PROJ), tab, tab, tab, tab, pl.BlockSpec((16, 128), lambda b, i: (0, 0))]
    args = [proj, *tabs, prm]
    if emit_cmp:
        in_specs.append(pl.BlockSpec((2, 2 * tm // CMP_STRIDE, tm), lambda b, i: (0, 0, 0)))
        args.append(wc)

    def sds(shape, dt):
        return jax.ShapeDtypeStruct(shape, dt)

    def slabs(n):
        return pl.BlockSpec((None, tm * n, 128), lambda b, i: (b, i, 0))

    out_shape = [
        sds((B, T * 8, 128), F32), sds((B, T * 4, 128), F32), sds((B, T * 2, 128), F32), sds((B, T, 64), F32),
        sds((B, T * 4, 128), F32),
        sds((B, A_HEADS, T, 128), qdt),
        sds((B, A_KV, T, 128), qdt), sds((B, A_KV, T, 128), qdt),
        sds((B, A_KV, T, 128), qdt), sds((B, A_KV, T, 128), qdt),
        sds((B, B_HEADS, T, 128), qdt), sds((B, T, 128), qdt), sds((B, T, 128), qdt),
        sds((B, IDX_HEADS, T, 128), qdt), sds((B, T, 128), qdt), sds((B, T, 128), F32),
        sds((B, C_KV, 4, T, 128), qdt), sds((B, C_KV, T, 128), qdt), sds((B, C_KV, T, 128), qdt),
        sds((B, A_KV, T, 128), F32),
    ]
    out_specs = [
        slabs(8), slabs(4), slabs(2), row(64), slabs(4),
        heads(A_HEADS), heads(A_KV), heads(A_KV), heads(A_KV), heads(A_KV),
        heads(B_HEADS), row(128), row(128), heads(IDX_HEADS), row(128), row(128),
        heads(C_KV, 4), heads(C_KV), heads(C_KV), heads(A_KV),
    ]
    if emit_cmp:
        nc = T // CMP_STRIDE
        out_shape += [sds((B, 4, nc, 128), F32), sds((B, 4, nc, 128), F32)]
        spec = pl.BlockSpec((None, 4, tm // CMP_STRIDE, 128), lambda b, i: (b, 0, i, 0))
        out_specs += [spec, spec]
    return pl.pallas_call(
        functools.partial(_post_kernel, emit_cmp=emit_cmp, tm=tm),
        grid=(B, nt),
        in_specs=in_specs,
        out_specs=out_specs,
        out_shape=out_shape,
        compiler_params=_cparams(("parallel", "parallel")),
        name="post_project",
    )(*args)


def _bias(mask):
    return jnp.where(mask, 0.0, MASKED)


def _softmax_parts(s, bias):
    s = s + bias
    m = jnp.maximum(jnp.max(s, axis=-1, keepdims=True), NEG)
    e = jnp.exp2(s - m)
    return e, 1.0 / jnp.maximum(jnp.sum(e, axis=-1, keepdims=True), 1e-30)


def _masked_softmax(s, bias):
    e, r = _softmax_parts(s, bias)
    return e * r


def _online_step(s, bias, v, m, l, acc):
    if bias is not None:
        s = s + bias
    m_new = jnp.maximum(m, jnp.max(s, axis=-1, keepdims=True))
    alpha = jnp.exp2(m - m_new)
    e = jnp.exp2(s - m_new)
    l_new = alpha * l + jnp.sum(e, axis=-1, keepdims=True)
    rows = acc.shape[0]
    pv = _dot(e.reshape(rows, e.shape[-1]).astype(BF16), v)
    return m_new, l_new, alpha.reshape(rows, 1) * acc + pv


N_CHAINS = 1


def _flash_init(heads, tq):
    hc = heads // N_CHAINS
    return tuple((jnp.full((hc, tq, 1), NEG, F32), jnp.zeros((hc, tq, 1), F32), jnp.zeros((hc * tq, 128), F32))
                 for _ in range(N_CHAINS))


def _flash_tile(q, k, v, bias, carry):
    out = []
    for c, (m, l, acc) in enumerate(carry):
        rows = acc.shape[0]
        s = _dot_nt(q[c * rows:(c + 1) * rows], k).reshape(m.shape[0], m.shape[1], k.shape[0])
        out.append(_online_step(s, bias, v, m, l, acc))
    return tuple(out)


def _flash_finish(carry):
    return jnp.concatenate([acc / jnp.maximum(l.reshape(acc.shape[0], 1), 1e-30) for _, l, acc in carry], axis=0)


def _online_single(s, bias, v_row, m, l, acc):
    if bias is not None:
        s = s + bias
    m_new = jnp.maximum(m, s)
    alpha = jnp.exp2(m - m_new)
    e = jnp.exp2(s - m_new)
    return m_new, alpha * l + e, alpha * acc + e.astype(BF16).astype(F32) * v_row


def _rowdot(q, k_row):
    return jnp.sum(q.astype(BF16).astype(F32) * k_row.astype(BF16).astype(F32), axis=-1, keepdims=True)


def _sortable(x):
    b = lax.bitcast_convert_type(x + 0.0, I32)
    return jnp.where(b < 0, b ^ jnp.int32(0x7FFFFFFF), b)


def _lambda_of(lp, lam_init):
    a = jnp.sum(lp[0:1] * lp[1:2], axis=-1, keepdims=True)
    b = jnp.sum(lp[2:3] * lp[3:4], axis=-1, keepdims=True)
    return jnp.exp(a) - jnp.exp(b) + lam_init


def _nsa_prompt_kernel(q_ref, pak_ref, pbk_ref, pav_ref, pbv_ref, cc_ref, sc_ref,
                       ks_ref, vs_ref, kw_ref, vw_ref, g_ref, o_ref, kc_ref, vc_ref, *, T):
    TQ = TQ_NSA
    qi = pl.program_id(2)
    nc = T // CMP_STRIDE
    ns = T // SEL_BLOCK
    R = A_G * TQ
    band = WINDOW + TQ

    @pl.when(qi == 0)
    def _():
        kraw = pak_ref[...] + pltpu.roll(pbk_ref[...], nc - 1, 0)
        kc_ref[...] = (kraw * cc_ref[...] + pltpu.roll(kraw, 64, 1) * sc_ref[...]).astype(BF16)
        vc_ref[...] = (pav_ref[...] + pltpu.roll(pbv_ref[...], nc - 1, 0)).astype(BF16)

    q = q_ref[...].reshape(R, 128)
    t0 = qi * TQ
    qp3 = t0 + lax.broadcasted_iota(I32, (1, TQ, 1), 1)

    s_c = _dot_nt(q, kc_ref[...]).reshape(A_G, TQ, nc)
    cend = lax.broadcasted_iota(I32, (1, TQ, nc), 2) * CMP_STRIDE + (CMP_LEN - 1)
    e_c, r_c = _softmax_parts(s_c, _bias(cend <= qp3))
    o_c = _dot(e_c.reshape(R, nc).astype(BF16), vc_ref[...]) * r_c.reshape(R, 1)

    sj = lax.broadcasted_iota(I32, (ns, nc), 0) * SEL_BLOCK
    ci = lax.broadcasted_iota(I32, (ns, nc), 1) * CMP_STRIDE
    overlap_t = ((ci < sj + SEL_BLOCK) & (ci + CMP_LEN > sj)).astype(F32)
    imp = _dot_nt(overlap_t, jnp.sum(e_c * r_c, axis=0), HI)
    jidx = lax.broadcasted_iota(I32, (ns, TQ), 0)
    jq = (t0 + lax.broadcasted_iota(I32, (1, TQ), 1)) >> SEL_SHIFT
    forced = (jidx == 0) | (jidx == jq) | (jidx == jq - 1)
    imp = jnp.where(forced, imp + FORCE_BONUS, imp)
    imp = jnp.where(jidx > jq, NEG, imp)
    ng = ns // 8
    sub = lax.broadcasted_iota(I32, (8, TQ), 0)
    imp_g = [imp[8 * g:8 * g + 8, :] for g in range(ng)]
    rank_g = [jnp.zeros((8, TQ), F32) for _ in range(ng)]
    for j in range(ns):
        rj = jnp.broadcast_to(imp[j:j + 1, :], (8, TQ))
        for g in range(ng):
            if g < j // 8:
                ahead = rj > imp_g[g]
            elif g > j // 8:
                ahead = rj >= imp_g[g]
            else:
                ahead = (rj > imp_g[g]) | ((rj == imp_g[g]) & (sub > j % 8))
            rank_g[g] = rank_g[g] + jnp.where(ahead, 1.0, 0.0)
    rank = jnp.concatenate(rank_g, axis=0)
    selb = jnp.where(rank < min(SEL_TOPN, ns), 1.0, 0.0).T.astype(BF16)

    erow = lax.broadcasted_iota(I32, (ns, TK), 0)
    ecol = lax.broadcasted_iota(I32, (ns, TK), 1)
    tcol = lax.broadcasted_iota(I32, (1, TQ, TK), 2)

    def sel_step(kt, carry):
        base = pl.multiple_of(kt * TK, TK)
        expand = (erow == ((ecol + base) >> SEL_SHIFT)).astype(BF16)
        chosen = _dot(selb, expand).reshape(1, TQ, TK) > 0.5
        bias = _bias(chosen & (tcol + base <= qp3))
        return _flash_tile(q, ks_ref[pl.ds(base, TK), :], vs_ref[pl.ds(base, TK), :], bias, carry)

    nkt = (t0 + TQ + TK - 1) // TK
    o_s = _flash_finish(lax.fori_loop(0, nkt, sel_step, _flash_init(A_G, TQ)))

    start = pl.multiple_of(jnp.maximum(t0 - WINDOW, 0), TQ)
    kwin = kw_ref[pl.ds(start, band), :]
    vwin = vw_ref[pl.ds(start, band), :]
    dist = qp3 - (start + lax.broadcasted_iota(I32, (1, TQ, band), 2))
    s_w = _dot_nt(q, kwin).reshape(A_G, TQ, band)
    e_w, r_w = _softmax_parts(s_w, _bias((dist >= 0) & (dist <= WINDOW)))
    o_w = _dot(e_w.reshape(R, band).astype(BF16), vwin) * r_w.reshape(R, 1)

    g = g_ref[...]
    for h in range(A_G):
        r = slice(h * TQ, (h + 1) * TQ)
        o = g[:, 3 * h:3 * h + 1] * o_c[r] + g[:, 3 * h + 1:3 * h + 2] * o_s[r] + g[:, 3 * h + 2:3 * h + 3] * o_w[r]
        o_ref[:, h * 128:(h + 1) * 128] = o.astype(o_ref.dtype)


def _nsa_prompt(qa, pa, pb, cc, sc, ksel, vsel, kw, vw, gat):
    B, _, T, _ = qa.shape
    TQ = TQ_NSA
    nc = T // CMP_STRIDE
    part_k = pl.BlockSpec((None, None, nc, 128), lambda b, kv, i: (b, kv, 0, 0))
    part_v = pl.BlockSpec((None, None, nc, 128), lambda b, kv, i: (b, 2 + kv, 0, 0))
    tabc = pl.BlockSpec((nc, 128), lambda b, kv, i: (0, 0))
    full = pl.BlockSpec((None, None, T, 128), lambda b, kv, i: (b, kv, 0, 0))
    return pl.pallas_call(
        functools.partial(_nsa_prompt_kernel, T=T),
        grid=(B, A_KV, T // TQ),
        in_specs=[
            pl.BlockSpec((None, A_G, TQ, 128), lambda b, kv, i: (b, kv, i, 0)),
            part_k, part_k, part_v, part_v, tabc, tabc, full, full, full, full,
            pl.BlockSpec((None, None, TQ, 128), lambda b, kv, i: (b, kv, i, 0)),
        ],
        out_specs=pl.BlockSpec((None, TQ, A_G * 128), lambda b, kv, i: (b, i, kv)),
        out_shape=jax.ShapeDtypeStruct((B, T, A_HEADS * 128), BF16),
        scratch_shapes=[pltpu.VMEM((nc, 128), BF16), pltpu.VMEM((nc, 128), BF16)],
        compiler_params=_cparams(("parallel", "parallel", "arbitrary")),
        name="nsa_prompt",
    )(qa, pa, pb, pa, pb, cc, sc, ksel, vsel, kw, vw, gat)


def _kth_threshold(count_ge, shape, k):
    def step(it, t):
        cand = t + jnp.left_shift(jnp.int32(1), 31 - it)
        return jnp.where(count_ge(cand) >= k, cand, t)

    return lax.fori_loop(0, 32, step, jnp.full(shape, INT_MIN, I32))


def _dsa_prompt_kernel(q_ref, iq_ref, iw_ref, ik_ref, k_ref, v_ref, o_ref, key_ref, *, T):
    TQ = TQ_DSA
    qi = pl.program_id(1)
    t0 = qi * TQ
    nkt = (t0 + TQ + TK - 1) // TK
    k_top = min(IDX_TOPK, T // 4)
    qp_l = t0 + lax.broadcasted_iota(I32, (1, TQ), 1)
    trow = lax.broadcasted_iota(I32, (TK, TQ), 0)
    iq = iq_ref[...].reshape(IDX_HEADS * TQ, 128)
    iw_t = iw_ref[...].T

    def score_step(kt, _):
        base = pl.multiple_of(kt * TK, TK)
        s = _dot_nt(ik_ref[pl.ds(base, TK), :], iq)
        sc = jnp.zeros((TK, TQ), F32)
        for h in range(IDX_HEADS):
            sc = sc + iw_t[h:h + 1, :] * jnp.maximum(s[:, h * TQ:(h + 1) * TQ], 0.0)
        sc = jnp.where(trow + base <= qp_l, sc, NEG)
        key_ref[pl.ds(base, TK), :] = _sortable(sc)
        return 0

    lax.fori_loop(0, nkt, score_step, 0)

    def count_ge(cand):
        def cstep(kt, c):
            base = pl.multiple_of(kt * TK, TK)
            hit = jnp.where(key_ref[pl.ds(base, TK), :] >= cand, 1.0, 0.0)
            return c + jnp.sum(hit.reshape(TK // 64, 8, 8, TQ), axis=0)

        c = lax.fori_loop(0, nkt, cstep, jnp.zeros((8, 8, TQ), F32))
        return jnp.sum(jnp.sum(c, axis=0), axis=0, keepdims=True)

    thr = _kth_threshold(count_ge, (1, TQ), float(k_top))

    q = q_ref[...].reshape(B_HEADS * TQ, 128)

    def att_step(kt, carry):
        base = pl.multiple_of(kt * TK, TK)
        keep = (key_ref[pl.ds(base, TK), :] >= thr) & (trow + base <= qp_l)
        bias = _bias(keep).T.reshape(1, TQ, TK)
        return _flash_tile(q, k_ref[pl.ds(base, TK), :], v_ref[pl.ds(base, TK), :], bias, carry)

    o = _flash_finish(lax.fori_loop(0, nkt, att_step, _flash_init(B_HEADS, TQ)))
    for h in range(B_HEADS):
        o_ref[:, h * 128:(h + 1) * 128] = o[h * TQ:(h + 1) * TQ].astype(o_ref.dtype)


def _dsa_prompt(qb, iq, iw, ikd, kb, vb):
    B, _, T, _ = qb.shape
    TQ = TQ_DSA
    full = pl.BlockSpec((None, T, 128), lambda b, i: (b, 0, 0))
    return pl.pallas_call(
        functools.partial(_dsa_prompt_kernel, T=T),
        grid=(B, T // TQ),
        in_specs=[
            pl.BlockSpec((None, B_HEADS, TQ, 128), lambda b, i: (b, 0, i, 0)),
            pl.BlockSpec((None, IDX_HEADS, TQ, 128), lambda b, i: (b, 0, i, 0)),
            pl.BlockSpec((None, TQ, 128), lambda b, i: (b, i, 0)),
            full, full, full,
        ],
        out_specs=pl.BlockSpec((None, TQ, B_HEADS * 128), lambda b, i: (b, i, 0)),
        out_shape=jax.ShapeDtypeStruct((B, T, B_HEADS * 128), BF16),
        scratch_shapes=[pltpu.VMEM((T, TQ), I32)],
        compiler_params=_cparams(("parallel", "arbitrary")),
        name="dsa_prompt",
    )(qb, iq, iw, ikd, kb, vb)


def _diff_finish(o, lam, subln, lam_init, rows):
    outs = []
    for g in range(2):
        a0 = o[(2 * g) * rows:(2 * g + 1) * rows]
        a1 = o[(2 * g + 1) * rows:(2 * g + 2) * rows]
        d = a0 - lam * a1
        d = d * lax.rsqrt(jnp.mean(d * d, axis=-1, keepdims=True) + EPS) * subln
        outs.append(d * (1.0 - lam_init))
    return outs


def _diff_prompt_kernel(q_ref, k_ref, v_ref, lp_ref, sub_ref, o_ref, *, lam_init):
    TQ = TQ_DIFF
    qi = pl.program_id(2)
    t0 = qi * TQ
    nkt = (t0 + TQ + TK - 1) // TK
    R = 4 * TQ
    qp = t0 + lax.broadcasted_iota(I32, (1, TQ, 1), 1)
    tcol = lax.broadcasted_iota(I32, (1, TQ, TK), 2)
    q = q_ref[...].reshape(R, 128)

    def step(kt, carry, causal):
        base = pl.multiple_of(kt * TK, TK)
        bias = _bias(tcol + base <= qp) if causal else None
        return _flash_tile(q, k_ref[pl.ds(base, TK), :], v_ref[pl.ds(base, TK), :], bias, carry)

    carry = lax.fori_loop(0, nkt - 1, functools.partial(step, causal=False), _flash_init(4, TQ))
    o = _flash_finish(step(nkt - 1, carry, True))
    lam = _lambda_of(lp_ref[...], lam_init)
    outs = _diff_finish(o, lam, sub_ref[...], lam_init, TQ)
    for g in range(2):
        o_ref[:, g * 128:(g + 1) * 128] = outs[g].astype(o_ref.dtype)


def _diff_prompt(qc, kcd, vcd, lp, subln, lam_init):
    B, _, _, T, _ = qc.shape
    TQ = TQ_DIFF
    full = pl.BlockSpec((None, None, T, 128), lambda b, kv, i: (b, kv, 0, 0))
    return pl.pallas_call(
        functools.partial(_diff_prompt_kernel, lam_init=lam_init),
        grid=(B, C_KV, T // TQ),
        in_specs=[
            pl.BlockSpec((None, None, 4, TQ, 128), lambda b, kv, i: (b, kv, 0, i, 0)),
            full, full,
            pl.BlockSpec((4, C_HALF), lambda b, kv, i: (0, 0)),
            pl.BlockSpec((1, 128), lambda b, kv, i: (0, 0)),
        ],
        out_specs=pl.BlockSpec((None, TQ, 256), lambda b, kv, i: (b, i, kv)),
        out_shape=jax.ShapeDtypeStruct((B, T, C_HEADS * 128), BF16),
        compiler_params=_cparams(("parallel", "parallel", "arbitrary")),
        name="diff_prompt",
    )(qc, kcd, vcd, lp, subln)


def _page_specs(block, layer, n_lead_zero):
    specs = []
    for u in range(PAGES_PER_STEP):
        def imap(b, pc, pt, u=u):
            return (layer, pt[b, pc * PAGES_PER_STEP + u]) + (0,) * n_lead_zero
        specs.append(pl.BlockSpec(block, imap))
    return specs


def _nsa_cmp_decode_kernel(pt_ref, *refs, past):
    pages = refs[:PAGES_PER_STEP]
    q_ref, new_ref, wc_ref, cc_ref, sc_ref, oc_ref, sel_ref, a_ref, b_ref = refs[PAGES_PER_STEP:]
    pc = pl.program_id(1)
    nc = past // CMP_STRIDE
    ns = past // SEL_BLOCK + 1
    nsp = ((ns + 127) // 128) * 128
    cpp = PAGE_SIZE // CMP_STRIDE

    def slab(ref, s):
        return ref[pl.ds(s, PAGE_SIZE, stride=8), :]

    for u in range(PAGES_PER_STEP):
        r0 = pl.multiple_of((pc * PAGES_PER_STEP + u) * cpp, cpp)
        for c in range(2):
            x = jnp.concatenate([slab(pages[u], 2 * c), slab(pages[u], 2 * c + 1)], axis=1)
            part = _dot3(wc_ref[c], x)
            a_ref[pl.ds(r0, cpp), 256 * c:256 * (c + 1)] = part[0:cpp]
            b_ref[pl.ds(r0, cpp), 256 * c:256 * (c + 1)] = part[cpp:2 * cpp]

    @pl.when(pc == pl.num_programs(1) - 1)
    def _():
        rowi = lax.broadcasted_iota(I32, (nc, 128), 0)
        cend = lax.broadcasted_iota(I32, (A_G, nc), 1) * CMP_STRIDE + (CMP_LEN - 1)
        ci = lax.broadcasted_iota(I32, (nc, nsp), 0) * CMP_STRIDE
        sj = lax.broadcasted_iota(I32, (nc, nsp), 1) * SEL_BLOCK
        overlap = ((ci < sj + SEL_BLOCK) & (ci + CMP_LEN > sj)).astype(F32)
        jrow = lax.broadcasted_iota(I32, (1, nsp), 1)
        jq = past // SEL_BLOCK
        ii = lax.broadcasted_iota(I32, (nsp, nsp), 0)
        jj = lax.broadcasted_iota(I32, (nsp, nsp), 1)
        rr = lax.broadcasted_iota(I32, (SEL_TOPN, nsp), 0)
        jr = lax.broadcasted_iota(I32, (SEL_TOPN, nsp), 1).astype(F32)
        new = new_ref[...]
        q = q_ref[...].astype(BF16)
        for kv in range(A_KV):
            ko, vo = kv * 128, 256 + kv * 128
            bk = jnp.where(rowi == nc - 1, wc_ref[0][cpp:cpp + 1, 0:1] * new[:, ko:ko + 128],
                           pltpu.roll(b_ref[:, ko:ko + 128], nc - 1, 0))
            bv = jnp.where(rowi == nc - 1, wc_ref[1][cpp:cpp + 1, 0:1] * new[:, vo:vo + 128],
                           pltpu.roll(b_ref[:, vo:vo + 128], nc - 1, 0))
            kraw = a_ref[:, ko:ko + 128] + bk
            kc = (kraw * cc_ref[...] + pltpu.roll(kraw, 64, 1) * sc_ref[...]).astype(BF16)
            vc = (a_ref[:, vo:vo + 128] + bv).astype(BF16)
            s_c = _dot_nt(q[kv * A_G:(kv + 1) * A_G], kc)
            p_c = _masked_softmax(s_c, _bias(cend <= past))
            oc_ref[kv * A_G:(kv + 1) * A_G, :] = _dot(p_c.astype(BF16), vc)
            imp = _dot(jnp.sum(p_c, axis=0, keepdims=True), overlap, HI)
            forced = (jrow == 0) | (jrow == jq) | (jrow == jq - 1)
            imp = jnp.where(forced, imp + FORCE_BONUS, imp)
            imp = jnp.where(jrow > jq, NEG, imp)
            imp_col = jnp.sum(jnp.where(ii == jj, jnp.broadcast_to(imp, (nsp, nsp)), 0.0), axis=1, keepdims=True)
            beats = (imp_col > imp) | ((imp_col == imp) & (ii < jj))
            rank = jnp.sum(jnp.where(beats, 1.0, 0.0), axis=0, keepdims=True)
            pick = jnp.sum(jnp.where(rank == rr.astype(F32), jr, 0.0), axis=1, keepdims=True)
            sel_ref[kv] = pick.astype(I32)


def _nsa_cmp_decode(page_table, cache, layer, q, new_rows, wc, cc, sc, past):
    DB = q.shape[0]
    npc = page_table.shape[1] // PAGES_PER_STEP
    nc = past // CMP_STRIDE
    block = (None, None, PAGE_SIZE * 8, 128)
    const2 = lambda b, pc, pt: (0, 0)
    return pl.pallas_call(
        functools.partial(_nsa_cmp_decode_kernel, past=past),
        grid_spec=pltpu.PrefetchScalarGridSpec(
            num_scalar_prefetch=1,
            grid=(DB, npc),
            in_specs=_page_specs(block, layer, 2) + [
                pl.BlockSpec((None, A_HEADS, 128), lambda b, pc, pt: (b, 0, 0)),
                pl.BlockSpec((None, 1, 1024), lambda b, pc, pt: (b, 0, 0)),
                pl.BlockSpec((2, 2 * PAGE_SIZE // CMP_STRIDE, PAGE_SIZE), lambda b, pc, pt: (0, 0, 0)),
                pl.BlockSpec((nc, 128), const2),
                pl.BlockSpec((nc, 128), const2),
            ],
            out_specs=[
                pl.BlockSpec((None, A_HEADS, 128), lambda b, pc, pt: (b, 0, 0)),
                pl.BlockSpec((None, A_KV, SEL_TOPN, 1), lambda b, pc, pt: (b, 0, 0, 0)),
            ],
            scratch_shapes=[pltpu.VMEM((nc, 512), F32), pltpu.VMEM((nc, 512), F32)],
        ),
        out_shape=[jax.ShapeDtypeStruct((DB, A_HEADS, 128), F32),
                   jax.ShapeDtypeStruct((DB, A_KV, SEL_TOPN, 1), I32)],
        compiler_params=_cparams(("parallel", "arbitrary")),
        name="nsa_cmp_decode",
    )(page_table, *([cache] * PAGES_PER_STEP), q, new_rows, wc, cc, sc)


def _nsa_sel_decode_kernel(pt_ref, sel_ref, *refs, past, lw):
    blks = refs[:A_KV * SEL_PER_STEP]
    win_ref, q_ref, new_ref, neww_ref, oc_ref, g_ref, o_ref, m_ref, l_ref, acc_ref = refs[A_KV * SEL_PER_STEP:]
    b, s = pl.program_id(0), pl.program_id(1)
    ns = past // SEL_BLOCK + 1
    new = new_ref[...]
    neww = neww_ref[...]
    blk_of_lane = lax.broadcasted_iota(I32, (1, SEL_PER_STEP * SEL_BLOCK), 1) >> SEL_SHIFT

    for kv in range(A_KV):
        q = q_ref[kv * A_G:(kv + 1) * A_G, :].astype(BF16)
        mine = blks[kv * SEL_PER_STEP:(kv + 1) * SEL_PER_STEP]

        @pl.when(s == 0)
        def _():
            m_ref[kv] = _rowdot(q, new[:, 512 + kv * 128:640 + kv * 128])
            l_ref[kv] = jnp.ones((A_G, 1), F32)
            v_new = new[:, 768 + kv * 128:896 + kv * 128].astype(BF16).astype(F32)
            acc_ref[kv] = jnp.broadcast_to(v_new, (A_G, 128))

        k = jnp.concatenate([r[pl.ds(4 + kv, SEL_BLOCK, stride=8), :] for r in mine], axis=0).astype(BF16)
        v = jnp.concatenate([r[pl.ds(6 + kv, SEL_BLOCK, stride=8), :] for r in mine], axis=0).astype(BF16)
        bias = jnp.zeros((1, SEL_PER_STEP * SEL_BLOCK), F32)
        for u in range(SEL_PER_STEP):
            keep = sel_ref[b, kv, s * SEL_PER_STEP + u] != ns - 1
            bias = jnp.where(blk_of_lane == u, jnp.where(keep, 0.0, MASKED), bias)
        m, l, acc = _online_step(_dot_nt(q, k), bias, v, m_ref[kv], l_ref[kv], acc_ref[kv])
        m_ref[kv] = m
        l_ref[kv] = l
        acc_ref[kv] = acc

        @pl.when(s == pl.num_programs(1) - 1)
        def _():
            o_s = acc_ref[kv] / jnp.maximum(l_ref[kv], 1e-30)
            kw = win_ref[pl.ds(kv, lw, stride=4), :].astype(BF16)
            vw = win_ref[pl.ds(2 + kv, lw, stride=4), :].astype(BF16)
            vw_new = neww[:, 256 + kv * 128:384 + kv * 128].astype(BF16).astype(F32)
            s_w = _dot_nt(q, kw)
            s_n = _rowdot(q, neww[:, kv * 128:(kv + 1) * 128])
            mw = jnp.maximum(jnp.max(s_w, axis=-1, keepdims=True), s_n)
            e_w = jnp.exp2(s_w - mw)
            e_n = jnp.exp2(s_n - mw)
            den = jnp.sum(e_w, axis=-1, keepdims=True) + e_n
            o_w = (_dot(e_w.astype(BF16), vw) + e_n.astype(BF16).astype(F32) * vw_new) / den
            o_c = oc_ref[kv * A_G:(kv + 1) * A_G, :]
            g = g_ref[kv * A_G:(kv + 1) * A_G, :]
            o_ref[kv * A_G:(kv + 1) * A_G, :] = g[:, 0:1] * o_c + g[:, 1:2] * o_s + g[:, 2:3] * o_w


def _nsa_sel_decode(page_table, sel, cache, win, layer, q, new_rows, new_win, o_c, gates, past):
    DB = q.shape[0]
    n_pages = page_table.shape[1]
    lw = win.shape[2] // 4

    def blk_spec(kv, u):
        def imap(b, s, pt, sel):
            j = sel[b, kv, s * SEL_PER_STEP + u]
            return (layer, pt[b, jnp.minimum(j // 2, n_pages - 1)], j % 2, 0)
        return pl.BlockSpec((None, None, SEL_BLOCK * 8, 128), imap)

    per_b = lambda b, s, pt, sel: (b, 0, 0)
    n_blk = A_KV * SEL_PER_STEP
    return pl.pallas_call(
        functools.partial(_nsa_sel_decode_kernel, past=past, lw=lw),
        grid_spec=pltpu.PrefetchScalarGridSpec(
            num_scalar_prefetch=2,
            grid=(DB, SEL_TOPN // SEL_PER_STEP),
            in_specs=[blk_spec(kv, u) for kv in range(A_KV) for u in range(SEL_PER_STEP)] + [
                pl.BlockSpec((None, None, lw * 4, 128), lambda b, s, pt, sel: (layer, b, 0, 0)),
                pl.BlockSpec((None, A_HEADS, 128), per_b),
                pl.BlockSpec((None, 1, 1024), per_b),
                pl.BlockSpec((None, 1, 512), per_b),
                pl.BlockSpec((None, A_HEADS, 128), per_b),
                pl.BlockSpec((None, A_HEADS, 128), per_b),
            ],
            out_specs=pl.BlockSpec((None, A_HEADS, 128), per_b),
            scratch_shapes=[pltpu.VMEM((A_KV, A_G, 1), F32), pltpu.VMEM((A_KV, A_G, 1), F32),
                            pltpu.VMEM((A_KV, A_G, 128), F32)],
        ),
        out_shape=jax.ShapeDtypeStruct((DB, A_HEADS, 128), F32),
        compiler_params=_cparams(("parallel", "arbitrary")),
        name="nsa_sel_decode",
    )(page_table, sel, *([cache] * n_blk), win, q, new_rows, new_win, o_c, gates)


def _dsa_idx_decode_kernel(pt_ref, *refs, past):
    pages = refs[:PAGES_PER_STEP]
    iq_ref, iw_ref, ikn_ref, mask_ref, sc_ref = refs[PAGES_PER_STEP:]
    pc = pl.program_id(1)
    n_pages = past // PAGE_SIZE
    k_top = min(IDX_TOPK, (past + 1) // 4)
    iq = iq_ref[...].astype(BF16)
    iw = iw_ref[...]

    @pl.when(pc == 0)
    def _():
        sc_ref[...] = jnp.full(sc_ref.shape, NEG, F32)

    for u in range(PAGES_PER_STEP):
        s = _dot(iq, pages[u][...].astype(BF16))
        sc_ref[pl.ds(pc * PAGES_PER_STEP + u, 1), :] = jnp.sum(iw * jnp.maximum(s, 0.0), axis=0, keepdims=True)

    @pl.when(pc == pl.num_programs(1) - 1)
    def _():
        s_new = _rowdot(iq, ikn_ref[...])
        s_new = jnp.sum(iw * jnp.maximum(s_new, 0.0), axis=0, keepdims=True)
        lane0 = lax.broadcasted_iota(I32, (1, PAGE_SIZE), 1) == 0
        sc_ref[n_pages:n_pages + 1, :] = jnp.where(lane0, s_new, NEG)
        keys = _sortable(sc_ref[...])

        def count_ge(cand):
            c = jnp.sum(jnp.where(keys >= cand, 1.0, 0.0), axis=-1, keepdims=True)
            return jnp.sum(c, axis=0, keepdims=True)

        thr = _kth_threshold(count_ge, (1, 1), float(k_top))
        mask_ref[...] = _bias((keys >= thr) & (sc_ref[...] > 0.5 * NEG))


def _dsa_idx_decode(page_table, cache, layer, iq, iw, ik_new, past):
    DB = iq.shape[0]
    n_pages = page_table.shape[1]
    npc = n_pages // PAGES_PER_STEP
    rows = ((n_pages + 1 + 7) // 8) * 8
    per_b = lambda b, pc, pt: (b, 0, 0)
    return pl.pallas_call(
        functools.partial(_dsa_idx_decode_kernel, past=past),
        grid_spec=pltpu.PrefetchScalarGridSpec(
            num_scalar_prefetch=1,
            grid=(DB, npc),
            in_specs=_page_specs((None, None, IDX_DIM, PAGE_SIZE), layer, 2) + [
                pl.BlockSpec((None, IDX_HEADS, IDX_DIM), per_b),
                pl.BlockSpec((None, IDX_HEADS, 1), per_b),
                pl.BlockSpec((None, 1, IDX_DIM), per_b),
            ],
            out_specs=pl.BlockSpec((None, rows, PAGE_SIZE), per_b),
            scratch_shapes=[pltpu.VMEM((rows, PAGE_SIZE), F32)],
        ),
        out_shape=jax.ShapeDtypeStruct((DB, rows, PAGE_SIZE), F32),
        compiler_params=_cparams(("parallel", "arbitrary")),
        name="dsa_idx_decode",
    )(page_table, *([cache] * PAGES_PER_STEP), iq, iw, ik_new)


def _dsa_att_decode_kernel(pt_ref, *refs, past):
    pages = refs[:PAGES_PER_STEP]
    q_ref, new_ref, mask_ref, o_ref, m_ref, l_ref, acc_ref = refs[PAGES_PER_STEP:]
    pc = pl.program_id(1)
    n_pages = past // PAGE_SIZE
    q = q_ref[...].astype(BF16)

    @pl.when(pc == 0)
    def _():
        m_ref[...] = jnp.full(m_ref.shape, NEG, F32)
        l_ref[...] = jnp.zeros(l_ref.shape, F32)
        acc_ref[...] = jnp.zeros(acc_ref.shape, F32)

    k = jnp.concatenate([p[pl.ds(0, PAGE_SIZE, stride=2), :] for p in pages], axis=0).astype(BF16)
    v = jnp.concatenate([p[pl.ds(1, PAGE_SIZE, stride=2), :] for p in pages], axis=0).astype(BF16)
    r0 = pl.multiple_of(pc * PAGES_PER_STEP, PAGES_PER_STEP)
    mrows = mask_ref[pl.ds(r0, PAGES_PER_STEP), :]
    bias = jnp.concatenate([mrows[u:u + 1, :] for u in range(PAGES_PER_STEP)], axis=1)
    m, l, acc = _online_step(_dot_nt(q, k), bias, v, m_ref[...], l_ref[...], acc_ref[...])
    m_ref[...] = m
    l_ref[...] = l
    acc_ref[...] = acc

    @pl.when(pc == pl.num_programs(1) - 1)
    def _():
        new = new_ref[...]
        v_new = new[:, 128:256].astype(BF16).astype(F32)
        keep = mask_ref[n_pages:n_pages + 1, 0:1]
        _, l2, acc2 = _online_single(_rowdot(q, new[:, 0:128]), keep, v_new, m_ref[...], l_ref[...], acc_ref[...])
        o_ref[...] = acc2 / jnp.maximum(l2, 1e-30)


def _dsa_att_decode(page_table, cache, layer, q, new_rows, mask, past):
    DB = q.shape[0]
    npc = page_table.shape[1] // PAGES_PER_STEP
    rows = mask.shape[1]
    per_b = lambda b, pc, pt: (b, 0, 0)
    return pl.pallas_call(
        functools.partial(_dsa_att_decode_kernel, past=past),
        grid_spec=pltpu.PrefetchScalarGridSpec(
            num_scalar_prefetch=1,
            grid=(DB, npc),
            in_specs=_page_specs((None, None, PAGE_SIZE * 2, 128), layer, 2) + [
                pl.BlockSpec((None, B_HEADS, 128), per_b),
                pl.BlockSpec((None, 1, 256), per_b),
                pl.BlockSpec((None, rows, PAGE_SIZE), per_b),
            ],
            out_specs=pl.BlockSpec((None, B_HEADS, 128), per_b),
            scratch_shapes=[pltpu.VMEM((B_HEADS, 1), F32), pltpu.VMEM((B_HEADS, 1), F32),
                            pltpu.VMEM((B_HEADS, 128), F32)],
        ),
        out_shape=jax.ShapeDtypeStruct((DB, B_HEADS, 128), F32),
        compiler_params=_cparams(("parallel", "arbitrary")),
        name="dsa_att_decode",
    )(page_table, *([cache] * PAGES_PER_STEP), q, new_rows, mask)


def _diff_decode_kernel(pt_ref, *refs, lam_init):
    pages = refs[:PAGES_PER_STEP]
    q_ref, new_ref, lp_ref, sub_ref, o_ref, m_ref, l_ref, acc_ref = refs[PAGES_PER_STEP:]
    pc = pl.program_id(1)

    @pl.when(pc == 0)
    def _():
        m_ref[...] = jnp.full(m_ref.shape, NEG, F32)
        l_ref[...] = jnp.zeros(l_ref.shape, F32)
        acc_ref[...] = jnp.zeros(acc_ref.shape, F32)

    for kv in range(C_KV):
        q = q_ref[kv].astype(BF16)
        k = jnp.concatenate([p[pl.ds(kv, PAGE_SIZE, stride=4), :] for p in pages], axis=0).astype(BF16)
        v = jnp.concatenate([p[pl.ds(2 + kv, PAGE_SIZE, stride=4), :] for p in pages], axis=0).astype(BF16)
        sc = _dot_nt(q, k)
        m, l, acc = _online_step(sc, None, v, m_ref[kv], l_ref[kv], acc_ref[kv])
        m_ref[kv] = m
        l_ref[kv] = l
        acc_ref[kv] = acc

    @pl.when(pc == pl.num_programs(1) - 1)
    def _():
        new = new_ref[...]
        lam = _lambda_of(lp_ref[...], lam_init)
        for kv in range(C_KV):
            k_new = new[:, kv * 128:(kv + 1) * 128]
            v_new = new[:, 256 + kv * 128:384 + kv * 128].astype(BF16).astype(F32)
            _, l, acc = _online_single(_rowdot(q_ref[kv], k_new), None, v_new, m_ref[kv], l_ref[kv], acc_ref[kv])
            outs = _diff_finish(acc / jnp.maximum(l, 1e-30), lam, sub_ref[...], lam_init, 1)
            o_ref[2 * kv:2 * kv + 1, :] = outs[0]
            o_ref[2 * kv + 1:2 * kv + 2, :] = outs[1]


def _diff_decode(page_table, cache, layer, q, new_rows, lp, subln, lam_init):
    DB = q.shape[0]
    npc = page_table.shape[1] // PAGES_PER_STEP
    per_b = lambda b, pc, pt: (b, 0, 0)
    return pl.pallas_call(
        functools.partial(_diff_decode_kernel, lam_init=lam_init),
        grid_spec=pltpu.PrefetchScalarGridSpec(
            num_scalar_prefetch=1,
            grid=(DB, npc),
            in_specs=_page_specs((None, None, PAGE_SIZE * 4, 128), layer, 2) + [
                pl.BlockSpec((None, C_KV, 4, 128), lambda b, pc, pt: (b, 0, 0, 0)),
                pl.BlockSpec((None, 1, 512), per_b),
                pl.BlockSpec((4, C_HALF), lambda b, pc, pt: (0, 0)),
                pl.BlockSpec((1, 128), lambda b, pc, pt: (0, 0)),
            ],
            out_specs=pl.BlockSpec((None, C_HEADS, 128), per_b),
            scratch_shapes=[pltpu.VMEM((C_KV, 4, 1), F32), pltpu.VMEM((C_KV, 4, 1), F32),
                            pltpu.VMEM((C_KV, 4, 128), F32)],
        ),
        out_shape=jax.ShapeDtypeStruct((DB, C_HEADS, 128), F32),
        compiler_params=_cparams(("parallel", "arbitrary")),
        name="diff_decode",
    )(page_table, *([cache] * PAGES_PER_STEP), q, new_rows, lp, subln)


def _rope_tables(pos, d):
    half = d // 2
    inv = ROPE_THETA ** (-jnp.arange(half, dtype=F32) / half)
    ang = pos.astype(F32)[:, None] * inv[None, :]
    cos, sin = jnp.cos(ang), jnp.sin(ang)
    reps = 128 // d
    return jnp.tile(jnp.concatenate([cos, cos], axis=-1), (1, reps)), jnp.tile(jnp.concatenate([-sin, sin], axis=-1), (1, reps))


def _pack_params(nsa_qk_norm, dsa_qk_norm, dsa_idx_knorm, diff_qk_norm):
    rows = [nsa_qk_norm, dsa_qk_norm, jnp.tile(dsa_idx_knorm, 2)[None], jnp.tile(diff_qk_norm, (1, 2))]
    p = jnp.concatenate(rows, axis=0).astype(F32)
    return jnp.pad(p, ((0, 16 - p.shape[0]), (0, 0)))


def _compress_weights(cmp_w, rows):
    eye = jnp.eye(rows // CMP_STRIDE, dtype=F32)
    mats = []
    for c in range(2):
        halves = [jnp.kron(eye, cmp_w[c, h * CMP_STRIDE:(h + 1) * CMP_STRIDE][None, :]) for h in range(2)]
        mats.append(jnp.concatenate(halves, axis=0))
    return jnp.stack(mats, axis=0)


def kernel(x_prompt, x_sample, cache_nsa_kv, state_nsa_win, cache_dsa_kv, cache_dsa_idx, cache_diff_kv, page_table, attn_norm, w_in, nsa_qk_norm, nsa_cmp_w, dsa_qk_norm, dsa_idx_knorm, diff_qk_norm, diff_lambda, diff_subln, w_out, ffn_norm, w_gate_up, w_down):
    B, T, D = x_prompt.shape
    DB = x_sample.shape[0]
    depth = w_in.shape[0]
    n_pages = page_table.shape[1]
    past = n_pages * PAGE_SIZE
    M = B * T
    assert x_sample.shape[1] == 1 and T % TK == 0 and T >= WINDOW + TQ_NSA and n_pages % PAGES_PER_STEP == 0
    tm = min(1024, M)
    tm_ffn = min(1024, M)
    tm_post = 256

    w_in_p = _permute_cast_w_in(jnp.swapaxes(w_in, 1, 2))
    w_out_b = w_out.astype(BF16)
    w_gu_b = w_gate_up.astype(BF16)
    w_down_b = w_down.astype(BF16)

    pos_p = jnp.arange(T, dtype=I32)
    tabs_p = _rope_tables(pos_p, 128) + _rope_tables(pos_p, 64)
    pos_s = jnp.full((DB,), past, I32)
    tabs_s = _rope_tables(pos_s, 128) + _rope_tables(pos_s, 64)
    cend_p = jnp.arange(T // CMP_STRIDE, dtype=I32) * CMP_STRIDE + (CMP_LEN - 1)
    cc_p, sc_p = _rope_tables(cend_p, 128)
    cend_s = jnp.arange(past // CMP_STRIDE, dtype=I32) * CMP_STRIDE + (CMP_LEN - 1)
    cc_s, sc_s = _rope_tables(cend_s, 128)

    n_pool = cache_nsa_kv.shape[1]
    nsa_pages = cache_nsa_kv.reshape(depth, n_pool, PAGE_SIZE * 8, 128)
    dsa_pages = cache_dsa_kv.reshape(depth, n_pool, PAGE_SIZE * 2, 128)
    diff_pages = cache_diff_kv.reshape(depth, n_pool, PAGE_SIZE * 4, 128)
    win_rows = state_nsa_win.reshape(depth, DB, state_nsa_win.shape[2] * 4, 128)
    idx_pages = jnp.swapaxes(cache_dsa_idx, 2, 3)

    yp = x_prompt.reshape(M, D)
    ys = x_sample.reshape(DB, D)
    rows_p, rows_s = [], []
    for l in range(depth):
        lam_init = 0.8 - 0.6 * math.exp(-0.3 * l)
        prm = _pack_params(nsa_qk_norm[l], dsa_qk_norm[l], dsa_idx_knorm[l], diff_qk_norm[l])
        g_attn = attn_norm[l][None, :]
        g_ffn = ffn_norm[l][None, :]
        lp = diff_lambda[l].astype(F32)
        subln = diff_subln[l][None, :].astype(F32)

        proj = _norm_matmul(yp, g_attn, w_in_p, l, tm).reshape(B, T, N_PROJ)
        wc = _compress_weights(nsa_cmp_w[l], tm_post)
        (nsa, win, dsa, ik, dif, qa, ksel, vsel, kw, vw, qb, kb, vb, iq, ikd, iw, qc, kcd, vcd, gat,
         pa, pb) = _post_project(proj, tabs_p, prm, wc, tm_post, BF16)
        o_a = _nsa_prompt(qa, pa, pb, cc_p, sc_p, ksel, vsel, kw, vw, gat)
        o_b = _dsa_prompt(qb, iq, iw, ikd, kb, vb)
        o_c = _diff_prompt(qc, kcd, vcd, lp, subln, lam_init)
        yp = _mix_out_projection(o_a.reshape(M, -1), o_b.reshape(M, -1), o_c.reshape(M, -1), w_out_b, yp, l, tm)
        act = _norm_swiglu(yp, g_ffn, w_gu_b, l, tm_ffn)
        yp = _matmul_residual(act, w_down_b, yp, l, tm_ffn)
        w_keep = min(WINDOW, T)
        rows_p.append((nsa.reshape(B, T, 4, A_KV, 128), win[:, (T - w_keep) * 4:].reshape(B, w_keep, 2, A_KV, 128),
                       dsa.reshape(B, T, 2, 128), ik, dif.reshape(B, T, 2, C_KV, 128)))

        proj_s = _norm_matmul(ys, g_attn, w_in_p, l, DB).reshape(1, DB, N_PROJ)
        (nsa_s, win_s, dsa_s, ik_s, dif_s, qa_s, _, _, _, _, qb_s, _, _, iq_s, _, iw_s, qc_s, _, _,
         gat_s) = _post_project(proj_s, tabs_s, prm, None, DB, F32)
        nsa_new = nsa_s.reshape(DB, 1, 1024)
        win_new = win_s.reshape(DB, 1, 512)
        dsa_new = dsa_s.reshape(DB, 1, 256)
        ik_new = ik_s.reshape(DB, 1, IDX_DIM)
        dif_new = dif_s.reshape(DB, 1, 512)
        qa_d = jnp.transpose(qa_s[0], (1, 0, 2))
        qb_d = jnp.transpose(qb_s[0], (1, 0, 2))
        iq_d = jnp.transpose(iq_s[0], (1, 0, 2))
        iq_d = iq_d[:, :, :64] + iq_d[:, :, 64:]
        iw_d = iw_s[0, :, :IDX_HEADS, None]
        qc_d = jnp.transpose(qc_s[0], (2, 0, 1, 3))
        g_d = jnp.transpose(gat_s[0, :, :, :12].reshape(A_KV, DB, A_G, 3), (1, 0, 2, 3)).reshape(DB, A_HEADS, 3)
        g_d = jnp.pad(g_d, ((0, 0), (0, 0), (0, 125)))

        wc_s = _compress_weights(nsa_cmp_w[l], PAGE_SIZE)
        oc_d, sel = _nsa_cmp_decode(page_table, nsa_pages, l, qa_d, nsa_new, wc_s, cc_s, sc_s, past)
        oa_d = _nsa_sel_decode(page_table, sel.reshape(DB, A_KV, SEL_TOPN), nsa_pages, win_rows, l,
                               qa_d, nsa_new, win_new, oc_d, g_d, past)
        mask = _dsa_idx_decode(page_table, idx_pages, l, iq_d, iw_d, ik_new, past)
        ob_d = _dsa_att_decode(page_table, dsa_pages, l, qb_d, dsa_new, mask, past)
        od_d = _diff_decode(page_table, diff_pages, l, qc_d, dif_new, lp, subln, lam_init)
        ys = _mix_out_projection(oa_d.reshape(DB, 1024).astype(BF16), ob_d.reshape(DB, 512).astype(BF16),
                                 od_d.reshape(DB, 512).astype(BF16), w_out_b, ys, l, DB)
        act_s = _norm_swiglu(ys, g_ffn, w_gu_b, l, DB)
        ys = _matmul_residual(act_s, w_down_b, ys, l, DB)
        lw = state_nsa_win.shape[2]
        win_all = jnp.concatenate([state_nsa_win[l], win_new.reshape(DB, 1, 2, A_KV, 128)], axis=1)
        rows_s.append((nsa_new.reshape(DB, 1, 4, A_KV, 128), win_all[:, win_all.shape[1] - min(WINDOW, lw + 1):],
                       dsa_new.reshape(DB, 1, 2, 128), ik_new, dif_new.reshape(DB, 1, 2, C_KV, 128)))

    def stacked(rows, i):
        return jnp.stack([r[i] for r in rows], axis=0)

    return (yp.reshape(B, T, D), ys.reshape(DB, 1, D),
            stacked(rows_p, 0), stacked(rows_s, 0), stacked(rows_p, 1), stacked(rows_s, 1),
            stacked(rows_p, 2), stacked(rows_s, 2), stacked(rows_p, 3), stacked(rows_s, 3),
            stacked(rows_p, 4), stacked(rows_s, 4))
```

```python
import functools
import math

import jax
import jax.numpy as jnp
from jax import lax
from jax.experimental import pallas as pl
from jax.experimental.pallas import tpu as pltpu

F32 = jnp.float32
BF16 = jnp.bfloat16
I32 = jnp.int32
HI = lax.Precision.HIGHEST

D_MODEL = 2048
PAGE_SIZE = 128
D_HEAD = 128
A_HEADS = 8
A_KV = 2
A_G = A_HEADS // A_KV
B_HEADS = 4
C_HEADS = 4
C_KV = 2
C_HALF = 64
IDX_HEADS = 16
IDX_DIM = 64
IDX_TOPK = 256
CMP_LEN = 32
CMP_STRIDE = 16
SEL_BLOCK = 64
SEL_TOPN = 16
WINDOW = 512
FORCE_BONUS = 1.0e4
D_FF = 5632
ROPE_THETA = 10000.0
EPS = 1e-6
NEG = -1e30
INT_MIN = -2147483648

OFF_AQ = 0
OFF_AKV = 1024
OFF_BQ = 2560
OFF_BKV = 3072
OFF_BIQ = 3328
OFF_CQ = 4352
OFF_CKV = 4864
OFF_MISC = 5376
N_PROJ = 5632
MISC_GATE = 64
MISC_IW = 96

TQ_NSA = 256
TQ_DSA = 128
TQ_DIFF = 256
TK = 512
PAGES_PER_STEP = 16
SEL_SHIFT = 6
SEL_PER_STEP = 4
LOG2E = math.log2(math.e)
SCALE_D = D_HEAD ** -0.5 * LOG2E
SCALE_C = C_HALF ** -0.5 * LOG2E
MASKED = -2e30
VMEM_LIMIT = 56 * 1024 * 1024


def _cparams(sem):
    return pltpu.CompilerParams(dimension_semantics=sem, vmem_limit_bytes=VMEM_LIMIT)


def _dot(a, b, precision=None):
    return jnp.dot(a, b, preferred_element_type=F32, precision=precision)


def _dot3(a, b):
    a_hi = a.astype(BF16)
    b_hi = b.astype(BF16)
    a_lo = (a - a_hi.astype(F32)).astype(BF16)
    b_lo = (b - b_hi.astype(F32)).astype(BF16)
    return _dot(a_hi, b_hi) + (_dot(a_hi, b_lo) + _dot(a_lo, b_hi))


def _dot_nt(a, b, precision=None):
    return lax.dot_general(a, b, (((1,), (1,)), ((), ())), preferred_element_type=F32, precision=precision)


def _norm_mm_kernel(x_ref, g_ref, w_ref, o_ref, xn_ref):
    @pl.when(pl.program_id(1) == 0)
    def _():
        x = x_ref[...]
        ms = jnp.mean(x * x, axis=-1, keepdims=True)
        xn_ref[...] = (x * lax.rsqrt(ms + EPS) * g_ref[...]).astype(BF16)

    o_ref[...] = _dot_nt(xn_ref[...], w_ref[...])


def _norm_matmul(x, g, w_t, layer, tm, tn=512):
    M, K = x.shape
    N = w_t.shape[1]
    return pl.pallas_call(
        _norm_mm_kernel,
        grid=(M // tm, N // tn),
        in_specs=[
            pl.BlockSpec((tm, K), lambda i, j: (i, 0)),
            pl.BlockSpec((1, K), lambda i, j: (0, 0)),
            pl.BlockSpec((None, tn, K), lambda i, j: (layer, j, 0)),
        ],
        out_specs=pl.BlockSpec((tm, tn), lambda i, j: (i, j)),
        out_shape=jax.ShapeDtypeStruct((M, N), F32),
        scratch_shapes=[pltpu.VMEM((tm, K), BF16)],
        compiler_params=_cparams(("parallel", "arbitrary")),
        name="norm_matmul",
    )(x, g, w_t)


def _norm_swiglu_kernel(x_ref, g_ref, wg_ref, wu_ref, o_ref, xn_ref):
    @pl.when(pl.program_id(1) == 0)
    def _():
        x = x_ref[...]
        ms = jnp.mean(x * x, axis=-1, keepdims=True)
        xn_ref[...] = (x * lax.rsqrt(ms + EPS) * g_ref[...]).astype(BF16)

    xn = xn_ref[...]
    gate = _dot(xn, wg_ref[...])
    up = _dot(xn, wu_ref[...])
    o_ref[...] = (gate * jax.nn.sigmoid(gate) * up).astype(o_ref.dtype)


def _norm_swiglu(x, g, w, layer, tm, tn=512):
    M, K = x.shape
    nj = D_FF // tn
    return pl.pallas_call(
        _norm_swiglu_kernel,
        grid=(M // tm, nj),
        in_specs=[
            pl.BlockSpec((tm, K), lambda i, j: (i, 0)),
            pl.BlockSpec((1, K), lambda i, j: (0, 0)),
            pl.BlockSpec((None, K, tn), lambda i, j: (layer, 0, j)),
            pl.BlockSpec((None, K, tn), lambda i, j: (layer, 0, j + nj)),
        ],
        out_specs=pl.BlockSpec((tm, tn), lambda i, j: (i, j)),
        out_shape=jax.ShapeDtypeStruct((M, D_FF), BF16),
        scratch_shapes=[pltpu.VMEM((tm, K), BF16)],
        compiler_params=_cparams(("parallel", "arbitrary")),
        name="norm_swiglu",
    )(x, g, w, w)


def _mm_res_kernel(a_ref, w_ref, r_ref, o_ref):
    o_ref[...] = r_ref[...] + _dot(a_ref[...], w_ref[...])


def _matmul_residual(a, w, res, layer, tm, tn=512):
    M, K = a.shape
    N = w.shape[2]
    return pl.pallas_call(
        _mm_res_kernel,
        grid=(M // tm, N // tn),
        in_specs=[
            pl.BlockSpec((tm, K), lambda i, j: (i, 0)),
            pl.BlockSpec((None, K, tn), lambda i, j: (layer, 0, j)),
            pl.BlockSpec((tm, tn), lambda i, j: (i, j)),
        ],
        out_specs=pl.BlockSpec((tm, tn), lambda i, j: (i, j)),
        out_shape=jax.ShapeDtypeStruct((M, N), F32),
        compiler_params=_cparams(("parallel", "arbitrary")),
        name="matmul_residual",
    )(a, w, res)


def _mix_out_kernel(a_ref, b_ref, c_ref, w_ref, r_ref, o_ref):
    ka, kb = a_ref.shape[1], b_ref.shape[1]
    acc = _dot(a_ref[...], w_ref[0:ka, :])
    acc = acc + _dot(b_ref[...], w_ref[ka:ka + kb, :])
    acc = acc + _dot(c_ref[...], w_ref[ka + kb:, :])
    o_ref[...] = r_ref[...] + acc


def _mix_out_projection(o_a, o_b, o_c, w, res, layer, tm, tn=512):
    M, N = res.shape
    K = w.shape[1]

    def rows(x):
        return pl.BlockSpec((tm, x.shape[1]), lambda i, j: (i, 0))

    return pl.pallas_call(
        _mix_out_kernel,
        grid=(M // tm, N // tn),
        in_specs=[
            rows(o_a), rows(o_b), rows(o_c),
            pl.BlockSpec((None, K, tn), lambda i, j: (layer, 0, j)),
            pl.BlockSpec((tm, tn), lambda i, j: (i, j)),
        ],
        out_specs=pl.BlockSpec((tm, tn), lambda i, j: (i, j)),
        out_shape=jax.ShapeDtypeStruct((M, N), F32),
        compiler_params=_cparams(("parallel", "arbitrary")),
        name="mix_out_projection",
    )(o_a, o_b, o_c, w, res)


_W_IN_PIECES = ((OFF_AQ, 0, 2560), (OFF_BQ, 2584, 1792), (OFF_CQ, 4456, 1024), (OFF_MISC, 4392, 64),
                (OFF_MISC + MISC_GATE, 2560, 24), (OFF_MISC + MISC_IW, 4376, 16))
_W_IN_COLS = 5480


def _permute_cast_kernel(x_ref, o_ref):
    o_ref[OFF_MISC + MISC_GATE:, :] = jnp.zeros((N_PROJ - OFF_MISC - MISC_GATE, o_ref.shape[1]), BF16)
    for dst, src, width in _W_IN_PIECES:
        o_ref[dst:dst + width, :] = x_ref[src:src + width, :].astype(BF16)


def _permute_cast_w_in(w_in_t, tk=512):
    depth, _, K = w_in_t.shape
    return pl.pallas_call(
        _permute_cast_kernel,
        grid=(depth, K // tk),
        in_specs=[pl.BlockSpec((None, _W_IN_COLS, tk), lambda l, i: (l, 0, i))],
        out_specs=pl.BlockSpec((None, N_PROJ, tk), lambda l, i: (l, 0, i)),
        out_shape=jax.ShapeDtypeStruct((depth, N_PROJ, K), BF16),
        compiler_params=_cparams(("parallel", "parallel")),
        name="permute_cast_w_in",
    )(w_in_t)


def _post_kernel(*refs, emit_cmp, tm):
    if emit_cmp:
        x_ref, c1_ref, s1_ref, c2_ref, s2_ref, prm_ref, wc_ref = refs[:7]
        outs = refs[7:]
    else:
        x_ref, c1_ref, s1_ref, c2_ref, s2_ref, prm_ref = refs[:6]
        wc_ref = None
        outs = refs[6:]
    (nsa_ref, win_ref, dsa_ref, ik_ref, dif_ref, qa_ref, ksel_ref, vsel_ref, kw_ref, vw_ref,
     qb_ref, kb_ref, vb_ref, iq_ref, ikd_ref, iw_ref, qc_ref, kcd_ref, vcd_ref, gat_ref) = outs[:20]

    c1, s1, c2, s2 = c1_ref[...], s1_ref[...], c2_ref[...], s2_ref[...]
    prm = prm_ref[...]
    lane = lax.broadcasted_iota(I32, (tm, 128), 1)
    lo = lane < 64
    inner = (lane & 63) < 32

    def col(a):
        return x_ref[:, a:a + 128]

    def gain(r):
        return prm[r:r + 1, :]

    grp_r = lax.broadcasted_iota(I32, (128, 128), 0) >> 6
    grp_c = lax.broadcasted_iota(I32, (128, 128), 1) >> 6
    avg128 = jnp.full((128, 128), 1.0 / 128, BF16)
    avg64 = jnp.where(grp_r == grp_c, 1.0 / 64, 0.0).astype(BF16)

    def group_ms(v, avg):
        sq = v * v
        hi = sq.astype(BF16)
        lo_part = (sq - hi.astype(F32)).astype(BF16)
        return _dot(hi, avg) + _dot(lo_part, avg)

    def rms128(v, g):
        return v * lax.rsqrt(group_ms(v, avg128) + EPS) * g

    def rope128(v):
        return v * c1 + pltpu.roll(v, 64, 1) * s1

    def rms64(v, g):
        return v * lax.rsqrt(group_ms(v, avg64) + EPS) * g

    def rope64(v):
        rot = jnp.where(inner, pltpu.roll(v, 96, 1), pltpu.roll(v, 32, 1))
        return v * c2 + rot * s2

    for h in range(A_HEADS):
        qa_ref[h] = (rope128(rms128(col(OFF_AQ + h * 128), gain(0))) * SCALE_D).astype(qa_ref.dtype)
    for kv in range(A_KV):
        o = kv * 128
        kc = rms128(col(OFF_AKV + o), gain(1))
        vc = col(OFF_AKV + 256 + o)
        ks = rope128(rms128(col(OFF_AKV + 512 + o), gain(2)))
        vs = col(OFF_AKV + 768 + o)
        kw = rope128(rms128(col(OFF_AKV + 1024 + o), gain(3)))
        vw = col(OFF_AKV + 1280 + o)
        for slab, val in ((kv, kc), (2 + kv, vc), (4 + kv, ks), (6 + kv, vs)):
            nsa_ref[pl.ds(slab, tm, stride=8), :] = val
        win_ref[pl.ds(kv, tm, stride=4), :] = kw
        win_ref[pl.ds(2 + kv, tm, stride=4), :] = vw
        ksel_ref[kv] = ks.astype(ksel_ref.dtype)
        vsel_ref[kv] = vs.astype(vsel_ref.dtype)
        kw_ref[kv] = kw.astype(kw_ref.dtype)
        vw_ref[kv] = vw.astype(vw_ref.dtype)
        if emit_cmp:
            pa_ref, pb_ref = outs[20], outs[21]
            nch = tm // CMP_STRIDE
            pk = _dot3(wc_ref[0], kc)
            pv = _dot3(wc_ref[1], vc)
            pa_ref[kv] = pk[0:nch]
            pb_ref[kv] = pk[nch:2 * nch]
            pa_ref[2 + kv] = pv[0:nch]
            pb_ref[2 + kv] = pv[nch:2 * nch]
    for h in range(B_HEADS):
        qb_ref[h] = (rope128(rms128(col(OFF_BQ + h * 128), gain(4))) * SCALE_D).astype(qb_ref.dtype)
    kb = rope128(rms128(col(OFF_BKV), gain(5)))
    vb = col(OFF_BKV + 128)
    dsa_ref[pl.ds(0, tm, stride=2), :] = kb
    dsa_ref[pl.ds(1, tm, stride=2), :] = vb
    kb_ref[...] = kb.astype(kb_ref.dtype)
    vb_ref[...] = vb.astype(vb_ref.dtype)
    for p in range(IDX_HEADS // 2):
        v = rope64(col(OFF_BIQ + p * 128))
        iq_ref[2 * p] = jnp.where(lo, v, 0.0).astype(iq_ref.dtype)
        iq_ref[2 * p + 1] = jnp.where(lo, 0.0, v).astype(iq_ref.dtype)
    for h in range(C_HEADS):
        v = rope64(rms64(col(OFF_CQ + h * 128), gain(7))) * SCALE_C
        kv, g = h // 2, h % 2
        qc_ref[kv, 2 * g] = jnp.where(lo, v, 0.0).astype(qc_ref.dtype)
        qc_ref[kv, 2 * g + 1] = jnp.where(lo, 0.0, v).astype(qc_ref.dtype)
    for kv in range(C_KV):
        o = kv * 128
        kk = rope64(rms64(col(OFF_CKV + o), gain(8)))
        vv = col(OFF_CKV + 256 + o)
        dif_ref[pl.ds(kv, tm, stride=4), :] = kk
        dif_ref[pl.ds(2 + kv, tm, stride=4), :] = vv
        kcd_ref[kv] = kk.astype(kcd_ref.dtype)
        vcd_ref[kv] = vv.astype(vcd_ref.dtype)
    m = col(OFF_MISC)
    ikr = rope64(rms64(m, gain(6)))
    ik_ref[...] = ikr[:, 0:64]
    ikd_ref[...] = jnp.where(lo, ikr, pltpu.roll(ikr, 64, 1)).astype(ikd_ref.dtype)
    sig = jax.nn.sigmoid(m)
    for kv in range(A_KV):
        gat_ref[kv] = pltpu.roll(sig, 128 - MISC_GATE - 12 * kv, 1)
    iw_ref[...] = pltpu.roll(m, 128 - MISC_IW, 1) * ((IDX_DIM ** -0.5) * (IDX_HEADS ** -0.5))


def _post_project(proj, tabs, prm, wc, tm, qdt):
    B, T, _ = proj.shape
    emit_cmp = wc is not None
    nt = T // tm

    def row(c):
        return pl.BlockSpec((None, tm, c), lambda b, i: (b, i, 0))

    def heads(*lead):
        n = len(lead)
        return pl.BlockSpec((None,) + lead + (tm, 128), lambda b, i: (b,) + (0,) * n + (i, 0))

    tab = pl.BlockSpec((tm, 128), lambda b, i: (i, 0))
    in_specs = [row(N_PROJ), tab, tab, tab, tab, pl.BlockSpec((16, 128), lambda b, i: (0, 0))]
    args = [proj, *tabs, prm]
    if emit_cmp:
        in_specs.append(pl.BlockSpec((2, 2 * tm // CMP_STRIDE, tm), lambda b, i: (0, 0, 0)))
        args.append(wc)

    def sds(shape, dt):
        return jax.ShapeDtypeStruct(shape, dt)

    def slabs(n):
        return pl.BlockSpec((None, tm * n, 128), lambda b, i: (b, i, 0))

    out_shape = [
        sds((B, T * 8, 128), F32), sds((B, T * 4, 128), F32), sds((B, T * 2, 128), F32), sds((B, T, 64), F32),
        sds((B, T * 4, 128), F32),
        sds((B, A_HEADS, T, 128), qdt),
        sds((B, A_KV, T, 128), qdt), sds((B, A_KV, T, 128), qdt),
        sds((B, A_KV, T, 128), qdt), sds((B, A_KV, T, 128), qdt),
        sds((B, B_HEADS, T, 128), qdt), sds((B, T, 128), qdt), sds((B, T, 128), qdt),
        sds((B, IDX_HEADS, T, 128), qdt), sds((B, T, 128), qdt), sds((B, T, 128), F32),
        sds((B, C_KV, 4, T, 128), qdt), sds((B, C_KV, T, 128), qdt), sds((B, C_KV, T, 128), qdt),
        sds((B, A_KV, T, 128), F32),
    ]
    out_specs = [
        slabs(8), slabs(4), slabs(2), row(64), slabs(4),
        heads(A_HEADS), heads(A_KV), heads(A_KV), heads(A_KV), heads(A_KV),
        heads(B_HEADS), row(128), row(128), heads(IDX_HEADS), row(128), row(128),
        heads(C_KV, 4), heads(C_KV), heads(C_KV), heads(A_KV),
    ]
    if emit_cmp:
        nc = T // CMP_STRIDE
        out_shape += [sds((B, 4, nc, 128), F32), sds((B, 4, nc, 128), F32)]
        spec = pl.BlockSpec((None, 4, tm // CMP_STRIDE, 128), lambda b, i: (b, 0, i, 0))
        out_specs += [spec, spec]
    return pl.pallas_call(
        functools.partial(_post_kernel, emit_cmp=emit_cmp, tm=tm),
        grid=(B, nt),
        in_specs=in_specs,
        out_specs=out_specs,
        out_shape=out_shape,
        compiler_params=_cparams(("parallel", "parallel")),
        name="post_project",
    )(*args)


def _bias(mask):
    return jnp.where(mask, 0.0, MASKED)


def _softmax_parts(s, bias):
    s = s + bias
    m = jnp.maximum(jnp.max(s, axis=-1, keepdims=True), NEG)
    e = jnp.exp2(s - m)
    return e, 1.0 / jnp.maximum(jnp.sum(e, axis=-1, keepdims=True), 1e-30)


def _masked_softmax(s, bias):
    e, r = _softmax_parts(s, bias)
    return e * r


def _online_step(s, bias, v, m, l, acc):
    if bias is not None:
        s = s + bias
    m_new = jnp.maximum(m, jnp.max(s, axis=-1, keepdims=True))
    alpha = jnp.exp2(m - m_new)
    e = jnp.exp2(s - m_new)
    l_new = alpha * l + jnp.sum(e, axis=-1, keepdims=True)
    rows = acc.shape[0]
    pv = _dot(e.reshape(rows, e.shape[-1]).astype(BF16), v)
    return m_new, l_new, alpha.reshape(rows, 1) * acc + pv


N_CHAINS = 1


def _flash_init(heads, tq):
    hc = heads // N_CHAINS
    return tuple((jnp.full((hc, tq, 1), NEG, F32), jnp.zeros((hc, tq, 1), F32), jnp.zeros((hc * tq, 128), F32))
                 for _ in range(N_CHAINS))


def _flash_tile(q, k, v, bias, carry):
    out = []
    for c, (m, l, acc) in enumerate(carry):
        rows = acc.shape[0]
        s = _dot_nt(q[c * rows:(c + 1) * rows], k).reshape(m.shape[0], m.shape[1], k.shape[0])
        out.append(_online_step(s, bias, v, m, l, acc))
    return tuple(out)


def _flash_finish(carry):
    return jnp.concatenate([acc / jnp.maximum(l.reshape(acc.shape[0], 1), 1e-30) for _, l, acc in carry], axis=0)


def _online_single(s, bias, v_row, m, l, acc):
    if bias is not None:
        s = s + bias
    m_new = jnp.maximum(m, s)
    alpha = jnp.exp2(m - m_new)
    e = jnp.exp2(s - m_new)
    return m_new, alpha * l + e, alpha * acc + e.astype(BF16).astype(F32) * v_row


def _rowdot(q, k_row):
    return jnp.sum(q.astype(BF16).astype(F32) * k_row.astype(BF16).astype(F32), axis=-1, keepdims=True)


def _sortable(x):
    b = lax.bitcast_convert_type(x + 0.0, I32)
    return jnp.where(b < 0, b ^ jnp.int32(0x7FFFFFFF), b)


def _lambda_of(lp, lam_init):
    a = jnp.sum(lp[0:1] * lp[1:2], axis=-1, keepdims=True)
    b = jnp.sum(lp[2:3] * lp[3:4], axis=-1, keepdims=True)
    return jnp.exp(a) - jnp.exp(b) + lam_init


def _nsa_prompt_kernel(q_ref, pak_ref, pbk_ref, pav_ref, pbv_ref, cc_ref, sc_ref,
                       ks_ref, vs_ref, kw_ref, vw_ref, g_ref, o_ref, kc_ref, vc_ref, *, T):
    TQ = TQ_NSA
    qi = pl.program_id(2)
    nc = T // CMP_STRIDE
    ns = T // SEL_BLOCK
    R = A_G * TQ
    band = WINDOW + TQ

    @pl.when(qi == 0)
    def _():
        kraw = pak_ref[...] + pltpu.roll(pbk_ref[...], nc - 1, 0)
        kc_ref[...] = (kraw * cc_ref[...] + pltpu.roll(kraw, 64, 1) * sc_ref[...]).astype(BF16)
        vc_ref[...] = (pav_ref[...] + pltpu.roll(pbv_ref[...], nc - 1, 0)).astype(BF16)

    q = q_ref[...].reshape(R, 128)
    t0 = qi * TQ
    qp3 = t0 + lax.broadcasted_iota(I32, (1, TQ, 1), 1)

    s_c = _dot_nt(q, kc_ref[...]).reshape(A_G, TQ, nc)
    cend = lax.broadcasted_iota(I32, (1, TQ, nc), 2) * CMP_STRIDE + (CMP_LEN - 1)
    e_c, r_c = _softmax_parts(s_c, _bias(cend <= qp3))
    o_c = _dot(e_c.reshape(R, nc).astype(BF16), vc_ref[...]) * r_c.reshape(R, 1)

    sj = lax.broadcasted_iota(I32, (ns, nc), 0) * SEL_BLOCK
    ci = lax.broadcasted_iota(I32, (ns, nc), 1) * CMP_STRIDE
    overlap_t = ((ci < sj + SEL_BLOCK) & (ci + CMP_LEN > sj)).astype(F32)
    imp = _dot_nt(overlap_t, jnp.sum(e_c * r_c, axis=0), HI)
    jidx = lax.broadcasted_iota(I32, (ns, TQ), 0)
    jq = (t0 + lax.broadcasted_iota(I32, (1, TQ), 1)) >> SEL_SHIFT
    forced = (jidx == 0) | (jidx == jq) | (jidx == jq - 1)
    imp = jnp.where(forced, imp + FORCE_BONUS, imp)
    imp = jnp.where(jidx > jq, NEG, imp)
    ng = ns // 8
    sub = lax.broadcasted_iota(I32, (8, TQ), 0)
    imp_g = [imp[8 * g:8 * g + 8, :] for g in range(ng)]
    rank_g = [jnp.zeros((8, TQ), F32) for _ in range(ng)]
    for j in range(ns):
        rj = jnp.broadcast_to(imp[j:j + 1, :], (8, TQ))
        for g in range(ng):
            if g < j // 8:
                ahead = rj > imp_g[g]
            elif g > j // 8:
                ahead = rj >= imp_g[g]
            else:
                ahead = (rj > imp_g[g]) | ((rj == imp_g[g]) & (sub > j % 8))
            rank_g[g] = rank_g[g] + jnp.where(ahead, 1.0, 0.0)
    rank = jnp.concatenate(rank_g, axis=0)
    selb = jnp.where(rank < min(SEL_TOPN, ns), 1.0, 0.0).T.astype(BF16)

    erow = lax.broadcasted_iota(I32, (ns, TK), 0)
    ecol = lax.broadcasted_iota(I32, (ns, TK), 1)
    tcol = lax.broadcasted_iota(I32, (1, TQ, TK), 2)

    def sel_step(kt, carry):
        base = pl.multiple_of(kt * TK, TK)
        expand = (erow == ((ecol + base) >> SEL_SHIFT)).astype(BF16)
        chosen = _dot(selb, expand).reshape(1, TQ, TK) > 0.5
        bias = _bias(chosen & (tcol + base <= qp3))
        return _flash_tile(q, ks_ref[pl.ds(base, TK), :], vs_ref[pl.ds(base, TK), :], bias, carry)

    nkt = (t0 + TQ + TK - 1) // TK
    o_s = _flash_finish(lax.fori_loop(0, nkt, sel_step, _flash_init(A_G, TQ)))

    start = pl.multiple_of(jnp.maximum(t0 - WINDOW, 0), TQ)
    kwin = kw_ref[pl.ds(start, band), :]
    vwin = vw_ref[pl.ds(start, band), :]
    dist = qp3 - (start + lax.broadcasted_iota(I32, (1, TQ, band), 2))
    s_w = _dot_nt(q, kwin).reshape(A_G, TQ, band)
    e_w, r_w = _softmax_parts(s_w, _bias((dist >= 0) & (dist <= WINDOW)))
    o_w = _dot(e_w.reshape(R, band).astype(BF16), vwin) * r_w.reshape(R, 1)

    g = g_ref[...]
    for h in range(A_G):
        r = slice(h * TQ, (h + 1) * TQ)
        o = g[:, 3 * h:3 * h + 1] * o_c[r] + g[:, 3 * h + 1:3 * h + 2] * o_s[r] + g[:, 3 * h + 2:3 * h + 3] * o_w[r]
        o_ref[:, h * 128:(h + 1) * 128] = o.astype(o_ref.dtype)


def _nsa_prompt(qa, pa, pb, cc, sc, ksel, vsel, kw, vw, gat):
    B, _, T, _ = qa.shape
    TQ = TQ_NSA
    nc = T // CMP_STRIDE
    part_k = pl.BlockSpec((None, None, nc, 128), lambda b, kv, i: (b, kv, 0, 0))
    part_v = pl.BlockSpec((None, None, nc, 128), lambda b, kv, i: (b, 2 + kv, 0, 0))
    tabc = pl.BlockSpec((nc, 128), lambda b, kv, i: (0, 0))
    full = pl.BlockSpec((None, None, T, 128), lambda b, kv, i: (b, kv, 0, 0))
    return pl.pallas_call(
        functools.partial(_nsa_prompt_kernel, T=T),
        grid=(B, A_KV, T // TQ),
        in_specs=[
            pl.BlockSpec((None, A_G, TQ, 128), lambda b, kv, i: (b, kv, i, 0)),
            part_k, part_k, part_v, part_v, tabc, tabc, full, full, full, full,
            pl.BlockSpec((None, None, TQ, 128), lambda b, kv, i: (b, kv, i, 0)),
        ],
        out_specs=pl.BlockSpec((None, TQ, A_G * 128), lambda b, kv, i: (b, i, kv)),
        out_shape=jax.ShapeDtypeStruct((B, T, A_HEADS * 128), BF16),
        scratch_shapes=[pltpu.VMEM((nc, 128), BF16), pltpu.VMEM((nc, 128), BF16)],
        compiler_params=_cparams(("parallel", "parallel", "arbitrary")),
        name="nsa_prompt",
    )(qa, pa, pb, pa, pb, cc, sc, ksel, vsel, kw, vw, gat)


def _kth_threshold(count_ge, shape, k):
    def step(it, t):
        cand = t + jnp.left_shift(jnp.int32(1), 31 - it)
        return jnp.where(count_ge(cand) >= k, cand, t)

    return lax.fori_loop(0, 32, step, jnp.full(shape, INT_MIN, I32))


def _dsa_prompt_kernel(q_ref, iq_ref, iw_ref, ik_ref, k_ref, v_ref, o_ref, key_ref, *, T):
    TQ = TQ_DSA
    qi = pl.program_id(1)
    t0 = qi * TQ
    nkt = (t0 + TQ + TK - 1) // TK
    k_top = min(IDX_TOPK, T // 4)
    qp_l = t0 + lax.broadcasted_iota(I32, (1, TQ), 1)
    trow = lax.broadcasted_iota(I32, (TK, TQ), 0)
    iq = iq_ref[...].reshape(IDX_HEADS * TQ, 128)
    iw_t = iw_ref[...].T

    def score_step(kt, _):
        base = pl.multiple_of(kt * TK, TK)
        s = _dot_nt(ik_ref[pl.ds(base, TK), :], iq)
        sc = jnp.zeros((TK, TQ), F32)
        for h in range(IDX_HEADS):
            sc = sc + iw_t[h:h + 1, :] * jnp.maximum(s[:, h * TQ:(h + 1) * TQ], 0.0)
        sc = jnp.where(trow + base <= qp_l, sc, NEG)
        key_ref[pl.ds(base, TK), :] = _sortable(sc)
        return 0

    lax.fori_loop(0, nkt, score_step, 0)

    def count_ge(cand):
        def cstep(kt, c):
            base = pl.multiple_of(kt * TK, TK)
            hit = jnp.where(key_ref[pl.ds(base, TK), :] >= cand, 1.0, 0.0)
            return c + jnp.sum(hit.reshape(TK // 64, 8, 8, TQ), axis=0)

        c = lax.fori_loop(0, nkt, cstep, jnp.zeros((8, 8, TQ), F32))
        return jnp.sum(jnp.sum(c, axis=0), axis=0, keepdims=True)

    thr = _kth_threshold(count_ge, (1, TQ), float(k_top))

    q = q_ref[...].reshape(B_HEADS * TQ, 128)

    def att_step(kt, carry):
        base = pl.multiple_of(kt * TK, TK)
        keep = (key_ref[pl.ds(base, TK), :] >= thr) & (trow + base <= qp_l)
        bias = _bias(keep).T.reshape(1, TQ, TK)
        return _flash_tile(q, k_ref[pl.ds(base, TK), :], v_ref[pl.ds(base, TK), :], bias, carry)

    o = _flash_finish(lax.fori_loop(0, nkt, att_step, _flash_init(B_HEADS, TQ)))
    for h in range(B_HEADS):
        o_ref[:, h * 128:(h + 1) * 128] = o[h * TQ:(h + 1) * TQ].astype(o_ref.dtype)


def _dsa_prompt(qb, iq, iw, ikd, kb, vb):
    B, _, T, _ = qb.shape
    TQ = TQ_DSA
    full = pl.BlockSpec((None, T, 128), lambda b, i: (b, 0, 0))
    return pl.pallas_call(
        functools.partial(_dsa_prompt_kernel, T=T),
        grid=(B, T // TQ),
        in_specs=[
            pl.BlockSpec((None, B_HEADS, TQ, 128), lambda b, i: (b, 0, i, 0)),
            pl.BlockSpec((None, IDX_HEADS, TQ, 128), lambda b, i: (b, 0, i, 0)),
            pl.BlockSpec((None, TQ, 128), lambda b, i: (b, i, 0)),
            full, full, full,
        ],
        out_specs=pl.BlockSpec((None, TQ, B_HEADS * 128), lambda b, i: (b, i, 0)),
        out_shape=jax.ShapeDtypeStruct((B, T, B_HEADS * 128), BF16),
        scratch_shapes=[pltpu.VMEM((T, TQ), I32)],
        compiler_params=_cparams(("parallel", "arbitrary")),
        name="dsa_prompt",
    )(qb, iq, iw, ikd, kb, vb)


def _diff_finish(o, lam, subln, lam_init, rows):
    outs = []
    for g in range(2):
        a0 = o[(2 * g) * rows:(2 * g + 1) * rows]
        a1 = o[(2 * g + 1) * rows:(2 * g + 2) * rows]
        d = a0 - lam * a1
        d = d * lax.rsqrt(jnp.mean(d * d, axis=-1, keepdims=True) + EPS) * subln
        outs.append(d * (1.0 - lam_init))
    return outs


def _diff_prompt_kernel(q_ref, k_ref, v_ref, lp_ref, sub_ref, o_ref, *, lam_init):
    TQ = TQ_DIFF
    qi = pl.program_id(2)
    t0 = qi * TQ
    nkt = (t0 + TQ + TK - 1) // TK
    R = 4 * TQ
    qp = t0 + lax.broadcasted_iota(I32, (1, TQ, 1), 1)
    tcol = lax.broadcasted_iota(I32, (1, TQ, TK), 2)
    q = q_ref[...].reshape(R, 128)

    def step(kt, carry, causal):
        base = pl.multiple_of(kt * TK, TK)
        bias = _bias(tcol + base <= qp) if causal else None
        return _flash_tile(q, k_ref[pl.ds(base, TK), :], v_ref[pl.ds(base, TK), :], bias, carry)

    carry = lax.fori_loop(0, nkt - 1, functools.partial(step, causal=False), _flash_init(4, TQ))
    o = _flash_finish(step(nkt - 1, carry, True))
    lam = _lambda_of(lp_ref[...], lam_init)
    outs = _diff_finish(o, lam, sub_ref[...], lam_init, TQ)
    for g in range(2):
        o_ref[:, g * 128:(g + 1) * 128] = outs[g].astype(o_ref.dtype)


def _diff_prompt(qc, kcd, vcd, lp, subln, lam_init):
    B, _, _, T, _ = qc.shape
    TQ = TQ_DIFF
    full = pl.BlockSpec((None, None, T, 128), lambda b, kv, i: (b, kv, 0, 0))
    return pl.pallas_call(
        functools.partial(_diff_prompt_kernel, lam_init=lam_init),
        grid=(B, C_KV, T // TQ),
        in_specs=[
            pl.BlockSpec((None, None, 4, TQ, 128), lambda b, kv, i: (b, kv, 0, i, 0)),
            full, full,
            pl.BlockSpec((4, C_HALF), lambda b, kv, i: (0, 0)),
            pl.BlockSpec((1, 128), lambda b, kv, i: (0, 0)),
        ],
        out_specs=pl.BlockSpec((None, TQ, 256), lambda b, kv, i: (b, i, kv)),
        out_shape=jax.ShapeDtypeStruct((B, T, C_HEADS * 128), BF16),
        compiler_params=_cparams(("parallel", "parallel", "arbitrary")),
        name="diff_prompt",
    )(qc, kcd, vcd, lp, subln)


def _page_specs(block, layer, n_lead_zero):
    specs = []
    for u in range(PAGES_PER_STEP):
        def imap(b, pc, pt, u=u):
            return (layer, pt[b, pc * PAGES_PER_STEP + u]) + (0,) * n_lead_zero
        specs.append(pl.BlockSpec(block, imap))
    return specs


def _nsa_cmp_decode_kernel(pt_ref, *refs, past):
    pages = refs[:PAGES_PER_STEP]
    q_ref, new_ref, wc_ref, cc_ref, sc_ref, oc_ref, sel_ref, a_ref, b_ref = refs[PAGES_PER_STEP:]
    pc = pl.program_id(1)
    nc = past // CMP_STRIDE
    ns = past // SEL_BLOCK + 1
    nsp = ((ns + 127) // 128) * 128
    cpp = PAGE_SIZE // CMP_STRIDE

    def slab(ref, s):
        return ref[pl.ds(s, PAGE_SIZE, stride=8), :]

    for u in range(PAGES_PER_STEP):
        r0 = pl.multiple_of((pc * PAGES_PER_STEP + u) * cpp, cpp)
        for c in range(2):
            x = jnp.concatenate([slab(pages[u], 2 * c), slab(pages[u], 2 * c + 1)], axis=1)
            part = _dot3(wc_ref[c], x)
            a_ref[pl.ds(r0, cpp), 256 * c:256 * (c + 1)] = part[0:cpp]
            b_ref[pl.ds(r0, cpp), 256 * c:256 * (c + 1)] = part[cpp:2 * cpp]

    @pl.when(pc == pl.num_programs(1) - 1)
    def _():
        rowi = lax.broadcasted_iota(I32, (nc, 128), 0)
        cend = lax.broadcasted_iota(I32, (A_G, nc), 1) * CMP_STRIDE + (CMP_LEN - 1)
        ci = lax.broadcasted_iota(I32, (nc, nsp), 0) * CMP_STRIDE
        sj = lax.broadcasted_iota(I32, (nc, nsp), 1) * SEL_BLOCK
        overlap = ((ci < sj + SEL_BLOCK) & (ci + CMP_LEN > sj)).astype(F32)
        jrow = lax.broadcasted_iota(I32, (1, nsp), 1)
        jq = past // SEL_BLOCK
        ii = lax.broadcasted_iota(I32, (nsp, nsp), 0)
        jj = lax.broadcasted_iota(I32, (nsp, nsp), 1)
        rr = lax.broadcasted_iota(I32, (SEL_TOPN, nsp), 0)
        jr = lax.broadcasted_iota(I32, (SEL_TOPN, nsp), 1).astype(F32)
        new = new_ref[...]
        q = q_ref[...].astype(BF16)
        for kv in range(A_KV):
            ko, vo = kv * 128, 256 + kv * 128
            bk = jnp.where(rowi == nc - 1, wc_ref[0][cpp:cpp + 1, 0:1] * new[:, ko:ko + 128],
                           pltpu.roll(b_ref[:, ko:ko + 128], nc - 1, 0))
            bv = jnp.where(rowi == nc - 1, wc_ref[1][cpp:cpp + 1, 0:1] * new[:, vo:vo + 128],
                           pltpu.roll(b_ref[:, vo:vo + 128], nc - 1, 0))
            kraw = a_ref[:, ko:ko + 128] + bk
            kc = (kraw * cc_ref[...] + pltpu.roll(kraw, 64, 1) * sc_ref[...]).astype(BF16)
            vc = (a_ref[:, vo:vo + 128] + bv).astype(BF16)
            s_c = _dot_nt(q[kv * A_G:(kv + 1) * A_G], kc)
            p_c = _masked_softmax(s_c, _bias(cend <= past))
            oc_ref[kv * A_G:(kv + 1) * A_G, :] = _dot(p_c.astype(BF16), vc)
            imp = _dot(jnp.sum(p_c, axis=0, keepdims=True), overlap, HI)
            forced = (jrow == 0) | (jrow == jq) | (jrow == jq - 1)
            imp = jnp.where(forced, imp + FORCE_BONUS, imp)
            imp = jnp.where(jrow > jq, NEG, imp)
            imp_col = jnp.sum(jnp.where(ii == jj, jnp.broadcast_to(imp, (nsp, nsp)), 0.0), axis=1, keepdims=True)
            beats = (imp_col > imp) | ((imp_col == imp) & (ii < jj))
            rank = jnp.sum(jnp.where(beats, 1.0, 0.0), axis=0, keepdims=True)
            pick = jnp.sum(jnp.where(rank == rr.astype(F32), jr, 0.0), axis=1, keepdims=True)
            sel_ref[kv] = pick.astype(I32)


def _nsa_cmp_decode(page_table, cache, layer, q, new_rows, wc, cc, sc, past):
    DB = q.shape[0]
    npc = page_table.shape[1] // PAGES_PER_STEP
    nc = past // CMP_STRIDE
    block = (None, None, PAGE_SIZE * 8, 128)
    const2 = lambda b, pc, pt: (0, 0)
    return pl.pallas_call(
        functools.partial(_nsa_cmp_decode_kernel, past=past),
        grid_spec=pltpu.PrefetchScalarGridSpec(
            num_scalar_prefetch=1,
            grid=(DB, npc),
            in_specs=_page_specs(block, layer, 2) + [
                pl.BlockSpec((None, A_HEADS, 128), lambda b, pc, pt: (b, 0, 0)),
                pl.BlockSpec((None, 1, 1024), lambda b, pc, pt: (b, 0, 0)),
                pl.BlockSpec((2, 2 * PAGE_SIZE // CMP_STRIDE, PAGE_SIZE), lambda b, pc, pt: (0, 0, 0)),
                pl.BlockSpec((nc, 128), const2),
                pl.BlockSpec((nc, 128), const2),
            ],
            out_specs=[
                pl.BlockSpec((None, A_HEADS, 128), lambda b, pc, pt: (b, 0, 0)),
                pl.BlockSpec((None, A_KV, SEL_TOPN, 1), lambda b, pc, pt: (b, 0, 0, 0)),
            ],
            scratch_shapes=[pltpu.VMEM((nc, 512), F32), pltpu.VMEM((nc, 512), F32)],
        ),
        out_shape=[jax.ShapeDtypeStruct((DB, A_HEADS, 128), F32),
                   jax.ShapeDtypeStruct((DB, A_KV, SEL_TOPN, 1), I32)],
        compiler_params=_cparams(("parallel", "arbitrary")),
        name="nsa_cmp_decode",
    )(page_table, *([cache] * PAGES_PER_STEP), q, new_rows, wc, cc, sc)


def _nsa_sel_decode_kernel(pt_ref, sel_ref, *refs, past, lw):
    blks = refs[:A_KV * SEL_PER_STEP]
    win_ref, q_ref, new_ref, neww_ref, oc_ref, g_ref, o_ref, m_ref, l_ref, acc_ref = refs[A_KV * SEL_PER_STEP:]
    b, s = pl.program_id(0), pl.program_id(1)
    ns = past // SEL_BLOCK + 1
    new = new_ref[...]
    neww = neww_ref[...]
    blk_of_lane = lax.broadcasted_iota(I32, (1, SEL_PER_STEP * SEL_BLOCK), 1) >> SEL_SHIFT

    for kv in range(A_KV):
        q = q_ref[kv * A_G:(kv + 1) * A_G, :].astype(BF16)
        mine = blks[kv * SEL_PER_STEP:(kv + 1) * SEL_PER_STEP]

        @pl.when(s == 0)
        def _():
            m_ref[kv] = _rowdot(q, new[:, 512 + kv * 128:640 + kv * 128])
            l_ref[kv] = jnp.ones((A_G, 1), F32)
            v_new = new[:, 768 + kv * 128:896 + kv * 128].astype(BF16).astype(F32)
            acc_ref[kv] = jnp.broadcast_to(v_new, (A_G, 128))

        k = jnp.concatenate([r[pl.ds(4 + kv, SEL_BLOCK, stride=8), :] for r in mine], axis=0).astype(BF16)
        v = jnp.concatenate([r[pl.ds(6 + kv, SEL_BLOCK, stride=8), :] for r in mine], axis=0).astype(BF16)
        bias = jnp.zeros((1, SEL_PER_STEP * SEL_BLOCK), F32)
        for u in range(SEL_PER_STEP):
            keep = sel_ref[b, kv, s * SEL_PER_STEP + u] != ns - 1
            bias = jnp.where(blk_of_lane == u, jnp.where(keep, 0.0, MASKED), bias)
        m, l, acc = _online_step(_dot_nt(q, k), bias, v, m_ref[kv], l_ref[kv], acc_ref[kv])
        m_ref[kv] = m
        l_ref[kv] = l
        acc_ref[kv] = acc

        @pl.when(s == pl.num_programs(1) - 1)
        def _():
            o_s = acc_ref[kv] / jnp.maximum(l_ref[kv], 1e-30)
            kw = win_ref[pl.ds(kv, lw, stride=4), :].astype(BF16)
            vw = win_ref[pl.ds(2 + kv, lw, stride=4), :].astype(BF16)
            vw_new = neww[:, 256 + kv * 128:384 + kv * 128].astype(BF16).astype(F32)
            s_w = _dot_nt(q, kw)
            s_n = _rowdot(q, neww[:, kv * 128:(kv + 1) * 128])
            mw = jnp.maximum(jnp.max(s_w, axis=-1, keepdims=True), s_n)
            e_w = jnp.exp2(s_w - mw)
            e_n = jnp.exp2(s_n - mw)
            den = jnp.sum(e_w, axis=-1, keepdims=True) + e_n
            o_w = (_dot(e_w.astype(BF16), vw) + e_n.astype(BF16).astype(F32) * vw_new) / den
            o_c = oc_ref[kv * A_G:(kv + 1) * A_G, :]
            g = g_ref[kv * A_G:(kv + 1) * A_G, :]
            o_ref[kv * A_G:(kv + 1) * A_G, :] = g[:, 0:1] * o_c + g[:, 1:2] * o_s + g[:, 2:3] * o_w


def _nsa_sel_decode(page_table, sel, cache, win, layer, q, new_rows, new_win, o_c, gates, past):
    DB = q.shape[0]
    n_pages = page_table.shape[1]
    lw = win.shape[2] // 4

    def blk_spec(kv, u):
        def imap(b, s, pt, sel):
            j = sel[b, kv, s * SEL_PER_STEP + u]
            return (layer, pt[b, jnp.minimum(j // 2, n_pages - 1)], j % 2, 0)
        return pl.BlockSpec((None, None, SEL_BLOCK * 8, 128), imap)

    per_b = lambda b, s, pt, sel: (b, 0, 0)
    n_blk = A_KV * SEL_PER_STEP
    return pl.pallas_call(
        functools.partial(_nsa_sel_decode_kernel, past=past, lw=lw),
        grid_spec=pltpu.PrefetchScalarGridSpec(
            num_scalar_prefetch=2,
            grid=(DB, SEL_TOPN // SEL_PER_STEP),
            in_specs=[blk_spec(kv, u) for kv in range(A_KV) for u in range(SEL_PER_STEP)] + [
                pl.BlockSpec((None, None, lw * 4, 128), lambda b, s, pt, sel: (layer, b, 0, 0)),
                pl.BlockSpec((None, A_HEADS, 128), per_b),
                pl.BlockSpec((None, 1, 1024), per_b),
                pl.BlockSpec((None, 1, 512), per_b),
                pl.BlockSpec((None, A_HEADS, 128), per_b),
                pl.BlockSpec((None, A_HEADS, 128), per_b),
            ],
            out_specs=pl.BlockSpec((None, A_HEADS, 128), per_b),
            scratch_shapes=[pltpu.VMEM((A_KV, A_G, 1), F32), pltpu.VMEM((A_KV, A_G, 1), F32),
                            pltpu.VMEM((A_KV, A_G, 128), F32)],
        ),
        out_shape=jax.ShapeDtypeStruct((DB, A_HEADS, 128), F32),
        compiler_params=_cparams(("parallel", "arbitrary")),
        name="nsa_sel_decode",
    )(page_table, sel, *([cache] * n_blk), win, q, new_rows, new_win, o_c, gates)


def _dsa_idx_decode_kernel(pt_ref, *refs, past):
    pages = refs[:PAGES_PER_STEP]
    iq_ref, iw_ref, ikn_ref, mask_ref, sc_ref = refs[PAGES_PER_STEP:]
    pc = pl.program_id(1)
    n_pages = past // PAGE_SIZE
    k_top = min(IDX_TOPK, (past + 1) // 4)
    iq = iq_ref[...].astype(BF16)
    iw = iw_ref[...]

    @pl.when(pc == 0)
    def _():
        sc_ref[...] = jnp.full(sc_ref.shape, NEG, F32)

    for u in range(PAGES_PER_STEP):
        s = _dot(iq, pages[u][...].astype(BF16))
        sc_ref[pl.ds(pc * PAGES_PER_STEP + u, 1), :] = jnp.sum(iw * jnp.maximum(s, 0.0), axis=0, keepdims=True)

    @pl.when(pc == pl.num_programs(1) - 1)
    def _():
        s_new = _rowdot(iq, ikn_ref[...])
        s_new = jnp.sum(iw * jnp.maximum(s_new, 0.0), axis=0, keepdims=True)
        lane0 = lax.broadcasted_iota(I32, (1, PAGE_SIZE), 1) == 0
        sc_ref[n_pages:n_pages + 1, :] = jnp.where(lane0, s_new, NEG)
        keys = _sortable(sc_ref[...])

        def count_ge(cand):
            c = jnp.sum(jnp.where(keys >= cand, 1.0, 0.0), axis=-1, keepdims=True)
            return jnp.sum(c, axis=0, keepdims=True)

        thr = _kth_threshold(count_ge, (1, 1), float(k_top))
        mask_ref[...] = _bias((keys >= thr) & (sc_ref[...] > 0.5 * NEG))


def _dsa_idx_decode(page_table, cache, layer, iq, iw, ik_new, past):
    DB = iq.shape[0]
    n_pages = page_table.shape[1]
    npc = n_pages // PAGES_PER_STEP
    rows = ((n_pages + 1 + 7) // 8) * 8
    per_b = lambda b, pc, pt: (b, 0, 0)
    return pl.pallas_call(
        functools.partial(_dsa_idx_decode_kernel, past=past),
        grid_spec=pltpu.PrefetchScalarGridSpec(
            num_scalar_prefetch=1,
            grid=(DB, npc),
            in_specs=_page_specs((None, None, IDX_DIM, PAGE_SIZE), layer, 2) + [
                pl.BlockSpec((None, IDX_HEADS, IDX_DIM), per_b),
                pl.BlockSpec((None, IDX_HEADS, 1), per_b),
                pl.BlockSpec((None, 1, IDX_DIM), per_b),
            ],
            out_specs=pl.BlockSpec((None, rows, PAGE_SIZE), per_b),
            scratch_shapes=[pltpu.VMEM((rows, PAGE_SIZE), F32)],
        ),
        out_shape=jax.ShapeDtypeStruct((DB, rows, PAGE_SIZE), F32),
        compiler_params=_cparams(("parallel", "arbitrary")),
        name="dsa_idx_decode",
    )(page_table, *([cache] * PAGES_PER_STEP), iq, iw, ik_new)


def _dsa_att_decode_kernel(pt_ref, *refs, past):
    pages = refs[:PAGES_PER_STEP]
    q_ref, new_ref, mask_ref, o_ref, m_ref, l_ref, acc_ref = refs[PAGES_PER_STEP:]
    pc = pl.program_id(1)
    n_pages = past // PAGE_SIZE
    q = q_ref[...].astype(BF16)

    @pl.when(pc == 0)
    def _():
        m_ref[...] = jnp.full(m_ref.shape, NEG, F32)
        l_ref[...] = jnp.zeros(l_ref.shape, F32)
        acc_ref[...] = jnp.zeros(acc_ref.shape, F32)

    k = jnp.concatenate([p[pl.ds(0, PAGE_SIZE, stride=2), :] for p in pages], axis=0).astype(BF16)
    v = jnp.concatenate([p[pl.ds(1, PAGE_SIZE, stride=2), :] for p in pages], axis=0).astype(BF16)
    r0 = pl.multiple_of(pc * PAGES_PER_STEP, PAGES_PER_STEP)
    mrows = mask_ref[pl.ds(r0, PAGES_PER_STEP), :]
    bias = jnp.concatenate([mrows[u:u + 1, :] for u in range(PAGES_PER_STEP)], axis=1)
    m, l, acc = _online_step(_dot_nt(q, k), bias, v, m_ref[...], l_ref[...], acc_ref[...])
    m_ref[...] = m
    l_ref[...] = l
    acc_ref[...] = acc

    @pl.when(pc == pl.num_programs(1) - 1)
    def _():
        new = new_ref[...]
        v_new = new[:, 128:256].astype(BF16).astype(F32)
        keep = mask_ref[n_pages:n_pages + 1, 0:1]
        _, l2, acc2 = _online_single(_rowdot(q, new[:, 0:128]), keep, v_new, m_ref[...], l_ref[...], acc_ref[...])
        o_ref[...] = acc2 / jnp.maximum(l2, 1e-30)


def _dsa_att_decode(page_table, cache, layer, q, new_rows, mask, past):
    DB = q.shape[0]
    npc = page_table.shape[1] // PAGES_PER_STEP
    rows = mask.shape[1]
    per_b = lambda b, pc, pt: (b, 0, 0)
    return pl.pallas_call(
        functools.partial(_dsa_att_decode_kernel, past=past),
        grid_spec=pltpu.PrefetchScalarGridSpec(
            num_scalar_prefetch=1,
            grid=(DB, npc),
            in_specs=_page_specs((None, None, PAGE_SIZE * 2, 128), layer, 2) + [
                pl.BlockSpec((None, B_HEADS, 128), per_b),
                pl.BlockSpec((None, 1, 256), per_b),
                pl.BlockSpec((None, rows, PAGE_SIZE), per_b),
            ],
            out_specs=pl.BlockSpec((None, B_HEADS, 128), per_b),
            scratch_shapes=[pltpu.VMEM((B_HEADS, 1), F32), pltpu.VMEM((B_HEADS, 1), F32),
                            pltpu.VMEM((B_HEADS, 128), F32)],
        ),
        out_shape=jax.ShapeDtypeStruct((DB, B_HEADS, 128), F32),
        compiler_params=_cparams(("parallel", "arbitrary")),
        name="dsa_att_decode",
    )(page_table, *([cache] * PAGES_PER_STEP), q, new_rows, mask)


def _diff_decode_kernel(pt_ref, *refs, lam_init):
    pages = refs[:PAGES_PER_STEP]
    q_ref, new_ref, lp_ref, sub_ref, o_ref, m_ref, l_ref, acc_ref = refs[PAGES_PER_STEP:]
    pc = pl.program_id(1)

    @pl.when(pc == 0)
    def _():
        m_ref[...] = jnp.full(m_ref.shape, NEG, F32)
        l_ref[...] = jnp.zeros(l_ref.shape, F32)
        acc_ref[...] = jnp.zeros(acc_ref.shape, F32)

    for kv in range(C_KV):
        q = q_ref[kv].astype(BF16)
        k = jnp.concatenate([p[pl.ds(kv, PAGE_SIZE, stride=4), :] for p in pages], axis=0).astype(BF16)
        v = jnp.concatenate([p[pl.ds(2 + kv, PAGE_SIZE, stride=4), :] for p in pages], axis=0).astype(BF16)
        sc = _dot_nt(q, k)
        m, l, acc = _online_step(sc, None, v, m_ref[kv], l_ref[kv], acc_ref[kv])
        m_ref[kv] = m
        l_ref[kv] = l
        acc_ref[kv] = acc

    @pl.when(pc == pl.num_programs(1) - 1)
    def _():
        new = new_ref[...]
        lam = _lambda_of(lp_ref[...], lam_init)
        for kv in range(C_KV):
            k_new = new[:, kv * 128:(kv + 1) * 128]
            v_new = new[:, 256 + kv * 128:384 + kv * 128].astype(BF16).astype(F32)
            _, l, acc = _online_single(_rowdot(q_ref[kv], k_new), None, v_new, m_ref[kv], l_ref[kv], acc_ref[kv])
            outs = _diff_finish(acc / jnp.maximum(l, 1e-30), lam, sub_ref[...], lam_init, 1)
            o_ref[2 * kv:2 * kv + 1, :] = outs[0]
            o_ref[2 * kv + 1:2 * kv + 2, :] = outs[1]


def _diff_decode(page_table, cache, layer, q, new_rows, lp, subln, lam_init):
    DB = q.shape[0]
    npc = page_table.shape[1] // PAGES_PER_STEP
    per_b = lambda b, pc, pt: (b, 0, 0)
    return pl.pallas_call(
        functools.partial(_diff_decode_kernel, lam_init=lam_init),
        grid_spec=pltpu.PrefetchScalarGridSpec(
            num_scalar_prefetch=1,
            grid=(DB, npc),
            in_specs=_page_specs((None, None, PAGE_SIZE * 4, 128), layer, 2) + [
                pl.BlockSpec((None, C_KV, 4, 128), lambda b, pc, pt: (b, 0, 0, 0)),
                pl.BlockSpec((None, 1, 512), per_b),
                pl.BlockSpec((4, C_HALF), lambda b, pc, pt: (0, 0)),
                pl.BlockSpec((1, 128), lambda b, pc, pt: (0, 0)),
            ],
            out_specs=pl.BlockSpec((None, C_HEADS, 128), per_b),
            scratch_shapes=[pltpu.VMEM((C_KV, 4, 1), F32), pltpu.VMEM((C_KV, 4, 1), F32),
                            pltpu.VMEM((C_KV, 4, 128), F32)],
        ),
        out_shape=jax.ShapeDtypeStruct((DB, C_HEADS, 128), F32),
        compiler_params=_cparams(("parallel", "arbitrary")),
        name="diff_decode",
    )(page_table, *([cache] * PAGES_PER_STEP), q, new_rows, lp, subln)


def _rope_tables(pos, d):
    half = d // 2
    inv = ROPE_THETA ** (-jnp.arange(half, dtype=F32) / half)
    ang = pos.astype(F32)[:, None] * inv[None, :]
    cos, sin = jnp.cos(ang), jnp.sin(ang)
    reps = 128 // d
    return jnp.tile(jnp.concatenate([cos, cos], axis=-1), (1, reps)), jnp.tile(jnp.concatenate([-sin, sin], axis=-1), (1, reps))


def _pack_params(nsa_qk_norm, dsa_qk_norm, dsa_idx_knorm, diff_qk_norm):
    rows = [nsa_qk_norm, dsa_qk_norm, jnp.tile(dsa_idx_knorm, 2)[None], jnp.tile(diff_qk_norm, (1, 2))]
    p = jnp.concatenate(rows, axis=0).astype(F32)
    return jnp.pad(p, ((0, 16 - p.shape[0]), (0, 0)))


def _compress_weights(cmp_w, rows):
    eye = jnp.eye(rows // CMP_STRIDE, dtype=F32)
    mats = []
    for c in range(2):
        halves = [jnp.kron(eye, cmp_w[c, h * CMP_STRIDE:(h + 1) * CMP_STRIDE][None, :]) for h in range(2)]
        mats.append(jnp.concatenate(halves, axis=0))
    return jnp.stack(mats, axis=0)


def kernel(x_prompt, x_sample, cache_nsa_kv, state_nsa_win, cache_dsa_kv, cache_dsa_idx, cache_diff_kv, page_table, attn_norm, w_in, nsa_qk_norm, nsa_cmp_w, dsa_qk_norm, dsa_idx_knorm, diff_qk_norm, diff_lambda, diff_subln, w_out, ffn_norm, w_gate_up, w_down):
    B, T, D = x_prompt.shape
    DB = x_sample.shape[0]
    depth = w_in.shape[0]
    n_pages = page_table.shape[1]
    past = n_pages * PAGE_SIZE
    M = B * T
    assert x_sample.shape[1] == 1 and T % TK == 0 and T >= WINDOW + TQ_NSA and n_pages % PAGES_PER_STEP == 0
    tm = min(1024, M)
    tm_ffn = min(1024, M)
    tm_post = 256

    w_in_p = _permute_cast_w_in(jnp.swapaxes(w_in, 1, 2))
    w_out_b = w_out.astype(BF16)
    w_gu_b = w_gate_up.astype(BF16)
    w_down_b = w_down.astype(BF16)

    pos_p = jnp.arange(T, dtype=I32)
    tabs_p = _rope_tables(pos_p, 128) + _rope_tables(pos_p, 64)
    pos_s = jnp.full((DB,), past, I32)
    tabs_s = _rope_tables(pos_s, 128) + _rope_tables(pos_s, 64)
    cend_p = jnp.arange(T // CMP_STRIDE, dtype=I32) * CMP_STRIDE + (CMP_LEN - 1)
    cc_p, sc_p = _rope_tables(cend_p, 128)
    cend_s = jnp.arange(past // CMP_STRIDE, dtype=I32) * CMP_STRIDE + (CMP_LEN - 1)
    cc_s, sc_s = _rope_tables(cend_s, 128)

    n_pool = cache_nsa_kv.shape[1]
    nsa_pages = cache_nsa_kv.reshape(depth, n_pool, PAGE_SIZE * 8, 128)
    dsa_pages = cache_dsa_kv.reshape(depth, n_pool, PAGE_SIZE * 2, 128)
    diff_pages = cache_diff_kv.reshape(depth, n_pool, PAGE_SIZE * 4, 128)
    win_rows = state_nsa_win.reshape(depth, DB, state_nsa_win.shape[2] * 4, 128)
    idx_pages = jnp.swapaxes(cache_dsa_idx, 2, 3)

    yp = x_prompt.reshape(M, D)
    ys = x_sample.reshape(DB, D)
    rows_p, rows_s = [], []
    for l in range(depth):
        lam_init = 0.8 - 0.6 * math.exp(-0.3 * l)
        prm = _pack_params(nsa_qk_norm[l], dsa_qk_norm[l], dsa_idx_knorm[l], diff_qk_norm[l])
        g_attn = attn_norm[l][None, :]
        g_ffn = ffn_norm[l][None, :]
        lp = diff_lambda[l].astype(F32)
        subln = diff_subln[l][None, :].astype(F32)

        proj = _norm_matmul(yp, g_attn, w_in_p, l, tm).reshape(B, T, N_PROJ)
        wc = _compress_weights(nsa_cmp_w[l], tm_post)
        (nsa, win, dsa, ik, dif, qa, ksel, vsel, kw, vw, qb, kb, vb, iq, ikd, iw, qc, kcd, vcd, gat,
         pa, pb) = _post_project(proj, tabs_p, prm, wc, tm_post, BF16)
        o_a = _nsa_prompt(qa, pa, pb, cc_p, sc_p, ksel, vsel, kw, vw, gat)
        o_b = _dsa_prompt(qb, iq, iw, ikd, kb, vb)
        o_c = _diff_prompt(qc, kcd, vcd, lp, subln, lam_init)
        yp = _mix_out_projection(o_a.reshape(M, -1), o_b.reshape(M, -1), o_c.reshape(M, -1), w_out_b, yp, l, tm)
        act = _norm_swiglu(yp, g_ffn, w_gu_b, l, tm_ffn)
        yp = _matmul_residual(act, w_down_b, yp, l, tm_ffn)
        w_keep = min(WINDOW, T)
        rows_p.append((nsa.reshape(B, T, 4, A_KV, 128), win[:, (T - w_keep) * 4:].reshape(B, w_keep, 2, A_KV, 128),
                       dsa.reshape(B, T, 2, 128), ik, dif.reshape(B, T, 2, C_KV, 128)))

        proj_s = _norm_matmul(ys, g_attn, w_in_p, l, DB).reshape(1, DB, N_PROJ)
        (nsa_s, win_s, dsa_s, ik_s, dif_s, qa_s, _, _, _, _, qb_s, _, _, iq_s, _, iw_s, qc_s, _, _,
         gat_s) = _post_project(proj_s, tabs_s, prm, None, DB, F32)
        nsa_new = nsa_s.reshape(DB, 1, 1024)
        win_new = win_s.reshape(DB, 1, 512)
        dsa_new = dsa_s.reshape(DB, 1, 256)
        ik_new = ik_s.reshape(DB, 1, IDX_DIM)
        dif_new = dif_s.reshape(DB, 1, 512)
        qa_d = jnp.transpose(qa_s[0], (1, 0, 2))
        qb_d = jnp.transpose(qb_s[0], (1, 0, 2))
        iq_d = jnp.transpose(iq_s[0], (1, 0, 2))
        iq_d = iq_d[:, :, :64] + iq_d[:, :, 64:]
        iw_d = iw_s[0, :, :IDX_HEADS, None]
        qc_d = jnp.transpose(qc_s[0], (2, 0, 1, 3))
        g_d = jnp.transpose(gat_s[0, :, :, :12].reshape(A_KV, DB, A_G, 3), (1, 0, 2, 3)).reshape(DB, A_HEADS, 3)
        g_d = jnp.pad(g_d, ((0, 0), (0, 0), (0, 125)))

        wc_s = _compress_weights(nsa_cmp_w[l], PAGE_SIZE)
        oc_d, sel = _nsa_cmp_decode(page_table, nsa_pages, l, qa_d, nsa_new, wc_s, cc_s, sc_s, past)
        oa_d = _nsa_sel_decode(page_table, sel.reshape(DB, A_KV, SEL_TOPN), nsa_pages, win_rows, l,
                               qa_d, nsa_new, win_new, oc_d, g_d, past)
        mask = _dsa_idx_decode(page_table, idx_pages, l, iq_d, iw_d, ik_new, past)
        ob_d = _dsa_att_decode(page_table, dsa_pages, l, qb_d, dsa_new, mask, past)
        od_d = _diff_decode(page_table, diff_pages, l, qc_d, dif_new, lp, subln, lam_init)
        ys = _mix_out_projection(oa_d.reshape(DB, 1024).astype(BF16), ob_d.reshape(DB, 512).astype(BF16),
                                 od_d.reshape(DB, 512).astype(BF16), w_out_b, ys, l, DB)
        act_s = _norm_swiglu(ys, g_ffn, w_gu_b, l, DB)
        ys = _matmul_residual(act_s, w_down_b, ys, l, DB)
        lw = state_nsa_win.shape[2]
        win_all = jnp.concatenate([state_nsa_win[l], win_new.reshape(DB, 1, 2, A_KV, 128)], axis=1)
        rows_s.append((nsa_new.reshape(DB, 1, 4, A_KV, 128), win_all[:, win_all.shape[1] - min(WINDOW, lw + 1):],
                       dsa_new.reshape(DB, 1, 2, 128), ik_new, dif_new.reshape(DB, 1, 2, C_KV, 128)))

    def stacked(rows, i):
        return jnp.stack([r[i] for r in rows], axis=0)

    return (yp.reshape(B, T, D), ys.reshape(DB, 1, D),
            stacked(rows_p, 0), stacked(rows_s, 0), stacked(rows_p, 1), stacked(rows_s, 1),
            stacked(rows_p, 2), stacked(rows_s, 2), stacked(rows_p, 3), stacked(rows_s, 3),
            stacked(rows_p, 4), stacked(rows_s, 4))
```

```python
import functools
import math

import jax
import jax.numpy as jnp
from jax import lax
from jax.experimental import pallas as pl
from jax.experimental.pallas import tpu as pltpu

F32 = jnp.float32
BF16 = jnp.bfloat16
I32 = jnp.int32
HI = lax.Precision.HIGHEST

D_MODEL = 2048
PAGE_SIZE = 128
D_HEAD = 128
A_HEADS = 8
A_KV = 2
A_G = A_HEADS // A_KV
B_HEADS = 4
C_HEADS = 4
C_KV = 2
C_HALF = 64
IDX_HEADS = 16
IDX_DIM = 64
IDX_TOPK = 256
CMP_LEN = 32
CMP_STRIDE = 16
SEL_BLOCK = 64
SEL_TOPN = 16
WINDOW = 512
FORCE_BONUS = 1.0e4
D_FF = 5632
ROPE_THETA = 10000.0
EPS = 1e-6
NEG = -1e30
INT_MIN = -2147483648

OFF_AQ = 0
OFF_AKV = 1024
OFF_BQ = 2560
OFF_BKV = 3072
OFF_BIQ = 3328
OFF_CQ = 4352
OFF_CKV = 4864
OFF_MISC = 5376
N_PROJ = 5632
MISC_GATE = 64
MISC_IW = 96

TQ_NSA = 256
TQ_DSA = 128
TQ_DIFF = 256
TK = 512
PAGES_PER_STEP = 16
SEL_SHIFT = 6
SEL_PER_STEP = 4
LOG2E = math.log2(math.e)
SCALE_D = D_HEAD ** -0.5 * LOG2E
SCALE_C = C_HALF ** -0.5 * LOG2E
MASKED = -2e30
VMEM_LIMIT = 56 * 1024 * 1024


def _cparams(sem):
    return pltpu.CompilerParams(dimension_semantics=sem, vmem_limit_bytes=VMEM_LIMIT)


def _dot(a, b, precision=None):
    return jnp.dot(a, b, preferred_element_type=F32, precision=precision)


def _dot3(a, b):
    a_hi = a.astype(BF16)
    b_hi = b.astype(BF16)
    a_lo = (a - a_hi.astype(F32)).astype(BF16)
    b_lo = (b - b_hi.astype(F32)).astype(BF16)
    return _dot(a_hi, b_hi) + (_dot(a_hi, b_lo) + _dot(a_lo, b_hi))


def _dot_nt(a, b, precision=None):
    return lax.dot_general(a, b, (((1,), (1,)), ((), ())), preferred_element_type=F32, precision=precision)


def _norm_mm_kernel(x_ref, g_ref, w_ref, o_ref, xn_ref):
    @pl.when(pl.program_id(1) == 0)
    def _():
        x = x_ref[...]
        ms = jnp.mean(x * x, axis=-1, keepdims=True)
        xn_ref[...] = (x * lax.rsqrt(ms + EPS) * g_ref[...]).astype(BF16)

    o_ref[...] = _dot_nt(xn_ref[...], w_ref[...])


def _norm_matmul(x, g, w_t, layer, tm, tn=512):
    M, K = x.shape
    N = w_t.shape[1]
    return pl.pallas_call(
        _norm_mm_kernel,
        grid=(M // tm, N // tn),
        in_specs=[
            pl.BlockSpec((tm, K), lambda i, j: (i, 0)),
            pl.BlockSpec((1, K), lambda i, j: (0, 0)),
            pl.BlockSpec((None, tn, K), lambda i, j: (layer, j, 0)),
        ],
        out_specs=pl.BlockSpec((tm, tn), lambda i, j: (i, j)),
        out_shape=jax.ShapeDtypeStruct((M, N), F32),
        scratch_shapes=[pltpu.VMEM((tm, K), BF16)],
        compiler_params=_cparams(("parallel", "arbitrary")),
        name="norm_matmul",
    )(x, g, w_t)


def _norm_swiglu_kernel(x_ref, g_ref, wg_ref, wu_ref, o_ref, xn_ref):
    @pl.when(pl.program_id(1) == 0)
    def _():
        x = x_ref[...]
        ms = jnp.mean(x * x, axis=-1, keepdims=True)
        xn_ref[...] = (x * lax.rsqrt(ms + EPS) * g_ref[...]).astype(BF16)

    xn = xn_ref[...]
    gate = _dot(xn, wg_ref[...])
    up = _dot(xn, wu_ref[...])
    o_ref[...] = (gate * jax.nn.sigmoid(gate) * up).astype(o_ref.dtype)


def _norm_swiglu(x, g, w, layer, tm, tn=512):
    M, K = x.shape
    nj = D_FF // tn
    return pl.pallas_call(
        _norm_swiglu_kernel,
        grid=(M // tm, nj),
        in_specs=[
            pl.BlockSpec((tm, K), lambda i, j: (i, 0)),
            pl.BlockSpec((1, K), lambda i, j: (0, 0)),
            pl.BlockSpec((None, K, tn), lambda i, j: (layer, 0, j)),
            pl.BlockSpec((None, K, tn), lambda i, j: (layer, 0, j + nj)),
        ],
        out_specs=pl.BlockSpec((tm, tn), lambda i, j: (i, j)),
        out_shape=jax.ShapeDtypeStruct((M, D_FF), BF16),
        scratch_shapes=[pltpu.VMEM((tm, K), BF16)],
        compiler_params=_cparams(("parallel", "arbitrary")),
        name="norm_swiglu",
    )(x, g, w, w)


def _mm_res_kernel(a_ref, w_ref, r_ref, o_ref):
    o_ref[...] = r_ref[...] + _dot(a_ref[...], w_ref[...])


def _matmul_residual(a, w, res, layer, tm, tn=512):
    M, K = a.shape
    N = w.shape[2]
    return pl.pallas_call(
        _mm_res_kernel,
        grid=(M // tm, N // tn),
        in_specs=[
            pl.BlockSpec((tm, K), lambda i, j: (i, 0)),
            pl.BlockSpec((None, K, tn), lambda i, j: (layer, 0, j)),
            pl.BlockSpec((tm, tn), lambda i, j: (i, j)),
        ],
        out_specs=pl.BlockSpec((tm, tn), lambda i, j: (i, j)),
        out_shape=jax.ShapeDtypeStruct((M, N), F32),
        compiler_params=_cparams(("parallel", "arbitrary")),
        name="matmul_residual",
    )(a, w, res)


def _mix_out_kernel(a_ref, b_ref, c_ref, w_ref, r_ref, o_ref):
    ka, kb = a_ref.shape[1], b_ref.shape[1]
    acc = _dot(a_ref[...], w_ref[0:ka, :])
    acc = acc + _dot(b_ref[...], w_ref[ka:ka + kb, :])
    acc = acc + _dot(c_ref[...], w_ref[ka + kb:, :])
    o_ref[...] = r_ref[...] + acc


def _mix_out_projection(o_a, o_b, o_c, w, res, layer, tm, tn=512):
    M, N = res.shape
    K = w.shape[1]

    def rows(x):
        return pl.BlockSpec((tm, x.shape[1]), lambda i, j: (i, 0))

    return pl.pallas_call(
        _mix_out_kernel,
        grid=(M // tm, N // tn),
        in_specs=[
            rows(o_a), rows(o_b), rows(o_c),
            pl.BlockSpec((None, K, tn), lambda i, j: (layer, 0, j)),
            pl.BlockSpec((tm, tn), lambda i, j: (i, j)),
        ],
        out_specs=pl.BlockSpec((tm, tn), lambda i, j: (i, j)),
        out_shape=jax.ShapeDtypeStruct((M, N), F32),
        compiler_params=_cparams(("parallel", "arbitrary")),
        name="mix_out_projection",
    )(o_a, o_b, o_c, w, res)


_W_IN_PIECES = ((OFF_AQ, 0, 2560), (OFF_BQ, 2584, 1792), (OFF_CQ, 4456, 1024), (OFF_MISC, 4392, 64),
                (OFF_MISC + MISC_GATE, 2560, 24), (OFF_MISC + MISC_IW, 4376, 16))
_W_IN_COLS = 5480


def _permute_cast_kernel(x_ref, o_ref):
    o_ref[OFF_MISC + MISC_GATE:, :] = jnp.zeros((N_PROJ - OFF_MISC - MISC_GATE, o_ref.shape[1]), BF16)
    for dst, src, width in _W_IN_PIECES:
        o_ref[dst:dst + width, :] = x_ref[src:src + width, :].astype(BF16)


def _permute_cast_w_in(w_in_t, tk=512):
    depth, _, K = w_in_t.shape
    return pl.pallas_call(
        _permute_cast_kernel,
        grid=(depth, K // tk),
        in_specs=[pl.BlockSpec((None, _W_IN_COLS, tk), lambda l, i: (l, 0, i))],
        out_specs=pl.BlockSpec((None, N_PROJ, tk), lambda l, i: (l, 0, i)),
        out_shape=jax.ShapeDtypeStruct((depth, N_PROJ, K), BF16),
        compiler_params=_cparams(("parallel", "parallel")),
        name="permute_cast_w_in",
    )(w_in_t)


def _post_kernel(*refs, emit_cmp, tm):
    if emit_cmp:
        x_ref, c1_ref, s1_ref, c2_ref, s2_ref, prm_ref, wc_ref = refs[:7]
        outs = refs[7:]
    else:
        x_ref, c1_ref, s1_ref, c2_ref, s2_ref, prm_ref = refs[:6]
        wc_ref = None
        outs = refs[6:]
    (nsa_ref, win_ref, dsa_ref, ik_ref, dif_ref, qa_ref, ksel_ref, vsel_ref, kw_ref, vw_ref,
     qb_ref, kb_ref, vb_ref, iq_ref, ikd_ref, iw_ref, qc_ref, kcd_ref, vcd_ref, gat_ref) = outs[:20]

    c1, s1, c2, s2 = c1_ref[...], s1_ref[...], c2_ref[...], s2_ref[...]
    prm = prm_ref[...]
    lane = lax.broadcasted_iota(I32, (tm, 128), 1)
    lo = lane < 64
    inner = (lane & 63) < 32

    def col(a):
        return x_ref[:, a:a + 128]

    def gain(r):
        return prm[r:r + 1, :]

    grp_r = lax.broadcasted_iota(I32, (128, 128), 0) >> 6
    grp_c = lax.broadcasted_iota(I32, (128, 128), 1) >> 6
    avg128 = jnp.full((128, 128), 1.0 / 128, BF16)
    avg64 = jnp.where(grp_r == grp_c, 1.0 / 64, 0.0).astype(BF16)

    def group_ms(v, avg):
        sq = v * v
        hi = sq.astype(BF16)
        lo_part = (sq - hi.astype(F32)).astype(BF16)
        return _dot(hi, avg) + _dot(lo_part, avg)

    def rms128(v, g):
        return v * lax.rsqrt(group_ms(v, avg128) + EPS) * g

    def rope128(v):
        return v * c1 + pltpu.roll(v, 64, 1) * s1

    def rms64(v, g):
        return v * lax.rsqrt(group_ms(v, avg64) + EPS) * g

    def rope64(v):
        rot = jnp.where(inner, pltpu.roll(v, 96, 1), pltpu.roll(v, 32, 1))
        return v * c2 + rot * s2

    for h in range(A_HEADS):
        qa_ref[h] = (rope128(rms128(col(OFF_AQ + h * 128), gain(0))) * SCALE_D).astype(qa_ref.dtype)
    for kv in range(A_KV):
        o = kv * 128
        kc = rms128(col(OFF_AKV + o), gain(1))
        vc = col(OFF_AKV + 256 + o)
        ks = rope128(rms128(col(OFF_AKV + 512 + o), gain(2)))
        vs = col(OFF_AKV + 768 + o)
        kw = rope128(rms128(col(OFF_AKV + 1024 + o), gain(3)))
        vw = col(OFF_AKV + 1280 + o)
        for slab, val in ((kv, kc), (2 + kv, vc), (4 + kv, ks), (6 + kv, vs)):
            nsa_ref[pl.ds(slab, tm, stride=8), :] = val
        win_ref[pl.ds(kv, tm, stride=4), :] = kw
        win_ref[pl.ds(2 + kv, tm, stride=4), :] = vw
        ksel_ref[kv] = ks.astype(ksel_ref.dtype)
        vsel_ref[kv] = vs.astype(vsel_ref.dtype)
        kw_ref[kv] = kw.astype(kw_ref.dtype)
        vw_ref[kv] = vw.astype(vw_ref.dtype)
        if emit_cmp:
            pa_ref, pb_ref = outs[20], outs[21]
            nch = tm // CMP_STRIDE
            pk = _dot3(wc_ref[0], kc)
            pv = _dot3(wc_ref[1], vc)
            pa_ref[kv] = pk[0:nch]
            pb_ref[kv] = pk[nch:2 * nch]
            pa_ref[2 + kv] = pv[0:nch]
            pb_ref[2 + kv] = pv[nch:2 * nch]
    for h in range(B_HEADS):
        qb_ref[h] = (rope128(rms128(col(OFF_BQ + h * 128), gain(4))) * SCALE_D).astype(qb_ref.dtype)
    kb = rope128(rms128(col(OFF_BKV), gain(5)))
    vb = col(OFF_BKV + 128)
    dsa_ref[pl.ds(0, tm, stride=2), :] = kb
    dsa_ref[pl.ds(1, tm, stride=2), :] = vb
    kb_ref[...] = kb.astype(kb_ref.dtype)
    vb_ref[...] = vb.astype(vb_ref.dtype)
    for p in range(IDX_HEADS // 2):
        v = rope64(col(OFF_BIQ + p * 128))
        iq_ref[2 * p] = jnp.where(lo, v, 0.0).astype(iq_ref.dtype)
        iq_ref[2 * p + 1] = jnp.where(lo, 0.0, v).astype(iq_ref.dtype)
    for h in range(C_HEADS):
        v = rope64(rms64(col(OFF_CQ + h * 128), gain(7))) * SCALE_C
        kv, g = h // 2, h % 2
        qc_ref[kv, 2 * g] = jnp.where(lo, v, 0.0).astype(qc_ref.dtype)
        qc_ref[kv, 2 * g + 1] = jnp.where(lo, 0.0, v).astype(qc_ref.dtype)
    for kv in range(C_KV):
        o = kv * 128
        kk = rope64(rms64(col(OFF_CKV + o), gain(8)))
        vv = col(OFF_CKV + 256 + o)
        dif_ref[pl.ds(kv, tm, stride=4), :] = kk
        dif_ref[pl.ds(2 + kv, tm, stride=4), :] = vv
        kcd_ref[kv] = kk.astype(kcd_ref.dtype)
        vcd_ref[kv] = vv.astype(vcd_ref.dtype)
    m = col(OFF_MISC)
    ikr = rope64(rms64(m, gain(6)))
    ik_ref[...] = ikr[:, 0:64]
    ikd_ref[...] = jnp.where(lo, ikr, pltpu.roll(ikr, 64, 1)).astype(ikd_ref.dtype)
    sig = jax.nn.sigmoid(m)
    for kv in range(A_KV):
        gat_ref[kv] = pltpu.roll(sig, 128 - MISC_GATE - 12 * kv, 1)
    iw_ref[...] = pltpu.roll(m, 128 - MISC_IW, 1) * ((IDX_DIM ** -0.5) * (IDX_HEADS ** -0.5))


def _post_project(proj, tabs, prm, wc, tm, qdt):
    B, T, _ = proj.shape
    emit_cmp = wc is not None
    nt = T // tm

    def row(c):
        return pl.BlockSpec((None, tm, c), lambda b, i: (b, i, 0))

    def heads(*lead):
        n = len(lead)
        return pl.BlockSpec((None,) + lead + (tm, 128), lambda b, i: (b,) + (0,) * n + (i, 0))

    tab = pl.BlockSpec((tm, 128), lambda b, i: (i, 0))
    in_specs = [row(N_PROJ), tab, tab, tab, tab, pl.BlockSpec((16, 128), lambda b, i: (0, 0))]
    args = [proj, *tabs, prm]
    if emit_cmp:
        in_specs.append(pl.BlockSpec((2, 2 * tm // CMP_STRIDE, tm), lambda b, i: (0, 0, 0)))
        args.append(wc)

    def sds(shape, dt):
        return jax.ShapeDtypeStruct(shape, dt)

    def slabs(n):
        return pl.BlockSpec((None, tm * n, 128), lambda b, i: (b, i, 0))

    out_shape = [
        sds((B, T * 8, 128), F32), sds((B, T * 4, 128), F32), sds((B, T * 2, 128), F32), sds((B, T, 64), F32),
        sds((B, T * 4, 128), F32),
        sds((B, A_HEADS, T, 128), qdt),
        sds((B, A_KV, T, 128), qdt), sds((B, A_KV, T, 128), qdt),
        sds((B, A_KV, T, 128), qdt), sds((B, A_KV, T, 128), qdt),
        sds((B, B_HEADS, T, 128), qdt), sds((B, T, 128), qdt), sds((B, T, 128), qdt),
        sds((B, IDX_HEADS, T, 128), qdt), sds((B, T, 128), qdt), sds((B, T, 128), F32),
        sds((B, C_KV, 4, T, 128), qdt), sds((B, C_KV, T, 128), qdt), sds((B, C_KV, T, 128), qdt),
        sds((B, A_KV, T, 128), F32),
    ]
    out_specs = [
        slabs(8), slabs(4), slabs(2), row(64), slabs(4),
        heads(A_HEADS), heads(A_KV), heads(A_KV), heads(A_KV), heads(A_KV),
        heads(B_HEADS), row(128), row(128), heads(IDX_HEADS), row(128), row(128),
        heads(C_KV, 4), heads(C_KV), heads(C_KV), heads(A_KV),
    ]
    if emit_cmp:
        nc = T // CMP_STRIDE
        out_shape += [sds((B, 4, nc, 128), F32), sds((B, 4, nc, 128), F32)]
        spec = pl.BlockSpec((None, 4, tm // CMP_STRIDE, 128), lambda b, i: (b, 0, i, 0))
        out_specs += [spec, spec]
    return pl.pallas_call(
        functools.partial(_post_kernel, emit_cmp=emit_cmp, tm=tm),
        grid=(B, nt),
        in_specs=in_specs,
        out_specs=out_specs,
        out_shape=out_shape,
        compiler_params=_cparams(("parallel", "parallel")),
        name="post_project",
    )(*args)


def _bias(mask):
    return jnp.where(mask, 0.0, MASKED)


def _softmax_parts(s, bias):
    s = s + bias
    m = jnp.maximum(jnp.max(s, axis=-1, keepdims=True), NEG)
    e = jnp.exp2(s - m)
    return e, 1.0 / jnp.maximum(jnp.sum(e, axis=-1, keepdims=True), 1e-30)


def _masked_softmax(s, bias):
    e, r = _softmax_parts(s, bias)
    return e * r


def _online_step(s, bias, v, m, l, acc):
    if bias is not None:
        s = s + bias
    m_new = jnp.maximum(m, jnp.max(s, axis=-1, keepdims=True))
    alpha = jnp.exp2(m - m_new)
    e = jnp.exp2(s - m_new)
    l_new = alpha * l + jnp.sum(e, axis=-1, keepdims=True)
    rows = acc.shape[0]
    pv = _dot(e.reshape(rows, e.shape[-1]).astype(BF16), v)
    return m_new, l_new, alpha.reshape(rows, 1) * acc + pv


def _flash_init(heads, tq):
    return (jnp.full((heads, tq, 1), NEG, F32), jnp.zeros((heads, tq, 1), F32), jnp.zeros((heads * tq, 128), F32))


def _flash_tile(q, k, v, bias, carry):
    m, l, acc = carry
    s = _dot_nt(q, k).reshape(m.shape[0], m.shape[1], k.shape[0])
    return _online_step(s, bias, v, m, l, acc)


def _flash_finish(carry):
    _, l, acc = carry
    return acc / jnp.maximum(l.reshape(acc.shape[0], 1), 1e-30)


def _online_single(s, bias, v_row, m, l, acc):
    if bias is not None:
        s = s + bias
    m_new = jnp.maximum(m, s)
    alpha = jnp.exp2(m - m_new)
    e = jnp.exp2(s - m_new)
    return m_new, alpha * l + e, alpha * acc + e.astype(BF16).astype(F32) * v_row


def _rowdot(q, k_row):
    return jnp.sum(q.astype(BF16).astype(F32) * k_row.astype(BF16).astype(F32), axis=-1, keepdims=True)


def _sortable(x):
    b = lax.bitcast_convert_type(x + 0.0, I32)
    return jnp.where(b < 0, b ^ jnp.int32(0x7FFFFFFF), b)


def _lambda_of(lp, lam_init):
    a = jnp.sum(lp[0:1] * lp[1:2], axis=-1, keepdims=True)
    b = jnp.sum(lp[2:3] * lp[3:4], axis=-1, keepdims=True)
    return jnp.exp(a) - jnp.exp(b) + lam_init


def _nsa_prompt_kernel(q_ref, pak_ref, pbk_ref, pav_ref, pbv_ref, cc_ref, sc_ref,
                       ks_ref, vs_ref, kw_ref, vw_ref, g_ref, o_ref, kc_ref, vc_ref, *, T):
    TQ = TQ_NSA
    qi = pl.program_id(2)
    nc = T // CMP_STRIDE
    ns = T // SEL_BLOCK
    R = A_G * TQ
    band = WINDOW + TQ

    @pl.when(qi == 0)
    def _():
        kraw = pak_ref[...] + pltpu.roll(pbk_ref[...], nc - 1, 0)
        kc_ref[...] = (kraw * cc_ref[...] + pltpu.roll(kraw, 64, 1) * sc_ref[...]).astype(BF16)
        vc_ref[...] = (pav_ref[...] + pltpu.roll(pbv_ref[...], nc - 1, 0)).astype(BF16)

    q = q_ref[...].reshape(R, 128)
    t0 = qi * TQ
    qp3 = t0 + lax.broadcasted_iota(I32, (1, TQ, 1), 1)

    s_c = _dot_nt(q, kc_ref[...]).reshape(A_G, TQ, nc)
    cend = lax.broadcasted_iota(I32, (1, TQ, nc), 2) * CMP_STRIDE + (CMP_LEN - 1)
    e_c, r_c = _softmax_parts(s_c, _bias(cend <= qp3))
    o_c = _dot(e_c.reshape(R, nc).astype(BF16), vc_ref[...]) * r_c.reshape(R, 1)

    sj = lax.broadcasted_iota(I32, (ns, nc), 0) * SEL_BLOCK
    ci = lax.broadcasted_iota(I32, (ns, nc), 1) * CMP_STRIDE
    overlap_t = ((ci < sj + SEL_BLOCK) & (ci + CMP_LEN > sj)).astype(F32)
    imp = _dot_nt(overlap_t, jnp.sum(e_c * r_c, axis=0), HI)
    jidx = lax.broadcasted_iota(I32, (ns, TQ), 0)
    jq = (t0 + lax.broadcasted_iota(I32, (1, TQ), 1)) >> SEL_SHIFT
    forced = (jidx == 0) | (jidx == jq) | (jidx == jq - 1)
    imp = jnp.where(forced, imp + FORCE_BONUS, imp)
    imp = jnp.where(jidx > jq, NEG, imp)
    ng = ns // 8
    sub = lax.broadcasted_iota(I32, (8, TQ), 0)
    imp_g = [imp[8 * g:8 * g + 8, :] for g in range(ng)]
    rank_g = [jnp.zeros((8, TQ), F32) for _ in range(ng)]
    for j in range(ns):
        rj = jnp.broadcast_to(imp[j:j + 1, :], (8, TQ))
        for g in range(ng):
            if g < j // 8:
                ahead = rj > imp_g[g]
            elif g > j // 8:
                ahead = rj >= imp_g[g]
            else:
                ahead = (rj > imp_g[g]) | ((rj == imp_g[g]) & (sub > j % 8))
            rank_g[g] = rank_g[g] + jnp.where(ahead, 1.0, 0.0)
    rank = jnp.concatenate(rank_g, axis=0)
    selb = jnp.where(rank < min(SEL_TOPN, ns), 1.0, 0.0).T.astype(BF16)

    erow = lax.broadcasted_iota(I32, (ns, TK), 0)
    ecol = lax.broadcasted_iota(I32, (ns, TK), 1)
    tcol = lax.broadcasted_iota(I32, (1, TQ, TK), 2)

    def sel_step(kt, carry):
        base = pl.multiple_of(kt * TK, TK)
        expand = (erow == ((ecol + base) >> SEL_SHIFT)).astype(BF16)
        chosen = _dot(selb, expand).reshape(1, TQ, TK) > 0.5
        bias = _bias(chosen & (tcol + base <= qp3))
        return _flash_tile(q, ks_ref[pl.ds(base, TK), :], vs_ref[pl.ds(base, TK), :], bias, carry)

    nkt = (t0 + TQ + TK - 1) // TK
    o_s = _flash_finish(lax.fori_loop(0, nkt, sel_step, _flash_init(A_G, TQ)))

    start = pl.multiple_of(jnp.maximum(t0 - WINDOW, 0), TQ)
    kwin = kw_ref[pl.ds(start, band), :]
    vwin = vw_ref[pl.ds(start, band), :]
    dist = qp3 - (start + lax.broadcasted_iota(I32, (1, TQ, band), 2))
    s_w = _dot_nt(q, kwin).reshape(A_G, TQ, band)
    e_w, r_w = _softmax_parts(s_w, _bias((dist >= 0) & (dist <= WINDOW)))
    o_w = _dot(e_w.reshape(R, band).astype(BF16), vwin) * r_w.reshape(R, 1)

    g = g_ref[...]
    for h in range(A_G):
        r = slice(h * TQ, (h + 1) * TQ)
        o = g[:, 3 * h:3 * h + 1] * o_c[r] + g[:, 3 * h + 1:3 * h + 2] * o_s[r] + g[:, 3 * h + 2:3 * h + 3] * o_w[r]
        o_ref[:, h * 128:(h + 1) * 128] = o.astype(o_ref.dtype)


def _nsa_prompt(qa, pa, pb, cc, sc, ksel, vsel, kw, vw, gat):
    B, _, T, _ = qa.shape
    TQ = TQ_NSA
    nc = T // CMP_STRIDE
    part_k = pl.BlockSpec((None, None, nc, 128), lambda b, kv, i: (b, kv, 0, 0))
    part_v = pl.BlockSpec((None, None, nc, 128), lambda b, kv, i: (b, 2 + kv, 0, 0))
    tabc = pl.BlockSpec((nc, 128), lambda b, kv, i: (0, 0))
    full = pl.BlockSpec((None, None, T, 128), lambda b, kv, i: (b, kv, 0, 0))
    return pl.pallas_call(
        functools.partial(_nsa_prompt_kernel, T=T),
        grid=(B, A_KV, T // TQ),
        in_specs=[
            pl.BlockSpec((None, A_G, TQ, 128), lambda b, kv, i: (b, kv, i, 0)),
            part_k, part_k, part_v, part_v, tabc, tabc, full, full, full, full,
            pl.BlockSpec((None, None, TQ, 128), lambda b, kv, i: (b, kv, i, 0)),
        ],
        out_specs=pl.BlockSpec((None, TQ, A_G * 128), lambda b, kv, i: (b, i, kv)),
        out_shape=jax.ShapeDtypeStruct((B, T, A_HEADS * 128), BF16),
        scratch_shapes=[pltpu.VMEM((nc, 128), BF16), pltpu.VMEM((nc, 128), BF16)],
        compiler_params=_cparams(("parallel", "parallel", "arbitrary")),
        name="nsa_prompt",
    )(qa, pa, pb, pa, pb, cc, sc, ksel, vsel, kw, vw, gat)


def _kth_threshold(count_ge, shape, k):
    def step(it, t):
        cand = t + jnp.left_shift(jnp.int32(1), 31 - it)
        return jnp.where(count_ge(cand) >= k, cand, t)

    return lax.fori_loop(0, 32, step, jnp.full(shape, INT_MIN, I32))


def _dsa_prompt_kernel(q_ref, iq_ref, iw_ref, ik_ref, k_ref, v_ref, o_ref, key_ref, *, T):
    TQ = TQ_DSA
    qi = pl.program_id(1)
    t0 = qi * TQ
    nkt = (t0 + TQ + TK - 1) // TK
    k_top = min(IDX_TOPK, T // 4)
    qp_l = t0 + lax.broadcasted_iota(I32, (1, TQ), 1)
    trow = lax.broadcasted_iota(I32, (TK, TQ), 0)
    iq = iq_ref[...].reshape(IDX_HEADS * TQ, 128)
    iw_t = iw_ref[...].T

    def score_step(kt, _):
        base = pl.multiple_of(kt * TK, TK)
        s = _dot_nt(ik_ref[pl.ds(base, TK), :], iq)
        sc = jnp.zeros((TK, TQ), F32)
        for h in range(IDX_HEADS):
            sc = sc + iw_t[h:h + 1, :] * jnp.maximum(s[:, h * TQ:(h + 1) * TQ], 0.0)
        sc = jnp.where(trow + base <= qp_l, sc, NEG)
        key_ref[pl.ds(base, TK), :] = _sortable(sc)
        return 0

    lax.fori_loop(0, nkt, score_step, 0)

    def count_ge(cand):
        def cstep(kt, c):
            base = pl.multiple_of(kt * TK, TK)
            hit = jnp.where(key_ref[pl.ds(base, TK), :] >= cand, 1.0, 0.0)
            return c + jnp.sum(hit.reshape(TK // 64, 8, 8, TQ), axis=0)

        c = lax.fori_loop(0, nkt, cstep, jnp.zeros((8, 8, TQ), F32))
        return jnp.sum(jnp.sum(c, axis=0), axis=0, keepdims=True)

    thr = _kth_threshold(count_ge, (1, TQ), float(k_top))
    q = q_ref[...].reshape(B_HEADS * TQ, 128)

    def attend(kt, keep, flash):
        base = pl.multiple_of(kt * TK, TK)
        bias = _bias(keep & (trow + base <= qp_l)).T.reshape(1, TQ, TK)
        return _flash_tile(q, k_ref[pl.ds(base, TK), :], v_ref[pl.ds(base, TK), :], bias, flash)

    def run_plain(_):
        def step(kt, flash):
            base = pl.multiple_of(kt * TK, TK)
            return attend(kt, key_ref[pl.ds(base, TK), :] >= thr, flash)

        return _flash_finish(lax.fori_loop(0, nkt, step, _flash_init(B_HEADS, TQ)))

    def run_ties(_):
        need = float(k_top) - count_ge(thr + 1)
        tri = (lax.broadcasted_iota(I32, (TK, TK), 1) < lax.broadcasted_iota(I32, (TK, TK), 0)).astype(BF16)

        def step(kt, carry):
            flash, run = carry
            base = pl.multiple_of(kt * TK, TK)
            keys = key_ref[pl.ds(base, TK), :]
            tie = keys == thr
            tie_f = jnp.where(tie, 1.0, 0.0)
            before = _dot(tri, tie_f.astype(BF16)) + run
            flash = attend(kt, (keys > thr) | (tie & (before < need)), flash)
            return flash, run + jnp.sum(tie_f, axis=0, keepdims=True)

        init = (_flash_init(B_HEADS, TQ), jnp.zeros((1, TQ), F32))
        return _flash_finish(lax.fori_loop(0, nkt, step, init)[0])

    any_tie = jnp.max(count_ge(thr)) > float(k_top)
    o = lax.cond(any_tie, run_ties, run_plain, 0)
    for h in range(B_HEADS):
        o_ref[:, h * 128:(h + 1) * 128] = o[h * TQ:(h + 1) * TQ].astype(o_ref.dtype)


def _dsa_prompt(qb, iq, iw, ikd, kb, vb):
    B, _, T, _ = qb.shape
    TQ = TQ_DSA
    full = pl.BlockSpec((None, T, 128), lambda b, i: (b, 0, 0))
    return pl.pallas_call(
        functools.partial(_dsa_prompt_kernel, T=T),
        grid=(B, T // TQ),
        in_specs=[
            pl.BlockSpec((None, B_HEADS, TQ, 128), lambda b, i: (b, 0, i, 0)),
            pl.BlockSpec((None, IDX_HEADS, TQ, 128), lambda b, i: (b, 0, i, 0)),
            pl.BlockSpec((None, TQ, 128), lambda b, i: (b, i, 0)),
            full, full, full,
        ],
        out_specs=pl.BlockSpec((None, TQ, B_HEADS * 128), lambda b, i: (b, i, 0)),
        out_shape=jax.ShapeDtypeStruct((B, T, B_HEADS * 128), BF16),
        scratch_shapes=[pltpu.VMEM((T, TQ), I32)],
        compiler_params=_cparams(("parallel", "arbitrary")),
        name="dsa_prompt",
    )(qb, iq, iw, ikd, kb, vb)


def _diff_finish(o, lam, subln, lam_init, rows):
    outs = []
    for g in range(2):
        a0 = o[(2 * g) * rows:(2 * g + 1) * rows]
        a1 = o[(2 * g + 1) * rows:(2 * g + 2) * rows]
        d = a0 - lam * a1
        d = d * lax.rsqrt(jnp.mean(d * d, axis=-1, keepdims=True) + EPS) * subln
        outs.append(d * (1.0 - lam_init))
    return outs


def _diff_prompt_kernel(q_ref, k_ref, v_ref, lp_ref, sub_ref, o_ref, *, lam_init):
    TQ = TQ_DIFF
    qi = pl.program_id(2)
    t0 = qi * TQ
    nkt = (t0 + TQ + TK - 1) // TK
    R = 4 * TQ
    qp = t0 + lax.broadcasted_iota(I32, (1, TQ, 1), 1)
    tcol = lax.broadcasted_iota(I32, (1, TQ, TK), 2)
    q = q_ref[...].reshape(R, 128)

    def step(kt, carry, causal):
        base = pl.multiple_of(kt * TK, TK)
        bias = _bias(tcol + base <= qp) if causal else None
        return _flash_tile(q, k_ref[pl.ds(base, TK), :], v_ref[pl.ds(base, TK), :], bias, carry)

    carry = lax.fori_loop(0, nkt - 1, functools.partial(step, causal=False), _flash_init(4, TQ))
    o = _flash_finish(step(nkt - 1, carry, True))
    lam = _lambda_of(lp_ref[...], lam_init)
    outs = _diff_finish(o, lam, sub_ref[...], lam_init, TQ)
    for g in range(2):
        o_ref[:, g * 128:(g + 1) * 128] = outs[g].astype(o_ref.dtype)


def _diff_prompt(qc, kcd, vcd, lp, subln, lam_init):
    B, _, _, T, _ = qc.shape
    TQ = TQ_DIFF
    full = pl.BlockSpec((None, None, T, 128), lambda b, kv, i: (b, kv, 0, 0))
    return pl.pallas_call(
        functools.partial(_diff_prompt_kernel, lam_init=lam_init),
        grid=(B, C_KV, T // TQ),
        in_specs=[
            pl.BlockSpec((None, None, 4, TQ, 128), lambda b, kv, i: (b, kv, 0, i, 0)),
            full, full,
            pl.BlockSpec((4, C_HALF), lambda b, kv, i: (0, 0)),
            pl.BlockSpec((1, 128), lambda b, kv, i: (0, 0)),
        ],
        out_specs=pl.BlockSpec((None, TQ, 256), lambda b, kv, i: (b, i, kv)),
        out_shape=jax.ShapeDtypeStruct((B, T, C_HEADS * 128), BF16),
        compiler_params=_cparams(("parallel", "parallel", "arbitrary")),
        name="diff_prompt",
    )(qc, kcd, vcd, lp, subln)


def _page_specs(block, layer, n_lead_zero):
    specs = []
    for u in range(PAGES_PER_STEP):
        def imap(b, pc, pt, u=u):
            return (layer, pt[b, pc * PAGES_PER_STEP + u]) + (0,) * n_lead_zero
        specs.append(pl.BlockSpec(block, imap))
    return specs


def _nsa_cmp_decode_kernel(pt_ref, *refs, past):
    pages = refs[:PAGES_PER_STEP]
    q_ref, new_ref, wc_ref, cc_ref, sc_ref, oc_ref, sel_ref, a_ref, b_ref = refs[PAGES_PER_STEP:]
    pc = pl.program_id(1)
    nc = past // CMP_STRIDE
    ns = past // SEL_BLOCK + 1
    nsp = ((ns + 127) // 128) * 128
    cpp = PAGE_SIZE // CMP_STRIDE

    def slab(ref, s):
        return ref[pl.ds(s, PAGE_SIZE, stride=8), :]

    for u in range(PAGES_PER_STEP):
        r0 = pl.multiple_of((pc * PAGES_PER_STEP + u) * cpp, cpp)
        for c in range(2):
            x = jnp.concatenate([slab(pages[u], 2 * c), slab(pages[u], 2 * c + 1)], axis=1)
            part = _dot3(wc_ref[c], x)
            a_ref[pl.ds(r0, cpp), 256 * c:256 * (c + 1)] = part[0:cpp]
            b_ref[pl.ds(r0, cpp), 256 * c:256 * (c + 1)] = part[cpp:2 * cpp]

    @pl.when(pc == pl.num_programs(1) - 1)
    def _():
        rowi = lax.broadcasted_iota(I32, (nc, 128), 0)
        cend = lax.broadcasted_iota(I32, (A_G, nc), 1) * CMP_STRIDE + (CMP_LEN - 1)
        ci = lax.broadcasted_iota(I32, (nc, nsp), 0) * CMP_STRIDE
        sj = lax.broadcasted_iota(I32, (nc, nsp), 1) * SEL_BLOCK
        overlap = ((ci < sj + SEL_BLOCK) & (ci + CMP_LEN > sj)).astype(F32)
        jrow = lax.broadcasted_iota(I32, (1, nsp), 1)
        jq = past // SEL_BLOCK
        ii = lax.broadcasted_iota(I32, (nsp, nsp), 0)
        jj = lax.broadcasted_iota(I32, (nsp, nsp), 1)
        rr = lax.broadcasted_iota(I32, (SEL_TOPN, nsp), 0)
        jr = lax.broadcasted_iota(I32, (SEL_TOPN, nsp), 1).astype(F32)
        new = new_ref[...]
        q = q_ref[...].astype(BF16)
        for kv in range(A_KV):
            ko, vo = kv * 128, 256 + kv * 128
            bk = jnp.where(rowi == nc - 1, wc_ref[0][cpp:cpp + 1, 0:1] * new[:, ko:ko + 128],
                           pltpu.roll(b_ref[:, ko:ko + 128], nc - 1, 0))
            bv = jnp.where(rowi == nc - 1, wc_ref[1][cpp:cpp + 1, 0:1] * new[:, vo:vo + 128],
                           pltpu.roll(b_ref[:, vo:vo + 128], nc - 1, 0))
            kraw = a_ref[:, ko:ko + 128] + bk
            kc = (kraw * cc_ref[...] + pltpu.roll(kraw, 64, 1) * sc_ref[...]).astype(BF16)
            vc = (a_ref[:, vo:vo + 128] + bv).astype(BF16)
            s_c = _dot_nt(q[kv * A_G:(kv + 1) * A_G], kc)
            p_c = _masked_softmax(s_c, _bias(cend <= past))
            oc_ref[kv * A_G:(kv + 1) * A_G, :] = _dot(p_c.astype(BF16), vc)
            imp = _dot(jnp.sum(p_c, axis=0, keepdims=True), overlap, HI)
            forced = (jrow == 0) | (jrow == jq) | (jrow == jq - 1)
            imp = jnp.where(forced, imp + FORCE_BONUS, imp)
            imp = jnp.where(jrow > jq, NEG, imp)
            imp_col = jnp.sum(jnp.where(ii == jj, jnp.broadcast_to(imp, (nsp, nsp)), 0.0), axis=1, keepdims=True)
            beats = (imp_col > imp) | ((imp_col == imp) & (ii < jj))
            rank = jnp.sum(jnp.where(beats, 1.0, 0.0), axis=0, keepdims=True)
            pick = jnp.sum(jnp.where(rank == rr.astype(F32), jr, 0.0), axis=1, keepdims=True)
            sel_ref[kv] = pick.astype(I32)


def _nsa_cmp_decode(page_table, cache, layer, q, new_rows, wc, cc, sc, past):
    DB = q.shape[0]
    npc = page_table.shape[1] // PAGES_PER_STEP
    nc = past // CMP_STRIDE
    block = (None, None, PAGE_SIZE * 8, 128)
    const2 = lambda b, pc, pt: (0, 0)
    return pl.pallas_call(
        functools.partial(_nsa_cmp_decode_kernel, past=past),
        grid_spec=pltpu.PrefetchScalarGridSpec(
            num_scalar_prefetch=1,
            grid=(DB, npc),
            in_specs=_page_specs(block, layer, 2) + [
                pl.BlockSpec((None, A_HEADS, 128), lambda b, pc, pt: (b, 0, 0)),
                pl.BlockSpec((None, 1, 1024), lambda b, pc, pt: (b, 0, 0)),
                pl.BlockSpec((2, 2 * PAGE_SIZE // CMP_STRIDE, PAGE_SIZE), lambda b, pc, pt: (0, 0, 0)),
                pl.BlockSpec((nc, 128), const2),
                pl.BlockSpec((nc, 128), const2),
            ],
            out_specs=[
                pl.BlockSpec((None, A_HEADS, 128), lambda b, pc, pt: (b, 0, 0)),
                pl.BlockSpec((None, A_KV, SEL_TOPN, 1), lambda b, pc, pt: (b, 0, 0, 0)),
            ],
            scratch_shapes=[pltpu.VMEM((nc, 512), F32), pltpu.VMEM((nc, 512), F32)],
        ),
        out_shape=[jax.ShapeDtypeStruct((DB, A_HEADS, 128), F32),
                   jax.ShapeDtypeStruct((DB, A_KV, SEL_TOPN, 1), I32)],
        compiler_params=_cparams(("parallel", "arbitrary")),
        name="nsa_cmp_decode",
    )(page_table, *([cache] * PAGES_PER_STEP), q, new_rows, wc, cc, sc)


def _nsa_sel_decode_kernel(pt_ref, sel_ref, *refs, past, lw):
    blks = refs[:A_KV * SEL_PER_STEP]
    win_ref, q_ref, new_ref, neww_ref, oc_ref, g_ref, o_ref, m_ref, l_ref, acc_ref = refs[A_KV * SEL_PER_STEP:]
    b, s = pl.program_id(0), pl.program_id(1)
    ns = past // SEL_BLOCK + 1
    new = new_ref[...]
    neww = neww_ref[...]
    blk_of_lane = lax.broadcasted_iota(I32, (1, SEL_PER_STEP * SEL_BLOCK), 1) >> SEL_SHIFT

    for kv in range(A_KV):
        q = q_ref[kv * A_G:(kv + 1) * A_G, :].astype(BF16)
        mine = blks[kv * SEL_PER_STEP:(kv + 1) * SEL_PER_STEP]

        @pl.when(s == 0)
        def _():
            m_ref[kv] = _rowdot(q, new[:, 512 + kv * 128:640 + kv * 128])
            l_ref[kv] = jnp.ones((A_G, 1), F32)
            v_new = new[:, 768 + kv * 128:896 + kv * 128].astype(BF16).astype(F32)
            acc_ref[kv] = jnp.broadcast_to(v_new, (A_G, 128))

        k = jnp.concatenate([r[pl.ds(4 + kv, SEL_BLOCK, stride=8), :] for r in mine], axis=0).astype(BF16)
        v = jnp.concatenate([r[pl.ds(6 + kv, SEL_BLOCK, stride=8), :] for r in mine], axis=0).astype(BF16)
        bias = jnp.zeros((1, SEL_PER_STEP * SEL_BLOCK), F32)
        for u in range(SEL_PER_STEP):
            keep = sel_ref[b, kv, s * SEL_PER_STEP + u] != ns - 1
            bias = jnp.where(blk_of_lane == u, jnp.where(keep, 0.0, MASKED), bias)
        m, l, acc = _online_step(_dot_nt(q, k), bias, v, m_ref[kv], l_ref[kv], acc_ref[kv])
        m_ref[kv] = m
        l_ref[kv] = l
        acc_ref[kv] = acc

        @pl.when(s == pl.num_programs(1) - 1)
        def _():
            o_s = acc_ref[kv] / jnp.maximum(l_ref[kv], 1e-30)
            kw = win_ref[pl.ds(kv, lw, stride=4), :].astype(BF16)
            vw = win_ref[pl.ds(2 + kv, lw, stride=4), :].astype(BF16)
            vw_new = neww[:, 256 + kv * 128:384 + kv * 128].astype(BF16).astype(F32)
            s_w = _dot_nt(q, kw)
            s_n = _rowdot(q, neww[:, kv * 128:(kv + 1) * 128])
            mw = jnp.maximum(jnp.max(s_w, axis=-1, keepdims=True), s_n)
            e_w = jnp.exp2(s_w - mw)
            e_n = jnp.exp2(s_n - mw)
            den = jnp.sum(e_w, axis=-1, keepdims=True) + e_n
            o_w = (_dot(e_w.astype(BF16), vw) + e_n.astype(BF16).astype(F32) * vw_new) / den
            o_c = oc_ref[kv * A_G:(kv + 1) * A_G, :]
            g = g_ref[kv * A_G:(kv + 1) * A_G, :]
            o_ref[kv * A_G:(kv + 1) * A_G, :] = g[:, 0:1] * o_c + g[:, 1:2] * o_s + g[:, 2:3] * o_w


def _nsa_sel_decode(page_table, sel, cache, win, layer, q, new_rows, new_win, o_c, gates, past):
    DB = q.shape[0]
    n_pages = page_table.shape[1]
    lw = win.shape[2] // 4

    def blk_spec(kv, u):
        def imap(b, s, pt, sel):
            j = sel[b, kv, s * SEL_PER_STEP + u]
            return (layer, pt[b, jnp.minimum(j // 2, n_pages - 1)], j % 2, 0)
        return pl.BlockSpec((None, None, SEL_BLOCK * 8, 128), imap)

    per_b = lambda b, s, pt, sel: (b, 0, 0)
    n_blk = A_KV * SEL_PER_STEP
    return pl.pallas_call(
        functools.partial(_nsa_sel_decode_kernel, past=past, lw=lw),
        grid_spec=pltpu.PrefetchScalarGridSpec(
            num_scalar_prefetch=2,
            grid=(DB, SEL_TOPN // SEL_PER_STEP),
            in_specs=[blk_spec(kv, u) for kv in range(A_KV) for u in range(SEL_PER_STEP)] + [
                pl.BlockSpec((None, None, lw * 4, 128), lambda b, s, pt, sel: (layer, b, 0, 0)),
                pl.BlockSpec((None, A_HEADS, 128), per_b),
                pl.BlockSpec((None, 1, 1024), per_b),
                pl.BlockSpec((None, 1, 512), per_b),
                pl.BlockSpec((None, A_HEADS, 128), per_b),
                pl.BlockSpec((None, A_HEADS, 128), per_b),
            ],
            out_specs=pl.BlockSpec((None, A_HEADS, 128), per_b),
            scratch_shapes=[pltpu.VMEM((A_KV, A_G, 1), F32), pltpu.VMEM((A_KV, A_G, 1), F32),
                            pltpu.VMEM((A_KV, A_G, 128), F32)],
        ),
        out_shape=jax.ShapeDtypeStruct((DB, A_HEADS, 128), F32),
        compiler_params=_cparams(("parallel", "arbitrary")),
        name="nsa_sel_decode",
    )(page_table, sel, *([cache] * n_blk), win, q, new_rows, new_win, o_c, gates)


def _dsa_idx_decode_kernel(pt_ref, *refs, past):
    pages = refs[:PAGES_PER_STEP]
    iq_ref, iw_ref, ikn_ref, mask_ref, sc_ref = refs[PAGES_PER_STEP:]
    pc = pl.program_id(1)
    n_pages = past // PAGE_SIZE
    k_top = min(IDX_TOPK, (past + 1) // 4)
    iq = iq_ref[...].astype(BF16)
    iw = iw_ref[...]

    @pl.when(pc == 0)
    def _():
        sc_ref[...] = jnp.full(sc_ref.shape, NEG, F32)

    keys_t = jnp.concatenate([p[...] for p in pages], axis=1).astype(BF16)
    sc = jnp.sum(iw * jnp.maximum(_dot(iq, keys_t), 0.0), axis=0, keepdims=True)
    for u in range(PAGES_PER_STEP):
        sc_ref[pl.ds(pc * PAGES_PER_STEP + u, 1), :] = sc[:, u * PAGE_SIZE:(u + 1) * PAGE_SIZE]

    @pl.when(pc == pl.num_programs(1) - 1)
    def _():
        s_new = _rowdot(iq, ikn_ref[...])
        s_new = jnp.sum(iw * jnp.maximum(s_new, 0.0), axis=0, keepdims=True)
        lane0 = lax.broadcasted_iota(I32, (1, PAGE_SIZE), 1) == 0
        sc_ref[n_pages:n_pages + 1, :] = jnp.where(lane0, s_new, NEG)
        keys = _sortable(sc_ref[...])

        def count_ge(cand):
            c = jnp.sum(jnp.where(keys >= cand, 1.0, 0.0), axis=0, keepdims=True)
            return jnp.sum(c, axis=-1, keepdims=True)

        thr = _kth_threshold(count_ge, (1, 1), float(k_top))
        need = float(k_top) - count_ge(thr + 1)
        rows = keys.shape[0]
        tie = keys == thr
        tie_f = jnp.where(tie, 1.0, 0.0)
        upper = (lax.broadcasted_iota(I32, (PAGE_SIZE, PAGE_SIZE), 0)
                 < lax.broadcasted_iota(I32, (PAGE_SIZE, PAGE_SIZE), 1)).astype(BF16)
        lower = (lax.broadcasted_iota(I32, (rows, rows), 1) < lax.broadcasted_iota(I32, (rows, rows), 0)).astype(BF16)
        in_row = _dot(tie_f.astype(BF16), upper)
        row_tot = jnp.broadcast_to(jnp.sum(tie_f, axis=-1, keepdims=True), (rows, PAGE_SIZE))
        above = _dot(lower, row_tot.astype(BF16))
        keep = (keys > thr) | (tie & (in_row + above < need))
        mask_ref[...] = _bias(keep & (sc_ref[...] > 0.5 * NEG))


def _dsa_idx_decode(page_table, cache, layer, iq, iw, ik_new, past):
    DB = iq.shape[0]
    n_pages = page_table.shape[1]
    npc = n_pages // PAGES_PER_STEP
    rows = ((n_pages + 1 + 7) // 8) * 8
    per_b = lambda b, pc, pt: (b, 0, 0)
    return pl.pallas_call(
        functools.partial(_dsa_idx_decode_kernel, past=past),
        grid_spec=pltpu.PrefetchScalarGridSpec(
            num_scalar_prefetch=1,
            grid=(DB, npc),
            in_specs=_page_specs((None, None, IDX_DIM, PAGE_SIZE), layer, 2) + [
                pl.BlockSpec((None, IDX_HEADS, IDX_DIM), per_b),
                pl.BlockSpec((None, IDX_HEADS, 1), per_b),
                pl.BlockSpec((None, 1, IDX_DIM), per_b),
            ],
            out_specs=pl.BlockSpec((None, rows, PAGE_SIZE), per_b),
            scratch_shapes=[pltpu.VMEM((rows, PAGE_SIZE), F32)],
        ),
        out_shape=jax.ShapeDtypeStruct((DB, rows, PAGE_SIZE), F32),
        compiler_params=_cparams(("parallel", "arbitrary")),
        name="dsa_idx_decode",
    )(page_table, *([cache] * PAGES_PER_STEP), iq, iw, ik_new)


def _dsa_att_decode_kernel(pt_ref, *refs, past):
    pages = refs[:PAGES_PER_STEP]
    q_ref, new_ref, mask_ref, o_ref, m_ref, l_ref, acc_ref = refs[PAGES_PER_STEP:]
    pc = pl.program_id(1)
    n_pages = past // PAGE_SIZE
    q = q_ref[...].astype(BF16)

    @pl.when(pc == 0)
    def _():
        m_ref[...] = jnp.full(m_ref.shape, NEG, F32)
        l_ref[...] = jnp.zeros(l_ref.shape, F32)
        acc_ref[...] = jnp.zeros(acc_ref.shape, F32)

    k = jnp.concatenate([p[pl.ds(0, PAGE_SIZE, stride=2), :] for p in pages], axis=0).astype(BF16)
    v = jnp.concatenate([p[pl.ds(1, PAGE_SIZE, stride=2), :] for p in pages], axis=0).astype(BF16)
    r0 = pl.multiple_of(pc * PAGES_PER_STEP, PAGES_PER_STEP)
    mrows = mask_ref[pl.ds(r0, PAGES_PER_STEP), :]
    bias = jnp.concatenate([mrows[u:u + 1, :] for u in range(PAGES_PER_STEP)], axis=1)
    m, l, acc = _online_step(_dot_nt(q, k), bias, v, m_ref[...], l_ref[...], acc_ref[...])
    m_ref[...] = m
    l_ref[...] = l
    acc_ref[...] = acc

    @pl.when(pc == pl.num_programs(1) - 1)
    def _():
        new = new_ref[...]
        v_new = new[:, 128:256].astype(BF16).astype(F32)
        keep = mask_ref[n_pages:n_pages + 1, 0:1]
        _, l2, acc2 = _online_single(_rowdot(q, new[:, 0:128]), keep, v_new, m_ref[...], l_ref[...], acc_ref[...])
        o_ref[...] = acc2 / jnp.maximum(l2, 1e-30)


def _dsa_att_decode(page_table, cache, layer, q, new_rows, mask, past):
    DB = q.shape[0]
    npc = page_table.shape[1] // PAGES_PER_STEP
    rows = mask.shape[1]
    per_b = lambda b, pc, pt: (b, 0, 0)
    return pl.pallas_call(
        functools.partial(_dsa_att_decode_kernel, past=past),
        grid_spec=pltpu.PrefetchScalarGridSpec(
            num_scalar_prefetch=1,
            grid=(DB, npc),
            in_specs=_page_specs((None, None, PAGE_SIZE * 2, 128), layer, 2) + [
                pl.BlockSpec((None, B_HEADS, 128), per_b),
                pl.BlockSpec((None, 1, 256), per_b),
                pl.BlockSpec((None, rows, PAGE_SIZE), per_b),
            ],
            out_specs=pl.BlockSpec((None, B_HEADS, 128), per_b),
            scratch_shapes=[pltpu.VMEM((B_HEADS, 1), F32), pltpu.VMEM((B_HEADS, 1), F32),
                            pltpu.VMEM((B_HEADS, 128), F32)],
        ),
        out_shape=jax.ShapeDtypeStruct((DB, B_HEADS, 128), F32),
        compiler_params=_cparams(("parallel", "arbitrary")),
        name="dsa_att_decode",
    )(page_table, *([cache] * PAGES_PER_STEP), q, new_rows, mask)


def _diff_decode_kernel(pt_ref, *refs, lam_init):
    pages = refs[:PAGES_PER_STEP]
    q_ref, new_ref, lp_ref, sub_ref, o_ref, m_ref, l_ref, acc_ref = refs[PAGES_PER_STEP:]
    pc = pl.program_id(1)

    @pl.when(pc == 0)
    def _():
        m_ref[...] = jnp.full(m_ref.shape, NEG, F32)
        l_ref[...] = jnp.zeros(l_ref.shape, F32)
        acc_ref[...] = jnp.zeros(acc_ref.shape, F32)

    for kv in range(C_KV):
        q = q_ref[kv].astype(BF16)
        k = jnp.concatenate([p[pl.ds(kv, PAGE_SIZE, stride=4), :] for p in pages], axis=0).astype(BF16)
        v = jnp.concatenate([p[pl.ds(2 + kv, PAGE_SIZE, stride=4), :] for p in pages], axis=0).astype(BF16)
        sc = _dot_nt(q, k)
        m, l, acc = _online_step(sc, None, v, m_ref[kv], l_ref[kv], acc_ref[kv])
        m_ref[kv] = m
        l_ref[kv] = l
        acc_ref[kv] = acc

    @pl.when(pc == pl.num_programs(1) - 1)
    def _():
        new = new_ref[...]
        lam = _lambda_of(lp_ref[...], lam_init)
        for kv in range(C_KV):
            k_new = new[:, kv * 128:(kv + 1) * 128]
            v_new = new[:, 256 + kv * 128:384 + kv * 128].astype(BF16).astype(F32)
            _, l, acc = _online_single(_rowdot(q_ref[kv], k_new), None, v_new, m_ref[kv], l_ref[kv], acc_ref[kv])
            outs = _diff_finish(acc / jnp.maximum(l, 1e-30), lam, sub_ref[...], lam_init, 1)
            o_ref[2 * kv:2 * kv + 1, :] = outs[0]
            o_ref[2 * kv + 1:2 * kv + 2, :] = outs[1]


def _diff_decode(page_table, cache, layer, q, new_rows, lp, subln, lam_init):
    DB = q.shape[0]
    npc = page_table.shape[1] // PAGES_PER_STEP
    per_b = lambda b, pc, pt: (b, 0, 0)
    return pl.pallas_call(
        functools.partial(_diff_decode_kernel, lam_init=lam_init),
        grid_spec=pltpu.PrefetchScalarGridSpec(
            num_scalar_prefetch=1,
            grid=(DB, npc),
            in_specs=_page_specs((None, None, PAGE_SIZE * 4, 128), layer, 2) + [
                pl.BlockSpec((None, C_KV, 4, 128), lambda b, pc, pt: (b, 0, 0, 0)),
                pl.BlockSpec((None, 1, 512), per_b),
                pl.BlockSpec((4, C_HALF), lambda b, pc, pt: (0, 0)),
                pl.BlockSpec((1, 128), lambda b, pc, pt: (0, 0)),
            ],
            out_specs=pl.BlockSpec((None, C_HEADS, 128), per_b),
            scratch_shapes=[pltpu.VMEM((C_KV, 4, 1), F32), pltpu.VMEM((C_KV, 4, 1), F32),
                            pltpu.VMEM((C_KV, 4, 128), F32)],
        ),
        out_shape=jax.ShapeDtypeStruct((DB, C_HEADS, 128), F32),
        compiler_params=_cparams(("parallel", "arbitrary")),
        name="diff_decode",
    )(page_table, *([cache] * PAGES_PER_STEP), q, new_rows, lp, subln)


def _rope_tables(pos, d):
    half = d // 2
    inv = ROPE_THETA ** (-jnp.arange(half, dtype=F32) / half)
    ang = pos.astype(F32)[:, None] * inv[None, :]
    cos, sin = jnp.cos(ang), jnp.sin(ang)
    reps = 128 // d
    return jnp.tile(jnp.concatenate([cos, cos], axis=-1), (1, reps)), jnp.tile(jnp.concatenate([-sin, sin], axis=-1), (1, reps))


def _pack_params(nsa_qk_norm, dsa_qk_norm, dsa_idx_knorm, diff_qk_norm):
    rows = [nsa_qk_norm, dsa_qk_norm, jnp.tile(dsa_idx_knorm, 2)[None], jnp.tile(diff_qk_norm, (1, 2))]
    p = jnp.concatenate(rows, axis=0).astype(F32)
    return jnp.pad(p, ((0, 16 - p.shape[0]), (0, 0)))


def _compress_weights(cmp_w, rows):
    eye = jnp.eye(rows // CMP_STRIDE, dtype=F32)
    mats = []
    for c in range(2):
        halves = [jnp.kron(eye, cmp_w[c, h * CMP_STRIDE:(h + 1) * CMP_STRIDE][None, :]) for h in range(2)]
        mats.append(jnp.concatenate(halves, axis=0))
    return jnp.stack(mats, axis=0)


def kernel(x_prompt, x_sample, cache_nsa_kv, state_nsa_win, cache_dsa_kv, cache_dsa_idx, cache_diff_kv, page_table, attn_norm, w_in, nsa_qk_norm, nsa_cmp_w, dsa_qk_norm, dsa_idx_knorm, diff_qk_norm, diff_lambda, diff_subln, w_out, ffn_norm, w_gate_up, w_down):
    B, T, D = x_prompt.shape
    DB = x_sample.shape[0]
    depth = w_in.shape[0]
    n_pages = page_table.shape[1]
    past = n_pages * PAGE_SIZE
    M = B * T
    assert x_sample.shape[1] == 1 and T % TK == 0 and T >= WINDOW + TQ_NSA and n_pages % PAGES_PER_STEP == 0
    tm = min(1024, M)
    tm_ffn = min(1024, M)
    tm_post = 256

    w_in_p = _permute_cast_w_in(jnp.swapaxes(w_in, 1, 2))
    w_out_b = w_out.astype(BF16)
    w_gu_b = w_gate_up.astype(BF16)
    w_down_b = w_down.astype(BF16)

    pos_p = jnp.arange(T, dtype=I32)
    tabs_p = _rope_tables(pos_p, 128) + _rope_tables(pos_p, 64)
    pos_s = jnp.full((DB,), past, I32)
    tabs_s = _rope_tables(pos_s, 128) + _rope_tables(pos_s, 64)
    cend_p = jnp.arange(T // CMP_STRIDE, dtype=I32) * CMP_STRIDE + (CMP_LEN - 1)
    cc_p, sc_p = _rope_tables(cend_p, 128)
    cend_s = jnp.arange(past // CMP_STRIDE, dtype=I32) * CMP_STRIDE + (CMP_LEN - 1)
    cc_s, sc_s = _rope_tables(cend_s, 128)

    n_pool = cache_nsa_kv.shape[1]
    nsa_pages = cache_nsa_kv.reshape(depth, n_pool, PAGE_SIZE * 8, 128)
    dsa_pages = cache_dsa_kv.reshape(depth, n_pool, PAGE_SIZE * 2, 128)
    diff_pages = cache_diff_kv.reshape(depth, n_pool, PAGE_SIZE * 4, 128)
    win_rows = state_nsa_win.reshape(depth, DB, state_nsa_win.shape[2] * 4, 128)
    idx_pages = jnp.swapaxes(cache_dsa_idx, 2, 3)

    yp = x_prompt.reshape(M, D)
    ys = x_sample.reshape(DB, D)
    rows_p, rows_s = [], []
    for l in range(depth):
        lam_init = 0.8 - 0.6 * math.exp(-0.3 * l)
        prm = _pack_params(nsa_qk_norm[l], dsa_qk_norm[l], dsa_idx_knorm[l], diff_qk_norm[l])
        g_attn = attn_norm[l][None, :]
        g_ffn = ffn_norm[l][None, :]
        lp = diff_lambda[l].astype(F32)
        subln = diff_subln[l][None, :].astype(F32)

        proj = _norm_matmul(yp, g_attn, w_in_p, l, tm).reshape(B, T, N_PROJ)
        wc = _compress_weights(nsa_cmp_w[l], tm_post)
        (nsa, win, dsa, ik, dif, qa, ksel, vsel, kw, vw, qb, kb, vb, iq, ikd, iw, qc, kcd, vcd, gat,
         pa, pb) = _post_project(proj, tabs_p, prm, wc, tm_post, BF16)
        o_a = _nsa_prompt(qa, pa, pb, cc_p, sc_p, ksel, vsel, kw, vw, gat)
        o_b = _dsa_prompt(qb, iq, iw, ikd, kb, vb)
        o_c = _diff_prompt(qc, kcd, vcd, lp, subln, lam_init)
        yp = _mix_out_projection(o_a.reshape(M, -1), o_b.reshape(M, -1), o_c.reshape(M, -1), w_out_b, yp, l, tm)
        act = _norm_swiglu(yp, g_ffn, w_gu_b, l, tm_ffn)
        yp = _matmul_residual(act, w_down_b, yp, l, tm_ffn)
        w_keep = min(WINDOW, T)
        rows_p.append((nsa.reshape(B, T, 4, A_KV, 128), win[:, (T - w_keep) * 4:].reshape(B, w_keep, 2, A_KV, 128),
                       dsa.reshape(B, T, 2, 128), ik, dif.reshape(B, T, 2, C_KV, 128)))

        proj_s = _norm_matmul(ys, g_attn, w_in_p, l, DB).reshape(1, DB, N_PROJ)
        (nsa_s, win_s, dsa_s, ik_s, dif_s, qa_s, _, _, _, _, qb_s, _, _, iq_s, _, iw_s, qc_s, _, _,
         gat_s) = _post_project(proj_s, tabs_s, prm, None, DB, F32)
        nsa_new = nsa_s.reshape(DB, 1, 1024)
        win_new = win_s.reshape(DB, 1, 512)
        dsa_new = dsa_s.reshape(DB, 1, 256)
        ik_new = ik_s.reshape(DB, 1, IDX_DIM)
        dif_new = dif_s.reshape(DB, 1, 512)
        qa_d = jnp.transpose(qa_s[0], (1, 0, 2))
        qb_d = jnp.transpose(qb_s[0], (1, 0, 2))
        iq_d = jnp.transpose(iq_s[0], (1, 0, 2))
        iq_d = iq_d[:, :, :64] + iq_d[:, :, 64:]
        iw_d = iw_s[0, :, :IDX_HEADS, None]
        qc_d = jnp.transpose(qc_s[0], (2, 0, 1, 3))
        g_d = jnp.transpose(gat_s[0, :, :, :12].reshape(A_KV, DB, A_G, 3), (1, 0, 2, 3)).reshape(DB, A_HEADS, 3)
        g_d = jnp.pad(g_d, ((0, 0), (0, 0), (0, 125)))

        wc_s = _compress_weights(nsa_cmp_w[l], PAGE_SIZE)
        oc_d, sel = _nsa_cmp_decode(page_table, nsa_pages, l, qa_d, nsa_new, wc_s, cc_s, sc_s, past)
        oa_d = _nsa_sel_decode(page_table, sel.reshape(DB, A_KV, SEL_TOPN), nsa_pages, win_rows, l,
                               qa_d, nsa_new, win_new, oc_d, g_d, past)
        mask = _dsa_idx_decode(page_table, idx_pages, l, iq_d, iw_d, ik_new, past)
        ob_d = _dsa_att_decode(page_table, dsa_pages, l, qb_d, dsa_new, mask, past)
        od_d = _diff_decode(page_table, diff_pages, l, qc_d, dif_new, lp, subln, lam_init)
        ys = _mix_out_projection(oa_d.reshape(DB, 1024).astype(BF16), ob_d.reshape(DB, 512).astype(BF16),
                                 od_d.reshape(DB, 512).astype(BF16), w_out_b, ys, l, DB)
        act_s = _norm_swiglu(ys, g_ffn, w_gu_b, l, DB)
        ys = _matmul_residual(act_s, w_down_b, ys, l, DB)
        lw = state_nsa_win.shape[2]
        win_all = jnp.concatenate([state_nsa_win[l], win_new.reshape(DB, 1, 2, A_KV, 128)], axis=1)
        rows_s.append((nsa_new.reshape(DB, 1, 4, A_KV, 128), win_all[:, win_all.shape[1] - min(WINDOW, lw + 1):],
                       dsa_new.reshape(DB, 1, 2, 128), ik_new, dif_new.reshape(DB, 1, 2, C_KV, 128)))

    def stacked(rows, i):
        return jnp.stack([r[i] for r in rows], axis=0)

    return (yp.reshape(B, T, D), ys.reshape(DB, 1, D),
            stacked(rows_p, 0), stacked(rows_s, 0), stacked(rows_p, 1), stacked(rows_s, 1),
            stacked(rows_p, 2), stacked(rows_s, 2), stacked(rows_p, 3), stacked(rows_s, 3),
            stacked(rows_p, 4), stacked(rows_s, 4))
```

```python
import functools
import math

import jax
import jax.numpy as jnp
from jax import lax
from jax.experimental import pallas as pl
from jax.experimental.pallas import tpu as pltpu

F32 = jnp.float32
BF16 = jnp.bfloat16
I32 = jnp.int32
HI = lax.Precision.HIGHEST

D_MODEL = 2048
PAGE_SIZE = 128
D_HEAD = 128
A_HEADS = 8
A_KV = 2
A_G = A_HEADS // A_KV
B_HEADS = 4
C_HEADS = 4
C_KV = 2
C_HALF = 64
IDX_HEADS = 16
IDX_DIM = 64
IDX_TOPK = 256
CMP_LEN = 32
CMP_STRIDE = 16
SEL_BLOCK = 64
SEL_TOPN = 16
WINDOW = 512
FORCE_BONUS = 1.0e4
D_FF = 5632
ROPE_THETA = 10000.0
EPS = 1e-6
NEG = -1e30
INT_MIN = -2147483648

OFF_AQ = 0
OFF_AKV = 1024
OFF_BQ = 2560
OFF_BKV = 3072
OFF_BIQ = 3328
OFF_CQ = 4352
OFF_CKV = 4864
OFF_MISC = 5376
N_PROJ = 5632
MISC_GATE = 64
MISC_IW = 96

TQ_NSA = 256
TQ_DSA = 128
TQ_DIFF = 256
TK = 512
PAGES_PER_STEP = 16
SEL_SHIFT = 6
SEL_PER_STEP = 4
LOG2E = math.log2(math.e)
SCALE_D = D_HEAD ** -0.5 * LOG2E
SCALE_C = C_HALF ** -0.5 * LOG2E
MASKED = -2e30
VMEM_LIMIT = 56 * 1024 * 1024


def _cparams(sem):
    return pltpu.CompilerParams(dimension_semantics=sem, vmem_limit_bytes=VMEM_LIMIT)


def _dot(a, b, precision=None):
    return jnp.dot(a, b, preferred_element_type=F32, precision=precision)


def _dot3(a, b):
    a_hi = a.astype(BF16)
    b_hi = b.astype(BF16)
    a_lo = (a - a_hi.astype(F32)).astype(BF16)
    b_lo = (b - b_hi.astype(F32)).astype(BF16)
    return _dot(a_hi, b_hi) + (_dot(a_hi, b_lo) + _dot(a_lo, b_hi))


def _dot_nt(a, b, precision=None):
    return lax.dot_general(a, b, (((1,), (1,)), ((), ())), preferred_element_type=F32, precision=precision)


def _norm_mm_kernel(x_ref, g_ref, w_ref, o_ref, xn_ref):
    @pl.when(pl.program_id(1) == 0)
    def _():
        x = x_ref[...]
        ms = jnp.mean(x * x, axis=-1, keepdims=True)
        xn_ref[...] = (x * lax.rsqrt(ms + EPS) * g_ref[...]).astype(BF16)

    o_ref[...] = _dot_nt(xn_ref[...], w_ref[...])


def _norm_matmul(x, g, w_t, layer, tm, tn=512):
    M, K = x.shape
    N = w_t.shape[1]
    return pl.pallas_call(
        _norm_mm_kernel,
        grid=(M // tm, N // tn),
        in_specs=[
            pl.BlockSpec((tm, K), lambda i, j: (i, 0)),
            pl.BlockSpec((1, K), lambda i, j: (0, 0)),
            pl.BlockSpec((None, tn, K), lambda i, j: (layer, j, 0)),
        ],
        out_specs=pl.BlockSpec((tm, tn), lambda i, j: (i, j)),
        out_shape=jax.ShapeDtypeStruct((M, N), F32),
        scratch_shapes=[pltpu.VMEM((tm, K), BF16)],
        compiler_params=_cparams(("parallel", "arbitrary")),
        name="norm_matmul",
    )(x, g, w_t)


def _norm_swiglu_kernel(x_ref, g_ref, wg_ref, wu_ref, o_ref, xn_ref):
    @pl.when(pl.program_id(1) == 0)
    def _():
        x = x_ref[...]
        ms = jnp.mean(x * x, axis=-1, keepdims=True)
        xn_ref[...] = (x * lax.rsqrt(ms + EPS) * g_ref[...]).astype(BF16)

    xn = xn_ref[...]
    gate = _dot(xn, wg_ref[...])
    up = _dot(xn, wu_ref[...])
    o_ref[...] = (gate * jax.nn.sigmoid(gate) * up).astype(o_ref.dtype)


def _norm_swiglu(x, g, w, layer, tm, tn=512):
    M, K = x.shape
    nj = D_FF // tn
    return pl.pallas_call(
        _norm_swiglu_kernel,
        grid=(M // tm, nj),
        in_specs=[
            pl.BlockSpec((tm, K), lambda i, j: (i, 0)),
            pl.BlockSpec((1, K), lambda i, j: (0, 0)),
            pl.BlockSpec((None, K, tn), lambda i, j: (layer, 0, j)),
            pl.BlockSpec((None, K, tn), lambda i, j: (layer, 0, j + nj)),
        ],
        out_specs=pl.BlockSpec((tm, tn), lambda i, j: (i, j)),
        out_shape=jax.ShapeDtypeStruct((M, D_FF), BF16),
        scratch_shapes=[pltpu.VMEM((tm, K), BF16)],
        compiler_params=_cparams(("parallel", "arbitrary")),
        name="norm_swiglu",
    )(x, g, w, w)


def _mm_res_kernel(a_ref, w_ref, r_ref, o_ref):
    o_ref[...] = r_ref[...] + _dot(a_ref[...], w_ref[...])


def _matmul_residual(a, w, res, layer, tm, tn=512):
    M, K = a.shape
    N = w.shape[2]
    return pl.pallas_call(
        _mm_res_kernel,
        grid=(M // tm, N // tn),
        in_specs=[
            pl.BlockSpec((tm, K), lambda i, j: (i, 0)),
            pl.BlockSpec((None, K, tn), lambda i, j: (layer, 0, j)),
            pl.BlockSpec((tm, tn), lambda i, j: (i, j)),
        ],
        out_specs=pl.BlockSpec((tm, tn), lambda i, j: (i, j)),
        out_shape=jax.ShapeDtypeStruct((M, N), F32),
        compiler_params=_cparams(("parallel", "arbitrary")),
        name="matmul_residual",
    )(a, w, res)


def _mix_out_kernel(a_ref, b_ref, c_ref, w_ref, r_ref, o_ref):
    ka, kb = a_ref.shape[1], b_ref.shape[1]
    acc = _dot(a_ref[...], w_ref[0:ka, :])
    acc = acc + _dot(b_ref[...], w_ref[ka:ka + kb, :])
    acc = acc + _dot(c_ref[...], w_ref[ka + kb:, :])
    o_ref[...] = r_ref[...] + acc


def _mix_out_projection(o_a, o_b, o_c, w, res, layer, tm, tn=512):
    M, N = res.shape
    K = w.shape[1]

    def rows(x):
        return pl.BlockSpec((tm, x.shape[1]), lambda i, j: (i, 0))

    return pl.pallas_call(
        _mix_out_kernel,
        grid=(M // tm, N // tn),
        in_specs=[
            rows(o_a), rows(o_b), rows(o_c),
            pl.BlockSpec((None, K, tn), lambda i, j: (layer, 0, j)),
            pl.BlockSpec((tm, tn), lambda i, j: (i, j)),
        ],
        out_specs=pl.BlockSpec((tm, tn), lambda i, j: (i, j)),
        out_shape=jax.ShapeDtypeStruct((M, N), F32),
        compiler_params=_cparams(("parallel", "arbitrary")),
        name="mix_out_projection",
    )(o_a, o_b, o_c, w, res)


_W_IN_PIECES = ((OFF_AQ, 0, 2560), (OFF_BQ, 2584, 1792), (OFF_CQ, 4456, 1024), (OFF_MISC, 4392, 64),
                (OFF_MISC + MISC_GATE, 2560, 24), (OFF_MISC + MISC_IW, 4376, 16))
_W_IN_COLS = 5480


def _permute_cast_kernel(x_ref, o_ref):
    o_ref[OFF_MISC + MISC_GATE:, :] = jnp.zeros((N_PROJ - OFF_MISC - MISC_GATE, o_ref.shape[1]), BF16)
    for dst, src, width in _W_IN_PIECES:
        o_ref[dst:dst + width, :] = x_ref[src:src + width, :].astype(BF16)


def _permute_cast_w_in(w_in_t, tk=512):
    depth, _, K = w_in_t.shape
    return pl.pallas_call(
        _permute_cast_kernel,
        grid=(depth, K // tk),
        in_specs=[pl.BlockSpec((None, _W_IN_COLS, tk), lambda l, i: (l, 0, i))],
        out_specs=pl.BlockSpec((None, N_PROJ, tk), lambda l, i: (l, 0, i)),
        out_shape=jax.ShapeDtypeStruct((depth, N_PROJ, K), BF16),
        compiler_params=_cparams(("parallel", "parallel")),
        name="permute_cast_w_in",
    )(w_in_t)


def _post_kernel(*refs, emit_cmp, tm, n_carried):
    if emit_cmp:
        x_ref, c1_ref, s1_ref, c2_ref, s2_ref, prm_ref, wc_ref = refs[:7]
        outs = refs[7 + n_carried:]
    else:
        x_ref, c1_ref, s1_ref, c2_ref, s2_ref, prm_ref = refs[:6]
        wc_ref = None
        outs = refs[6:]
    (nsa_ref, win_ref, dsa_ref, ik_ref, dif_ref, qa_ref, ksel_ref, vsel_ref, kw_ref, vw_ref,
     qb_ref, kb_ref, vb_ref, iq_ref, ikd_ref, iw_ref, qc_ref, kcd_ref, vcd_ref, gat_ref) = outs[:20]

    c1, s1, c2, s2 = c1_ref[...], s1_ref[...], c2_ref[...], s2_ref[...]
    prm = prm_ref[...]
    lane = lax.broadcasted_iota(I32, (tm, 128), 1)
    lo = lane < 64
    inner = (lane & 63) < 32

    def col(a):
        return x_ref[:, a:a + 128]

    def gain(r):
        return prm[r:r + 1, :]

    grp_r = lax.broadcasted_iota(I32, (128, 128), 0) >> 6
    grp_c = lax.broadcasted_iota(I32, (128, 128), 1) >> 6
    avg128 = jnp.full((128, 128), 1.0 / 128, BF16)
    avg64 = jnp.where(grp_r == grp_c, 1.0 / 64, 0.0).astype(BF16)

    def group_ms(v, avg):
        sq = v * v
        hi = sq.astype(BF16)
        lo_part = (sq - hi.astype(F32)).astype(BF16)
        return _dot(hi, avg) + _dot(lo_part, avg)

    def rms128(v, g):
        return v * lax.rsqrt(group_ms(v, avg128) + EPS) * g

    def rope128(v):
        return v * c1 + pltpu.roll(v, 64, 1) * s1

    def rms64(v, g):
        return v * lax.rsqrt(group_ms(v, avg64) + EPS) * g

    def rope64(v):
        rot = jnp.where(inner, pltpu.roll(v, 96, 1), pltpu.roll(v, 32, 1))
        return v * c2 + rot * s2

    for h in range(A_HEADS):
        qa_ref[h] = (rope128(rms128(col(OFF_AQ + h * 128), gain(0))) * SCALE_D).astype(qa_ref.dtype)
    for kv in range(A_KV):
        o = kv * 128
        kc = rms128(col(OFF_AKV + o), gain(1))
        vc = col(OFF_AKV + 256 + o)
        ks = rope128(rms128(col(OFF_AKV + 512 + o), gain(2)))
        vs = col(OFF_AKV + 768 + o)
        kw = rope128(rms128(col(OFF_AKV + 1024 + o), gain(3)))
        vw = col(OFF_AKV + 1280 + o)
        for slab, val in ((kv, kc), (2 + kv, vc), (4 + kv, ks), (6 + kv, vs)):
            nsa_ref[pl.ds(slab, tm, stride=8), :] = val
        win_ref[pl.ds(kv, tm, stride=4), :] = kw
        win_ref[pl.ds(2 + kv, tm, stride=4), :] = vw
        ksel_ref[kv] = ks.astype(ksel_ref.dtype)
        vsel_ref[kv] = vs.astype(vsel_ref.dtype)
        kw_ref[kv] = kw.astype(kw_ref.dtype)
        vw_ref[kv] = vw.astype(vw_ref.dtype)
        if emit_cmp:
            pa_ref, pb_ref = outs[20], outs[21]
            nch = tm // CMP_STRIDE
            pk = _dot3(wc_ref[0], kc)
            pv = _dot3(wc_ref[1], vc)
            pa_ref[kv] = pk[0:nch]
            pb_ref[kv] = pk[nch:2 * nch]
            pa_ref[2 + kv] = pv[0:nch]
            pb_ref[2 + kv] = pv[nch:2 * nch]
    for h in range(B_HEADS):
        qb_ref[h] = (rope128(rms128(col(OFF_BQ + h * 128), gain(4))) * SCALE_D).astype(qb_ref.dtype)
    kb = rope128(rms128(col(OFF_BKV), gain(5)))
    vb = col(OFF_BKV + 128)
    dsa_ref[pl.ds(0, tm, stride=2), :] = kb
    dsa_ref[pl.ds(1, tm, stride=2), :] = vb
    kb_ref[...] = kb.astype(kb_ref.dtype)
    vb_ref[...] = vb.astype(vb_ref.dtype)
    for p in range(IDX_HEADS // 2):
        v = rope64(col(OFF_BIQ + p * 128))
        iq_ref[2 * p] = jnp.where(lo, v, 0.0).astype(iq_ref.dtype)
        iq_ref[2 * p + 1] = jnp.where(lo, 0.0, v).astype(iq_ref.dtype)
    for h in range(C_HEADS):
        v = rope64(rms64(col(OFF_CQ + h * 128), gain(7))) * SCALE_C
        kv, g = h // 2, h % 2
        qc_ref[kv, 2 * g] = jnp.where(lo, v, 0.0).astype(qc_ref.dtype)
        qc_ref[kv, 2 * g + 1] = jnp.where(lo, 0.0, v).astype(qc_ref.dtype)
    for kv in range(C_KV):
        o = kv * 128
        kk = rope64(rms64(col(OFF_CKV + o), gain(8)))
        vv = col(OFF_CKV + 256 + o)
        dif_ref[pl.ds(kv, tm, stride=4), :] = kk
        dif_ref[pl.ds(2 + kv, tm, stride=4), :] = vv
        kcd_ref[kv] = kk.astype(kcd_ref.dtype)
        vcd_ref[kv] = vv.astype(vcd_ref.dtype)
    m = col(OFF_MISC)
    ikr = rope64(rms64(m, gain(6)))
    ik_ref[...] = ikr[:, 0:64]
    ikd_ref[...] = jnp.where(lo, ikr, pltpu.roll(ikr, 64, 1)).astype(ikd_ref.dtype)
    sig = jax.nn.sigmoid(m)
    for kv in range(A_KV):
        gat_ref[kv] = pltpu.roll(sig, 128 - MISC_GATE - 12 * kv, 1)
    iw_ref[...] = pltpu.roll(m, 128 - MISC_IW, 1) * ((IDX_DIM ** -0.5) * (IDX_HEADS ** -0.5))


def _post_project(proj, tabs, prm, wc, tm, qdt, stack=None):
    B, T, _ = proj.shape
    emit_cmp = wc is not None
    nt = T // tm

    def row(c):
        return pl.BlockSpec((None, tm, c), lambda b, i: (b, i, 0))

    def heads(*lead):
        n = len(lead)
        return pl.BlockSpec((None,) + lead + (tm, 128), lambda b, i: (b,) + (0,) * n + (i, 0))

    tab = pl.BlockSpec((tm, 128), lambda b, i: (i, 0))
    in_specs = [row(N_PROJ), tab, tab, tab, tab, pl.BlockSpec((16, 128), lambda b, i: (0, 0))]
    args = [proj, *tabs, prm]
    if emit_cmp:
        in_specs.append(pl.BlockSpec((2, 2 * tm // CMP_STRIDE, tm), lambda b, i: (0, 0, 0)))
        args.append(wc)

    def sds(shape, dt):
        return jax.ShapeDtypeStruct(shape, dt)

    def slabs(n):
        return pl.BlockSpec((None, tm * n, 128), lambda b, i: (b, i, 0))

    out_shape = [
        sds((B, T * 8, 128), F32), sds((B, T * 4, 128), F32), sds((B, T * 2, 128), F32), sds((B, T, 64), F32),
        sds((B, T * 4, 128), F32),
        sds((B, A_HEADS, T, 128), qdt),
        sds((B, A_KV, T, 128), qdt), sds((B, A_KV, T, 128), qdt),
        sds((B, A_KV, T, 128), qdt), sds((B, A_KV, T, 128), qdt),
        sds((B, B_HEADS, T, 128), qdt), sds((B, T, 128), qdt), sds((B, T, 128), qdt),
        sds((B, IDX_HEADS, T, 128), qdt), sds((B, T, 128), qdt), sds((B, T, 128), F32),
        sds((B, C_KV, 4, T, 128), qdt), sds((B, C_KV, T, 128), qdt), sds((B, C_KV, T, 128), qdt),
        sds((B, A_KV, T, 128), F32),
    ]
    out_specs = [
        slabs(8), slabs(4), slabs(2), row(64), slabs(4),
        heads(A_HEADS), heads(A_KV), heads(A_KV), heads(A_KV), heads(A_KV),
        heads(B_HEADS), row(128), row(128), heads(IDX_HEADS), row(128), row(128),
        heads(C_KV, 4), heads(C_KV), heads(C_KV), heads(A_KV),
    ]
    if emit_cmp:
        nc = T // CMP_STRIDE
        out_shape += [sds((B, 4, nc, 128), F32), sds((B, 4, nc, 128), F32)]
        spec = pl.BlockSpec((None, 4, tm // CMP_STRIDE, 128), lambda b, i: (b, 0, i, 0))
        out_specs += [spec, spec]
    aliases = {}
    n_carried = 0
    if stack is not None:
        layer, depth, carried = stack
        for j in range(5):
            blk = out_specs[j].block_shape
            out_shape[j] = sds((depth,) + out_shape[j].shape, F32)
            out_specs[j] = pl.BlockSpec((None,) + tuple(blk), lambda b, i: (layer, b, i, 0))
        if carried is not None:
            n_carried = len(carried)
            aliases = {len(args) + j: j for j in range(n_carried)}
            in_specs += [pl.BlockSpec(memory_space=pl.ANY)] * n_carried
            args += list(carried)
    return pl.pallas_call(
        functools.partial(_post_kernel, emit_cmp=emit_cmp, tm=tm, n_carried=n_carried),
        grid=(B, nt),
        in_specs=in_specs,
        out_specs=out_specs,
        out_shape=out_shape,
        input_output_aliases=aliases,
        compiler_params=_cparams(("parallel", "parallel")),
        name="post_project",
    )(*args)


def _bias(mask):
    return jnp.where(mask, 0.0, MASKED)


def _softmax_parts(s, bias):
    s = s + bias
    m = jnp.maximum(jnp.max(s, axis=-1, keepdims=True), NEG)
    e = jnp.exp2(s - m)
    return e, 1.0 / jnp.maximum(jnp.sum(e, axis=-1, keepdims=True), 1e-30)


def _masked_softmax(s, bias):
    e, r = _softmax_parts(s, bias)
    return e * r


def _online_step(s, bias, v, m, l, acc):
    if bias is not None:
        s = s + bias
    m_new = jnp.maximum(m, jnp.max(s, axis=-1, keepdims=True))
    alpha = jnp.exp2(m - m_new)
    e = jnp.exp2(s - m_new)
    l_new = alpha * l + jnp.sum(e, axis=-1, keepdims=True)
    rows = acc.shape[0]
    pv = _dot(e.reshape(rows, e.shape[-1]).astype(BF16), v)
    return m_new, l_new, alpha.reshape(rows, 1) * acc + pv


def _flash_init(heads, tq):
    return (jnp.full((heads, tq, 1), NEG, F32), jnp.zeros((heads, tq, 1), F32), jnp.zeros((heads * tq, 128), F32))


def _flash_tile(q, k, v, bias, carry):
    m, l, acc = carry
    s = _dot_nt(q, k).reshape(m.shape[0], m.shape[1], k.shape[0])
    return _online_step(s, bias, v, m, l, acc)


def _flash_finish(carry):
    _, l, acc = carry
    return acc / jnp.maximum(l.reshape(acc.shape[0], 1), 1e-30)


def _online_single(s, bias, v_row, m, l, acc):
    if bias is not None:
        s = s + bias
    m_new = jnp.maximum(m, s)
    alpha = jnp.exp2(m - m_new)
    e = jnp.exp2(s - m_new)
    return m_new, alpha * l + e, alpha * acc + e.astype(BF16).astype(F32) * v_row


def _rowdot(q, k_row):
    return jnp.sum(q.astype(BF16).astype(F32) * k_row.astype(BF16).astype(F32), axis=-1, keepdims=True)


def _sortable(x):
    b = lax.bitcast_convert_type(x + 0.0, I32)
    return jnp.where(b < 0, b ^ jnp.int32(0x7FFFFFFF), b)


def _lambda_of(lp, lam_init):
    a = jnp.sum(lp[0:1] * lp[1:2], axis=-1, keepdims=True)
    b = jnp.sum(lp[2:3] * lp[3:4], axis=-1, keepdims=True)
    return jnp.exp(a) - jnp.exp(b) + lam_init


def _nsa_prompt_kernel(q_ref, pak_ref, pbk_ref, pav_ref, pbv_ref, cc_ref, sc_ref,
                       ks_ref, vs_ref, kw_ref, vw_ref, g_ref, o_ref, kc_ref, vc_ref, *, T):
    TQ = TQ_NSA
    qi = pl.program_id(2)
    nc = T // CMP_STRIDE
    ns = T // SEL_BLOCK
    R = A_G * TQ
    band = WINDOW + TQ

    @pl.when(qi == 0)
    def _():
        kraw = pak_ref[...] + pltpu.roll(pbk_ref[...], nc - 1, 0)
        kc_ref[...] = (kraw * cc_ref[...] + pltpu.roll(kraw, 64, 1) * sc_ref[...]).astype(BF16)
        vc_ref[...] = (pav_ref[...] + pltpu.roll(pbv_ref[...], nc - 1, 0)).astype(BF16)

    q = q_ref[...].reshape(R, 128)
    t0 = qi * TQ
    qp3 = t0 + lax.broadcasted_iota(I32, (1, TQ, 1), 1)

    s_c = _dot_nt(q, kc_ref[...]).reshape(A_G, TQ, nc)
    cend = lax.broadcasted_iota(I32, (1, TQ, nc), 2) * CMP_STRIDE + (CMP_LEN - 1)
    e_c, r_c = _softmax_parts(s_c, _bias(cend <= qp3))
    o_c = _dot(e_c.reshape(R, nc).astype(BF16), vc_ref[...]) * r_c.reshape(R, 1)

    sj = lax.broadcasted_iota(I32, (ns, nc), 0) * SEL_BLOCK
    ci = lax.broadcasted_iota(I32, (ns, nc), 1) * CMP_STRIDE
    overlap_t = ((ci < sj + SEL_BLOCK) & (ci + CMP_LEN > sj)).astype(F32)
    imp = _dot_nt(overlap_t, jnp.sum(e_c * r_c, axis=0), HI)
    jidx = lax.broadcasted_iota(I32, (ns, TQ), 0)
    jq = (t0 + lax.broadcasted_iota(I32, (1, TQ), 1)) >> SEL_SHIFT
    forced = (jidx == 0) | (jidx == jq) | (jidx == jq - 1)
    imp = jnp.where(forced, imp + FORCE_BONUS, imp)
    imp = jnp.where(jidx > jq, NEG, imp)
    ng = ns // 8
    sub = lax.broadcasted_iota(I32, (8, TQ), 0)
    imp_g = [imp[8 * g:8 * g + 8, :] for g in range(ng)]
    rank_g = [jnp.zeros((8, TQ), F32) for _ in range(ng)]
    for j in range(ns):
        rj = jnp.broadcast_to(imp[j:j + 1, :], (8, TQ))
        for g in range(ng):
            if g < j // 8:
                ahead = rj > imp_g[g]
            elif g > j // 8:
                ahead = rj >= imp_g[g]
            else:
                ahead = (rj > imp_g[g]) | ((rj == imp_g[g]) & (sub > j % 8))
            rank_g[g] = rank_g[g] + jnp.where(ahead, 1.0, 0.0)
    rank = jnp.concatenate(rank_g, axis=0)
    selb = jnp.where(rank < min(SEL_TOPN, ns), 1.0, 0.0).T.astype(BF16)

    erow = lax.broadcasted_iota(I32, (ns, TK), 0)
    ecol = lax.broadcasted_iota(I32, (ns, TK), 1)
    tcol = lax.broadcasted_iota(I32, (1, TQ, TK), 2)

    def sel_step(kt, carry):
        base = pl.multiple_of(kt * TK, TK)
        expand = (erow == ((ecol + base) >> SEL_SHIFT)).astype(BF16)
        chosen = _dot(selb, expand).reshape(1, TQ, TK) > 0.5
        bias = _bias(chosen & (tcol + base <= qp3))
        return _flash_tile(q, ks_ref[pl.ds(base, TK), :], vs_ref[pl.ds(base, TK), :], bias, carry)

    nkt = (t0 + TQ + TK - 1) // TK
    o_s = _flash_finish(lax.fori_loop(0, nkt, sel_step, _flash_init(A_G, TQ)))

    start = pl.multiple_of(jnp.maximum(t0 - WINDOW, 0), TQ)
    kwin = kw_ref[pl.ds(start, band), :]
    vwin = vw_ref[pl.ds(start, band), :]
    dist = qp3 - (start + lax.broadcasted_iota(I32, (1, TQ, band), 2))
    s_w = _dot_nt(q, kwin).reshape(A_G, TQ, band)
    e_w, r_w = _softmax_parts(s_w, _bias((dist >= 0) & (dist <= WINDOW)))
    o_w = _dot(e_w.reshape(R, band).astype(BF16), vwin) * r_w.reshape(R, 1)

    g = g_ref[...]
    for h in range(A_G):
        r = slice(h * TQ, (h + 1) * TQ)
        o = g[:, 3 * h:3 * h + 1] * o_c[r] + g[:, 3 * h + 1:3 * h + 2] * o_s[r] + g[:, 3 * h + 2:3 * h + 3] * o_w[r]
        o_ref[:, h * 128:(h + 1) * 128] = o.astype(o_ref.dtype)


def _nsa_prompt(qa, pa, pb, cc, sc, ksel, vsel, kw, vw, gat):
    B, _, T, _ = qa.shape
    TQ = TQ_NSA
    nc = T // CMP_STRIDE
    part_k = pl.BlockSpec((None, None, nc, 128), lambda b, kv, i: (b, kv, 0, 0))
    part_v = pl.BlockSpec((None, None, nc, 128), lambda b, kv, i: (b, 2 + kv, 0, 0))
    tabc = pl.BlockSpec((nc, 128), lambda b, kv, i: (0, 0))
    full = pl.BlockSpec((None, None, T, 128), lambda b, kv, i: (b, kv, 0, 0))
    return pl.pallas_call(
        functools.partial(_nsa_prompt_kernel, T=T),
        grid=(B, A_KV, T // TQ),
        in_specs=[
            pl.BlockSpec((None, A_G, TQ, 128), lambda b, kv, i: (b, kv, i, 0)),
            part_k, part_k, part_v, part_v, tabc, tabc, full, full, full, full,
            pl.BlockSpec((None, None, TQ, 128), lambda b, kv, i: (b, kv, i, 0)),
        ],
        out_specs=pl.BlockSpec((None, TQ, A_G * 128), lambda b, kv, i: (b, i, kv)),
        out_shape=jax.ShapeDtypeStruct((B, T, A_HEADS * 128), BF16),
        scratch_shapes=[pltpu.VMEM((nc, 128), BF16), pltpu.VMEM((nc, 128), BF16)],
        compiler_params=_cparams(("parallel", "parallel", "arbitrary")),
        name="nsa_prompt",
    )(qa, pa, pb, pa, pb, cc, sc, ksel, vsel, kw, vw, gat)


def _kth_threshold(count_ge, shape, k):
    def step(it, t):
        cand = t + jnp.left_shift(jnp.int32(1), 31 - it)
        return jnp.where(count_ge(cand) >= k, cand, t)

    return lax.fori_loop(0, 32, step, jnp.full(shape, INT_MIN, I32))


def _dsa_prompt_kernel(q_ref, iq_ref, iw_ref, ik_ref, k_ref, v_ref, o_ref, key_ref, *, T):
    TQ = TQ_DSA
    qi = pl.program_id(1)
    t0 = qi * TQ
    nkt = (t0 + TQ + TK - 1) // TK
    k_top = min(IDX_TOPK, T // 4)
    qp_l = t0 + lax.broadcasted_iota(I32, (1, TQ), 1)
    trow = lax.broadcasted_iota(I32, (TK, TQ), 0)
    iq = iq_ref[...].reshape(IDX_HEADS * TQ, 128)
    iw_t = iw_ref[...].T

    def score_step(kt, _):
        base = pl.multiple_of(kt * TK, TK)
        s = _dot_nt(ik_ref[pl.ds(base, TK), :], iq)
        sc = jnp.zeros((TK, TQ), F32)
        for h in range(IDX_HEADS):
            sc = sc + iw_t[h:h + 1, :] * jnp.maximum(s[:, h * TQ:(h + 1) * TQ], 0.0)
        sc = jnp.where(trow + base <= qp_l, sc, NEG)
        key_ref[pl.ds(base, TK), :] = _sortable(sc)
        return 0

    lax.fori_loop(0, nkt, score_step, 0)

    def count_ge(cand):
        def cstep(kt, c):
            base = pl.multiple_of(kt * TK, TK)
            hit = jnp.where(key_ref[pl.ds(base, TK), :] >= cand, 1.0, 0.0)
            return c + jnp.sum(hit.reshape(TK // 64, 8, 8, TQ), axis=0)

        c = lax.fori_loop(0, nkt, cstep, jnp.zeros((8, 8, TQ), F32))
        return jnp.sum(jnp.sum(c, axis=0), axis=0, keepdims=True)

    thr = _kth_threshold(count_ge, (1, TQ), float(k_top))
    q = q_ref[...].reshape(B_HEADS * TQ, 128)

    def attend(kt, keep, flash):
        base = pl.multiple_of(kt * TK, TK)
        bias = _bias(keep & (trow + base <= qp_l)).T.reshape(1, TQ, TK)
        return _flash_tile(q, k_ref[pl.ds(base, TK), :], v_ref[pl.ds(base, TK), :], bias, flash)

    def run_plain(_):
        def step(kt, flash):
            base = pl.multiple_of(kt * TK, TK)
            return attend(kt, key_ref[pl.ds(base, TK), :] >= thr, flash)

        return _flash_finish(lax.fori_loop(0, nkt, step, _flash_init(B_HEADS, TQ)))

    def run_ties(_):
        need = float(k_top) - count_ge(thr + 1)
        tri = (lax.broadcasted_iota(I32, (TK, TK), 1) < lax.broadcasted_iota(I32, (TK, TK), 0)).astype(BF16)

        def step(kt, carry):
            flash, run = carry
            base = pl.multiple_of(kt * TK, TK)
            keys = key_ref[pl.ds(base, TK), :]
            tie = keys == thr
            tie_f = jnp.where(tie, 1.0, 0.0)
            before = _dot(tri, tie_f.astype(BF16)) + run
            flash = attend(kt, (keys > thr) | (tie & (before < need)), flash)
            return flash, run + jnp.sum(tie_f, axis=0, keepdims=True)

        init = (_flash_init(B_HEADS, TQ), jnp.zeros((1, TQ), F32))
        return _flash_finish(lax.fori_loop(0, nkt, step, init)[0])

    any_tie = jnp.max(count_ge(thr)) > float(k_top)
    o = lax.cond(any_tie, run_ties, run_plain, 0)
    for h in range(B_HEADS):
        o_ref[:, h * 128:(h + 1) * 128] = o[h * TQ:(h + 1) * TQ].astype(o_ref.dtype)


def _dsa_prompt(qb, iq, iw, ikd, kb, vb):
    B, _, T, _ = qb.shape
    TQ = TQ_DSA
    full = pl.BlockSpec((None, T, 128), lambda b, i: (b, 0, 0))
    return pl.pallas_call(
        functools.partial(_dsa_prompt_kernel, T=T),
        grid=(B, T // TQ),
        in_specs=[
            pl.BlockSpec((None, B_HEADS, TQ, 128), lambda b, i: (b, 0, i, 0)),
            pl.BlockSpec((None, IDX_HEADS, TQ, 128), lambda b, i: (b, 0, i, 0)),
            pl.BlockSpec((None, TQ, 128), lambda b, i: (b, i, 0)),
            full, full, full,
        ],
        out_specs=pl.BlockSpec((None, TQ, B_HEADS * 128), lambda b, i: (b, i, 0)),
        out_shape=jax.ShapeDtypeStruct((B, T, B_HEADS * 128), BF16),
        scratch_shapes=[pltpu.VMEM((T, TQ), I32)],
        compiler_params=_cparams(("parallel", "arbitrary")),
        name="dsa_prompt",
    )(qb, iq, iw, ikd, kb, vb)


def _diff_finish(o, lam, subln, lam_init, rows):
    outs = []
    for g in range(2):
        a0 = o[(2 * g) * rows:(2 * g + 1) * rows]
        a1 = o[(2 * g + 1) * rows:(2 * g + 2) * rows]
        d = a0 - lam * a1
        d = d * lax.rsqrt(jnp.mean(d * d, axis=-1, keepdims=True) + EPS) * subln
        outs.append(d * (1.0 - lam_init))
    return outs


def _diff_prompt_kernel(q_ref, k_ref, v_ref, lp_ref, sub_ref, o_ref, *, lam_init):
    TQ = TQ_DIFF
    qi = pl.program_id(2)
    t0 = qi * TQ
    nkt = (t0 + TQ + TK - 1) // TK
    R = 4 * TQ
    qp = t0 + lax.broadcasted_iota(I32, (1, TQ, 1), 1)
    tcol = lax.broadcasted_iota(I32, (1, TQ, TK), 2)
    q = q_ref[...].reshape(R, 128)

    def step(kt, carry, causal):
        base = pl.multiple_of(kt * TK, TK)
        bias = _bias(tcol + base <= qp) if causal else None
        return _flash_tile(q, k_ref[pl.ds(base, TK), :], v_ref[pl.ds(base, TK), :], bias, carry)

    carry = lax.fori_loop(0, nkt - 1, functools.partial(step, causal=False), _flash_init(4, TQ))
    o = _flash_finish(step(nkt - 1, carry, True))
    lam = _lambda_of(lp_ref[...], lam_init)
    outs = _diff_finish(o, lam, sub_ref[...], lam_init, TQ)
    for g in range(2):
        o_ref[:, g * 128:(g + 1) * 128] = outs[g].astype(o_ref.dtype)


def _diff_prompt(qc, kcd, vcd, lp, subln, lam_init):
    B, _, _, T, _ = qc.shape
    TQ = TQ_DIFF
    full = pl.BlockSpec((None, None, T, 128), lambda b, kv, i: (b, kv, 0, 0))
    return pl.pallas_call(
        functools.partial(_diff_prompt_kernel, lam_init=lam_init),
        grid=(B, C_KV, T // TQ),
        in_specs=[
            pl.BlockSpec((None, None, 4, TQ, 128), lambda b, kv, i: (b, kv, 0, i, 0)),
            full, full,
            pl.BlockSpec((4, C_HALF), lambda b, kv, i: (0, 0)),
            pl.BlockSpec((1, 128), lambda b, kv, i: (0, 0)),
        ],
        out_specs=pl.BlockSpec((None, TQ, 256), lambda b, kv, i: (b, i, kv)),
        out_shape=jax.ShapeDtypeStruct((B, T, C_HEADS * 128), BF16),
        compiler_params=_cparams(("parallel", "parallel", "arbitrary")),
        name="diff_prompt",
    )(qc, kcd, vcd, lp, subln)


def _page_specs(block, layer, n_lead_zero):
    specs = []
    for u in range(PAGES_PER_STEP):
        def imap(b, pc, pt, u=u):
            return (layer, pt[b, pc * PAGES_PER_STEP + u]) + (0,) * n_lead_zero
        specs.append(pl.BlockSpec(block, imap))
    return specs


def _nsa_cmp_decode_kernel(pt_ref, *refs, past):
    pages = refs[:PAGES_PER_STEP]
    q_ref, new_ref, wc_ref, cc_ref, sc_ref, oc_ref, sel_ref, a_ref, b_ref = refs[PAGES_PER_STEP:]
    pc = pl.program_id(1)
    nc = past // CMP_STRIDE
    ns = past // SEL_BLOCK + 1
    nsp = ((ns + 127) // 128) * 128
    cpp = PAGE_SIZE // CMP_STRIDE

    def slab(ref, s):
        return ref[pl.ds(s, PAGE_SIZE, stride=8), :]

    for u in range(PAGES_PER_STEP):
        r0 = pl.multiple_of((pc * PAGES_PER_STEP + u) * cpp, cpp)
        for c in range(2):
            x = jnp.concatenate([slab(pages[u], 2 * c), slab(pages[u], 2 * c + 1)], axis=1)
            part = _dot3(wc_ref[c], x)
            a_ref[pl.ds(r0, cpp), 256 * c:256 * (c + 1)] = part[0:cpp]
            b_ref[pl.ds(r0, cpp), 256 * c:256 * (c + 1)] = part[cpp:2 * cpp]

    @pl.when(pc == pl.num_programs(1) - 1)
    def _():
        rowi = lax.broadcasted_iota(I32, (nc, 128), 0)
        cend = lax.broadcasted_iota(I32, (A_G, nc), 1) * CMP_STRIDE + (CMP_LEN - 1)
        ci = lax.broadcasted_iota(I32, (nc, nsp), 0) * CMP_STRIDE
        sj = lax.broadcasted_iota(I32, (nc, nsp), 1) * SEL_BLOCK
        overlap = ((ci < sj + SEL_BLOCK) & (ci + CMP_LEN > sj)).astype(F32)
        jrow = lax.broadcasted_iota(I32, (1, nsp), 1)
        jq = past // SEL_BLOCK
        ii = lax.broadcasted_iota(I32, (nsp, nsp), 0)
        jj = lax.broadcasted_iota(I32, (nsp, nsp), 1)
        rr = lax.broadcasted_iota(I32, (SEL_TOPN, nsp), 0)
        jr = lax.broadcasted_iota(I32, (SEL_TOPN, nsp), 1).astype(F32)
        new = new_ref[...]
        q = q_ref[...].astype(BF16)
        for kv in range(A_KV):
            ko, vo = kv * 128, 256 + kv * 128
            bk = jnp.where(rowi == nc - 1, wc_ref[0][cpp:cpp + 1, 0:1] * new[:, ko:ko + 128],
                           pltpu.roll(b_ref[:, ko:ko + 128], nc - 1, 0))
            bv = jnp.where(rowi == nc - 1, wc_ref[1][cpp:cpp + 1, 0:1] * new[:, vo:vo + 128],
                           pltpu.roll(b_ref[:, vo:vo + 128], nc - 1, 0))
            kraw = a_ref[:, ko:ko + 128] + bk
            kc = (kraw * cc_ref[...] + pltpu.roll(kraw, 64, 1) * sc_ref[...]).astype(BF16)
            vc = (a_ref[:, vo:vo + 128] + bv).astype(BF16)
            s_c = _dot_nt(q[kv * A_G:(kv + 1) * A_G], kc)
            p_c = _masked_softmax(s_c, _bias(cend <= past))
            oc_ref[kv * A_G:(kv + 1) * A_G, :] = _dot(p_c.astype(BF16), vc)
            imp = _dot(jnp.sum(p_c, axis=0, keepdims=True), overlap, HI)
            forced = (jrow == 0) | (jrow == jq) | (jrow == jq - 1)
            imp = jnp.where(forced, imp + FORCE_BONUS, imp)
            imp = jnp.where(jrow > jq, NEG, imp)
            imp_col = jnp.sum(jnp.where(ii == jj, jnp.broadcast_to(imp, (nsp, nsp)), 0.0), axis=1, keepdims=True)
            beats = (imp_col > imp) | ((imp_col == imp) & (ii < jj))
            rank = jnp.sum(jnp.where(beats, 1.0, 0.0), axis=0, keepdims=True)
            pick = jnp.sum(jnp.where(rank == rr.astype(F32), jr, 0.0), axis=1, keepdims=True)
            sel_ref[kv] = pick.astype(I32)


def _nsa_cmp_decode(page_table, cache, layer, q, new_rows, wc, cc, sc, past):
    DB = q.shape[0]
    npc = page_table.shape[1] // PAGES_PER_STEP
    nc = past // CMP_STRIDE
    block = (None, None, PAGE_SIZE * 8, 128)
    const2 = lambda b, pc, pt: (0, 0)
    return pl.pallas_call(
        functools.partial(_nsa_cmp_decode_kernel, past=past),
        grid_spec=pltpu.PrefetchScalarGridSpec(
            num_scalar_prefetch=1,
            grid=(DB, npc),
            in_specs=_page_specs(block, layer, 2) + [
                pl.BlockSpec((None, A_HEADS, 128), lambda b, pc, pt: (b, 0, 0)),
                pl.BlockSpec((None, 1, 1024), lambda b, pc, pt: (b, 0, 0)),
                pl.BlockSpec((2, 2 * PAGE_SIZE // CMP_STRIDE, PAGE_SIZE), lambda b, pc, pt: (0, 0, 0)),
                pl.BlockSpec((nc, 128), const2),
                pl.BlockSpec((nc, 128), const2),
            ],
            out_specs=[
                pl.BlockSpec((None, A_HEADS, 128), lambda b, pc, pt: (b, 0, 0)),
                pl.BlockSpec((None, A_KV, SEL_TOPN, 1), lambda b, pc, pt: (b, 0, 0, 0)),
            ],
            scratch_shapes=[pltpu.VMEM((nc, 512), F32), pltpu.VMEM((nc, 512), F32)],
        ),
        out_shape=[jax.ShapeDtypeStruct((DB, A_HEADS, 128), F32),
                   jax.ShapeDtypeStruct((DB, A_KV, SEL_TOPN, 1), I32)],
        compiler_params=_cparams(("parallel", "arbitrary")),
        name="nsa_cmp_decode",
    )(page_table, *([cache] * PAGES_PER_STEP), q, new_rows, wc, cc, sc)


def _nsa_sel_decode_kernel(pt_ref, sel_ref, *refs, past, lw):
    blks = refs[:A_KV * SEL_PER_STEP]
    win_ref, q_ref, new_ref, neww_ref, oc_ref, g_ref, o_ref, m_ref, l_ref, acc_ref = refs[A_KV * SEL_PER_STEP:]
    b, s = pl.program_id(0), pl.program_id(1)
    ns = past // SEL_BLOCK + 1
    new = new_ref[...]
    neww = neww_ref[...]
    blk_of_lane = lax.broadcasted_iota(I32, (1, SEL_PER_STEP * SEL_BLOCK), 1) >> SEL_SHIFT

    for kv in range(A_KV):
        q = q_ref[kv * A_G:(kv + 1) * A_G, :].astype(BF16)
        mine = blks[kv * SEL_PER_STEP:(kv + 1) * SEL_PER_STEP]

        @pl.when(s == 0)
        def _():
            m_ref[kv] = _rowdot(q, new[:, 512 + kv * 128:640 + kv * 128])
            l_ref[kv] = jnp.ones((A_G, 1), F32)
            v_new = new[:, 768 + kv * 128:896 + kv * 128].astype(BF16).astype(F32)
            acc_ref[kv] = jnp.broadcast_to(v_new, (A_G, 128))

        k = jnp.concatenate([r[pl.ds(4 + kv, SEL_BLOCK, stride=8), :] for r in mine], axis=0).astype(BF16)
        v = jnp.concatenate([r[pl.ds(6 + kv, SEL_BLOCK, stride=8), :] for r in mine], axis=0).astype(BF16)
        bias = jnp.zeros((1, SEL_PER_STEP * SEL_BLOCK), F32)
        for u in range(SEL_PER_STEP):
            keep = sel_ref[b, kv, s * SEL_PER_STEP + u] != ns - 1
            bias = jnp.where(blk_of_lane == u, jnp.where(keep, 0.0, MASKED), bias)
        m, l, acc = _online_step(_dot_nt(q, k), bias, v, m_ref[kv], l_ref[kv], acc_ref[kv])
        m_ref[kv] = m
        l_ref[kv] = l
        acc_ref[kv] = acc

        @pl.when(s == pl.num_programs(1) - 1)
        def _():
            o_s = acc_ref[kv] / jnp.maximum(l_ref[kv], 1e-30)
            kw = win_ref[pl.ds(kv, lw, stride=4), :].astype(BF16)
            vw = win_ref[pl.ds(2 + kv, lw, stride=4), :].astype(BF16)
            vw_new = neww[:, 256 + kv * 128:384 + kv * 128].astype(BF16).astype(F32)
            s_w = _dot_nt(q, kw)
            s_n = _rowdot(q, neww[:, kv * 128:(kv + 1) * 128])
            mw = jnp.maximum(jnp.max(s_w, axis=-1, keepdims=True), s_n)
            e_w = jnp.exp2(s_w - mw)
            e_n = jnp.exp2(s_n - mw)
            den = jnp.sum(e_w, axis=-1, keepdims=True) + e_n
            o_w = (_dot(e_w.astype(BF16), vw) + e_n.astype(BF16).astype(F32) * vw_new) / den
            o_c = oc_ref[kv * A_G:(kv + 1) * A_G, :]
            g = g_ref[kv * A_G:(kv + 1) * A_G, :]
            o_ref[kv * A_G:(kv + 1) * A_G, :] = g[:, 0:1] * o_c + g[:, 1:2] * o_s + g[:, 2:3] * o_w


def _nsa_sel_decode(page_table, sel, cache, win, layer, q, new_rows, new_win, o_c, gates, past):
    DB = q.shape[0]
    n_pages = page_table.shape[1]
    lw = win.shape[2] // 4

    def blk_spec(kv, u):
        def imap(b, s, pt, sel):
            j = sel[b, kv, s * SEL_PER_STEP + u]
            return (layer, pt[b, jnp.minimum(j // 2, n_pages - 1)], j % 2, 0)
        return pl.BlockSpec((None, None, SEL_BLOCK * 8, 128), imap)

    per_b = lambda b, s, pt, sel: (b, 0, 0)
    n_blk = A_KV * SEL_PER_STEP
    return pl.pallas_call(
        functools.partial(_nsa_sel_decode_kernel, past=past, lw=lw),
        grid_spec=pltpu.PrefetchScalarGridSpec(
            num_scalar_prefetch=2,
            grid=(DB, SEL_TOPN // SEL_PER_STEP),
            in_specs=[blk_spec(kv, u) for kv in range(A_KV) for u in range(SEL_PER_STEP)] + [
                pl.BlockSpec((None, None, lw * 4, 128), lambda b, s, pt, sel: (layer, b, 0, 0)),
                pl.BlockSpec((None, A_HEADS, 128), per_b),
                pl.BlockSpec((None, 1, 1024), per_b),
                pl.BlockSpec((None, 1, 512), per_b),
                pl.BlockSpec((None, A_HEADS, 128), per_b),
                pl.BlockSpec((None, A_HEADS, 128), per_b),
            ],
            out_specs=pl.BlockSpec((None, A_HEADS, 128), per_b),
            scratch_shapes=[pltpu.VMEM((A_KV, A_G, 1), F32), pltpu.VMEM((A_KV, A_G, 1), F32),
                            pltpu.VMEM((A_KV, A_G, 128), F32)],
        ),
        out_shape=jax.ShapeDtypeStruct((DB, A_HEADS, 128), F32),
        compiler_params=_cparams(("parallel", "arbitrary")),
        name="nsa_sel_decode",
    )(page_table, sel, *([cache] * n_blk), win, q, new_rows, new_win, o_c, gates)


def _dsa_idx_decode_kernel(pt_ref, *refs, past):
    pages = refs[:PAGES_PER_STEP]
    iq_ref, iw_ref, ikn_ref, mask_ref, sc_ref = refs[PAGES_PER_STEP:]
    pc = pl.program_id(1)
    n_pages = past // PAGE_SIZE
    k_top = min(IDX_TOPK, (past + 1) // 4)
    iq = iq_ref[...].astype(BF16)
    iw = iw_ref[...]

    @pl.when(pc == 0)
    def _():
        sc_ref[...] = jnp.full(sc_ref.shape, NEG, F32)

    keys_t = jnp.concatenate([p[...] for p in pages], axis=1).astype(BF16)
    sc = jnp.sum(iw * jnp.maximum(_dot(iq, keys_t), 0.0), axis=0, keepdims=True)
    for u in range(PAGES_PER_STEP):
        sc_ref[pl.ds(pc * PAGES_PER_STEP + u, 1), :] = sc[:, u * PAGE_SIZE:(u + 1) * PAGE_SIZE]

    @pl.when(pc == pl.num_programs(1) - 1)
    def _():
        s_new = _rowdot(iq, ikn_ref[...])
        s_new = jnp.sum(iw * jnp.maximum(s_new, 0.0), axis=0, keepdims=True)
        lane0 = lax.broadcasted_iota(I32, (1, PAGE_SIZE), 1) == 0
        sc_ref[n_pages:n_pages + 1, :] = jnp.where(lane0, s_new, NEG)
        keys = _sortable(sc_ref[...])

        def count_ge(cand):
            c = jnp.sum(jnp.where(keys >= cand, 1.0, 0.0), axis=0, keepdims=True)
            return jnp.sum(c, axis=-1, keepdims=True)

        thr = _kth_threshold(count_ge, (1, 1), float(k_top))
        need = float(k_top) - count_ge(thr + 1)
        rows = keys.shape[0]
        tie = keys == thr
        tie_f = jnp.where(tie, 1.0, 0.0)
        upper = (lax.broadcasted_iota(I32, (PAGE_SIZE, PAGE_SIZE), 0)
                 < lax.broadcasted_iota(I32, (PAGE_SIZE, PAGE_SIZE), 1)).astype(BF16)
        lower = (lax.broadcasted_iota(I32, (rows, rows), 1) < lax.broadcasted_iota(I32, (rows, rows), 0)).astype(BF16)
        in_row = _dot(tie_f.astype(BF16), upper)
        row_tot = jnp.broadcast_to(jnp.sum(tie_f, axis=-1, keepdims=True), (rows, PAGE_SIZE))
        above = _dot(lower, row_tot.astype(BF16))
        keep = (keys > thr) | (tie & (in_row + above < need))
        mask_ref[...] = _bias(keep & (sc_ref[...] > 0.5 * NEG))


def _dsa_idx_decode(page_table, cache, layer, iq, iw, ik_new, past):
    DB = iq.shape[0]
    n_pages = page_table.shape[1]
    npc = n_pages // PAGES_PER_STEP
    rows = ((n_pages + 1 + 7) // 8) * 8
    per_b = lambda b, pc, pt: (b, 0, 0)
    return pl.pallas_call(
        functools.partial(_dsa_idx_decode_kernel, past=past),
        grid_spec=pltpu.PrefetchScalarGridSpec(
            num_scalar_prefetch=1,
            grid=(DB, npc),
            in_specs=_page_specs((None, None, IDX_DIM, PAGE_SIZE), layer, 2) + [
                pl.BlockSpec((None, IDX_HEADS, IDX_DIM), per_b),
                pl.BlockSpec((None, IDX_HEADS, 1), per_b),
                pl.BlockSpec((None, 1, IDX_DIM), per_b),
            ],
            out_specs=pl.BlockSpec((None, rows, PAGE_SIZE), per_b),
            scratch_shapes=[pltpu.VMEM((rows, PAGE_SIZE), F32)],
        ),
        out_shape=jax.ShapeDtypeStruct((DB, rows, PAGE_SIZE), F32),
        compiler_params=_cparams(("parallel", "arbitrary")),
        name="dsa_idx_decode",
    )(page_table, *([cache] * PAGES_PER_STEP), iq, iw, ik_new)


def _dsa_att_decode_kernel(pt_ref, *refs, past):
    pages = refs[:PAGES_PER_STEP]
    q_ref, new_ref, mask_ref, o_ref, m_ref, l_ref, acc_ref = refs[PAGES_PER_STEP:]
    pc = pl.program_id(1)
    n_pages = past // PAGE_SIZE
    q = q_ref[...].astype(BF16)

    @pl.when(pc == 0)
    def _():
        m_ref[...] = jnp.full(m_ref.shape, NEG, F32)
        l_ref[...] = jnp.zeros(l_ref.shape, F32)
        acc_ref[...] = jnp.zeros(acc_ref.shape, F32)

    k = jnp.concatenate([p[pl.ds(0, PAGE_SIZE, stride=2), :] for p in pages], axis=0).astype(BF16)
    v = jnp.concatenate([p[pl.ds(1, PAGE_SIZE, stride=2), :] for p in pages], axis=0).astype(BF16)
    r0 = pl.multiple_of(pc * PAGES_PER_STEP, PAGES_PER_STEP)
    mrows = mask_ref[pl.ds(r0, PAGES_PER_STEP), :]
    bias = jnp.concatenate([mrows[u:u + 1, :] for u in range(PAGES_PER_STEP)], axis=1)
    m, l, acc = _online_step(_dot_nt(q, k), bias, v, m_ref[...], l_ref[...], acc_ref[...])
    m_ref[...] = m
    l_ref[...] = l
    acc_ref[...] = acc

    @pl.when(pc == pl.num_programs(1) - 1)
    def _():
        new = new_ref[...]
        v_new = new[:, 128:256].astype(BF16).astype(F32)
        keep = mask_ref[n_pages:n_pages + 1, 0:1]
        _, l2, acc2 = _online_single(_rowdot(q, new[:, 0:128]), keep, v_new, m_ref[...], l_ref[...], acc_ref[...])
        o_ref[...] = acc2 / jnp.maximum(l2, 1e-30)


def _dsa_att_decode(page_table, cache, layer, q, new_rows, mask, past):
    DB = q.shape[0]
    npc = page_table.shape[1] // PAGES_PER_STEP
    rows = mask.shape[1]
    per_b = lambda b, pc, pt: (b, 0, 0)
    return pl.pallas_call(
        functools.partial(_dsa_att_decode_kernel, past=past),
        grid_spec=pltpu.PrefetchScalarGridSpec(
            num_scalar_prefetch=1,
            grid=(DB, npc),
            in_specs=_page_specs((None, None, PAGE_SIZE * 2, 128), layer, 2) + [
                pl.BlockSpec((None, B_HEADS, 128), per_b),
                pl.BlockSpec((None, 1, 256), per_b),
                pl.BlockSpec((None, rows, PAGE_SIZE), per_b),
            ],
            out_specs=pl.BlockSpec((None, B_HEADS, 128), per_b),
            scratch_shapes=[pltpu.VMEM((B_HEADS, 1), F32), pltpu.VMEM((B_HEADS, 1), F32),
                            pltpu.VMEM((B_HEADS, 128), F32)],
        ),
        out_shape=jax.ShapeDtypeStruct((DB, B_HEADS, 128), F32),
        compiler_params=_cparams(("parallel", "arbitrary")),
        name="dsa_att_decode",
    )(page_table, *([cache] * PAGES_PER_STEP), q, new_rows, mask)


def _diff_decode_kernel(pt_ref, *refs, lam_init):
    pages = refs[:PAGES_PER_STEP]
    q_ref, new_ref, lp_ref, sub_ref, o_ref, m_ref, l_ref, acc_ref = refs[PAGES_PER_STEP:]
    pc = pl.program_id(1)

    @pl.when(pc == 0)
    def _():
        m_ref[...] = jnp.full(m_ref.shape, NEG, F32)
        l_ref[...] = jnp.zeros(l_ref.shape, F32)
        acc_ref[...] = jnp.zeros(acc_ref.shape, F32)

    for kv in range(C_KV):
        q = q_ref[kv].astype(BF16)
        k = jnp.concatenate([p[pl.ds(kv, PAGE_SIZE, stride=4), :] for p in pages], axis=0).astype(BF16)
        v = jnp.concatenate([p[pl.ds(2 + kv, PAGE_SIZE, stride=4), :] for p in pages], axis=0).astype(BF16)
        sc = _dot_nt(q, k)
        m, l, acc = _online_step(sc, None, v, m_ref[kv], l_ref[kv], acc_ref[kv])
        m_ref[kv] = m
        l_ref[kv] = l
        acc_ref[kv] = acc

    @pl.when(pc == pl.num_programs(1) - 1)
    def _():
        new = new_ref[...]
        lam = _lambda_of(lp_ref[...], lam_init)
        for kv in range(C_KV):
            k_new = new[:, kv * 128:(kv + 1) * 128]
            v_new = new[:, 256 + kv * 128:384 + kv * 128].astype(BF16).astype(F32)
            _, l, acc = _online_single(_rowdot(q_ref[kv], k_new), None, v_new, m_ref[kv], l_ref[kv], acc_ref[kv])
            outs = _diff_finish(acc / jnp.maximum(l, 1e-30), lam, sub_ref[...], lam_init, 1)
            o_ref[2 * kv:2 * kv + 1, :] = outs[0]
            o_ref[2 * kv + 1:2 * kv + 2, :] = outs[1]


def _diff_decode(page_table, cache, layer, q, new_rows, lp, subln, lam_init):
    DB = q.shape[0]
    npc = page_table.shape[1] // PAGES_PER_STEP
    per_b = lambda b, pc, pt: (b, 0, 0)
    return pl.pallas_call(
        functools.partial(_diff_decode_kernel, lam_init=lam_init),
        grid_spec=pltpu.PrefetchScalarGridSpec(
            num_scalar_prefetch=1,
            grid=(DB, npc),
            in_specs=_page_specs((None, None, PAGE_SIZE * 4, 128), layer, 2) + [
                pl.BlockSpec((None, C_KV, 4, 128), lambda b, pc, pt: (b, 0, 0, 0)),
                pl.BlockSpec((None, 1, 512), per_b),
                pl.BlockSpec((4, C_HALF), lambda b, pc, pt: (0, 0)),
                pl.BlockSpec((1, 128), lambda b, pc, pt: (0, 0)),
            ],
            out_specs=pl.BlockSpec((None, C_HEADS, 128), per_b),
            scratch_shapes=[pltpu.VMEM((C_KV, 4, 1), F32), pltpu.VMEM((C_KV, 4, 1), F32),
                            pltpu.VMEM((C_KV, 4, 128), F32)],
        ),
        out_shape=jax.ShapeDtypeStruct((DB, C_HEADS, 128), F32),
        compiler_params=_cparams(("parallel", "arbitrary")),
        name="diff_decode",
    )(page_table, *([cache] * PAGES_PER_STEP), q, new_rows, lp, subln)


def _rope_tables(pos, d):
    half = d // 2
    inv = ROPE_THETA ** (-jnp.arange(half, dtype=F32) / half)
    ang = pos.astype(F32)[:, None] * inv[None, :]
    cos, sin = jnp.cos(ang), jnp.sin(ang)
    reps = 128 // d
    return jnp.tile(jnp.concatenate([cos, cos], axis=-1), (1, reps)), jnp.tile(jnp.concatenate([-sin, sin], axis=-1), (1, reps))


def _pack_params(nsa_qk_norm, dsa_qk_norm, dsa_idx_knorm, diff_qk_norm):
    rows = [nsa_qk_norm, dsa_qk_norm, jnp.tile(dsa_idx_knorm, 2)[None], jnp.tile(diff_qk_norm, (1, 2))]
    p = jnp.concatenate(rows, axis=0).astype(F32)
    return jnp.pad(p, ((0, 16 - p.shape[0]), (0, 0)))


def _compress_weights(cmp_w, rows):
    eye = jnp.eye(rows // CMP_STRIDE, dtype=F32)
    mats = []
    for c in range(2):
        halves = [jnp.kron(eye, cmp_w[c, h * CMP_STRIDE:(h + 1) * CMP_STRIDE][None, :]) for h in range(2)]
        mats.append(jnp.concatenate(halves, axis=0))
    return jnp.stack(mats, axis=0)


def kernel(x_prompt, x_sample, cache_nsa_kv, state_nsa_win, cache_dsa_kv, cache_dsa_idx, cache_diff_kv, page_table, attn_norm, w_in, nsa_qk_norm, nsa_cmp_w, dsa_qk_norm, dsa_idx_knorm, diff_qk_norm, diff_lambda, diff_subln, w_out, ffn_norm, w_gate_up, w_down):
    B, T, D = x_prompt.shape
    DB = x_sample.shape[0]
    depth = w_in.shape[0]
    n_pages = page_table.shape[1]
    past = n_pages * PAGE_SIZE
    M = B * T
    assert x_sample.shape[1] == 1 and T % TK == 0 and T >= WINDOW + TQ_NSA and n_pages % PAGES_PER_STEP == 0
    tm = min(1024, M)
    tm_ffn = min(1024, M)
    tm_post = 256

    w_in_p = _permute_cast_w_in(jnp.swapaxes(w_in, 1, 2))
    w_out_b = w_out.astype(BF16)
    w_gu_b = w_gate_up.astype(BF16)
    w_down_b = w_down.astype(BF16)

    pos_p = jnp.arange(T, dtype=I32)
    tabs_p = _rope_tables(pos_p, 128) + _rope_tables(pos_p, 64)
    pos_s = jnp.full((DB,), past, I32)
    tabs_s = _rope_tables(pos_s, 128) + _rope_tables(pos_s, 64)
    cend_p = jnp.arange(T // CMP_STRIDE, dtype=I32) * CMP_STRIDE + (CMP_LEN - 1)
    cc_p, sc_p = _rope_tables(cend_p, 128)
    cend_s = jnp.arange(past // CMP_STRIDE, dtype=I32) * CMP_STRIDE + (CMP_LEN - 1)
    cc_s, sc_s = _rope_tables(cend_s, 128)

    n_pool = cache_nsa_kv.shape[1]
    nsa_pages = cache_nsa_kv.reshape(depth, n_pool, PAGE_SIZE * 8, 128)
    dsa_pages = cache_dsa_kv.reshape(depth, n_pool, PAGE_SIZE * 2, 128)
    diff_pages = cache_diff_kv.reshape(depth, n_pool, PAGE_SIZE * 4, 128)
    win_rows = state_nsa_win.reshape(depth, DB, state_nsa_win.shape[2] * 4, 128)
    idx_pages = jnp.swapaxes(cache_dsa_idx, 2, 3)

    yp = x_prompt.reshape(M, D)
    ys = x_sample.reshape(DB, D)
    cache_rows_p, rows_s = None, []
    for l in range(depth):
        lam_init = 0.8 - 0.6 * math.exp(-0.3 * l)
        prm = _pack_params(nsa_qk_norm[l], dsa_qk_norm[l], dsa_idx_knorm[l], diff_qk_norm[l])
        g_attn = attn_norm[l][None, :]
        g_ffn = ffn_norm[l][None, :]
        lp = diff_lambda[l].astype(F32)
        subln = diff_subln[l][None, :].astype(F32)

        proj = _norm_matmul(yp, g_attn, w_in_p, l, tm).reshape(B, T, N_PROJ)
        wc = _compress_weights(nsa_cmp_w[l], tm_post)
        (nsa, win, dsa, ik, dif, qa, ksel, vsel, kw, vw, qb, kb, vb, iq, ikd, iw, qc, kcd, vcd, gat,
         pa, pb) = _post_project(proj, tabs_p, prm, wc, tm_post, BF16, stack=(l, depth, cache_rows_p))
        cache_rows_p = (nsa, win, dsa, ik, dif)
        o_a = _nsa_prompt(qa, pa, pb, cc_p, sc_p, ksel, vsel, kw, vw, gat)
        o_b = _dsa_prompt(qb, iq, iw, ikd, kb, vb)
        o_c = _diff_prompt(qc, kcd, vcd, lp, subln, lam_init)
        yp = _mix_out_projection(o_a.reshape(M, -1), o_b.reshape(M, -1), o_c.reshape(M, -1), w_out_b, yp, l, tm)
        act = _norm_swiglu(yp, g_ffn, w_gu_b, l, tm_ffn)
        yp = _matmul_residual(act, w_down_b, yp, l, tm_ffn)

        proj_s = _norm_matmul(ys, g_attn, w_in_p, l, DB).reshape(1, DB, N_PROJ)
        (nsa_s, win_s, dsa_s, ik_s, dif_s, qa_s, _, _, _, _, qb_s, _, _, iq_s, _, iw_s, qc_s, _, _,
         gat_s) = _post_project(proj_s, tabs_s, prm, None, DB, F32)
        nsa_new = nsa_s.reshape(DB, 1, 1024)
        win_new = win_s.reshape(DB, 1, 512)
        dsa_new = dsa_s.reshape(DB, 1, 256)
        ik_new = ik_s.reshape(DB, 1, IDX_DIM)
        dif_new = dif_s.reshape(DB, 1, 512)
        qa_d = jnp.transpose(qa_s[0], (1, 0, 2))
        qb_d = jnp.transpose(qb_s[0], (1, 0, 2))
        iq_d = jnp.transpose(iq_s[0], (1, 0, 2))
        iq_d = iq_d[:, :, :64] + iq_d[:, :, 64:]
        iw_d = iw_s[0, :, :IDX_HEADS, None]
        qc_d = jnp.transpose(qc_s[0], (2, 0, 1, 3))
        g_d = jnp.transpose(gat_s[0, :, :, :12].reshape(A_KV, DB, A_G, 3), (1, 0, 2, 3)).reshape(DB, A_HEADS, 3)
        g_d = jnp.pad(g_d, ((0, 0), (0, 0), (0, 125)))

        wc_s = _compress_weights(nsa_cmp_w[l], PAGE_SIZE)
        oc_d, sel = _nsa_cmp_decode(page_table, nsa_pages, l, qa_d, nsa_new, wc_s, cc_s, sc_s, past)
        oa_d = _nsa_sel_decode(page_table, sel.reshape(DB, A_KV, SEL_TOPN), nsa_pages, win_rows, l,
                               qa_d, nsa_new, win_new, oc_d, g_d, past)
        mask = _dsa_idx_decode(page_table, idx_pages, l, iq_d, iw_d, ik_new, past)
        ob_d = _dsa_att_decode(page_table, dsa_pages, l, qb_d, dsa_new, mask, past)
        od_d = _diff_decode(page_table, diff_pages, l, qc_d, dif_new, lp, subln, lam_init)
        ys = _mix_out_projection(oa_d.reshape(DB, 1024).astype(BF16), ob_d.reshape(DB, 512).astype(BF16),
                                 od_d.reshape(DB, 512).astype(BF16), w_out_b, ys, l, DB)
        act_s = _norm_swiglu(ys, g_ffn, w_gu_b, l, DB)
        ys = _matmul_residual(act_s, w_down_b, ys, l, DB)
        lw = state_nsa_win.shape[2]
        win_all = jnp.concatenate([state_nsa_win[l], win_new.reshape(DB, 1, 2, A_KV, 128)], axis=1)
        rows_s.append((nsa_new.reshape(DB, 1, 4, A_KV, 128), win_all[:, win_all.shape[1] - min(WINDOW, lw + 1):],
                       dsa_new.reshape(DB, 1, 2, 128), ik_new, dif_new.reshape(DB, 1, 2, C_KV, 128)))

    def stacked(rows, i):
        return jnp.stack([r[i] for r in rows], axis=0)

    nsa, win, dsa, ik, dif = cache_rows_p
    w_keep = min(WINDOW, T)
    return (yp.reshape(B, T, D), ys.reshape(DB, 1, D),
            nsa.reshape(depth, B, T, 4, A_KV, 128), stacked(rows_s, 0),
            win[:, :, (T - w_keep) * 4:].reshape(depth, B, w_keep, 2, A_KV, 128), stacked(rows_s, 1),
            dsa.reshape(depth, B, T, 2, 128), stacked(rows_s, 2), ik, stacked(rows_s, 3),
            dif.reshape(depth, B, T, 2, C_KV, 128), stacked(rows_s, 4))
```
